```python
import jax, jax.numpy as jnp
from jax import lax
import numpy as np

D_MODEL = 2048
BATCH = 1
SEQ = 8192
DEPTH = 1
DEC_BATCH = 16
DEC_SEQ = 16
PAST_LEN = 2048

CHUNK = 64
Q_BLOCK = 128
H_SB = 8
D_SB = 128
H_RET = 4
DK_RET = 256
DV_RET = 256
D_FF = 5632
ROPE_BASE = 10000.0
EPS = 1e-6
SB_W = H_SB * D_SB
RET_QK_W = H_RET * DK_RET
RET_V_W = H_RET * DV_RET
D_IN = 3 * SB_W + 2 * RET_QK_W + 2 * RET_V_W
D_MIX = SB_W + RET_V_W
SPLITS = [SB_W, 2 * SB_W, 3 * SB_W, 3 * SB_W + RET_QK_W, 3 * SB_W + 2 * RET_QK_W,
          3 * SB_W + 2 * RET_QK_W + RET_V_W]

kernel_name = "stickbreak_retention_macaron_stream"


def _rmsnorm(x, w):
    xf = x.astype(jnp.float32)
    y = xf * lax.rsqrt(jnp.mean(xf * xf, axis=-1, keepdims=True) + EPS)
    return (y * w.astype(jnp.float32)).astype(x.dtype)


def _swiglu(x, w_gate, w_up, w_down):
    return (jax.nn.silu(x @ w_gate) * (x @ w_up)) @ w_down


def _heads(t, n):
    b, s, _ = t.shape
    return t.reshape(b, s, n, -1)


def _rotary(x, pos):
    xf = x.astype(jnp.float32)
    half = xf.shape[-1] // 2
    inv = ROPE_BASE ** (-jnp.arange(half, dtype=jnp.float32) / half)
    ang = pos.astype(jnp.float32)[:, None] * inv[None, :]
    cos = jnp.cos(ang)[None, :, None, :]
    sin = jnp.sin(ang)[None, :, None, :]
    x1, x2 = xf[..., :half], xf[..., half:]
    return jnp.concatenate([x1 * cos - x2 * sin, x1 * sin + x2 * cos], axis=-1)


def _bhsd(t):
    return t.transpose(0, 2, 1, 3)


def _pre(x, pos, ffn1_norm, ffn1_w_gate, ffn1_w_up, ffn1_w_down, mix_norm, w_in, sb_q_norm, sb_k_norm):
    x = x + 0.5 * _swiglu(_rmsnorm(x, ffn1_norm), ffn1_w_gate, ffn1_w_up, ffn1_w_down)
    p = _rmsnorm(x, mix_norm) @ w_in
    sq, sk, sv, rq, rk, rv, g = jnp.split(p, SPLITS, axis=-1)
    sq = _bhsd(_rmsnorm(_heads(sq, H_SB), sb_q_norm)).astype(jnp.float32)
    sk = _bhsd(_rmsnorm(_heads(sk, H_SB), sb_k_norm))
    sv = _bhsd(_heads(sv, H_SB))
    rq = _bhsd(_rotary(_heads(rq, H_RET), pos))
    rk = _bhsd(_rotary(_heads(rk, H_RET), pos)) * (DK_RET ** -0.5)
    rv = _bhsd(_heads(rv, H_RET)).astype(jnp.float32)
    return x, sq, sk, sv, rq, rk, rv, g


def _post(x, sb_o, ret_o, gate, ret_out_norm, w_out, ffn2_norm, ffn2_w_gate, ffn2_w_up, ffn2_w_down, final_norm):
    b, _, s, _ = sb_o.shape
    sb = sb_o.transpose(0, 2, 1, 3).reshape(b, s, SB_W)
    r = ret_o.transpose(0, 2, 1, 3)
    r = r * lax.rsqrt(jnp.mean(r * r, axis=-1, keepdims=True) + EPS)
    r = r * ret_out_norm.astype(jnp.float32).reshape(H_RET, DV_RET)
    r = r.reshape(b, s, RET_V_W) * jax.nn.silu(gate.astype(jnp.float32))
    x = x + jnp.concatenate([sb, r], axis=-1).astype(x.dtype) @ w_out
    x = x + 0.5 * _swiglu(_rmsnorm(x, ffn2_norm), ffn2_w_gate, ffn2_w_up, ffn2_w_down)
    return _rmsnorm(x, final_norm)


def _sb_weights(q, k, q_pos, k_pos):
    z = jnp.einsum('bhqd,bhkd->bhqk', q, k.astype(jnp.float32)) * (D_SB ** -0.5)
    mask = k_pos[None, :] < q_pos[:, None]
    log_keep = jnp.where(mask, jax.nn.log_sigmoid(-z), 0.0)
    log_rest = lax.cumsum(log_keep, axis=3, reverse=True) - log_keep
    return jnp.where(mask, jnp.exp(jax.nn.log_sigmoid(z) + log_rest), 0.0)


def _sb_prompt(q, k, v):
    b, h, s, d = q.shape
    k_pos = jnp.arange(s)
    vf = v.astype(jnp.float32)

    def block(i):
        start = i * Q_BLOCK
        qb = lax.dynamic_slice_in_dim(q, start, Q_BLOCK, axis=2)
        a = _sb_weights(qb, k, start + jnp.arange(Q_BLOCK), k_pos)
        return jnp.einsum('bhqk,bhkd->bhqd', a, vf)

    o = lax.map(block, jnp.arange(s // Q_BLOCK))
    return jnp.moveaxis(o, 0, 2).reshape(b, h, s, d)


def _sb_sample(q, k_new, v_new, cache_k, cache_v):
    p = cache_k.shape[2]
    t = q.shape[2]
    k_all = jnp.concatenate([cache_k.astype(jnp.float32), k_new.astype(jnp.float32)], axis=2)
    v_all = jnp.concatenate([cache_v.astype(jnp.float32), v_new.astype(jnp.float32)], axis=2)
    a = _sb_weights(q, k_all, p + jnp.arange(t), jnp.arange(p + t))
    return jnp.einsum('bhqk,bhkd->bhqd', a, v_all)


def _ret_decays(length):
    lg = jnp.log(1.0 - 2.0 ** (-5.0 - jnp.arange(H_RET, dtype=jnp.float32)))
    i = jnp.arange(length, dtype=jnp.float32)
    diff = i[:, None] - i[None, :]
    intra = jnp.where(diff >= 0, jnp.exp(lg[:, None, None] * jnp.maximum(diff, 0.0)), 0.0)
    q_dec = jnp.exp(lg[:, None] * (i + 1.0))
    k_dec = jnp.exp(lg[:, None] * (length - 1.0 - i))
    c_dec = jnp.exp(lg * length)
    return intra, q_dec, k_dec, c_dec


def _ret_chunk(state, q, k, v, decays):
    intra, q_dec, k_dec, c_dec = decays
    scores = jnp.einsum('bhid,bhjd->bhij', q, k) * intra[None]
    o = (jnp.einsum('bhij,bhje->bhie', scores, v)
         + jnp.einsum('bhid,bhde->bhie', q, state) * q_dec[None, :, :, None])
    new_state = (c_dec[None, :, None, None] * state
                 + jnp.einsum('bhjd,bhje->bhde', k * k_dec[None, :, :, None], v))
    return new_state, o


def _ret_prompt(q, k, v):
    b, h, s, _ = q.shape
    nc = s // CHUNK

    def chunks(t):
        return jnp.moveaxis(t.reshape(b, h, nc, CHUNK, t.shape[-1]), 2, 0)

    dec = _ret_decays(CHUNK)
    s0 = jnp.zeros((b, h, DK_RET, DV_RET), jnp.float32)
    s_final, o = lax.scan(lambda st, xs: _ret_chunk(st, xs[0], xs[1], xs[2], dec), s0,
                          (chunks(q), chunks(k), chunks(v)))
    return jnp.moveaxis(o, 0, 2).reshape(b, h, s, DV_RET), s_final


def setup_inputs(seed: int = 0) -> dict:
    key = jax.random.key(seed)
    ks = jax.random.split(key, 24)
    f32 = jnp.float32

    def w(k, shape, fan_in):
        return jax.random.normal(k, shape, f32) * (fan_in ** -0.5)

    def gain(k, n):
        return 1.0 + 0.01 * jax.random.normal(k, (DEPTH, n), f32)

    return {
        "x_prompt": jax.random.normal(ks[0], (BATCH, SEQ, D_MODEL), f32),
        "x_sample": jax.random.normal(ks[1], (DEC_BATCH, DEC_SEQ, D_MODEL), f32),
        "cache_sb_k": jax.random.normal(ks[2], (DEPTH, DEC_BATCH, H_SB, PAST_LEN, D_SB), f32),
        "cache_sb_v": jax.random.normal(ks[3], (DEPTH, DEC_BATCH, H_SB, PAST_LEN, D_SB), f32),
        "state_ret": 0.5 * jax.random.normal(ks[4], (DEPTH, DEC_BATCH, H_RET, DK_RET, DV_RET), f32),
        "ffn1_norm": gain(ks[5], D_MODEL),
        "ffn1_w_gate": w(ks[6], (DEPTH, D_MODEL, D_FF), D_MODEL),
        "ffn1_w_up": w(ks[7], (DEPTH, D_MODEL, D_FF), D_MODEL),
        "ffn1_w_down": w(ks[8], (DEPTH, D_FF, D_MODEL), D_FF),
        "mix_norm": gain(ks[9], D_MODEL),
        "w_in": w(ks[10], (DEPTH, D_MODEL, D_IN), D_MODEL),
        "sb_q_norm": gain(ks[11], D_SB),
        "sb_k_norm": gain(ks[12], D_SB),
        "ret_out_norm": gain(ks[13], RET_V_W),
        "w_out": w(ks[14], (DEPTH, D_MIX, D_MODEL), D_MIX),
        "ffn2_norm": gain(ks[15], D_MODEL),
        "ffn2_w_gate": w(ks[16], (DEPTH, D_MODEL, D_FF), D_MODEL),
        "ffn2_w_up": w(ks[17], (DEPTH, D_MODEL, D_FF), D_MODEL),
        "ffn2_w_down": w(ks[18], (DEPTH, D_FF, D_MODEL), D_FF),
        "final_norm": gain(ks[19], D_MODEL),
    }


def reference(x_prompt, x_sample, cache_sb_k, cache_sb_v, state_ret,
              ffn1_norm, ffn1_w_gate, ffn1_w_up, ffn1_w_down, mix_norm, w_in,
              sb_q_norm, sb_k_norm, ret_out_norm, w_out,
              ffn2_norm, ffn2_w_gate, ffn2_w_up, ffn2_w_down, final_norm):
    past = cache_sb_k.shape[3]
    pos_p = jnp.arange(x_prompt.shape[1])
    pos_s = past + jnp.arange(x_sample.shape[1])
    xp, xs = x_prompt, x_sample
    kp_l, vp_l, sp_l, ks_l, vs_l, ss_l = [], [], [], [], [], []
    for l in range(DEPTH):
        pre = (ffn1_norm[l], ffn1_w_gate[l], ffn1_w_up[l], ffn1_w_down[l], mix_norm[l], w_in[l],
               sb_q_norm[l], sb_k_norm[l])
        post = (ret_out_norm[l], w_out[l], ffn2_norm[l], ffn2_w_gate[l], ffn2_w_up[l],
                ffn2_w_down[l], final_norm[l])
        hp, sq, sk, sv, rq, rk, rv, g = _pre(xp, pos_p, *pre)
        sb_o = _sb_prompt(sq, sk, sv)
        ret_o, st_p = _ret_prompt(rq, rk, rv)
        xp = _post(hp, sb_o, ret_o, g, *post)
        kp_l.append(sk)
        vp_l.append(sv)
        sp_l.append(st_p.astype(x_prompt.dtype))
        hs, sq, sk, sv, rq, rk, rv, g = _pre(xs, pos_s, *pre)
        sb_o = _sb_sample(sq, sk, sv, cache_sb_k[l], cache_sb_v[l])
        st_s, ret_o = _ret_chunk(state_ret[l].astype(jnp.float32), rq, rk, rv,
                                 _ret_decays(x_sample.shape[1]))
        xs = _post(hs, sb_o, ret_o, g, *post)
        ks_l.append(sk.astype(cache_sb_k.dtype))
        vs_l.append(sv.astype(cache_sb_v.dtype))
        ss_l.append(st_s.astype(state_ret.dtype))
    return (xp, xs, jnp.stack(kp_l), jnp.stack(vp_l), jnp.stack(sp_l),
            jnp.stack(ks_l), jnp.stack(vs_l), jnp.stack(ss_l))
```

```python
import functools
import math

import jax
import jax.numpy as jnp
from jax import lax
from jax.experimental import pallas as pl
from jax.experimental.pallas import tpu as pltpu

F32 = jnp.float32
BF16 = jnp.bfloat16

D_MODEL = 2048
D_FF = 5632
H_SB = 8
D_SB = 128
H_RET = 4
DK_RET = 256
DV_RET = 256
SB_W = H_SB * D_SB
RET_W = H_RET * DK_RET
N_SECTIONS = 7
SECTION_W = 1024
ROPE_BASE = 10000.0
EPS = 1e-6

V7X_VMEM_LIMIT_BYTES = 56 * 1024 * 1024
SB_BLOCK = 256
RET_CHUNK = 256
FFN_TF = 512

LOG2E = 1.4426950408889634
LN2 = 0.6931471805599453


def _cparams(sem):
    return pltpu.CompilerParams(dimension_semantics=sem,
                                vmem_limit_bytes=V7X_VMEM_LIMIT_BYTES)


def _rmsnorm_rows(x, w):
    ms = jnp.mean(x * x, axis=-1, keepdims=True)
    return x * lax.rsqrt(ms + EPS) * w


def _dot(a, b):
    return jnp.dot(a, b, preferred_element_type=F32)


def _dot_nt(a, b):
    return lax.dot_general(a, b, (((1,), (1,)), ((), ())), preferred_element_type=F32)


def _dot_tn(a, b):
    return lax.dot_general(a, b, (((0,), (0,)), ((), ())), preferred_element_type=F32)


def _ffn_kernel(*refs, n_steps, final_norm):
    if final_norm:
        x_ref, nw_ref, wg_ref, wu_ref, wd_ref, fw_ref, o_ref, h_ref = refs
    else:
        x_ref, nw_ref, wg_ref, wu_ref, wd_ref, o_ref, h_ref = refs
        fw_ref = None
    j = pl.program_id(1)

    @pl.when(j == 0)
    def _():
        x = x_ref[...]
        h_ref[...] = _rmsnorm_rows(x, nw_ref[...]).astype(BF16)
        o_ref[...] = x

    h = h_ref[...]
    g = _dot(h, wg_ref[...])
    u = _dot(h, wu_ref[...])
    a = (g * jax.nn.sigmoid(g) * (0.5 * u)).astype(BF16)
    o_ref[...] += _dot(a, wd_ref[...])

    if final_norm:
        @pl.when(j == n_steps - 1)
        def _():
            o_ref[...] = _rmsnorm_rows(o_ref[...], fw_ref[...])


def _ffn(x, norm_w, wg, wu, wd, final_w=None, *, tm):
    m = x.shape[0]
    n_steps = D_FF // FFN_TF
    in_specs = [
        pl.BlockSpec((tm, D_MODEL), lambda i, j: (i, 0)),
        pl.BlockSpec((1, D_MODEL), lambda i, j: (0, 0)),
        pl.BlockSpec((D_MODEL, FFN_TF), lambda i, j: (0, j)),
        pl.BlockSpec((D_MODEL, FFN_TF), lambda i, j: (0, j)),
        pl.BlockSpec((FFN_TF, D_MODEL), lambda i, j: (j, 0)),
    ]
    args = [x, norm_w, wg, wu, wd]
    if final_w is not None:
        in_specs.append(pl.BlockSpec((1, D_MODEL), lambda i, j: (0, 0)))
        args.append(final_w)
    return pl.pallas_call(
        functools.partial(_ffn_kernel, n_steps=n_steps, final_norm=final_w is not None),
        out_shape=jax.ShapeDtypeStruct((m, D_MODEL), F32),
        grid=(m // tm, n_steps),
        in_specs=in_specs,
        out_specs=pl.BlockSpec((tm, D_MODEL), lambda i, j: (i, 0)),
        scratch_shapes=[pltpu.VMEM((tm, D_MODEL), BF16)],
        compiler_params=_cparams(("arbitrary", "arbitrary")),
        name="ffn_final" if final_w is not None else "ffn",
    )(*args)


def _inproj_kernel(x_ref, nw_ref, w_ref, qn_ref, kn_ref, inv_ref,
                   sq_ref, sk_ref, sk16_ref, sv_ref, sv16_ref,
                   rq_ref, rk_ref, rv_ref, g_ref,
                   h_ref, cos_ref, sin_ref, *, tm, pos_base, pos_mod):
    i = pl.program_id(0)
    j = pl.program_id(1)

    @pl.when(j == 0)
    def _():
        h_ref[...] = _rmsnorm_rows(x_ref[...], nw_ref[...]).astype(BF16)
        row = lax.broadcasted_iota(jnp.int32, (tm, DK_RET // 2), 0) + i * tm
        pos = (pos_base + lax.rem(row, pos_mod)).astype(F32)
        ang = pos * inv_ref[...]
        cos_ref[...] = jnp.cos(ang)
        sin_ref[...] = jnp.sin(ang)

    def proj():
        return _dot(h_ref[...], w_ref[...])

    def head(p, hd, width):
        return p[:, hd * width:(hd + 1) * width]

    @pl.when(j == 0)
    def _():
        p = proj()
        w = qn_ref[...] * (D_SB ** -0.5)
        for hd in range(H_SB):
            sq_ref[hd] = _rmsnorm_rows(head(p, hd, D_SB), w).astype(BF16)

    @pl.when(j == 1)
    def _():
        p = proj()
        for hd in range(H_SB):
            k = _rmsnorm_rows(head(p, hd, D_SB), kn_ref[...])
            sk_ref[hd] = k
            sk16_ref[hd] = k.astype(BF16)

    @pl.when(j == 2)
    def _():
        p = proj()
        for hd in range(H_SB):
            v = head(p, hd, D_SB)
            sv_ref[hd] = v
            sv16_ref[hd] = v.astype(BF16)

    def rotary(ph):
        half = DK_RET // 2
        x1, x2 = ph[:, :half], ph[:, half:]
        c, s = cos_ref[...], sin_ref[...]
        return jnp.concatenate([x1 * c - x2 * s, x1 * s + x2 * c], axis=-1)

    @pl.when(j == 3)
    def _():
        p = proj()
        for hd in range(H_RET):
            rq_ref[hd] = rotary(head(p, hd, DK_RET)).astype(BF16)

    @pl.when(j == 4)
    def _():
        p = proj()
        for hd in range(H_RET):
            rk_ref[hd] = rotary(head(p, hd, DK_RET)) * (DK_RET ** -0.5)

    @pl.when(j == 5)
    def _():
        p = proj()
        for hd in range(H_RET):
            rv_ref[hd] = head(p, hd, DV_RET).astype(BF16)

    @pl.when(j == 6)
    def _():
        g_ref[...] = proj()


def _inproj(x, norm_w, w_in, qn, kn, inv_freq, *, tm, pos_base, pos_mod):
    m = x.shape[0]
    head_spec_sb = pl.BlockSpec((H_SB, tm, D_SB), lambda i, j: (0, i, 0))
    head_spec_ret = pl.BlockSpec((H_RET, tm, DK_RET), lambda i, j: (0, i, 0))
    sb16 = jax.ShapeDtypeStruct((H_SB, m, D_SB), BF16)
    sb32 = jax.ShapeDtypeStruct((H_SB, m, D_SB), F32)
    ret16 = jax.ShapeDtypeStruct((H_RET, m, DK_RET), BF16)
    ret32 = jax.ShapeDtypeStruct((H_RET, m, DK_RET), F32)
    return pl.pallas_call(
        functools.partial(_inproj_kernel, tm=tm, pos_base=pos_base, pos_mod=pos_mod),
        out_shape=(sb16, sb32, sb16, sb32, sb16, ret16, ret32, ret16,
                   jax.ShapeDtypeStruct((m, SECTION_W), F32)),
        grid=(m // tm, N_SECTIONS),
        in_specs=[
            pl.BlockSpec((tm, D_MODEL), lambda i, j: (i, 0)),
            pl.BlockSpec((1, D_MODEL), lambda i, j: (0, 0)),
            pl.BlockSpec((D_MODEL, SECTION_W), lambda i, j: (0, j)),
            pl.BlockSpec((1, D_SB), lambda i, j: (0, 0)),
            pl.BlockSpec((1, D_SB), lambda i, j: (0, 0)),
            pl.BlockSpec((1, DK_RET // 2), lambda i, j: (0, 0)),
        ],
        out_specs=(head_spec_sb, head_spec_sb, head_spec_sb, head_spec_sb, head_spec_sb,
                   head_spec_ret, head_spec_ret, head_spec_ret,
                   pl.BlockSpec((tm, SECTION_W), lambda i, j: (i, 0))),
        scratch_shapes=[pltpu.VMEM((tm, D_MODEL), BF16),
                        pltpu.VMEM((tm, DK_RET // 2), F32),
                        pltpu.VMEM((tm, DK_RET // 2), F32)],
        compiler_params=_cparams(("arbitrary", "arbitrary")),
        name="inproj",
    )(x, norm_w, w_in, qn, kn, inv_freq)


def _softplus(z):
    return jnp.maximum(z, 0.0) + jnp.log(1.0 + jnp.exp(-jnp.abs(z)))


def _suffix_matrix(n):
    r = lax.broadcasted_iota(jnp.int32, (n, n), 0)
    c = lax.broadcasted_iota(jnp.int32, (n, n), 1)
    return jnp.where(r >= c, 1.0, 0.0).astype(BF16)


def _sb_block(q, k, v, carry, tri, mask):
    z = _dot_nt(q, k)
    sp = _softplus(z)
    if mask is not None:
        sp = jnp.where(mask, sp, 0.0)
    hi = sp.astype(BF16)
    lo = (sp - hi.astype(F32)).astype(BF16)
    incl = _dot(hi, tri) + _dot(lo, tri) + carry
    a = jnp.exp(z - incl)
    if mask is not None:
        a = jnp.where(mask, a, 0.0)
    return _dot(a.astype(BF16), v), incl[:, 0:1]


def _sb_prompt_kernel(q_ref, k_ref, v_ref, o_ref):
    qi = pl.program_id(1)
    blk = SB_BLOCK
    q = q_ref[...]
    tri = _suffix_matrix(blk)
    r = lax.broadcasted_iota(jnp.int32, (blk, blk), 0)
    c = lax.broadcasted_iota(jnp.int32, (blk, blk), 1)
    start = pl.multiple_of(qi * blk, blk)
    acc, carry = _sb_block(q, k_ref[pl.ds(start, blk), :], v_ref[pl.ds(start, blk), :],
                           jnp.zeros((blk, 1), F32), tri, c < r)

    def body(t, state):
        acc, carry = state
        s0 = pl.multiple_of((qi - 1 - t) * blk, blk)
        d, carry = _sb_block(q, k_ref[pl.ds(s0, blk), :], v_ref[pl.ds(s0, blk), :],
                             carry, tri, None)
        return acc + d, carry

    acc, _ = lax.fori_loop(0, qi, body, (acc, carry))
    o_ref[...] = acc.astype(o_ref.dtype)


def _sb_prompt(q16, k16, v16):
    h, s, d = q16.shape
    blk = SB_BLOCK
    return pl.pallas_call(
        _sb_prompt_kernel,
        out_shape=jax.ShapeDtypeStruct((s, h * d), BF16),
        grid=(h, s // blk),
        in_specs=[
            pl.BlockSpec((None, blk, d), lambda hd, qi: (hd, qi, 0)),
            pl.BlockSpec((None, s, d), lambda hd, qi: (hd, 0, 0)),
            pl.BlockSpec((None, s, d), lambda hd, qi: (hd, 0, 0)),
        ],
        out_specs=pl.BlockSpec((blk, d), lambda hd, qi: (qi, hd)),
        compiler_params=_cparams(("arbitrary", "arbitrary")),
        name="sb_prompt",
    )(q16, k16, v16)


def _sb_sample_kernel(q_ref, kn_ref, vn_ref, kc_ref, vc_ref, o_ref, *, past):
    blk = SB_BLOCK
    t = q_ref.shape[0]
    q = q_ref[...]
    r = lax.broadcasted_iota(jnp.int32, (t, t), 0)
    c = lax.broadcasted_iota(jnp.int32, (t, t), 1)
    acc, carry = _sb_block(q, kn_ref[...], vn_ref[...], jnp.zeros((t, 1), F32),
                           _suffix_matrix(t), c < r)
    tri = _suffix_matrix(blk)
    for b in range(past // blk - 1, -1, -1):
        k = kc_ref[b * blk:(b + 1) * blk, :].astype(BF16)
        v = vc_ref[b * blk:(b + 1) * blk, :].astype(BF16)
        d, carry = _sb_block(q, k, v, carry, tri, None)
        acc = acc + d
    o_ref[...] = acc.astype(o_ref.dtype)


def _sb_sample(q16, k16, v16, cache_k, cache_v, *, t):
    h, m, d = q16.shape
    nb = m // t
    past = cache_k.shape[2]
    new_spec = pl.BlockSpec((None, t, d), lambda b, hd: (hd, b, 0))
    cache_spec = pl.BlockSpec((None, None, past, d), lambda b, hd: (b, hd, 0, 0))
    return pl.pallas_call(
        functools.partial(_sb_sample_kernel, past=past),
        out_shape=jax.ShapeDtypeStruct((m, h * d), BF16),
        grid=(nb, h),
        in_specs=[new_spec, new_spec, new_spec, cache_spec, cache_spec],
        out_specs=pl.BlockSpec((t, d), lambda b, hd: (b, hd)),
        compiler_params=_cparams(("arbitrary", "arbitrary")),
        name="sb_sample",
    )(q16, k16, v16, cache_k, cache_v)


def _ret_log_decay(hd):
    return math.log(1.0 - 2.0 ** (-5.0 - hd))


def _ret_kernel(q_ref, k_ref, v_ref, g_ref, nw_ref, s0_ref, r_ref, st_ref, *, chunk):
    c = pl.program_id(1)

    @pl.when(c == 0)
    def _():
        st_ref[...] = s0_ref[...]

    row = lax.broadcasted_iota(jnp.int32, (chunk, chunk), 0)
    col = lax.broadcasted_iota(jnp.int32, (chunk, chunk), 1)
    diff = (row - col).astype(F32)
    causal = row >= col
    pos = lax.broadcasted_iota(jnp.int32, (chunk, DK_RET), 0).astype(F32)
    for hd in range(H_RET):
        lg = _ret_log_decay(hd)
        intra = jnp.where(causal, jnp.exp(lg * jnp.maximum(diff, 0.0)), 0.0)
        q_dec = jnp.exp(lg * (pos + 1.0))
        k_dec = jnp.exp(lg * (chunk - 1.0 - pos))
        c_dec = math.exp(lg * chunk)
        q = q_ref[hd]
        k = k_ref[hd]
        v = v_ref[hd]
        state = st_ref[hd]
        scores = (_dot_nt(q, k.astype(BF16)) * intra).astype(BF16)
        o = _dot(scores, v) + _dot(q, state.astype(BF16)) * q_dec
        st_ref[hd] = c_dec * state + _dot_tn((k * k_dec).astype(BF16), v)
        o = o * lax.rsqrt(jnp.mean(o * o, axis=-1, keepdims=True) + EPS)
        sl = slice(hd * DV_RET, (hd + 1) * DV_RET)
        gate = g_ref[:, sl]
        r_ref[:, sl] = (o * nw_ref[:, sl] * (gate * jax.nn.sigmoid(gate))).astype(r_ref.dtype)


def _retention(rq, rk, rv, gate, norm_w, state0, *, chunk):
    h, m, _ = rq.shape
    nb = state0.shape[0]
    nc = m // (nb * chunk)
    qkv_spec = pl.BlockSpec((h, chunk, DK_RET), lambda b, c: (0, b * nc + c, 0))
    st_spec = pl.BlockSpec((None, h, DK_RET, DV_RET), lambda b, c: (b, 0, 0, 0))
    return pl.pallas_call(
        functools.partial(_ret_kernel, chunk=chunk),
        out_shape=(jax.ShapeDtypeStruct((m, h * DV_RET), BF16),
                   jax.ShapeDtypeStruct(state0.shape, F32)),
        grid=(nb, nc),
        in_specs=[qkv_spec, qkv_spec, qkv_spec,
                  pl.BlockSpec((chunk, h * DV_RET), lambda b, c: (b * nc + c, 0)),
                  pl.BlockSpec((1, h * DV_RET), lambda b, c: (0, 0)),
                  st_spec],
        out_specs=(pl.BlockSpec((chunk, h * DV_RET), lambda b, c: (b * nc + c, 0)), st_spec),
        compiler_params=_cparams(("arbitrary", "arbitrary")),
        name="retention",
    )(rq, rk, rv, gate, norm_w, state0)


def _outproj_kernel(x_ref, a_ref, b_ref, wa_ref, wb_ref, o_ref):
    o_ref[...] = x_ref[...] + _dot(a_ref[...], wa_ref[...]) + _dot(b_ref[...], wb_ref[...])


def _outproj(x, a_sb, a_ret, w_out, *, tm):
    m = x.shape[0]
    row = lambda i: (i, 0)
    return pl.pallas_call(
        _outproj_kernel,
        out_shape=jax.ShapeDtypeStruct((m, D_MODEL), F32),
        grid=(m // tm,),
        in_specs=[
            pl.BlockSpec((tm, D_MODEL), row),
            pl.BlockSpec((tm, SB_W), row),
            pl.BlockSpec((tm, RET_W), row),
            pl.BlockSpec((SB_W, D_MODEL), lambda i: (0, 0)),
            pl.BlockSpec((RET_W, D_MODEL), lambda i: (1, 0)),
        ],
        out_specs=pl.BlockSpec((tm, D_MODEL), row),
        compiler_params=_cparams(("arbitrary",)),
        name="outproj",
    )(x, a_sb, a_ret, w_out, w_out)


def _layer(x, weights, sb_fn, state0, *, tm, pos_base, pos_mod, chunk):
    (n1, wg1, wu1, wd1, nmix, w_in, qn, kn, ron, w_out, n2, wg2, wu2, wd2, nf, inv_freq) = weights
    x1 = _ffn(x, n1, wg1, wu1, wd1, tm=tm)
    sq, sk, sk16, sv, sv16, rq, rk, rv, gate = _inproj(
        x1, nmix, w_in, qn, kn, inv_freq, tm=tm, pos_base=pos_base, pos_mod=pos_mod)
    a_sb = sb_fn(sq, sk16, sv16)
    a_ret, state = _retention(rq, rk, rv, gate, ron, state0, chunk=chunk)
    x2 = _outproj(x1, a_sb, a_ret, w_out, tm=tm)
    y = _ffn(x2, n2, wg2, wu2, wd2, nf, tm=tm)
    return y, sk, sv, state


def kernel(x_prompt, x_sample, cache_sb_k, cache_sb_v, state_ret, ffn1_norm, ffn1_w_gate, ffn1_w_up, ffn1_w_down, mix_norm, w_in, sb_q_norm, sb_k_norm, ret_out_norm, w_out, ffn2_norm, ffn2_w_gate, ffn2_w_up, ffn2_w_down, final_norm):
    depth = ffn1_norm.shape[0]
    assert depth == 1
    nb_p, seq, _ = x_prompt.shape
    nb_s, dec_seq, _ = x_sample.shape
    past = cache_sb_k.shape[3]
    assert nb_p == 1

    half = DK_RET // 2
    inv_freq = (ROPE_BASE ** (-jnp.arange(half, dtype=F32) / half)).reshape(1, half)
    l = 0
    weights = (ffn1_norm[l][None], ffn1_w_gate[l].astype(BF16), ffn1_w_up[l].astype(BF16),
               ffn1_w_down[l].astype(BF16), mix_norm[l][None], w_in[l].astype(BF16),
               sb_q_norm[l][None], sb_k_norm[l][None], ret_out_norm[l][None],
               w_out[l].astype(BF16), ffn2_norm[l][None], ffn2_w_gate[l].astype(BF16),
               ffn2_w_up[l].astype(BF16), ffn2_w_down[l].astype(BF16), final_norm[l][None],
               inv_freq)

    xp = x_prompt.reshape(seq, D_MODEL)
    zero_state = jnp.zeros((1, H_RET, DK_RET, DV_RET), F32)
    yp, skp, svp, stp = _layer(xp, weights, _sb_prompt, zero_state,
                               tm=512, pos_base=0, pos_mod=seq, chunk=RET_CHUNK)

    xs = x_sample.reshape(nb_s * dec_seq, D_MODEL)
    sb_s = functools.partial(_sb_sample, cache_k=cache_sb_k[l], cache_v=cache_sb_v[l], t=dec_seq)
    ys, sks, svs, sts = _layer(xs, weights, sb_s, state_ret[l],
                               tm=nb_s * dec_seq, pos_base=past, pos_mod=dec_seq, chunk=dec_seq)

    def cache_layout(t):
        return t.reshape(H_SB, nb_s, dec_seq, D_SB).transpose(1, 0, 2, 3)[None]

    return (yp.reshape(1, seq, D_MODEL), ys.reshape(nb_s, dec_seq, D_MODEL),
            skp[None, None], svp[None, None], stp[None],
            cache_layout(sks), cache_layout(svs), sts[None])
```

```python
import functools
import math

import jax
import jax.numpy as jnp
from jax import lax
from jax.experimental import pallas as pl
from jax.experimental.pallas import tpu as pltpu

F32 = jnp.float32
BF16 = jnp.bfloat16

D_MODEL = 2048
D_FF = 5632
H_SB = 8
D_SB = 128
H_RET = 4
DK_RET = 256
DV_RET = 256
SB_W = H_SB * D_SB
RET_W = H_RET * DK_RET
N_SECTIONS = 7
SECTION_W = 1024
ROPE_BASE = 10000.0
EPS = 1e-6

V7X_VMEM_LIMIT_BYTES = 56 * 1024 * 1024
SB_BLOCK = 256
RET_CHUNK = 256
FFN_TF = 512

LOG2E = 1.4426950408889634
LN2 = 0.6931471805599453


def _cparams(sem):
    return pltpu.CompilerParams(dimension_semantics=sem,
                                vmem_limit_bytes=V7X_VMEM_LIMIT_BYTES)


def _rmsnorm_rows(x, w):
    ms = jnp.mean(x * x, axis=-1, keepdims=True)
    return x * lax.rsqrt(ms + EPS) * w


def _dot(a, b):
    return jnp.dot(a, b, preferred_element_type=F32)


def _dot_nt(a, b):
    return lax.dot_general(a, b, (((1,), (1,)), ((), ())), preferred_element_type=F32)


def _dot_tn(a, b):
    return lax.dot_general(a, b, (((0,), (0,)), ((), ())), preferred_element_type=F32)


def _ffn_kernel(*refs, n_steps, final_norm):
    if final_norm:
        x_ref, nw_ref, wg_ref, wu_ref, wd_ref, fw_ref, o_ref, h_ref = refs
    else:
        x_ref, nw_ref, wg_ref, wu_ref, wd_ref, o_ref, h_ref = refs
        fw_ref = None
    j = pl.program_id(1)

    @pl.when(j == 0)
    def _():
        x = x_ref[...]
        h_ref[...] = _rmsnorm_rows(x, nw_ref[...]).astype(BF16)
        o_ref[...] = x

    h = h_ref[...]
    g = _dot(h, wg_ref[...])
    u = _dot(h, wu_ref[...])
    a = (g * jax.nn.sigmoid(g) * (0.5 * u)).astype(BF16)
    o_ref[...] += _dot(a, wd_ref[...])

    if final_norm:
        @pl.when(j == n_steps - 1)
        def _():
            o_ref[...] = _rmsnorm_rows(o_ref[...], fw_ref[...])


def _ffn(x, norm_w, wg, wu, wd, final_w=None, *, tm):
    m = x.shape[0]
    n_steps = D_FF // FFN_TF
    in_specs = [
        pl.BlockSpec((tm, D_MODEL), lambda i, j: (i, 0)),
        pl.BlockSpec((1, D_MODEL), lambda i, j: (0, 0)),
        pl.BlockSpec((D_MODEL, FFN_TF), lambda i, j: (0, j)),
        pl.BlockSpec((D_MODEL, FFN_TF), lambda i, j: (0, j)),
        pl.BlockSpec((FFN_TF, D_MODEL), lambda i, j: (j, 0)),
    ]
    args = [x, norm_w, wg, wu, wd]
    if final_w is not None:
        in_specs.append(pl.BlockSpec((1, D_MODEL), lambda i, j: (0, 0)))
        args.append(final_w)
    return pl.pallas_call(
        functools.partial(_ffn_kernel, n_steps=n_steps, final_norm=final_w is not None),
        out_shape=jax.ShapeDtypeStruct((m, D_MODEL), F32),
        grid=(m // tm, n_steps),
        in_specs=in_specs,
        out_specs=pl.BlockSpec((tm, D_MODEL), lambda i, j: (i, 0)),
        scratch_shapes=[pltpu.VMEM((tm, D_MODEL), BF16)],
        compiler_params=_cparams(("arbitrary", "arbitrary")),
        name="ffn_final" if final_w is not None else "ffn",
    )(*args)


def _inproj_kernel(x_ref, nw_ref, w_ref, qn_ref, kn_ref, inv_ref,
                   sq_ref, sk_ref, sk16_ref, sv_ref, sv16_ref,
                   rq_ref, rk_ref, rv_ref, g_ref,
                   h_ref, cos_ref, sin_ref, *, tm, pos_base, pos_mod):
    i = pl.program_id(0)
    j = pl.program_id(1)

    @pl.when(j == 0)
    def _():
        h_ref[...] = _rmsnorm_rows(x_ref[...], nw_ref[...]).astype(BF16)
        row = lax.broadcasted_iota(jnp.int32, (tm, DK_RET // 2), 0) + i * tm
        pos = (pos_base + lax.rem(row, pos_mod)).astype(F32)
        ang = pos * inv_ref[...]
        cos_ref[...] = jnp.cos(ang)
        sin_ref[...] = jnp.sin(ang)

    def proj():
        return _dot(h_ref[...], w_ref[...])

    def head(p, hd, width):
        return p[:, hd * width:(hd + 1) * width]

    @pl.when(j == 0)
    def _():
        p = proj()
        w = qn_ref[...] * (D_SB ** -0.5 * LOG2E)
        for hd in range(H_SB):
            sq_ref[hd] = _rmsnorm_rows(head(p, hd, D_SB), w).astype(BF16)

    @pl.when(j == 1)
    def _():
        p = proj()
        for hd in range(H_SB):
            k = _rmsnorm_rows(head(p, hd, D_SB), kn_ref[...])
            sk_ref[hd] = k
            sk16_ref[hd] = k.astype(BF16)

    @pl.when(j == 2)
    def _():
        p = proj()
        for hd in range(H_SB):
            v = head(p, hd, D_SB)
            sv_ref[hd] = v
            sv16_ref[hd] = v.astype(BF16)

    def rotary(ph):
        half = DK_RET // 2
        x1, x2 = ph[:, :half], ph[:, half:]
        c, s = cos_ref[...], sin_ref[...]
        return jnp.concatenate([x1 * c - x2 * s, x1 * s + x2 * c], axis=-1)

    @pl.when(j == 3)
    def _():
        p = proj()
        for hd in range(H_RET):
            rq_ref[hd] = rotary(head(p, hd, DK_RET)).astype(BF16)

    @pl.when(j == 4)
    def _():
        p = proj()
        for hd in range(H_RET):
            rk_ref[hd] = rotary(head(p, hd, DK_RET)) * (DK_RET ** -0.5)

    @pl.when(j == 5)
    def _():
        p = proj()
        for hd in range(H_RET):
            rv_ref[hd] = head(p, hd, DV_RET).astype(BF16)

    @pl.when(j == 6)
    def _():
        g_ref[...] = proj()


def _inproj(x, norm_w, w_in, qn, kn, inv_freq, *, tm, pos_base, pos_mod):
    m = x.shape[0]
    head_spec_sb = pl.BlockSpec((H_SB, tm, D_SB), lambda i, j: (0, i, 0))
    head_spec_ret = pl.BlockSpec((H_RET, tm, DK_RET), lambda i, j: (0, i, 0))
    sb16 = jax.ShapeDtypeStruct((H_SB, m, D_SB), BF16)
    sb32 = jax.ShapeDtypeStruct((H_SB, m, D_SB), F32)
    ret16 = jax.ShapeDtypeStruct((H_RET, m, DK_RET), BF16)
    ret32 = jax.ShapeDtypeStruct((H_RET, m, DK_RET), F32)
    return pl.pallas_call(
        functools.partial(_inproj_kernel, tm=tm, pos_base=pos_base, pos_mod=pos_mod),
        out_shape=(sb16, sb32, sb16, sb32, sb16, ret16, ret32, ret16,
                   jax.ShapeDtypeStruct((m, SECTION_W), F32)),
        grid=(m // tm, N_SECTIONS),
        in_specs=[
            pl.BlockSpec((tm, D_MODEL), lambda i, j: (i, 0)),
            pl.BlockSpec((1, D_MODEL), lambda i, j: (0, 0)),
            pl.BlockSpec((D_MODEL, SECTION_W), lambda i, j: (0, j)),
            pl.BlockSpec((1, D_SB), lambda i, j: (0, 0)),
            pl.BlockSpec((1, D_SB), lambda i, j: (0, 0)),
            pl.BlockSpec((1, DK_RET // 2), lambda i, j: (0, 0)),
        ],
        out_specs=(head_spec_sb, head_spec_sb, head_spec_sb, head_spec_sb, head_spec_sb,
                   head_spec_ret, head_spec_ret, head_spec_ret,
                   pl.BlockSpec((tm, SECTION_W), lambda i, j: (i, 0))),
        scratch_shapes=[pltpu.VMEM((tm, D_MODEL), BF16),
                        pltpu.VMEM((tm, DK_RET // 2), F32),
                        pltpu.VMEM((tm, DK_RET // 2), F32)],
        compiler_params=_cparams(("arbitrary", "arbitrary")),
        name="inproj",
    )(x, norm_w, w_in, qn, kn, inv_freq)


def _softplus2(z):
    return jnp.maximum(z, 0.0) + jnp.log2(1.0 + jnp.exp2(-jnp.abs(z)))


def _suffix_matrix(n):
    r = lax.broadcasted_iota(jnp.int32, (2 * n, n), 0)
    c = lax.broadcasted_iota(jnp.int32, (2 * n, n), 1)
    return jnp.where(jnp.where(r >= n, r - n, r) >= c, 1.0, 0.0).astype(BF16)


def _sb_block(q, k, v, carry, tri, mask):
    z = _dot_nt(q, k)
    sp = _softplus2(z)
    if mask is not None:
        sp = jnp.where(mask, sp, 0.0)
    hi = sp.astype(BF16)
    lo = (sp - hi.astype(F32)).astype(BF16)
    n = hi.shape[1]
    if n % 128 == 0:
        incl = _dot(jnp.concatenate([hi, lo], axis=1), tri) + carry
    else:
        incl = _dot(hi, tri[:n]) + _dot(lo, tri[:n]) + carry
    a = jnp.exp2(z - incl)
    if mask is not None:
        a = jnp.where(mask, a, 0.0)
    return _dot(a.astype(BF16), v), incl[:, 0:1]


SB_DEAD_CARRY = 150.0


def _sb_prompt_kernel(q_ref, k_ref, v_ref, o_ref):
    qi = pl.program_id(1)
    blk = SB_BLOCK
    tri = _suffix_matrix(blk)
    r = lax.broadcasted_iota(jnp.int32, (blk, blk), 0)
    c = lax.broadcasted_iota(jnp.int32, (blk, blk), 1)
    strict = c < r
    q = q_ref[...]
    q_a, q_b = q[:blk], q[blk:]

    def kv(block):
        s0 = pl.multiple_of(block * blk, blk)
        return k_ref[pl.ds(s0, blk), :], v_ref[pl.ds(s0, blk), :]

    zero = jnp.zeros((blk, 1), F32)
    k0, v0 = kv(2 * qi)
    k1, v1 = kv(2 * qi + 1)
    acc_a, carry_a = _sb_block(q_a, k0, v0, zero, tri, strict)
    acc_b, carry_b = _sb_block(q_b, k1, v1, zero, tri, strict)
    d_b, carry_b = _sb_block(q_b, k0, v0, carry_b, tri, None)
    acc = jnp.concatenate([acc_a, acc_b + d_b], axis=0)
    carry = jnp.concatenate([carry_a, carry_b], axis=0)

    def alive(carry):
        return (jnp.min(carry) < SB_DEAD_CARRY).astype(jnp.int32)

    def cond(state):
        t, live, _, _ = state
        return jnp.logical_and(t >= 0, live > 0)

    def body(state):
        t, _, acc, carry = state
        kb, vb = kv(t)
        d, carry = _sb_block(q, kb, vb, carry, tri, None)
        return t - 1, alive(carry), acc + d, carry

    _, _, acc, _ = lax.while_loop(cond, body, (2 * qi - 1, alive(carry), acc, carry))
    o_ref[...] = acc.astype(o_ref.dtype)


def _sb_prompt(q16, k16, v16):
    h, s, d = q16.shape
    tq = 2 * SB_BLOCK
    return pl.pallas_call(
        _sb_prompt_kernel,
        out_shape=jax.ShapeDtypeStruct((s, h * d), BF16),
        grid=(h, s // tq),
        in_specs=[
            pl.BlockSpec((None, tq, d), lambda hd, qi: (hd, qi, 0)),
            pl.BlockSpec((None, s, d), lambda hd, qi: (hd, 0, 0)),
            pl.BlockSpec((None, s, d), lambda hd, qi: (hd, 0, 0)),
        ],
        out_specs=pl.BlockSpec((tq, d), lambda hd, qi: (qi, hd)),
        compiler_params=_cparams(("arbitrary", "arbitrary")),
        name="sb_prompt",
    )(q16, k16, v16)


def _sb_sample_kernel(q_ref, kn_ref, vn_ref, kc_ref, vc_ref, o_ref, *, past):
    blk = SB_BLOCK
    t = q_ref.shape[0]
    q = q_ref[...]
    r = lax.broadcasted_iota(jnp.int32, (t, t), 0)
    c = lax.broadcasted_iota(jnp.int32, (t, t), 1)
    acc, carry = _sb_block(q, kn_ref[...], vn_ref[...], jnp.zeros((t, 1), F32),
                           _suffix_matrix(t), c < r)
    tri = _suffix_matrix(blk)
    for b in range(past // blk - 1, -1, -1):
        k = kc_ref[b * blk:(b + 1) * blk, :].astype(BF16)
        v = vc_ref[b * blk:(b + 1) * blk, :].astype(BF16)
        d, carry = _sb_block(q, k, v, carry, tri, None)
        acc = acc + d
    o_ref[...] = acc.astype(o_ref.dtype)


def _sb_sample(q16, k16, v16, cache_k, cache_v, *, t):
    h, m, d = q16.shape
    nb = m // t
    past = cache_k.shape[2]
    new_spec = pl.BlockSpec((None, t, d), lambda b, hd: (hd, b, 0))
    cache_spec = pl.BlockSpec((None, None, past, d), lambda b, hd: (b, hd, 0, 0))
    return pl.pallas_call(
        functools.partial(_sb_sample_kernel, past=past),
        out_shape=jax.ShapeDtypeStruct((m, h * d), BF16),
        grid=(nb, h),
        in_specs=[new_spec, new_spec, new_spec, cache_spec, cache_spec],
        out_specs=pl.BlockSpec((t, d), lambda b, hd: (b, hd)),
        compiler_params=_cparams(("arbitrary", "arbitrary")),
        name="sb_sample",
    )(q16, k16, v16, cache_k, cache_v)


def _ret_log_decay(hd):
    return math.log(1.0 - 2.0 ** (-5.0 - hd))


def _ret_kernel(q_ref, k_ref, v_ref, g_ref, nw_ref, s0_ref, r_ref, st_ref, *, chunk):
    c = pl.program_id(1)

    @pl.when(c == 0)
    def _():
        st_ref[...] = s0_ref[...]

    row = lax.broadcasted_iota(jnp.int32, (chunk, chunk), 0)
    col = lax.broadcasted_iota(jnp.int32, (chunk, chunk), 1)
    diff = (row - col).astype(F32)
    causal = row >= col
    pos = lax.broadcasted_iota(jnp.int32, (chunk, DK_RET), 0).astype(F32)
    for hd in range(H_RET):
        lg = _ret_log_decay(hd)
        intra = jnp.where(causal, jnp.exp(lg * jnp.maximum(diff, 0.0)), 0.0)
        q_dec = jnp.exp(lg * (pos + 1.0))
        k_dec = jnp.exp(lg * (chunk - 1.0 - pos))
        c_dec = math.exp(lg * chunk)
        q = q_ref[hd]
        k = k_ref[hd]
        v = v_ref[hd]
        state = st_ref[hd]
        scores = (_dot_nt(q, k.astype(BF16)) * intra).astype(BF16)
        o = _dot(scores, v) + _dot(q, state.astype(BF16)) * q_dec
        st_ref[hd] = c_dec * state + _dot_tn((k * k_dec).astype(BF16), v)
        o = o * lax.rsqrt(jnp.mean(o * o, axis=-1, keepdims=True) + EPS)
        sl = slice(hd * DV_RET, (hd + 1) * DV_RET)
        gate = g_ref[:, sl]
        r_ref[:, sl] = (o * nw_ref[:, sl] * (gate * jax.nn.sigmoid(gate))).astype(r_ref.dtype)


def _retention(rq, rk, rv, gate, norm_w, state0, *, chunk):
    h, m, _ = rq.shape
    nb = state0.shape[0]
    nc = m // (nb * chunk)
    qkv_spec = pl.BlockSpec((h, chunk, DK_RET), lambda b, c: (0, b * nc + c, 0))
    st_spec = pl.BlockSpec((None, h, DK_RET, DV_RET), lambda b, c: (b, 0, 0, 0))
    return pl.pallas_call(
        functools.partial(_ret_kernel, chunk=chunk),
        out_shape=(jax.ShapeDtypeStruct((m, h * DV_RET), BF16),
                   jax.ShapeDtypeStruct(state0.shape, F32)),
        grid=(nb, nc),
        in_specs=[qkv_spec, qkv_spec, qkv_spec,
                  pl.BlockSpec((chunk, h * DV_RET), lambda b, c: (b * nc + c, 0)),
                  pl.BlockSpec((1, h * DV_RET), lambda b, c: (0, 0)),
                  st_spec],
        out_specs=(pl.BlockSpec((chunk, h * DV_RET), lambda b, c: (b * nc + c, 0)), st_spec),
        compiler_params=_cparams(("arbitrary", "arbitrary")),
        name="retention",
    )(rq, rk, rv, gate, norm_w, state0)


def _outproj_kernel(x_ref, a_ref, b_ref, wa_ref, wb_ref, o_ref):
    o_ref[...] = x_ref[...] + _dot(a_ref[...], wa_ref[...]) + _dot(b_ref[...], wb_ref[...])


def _outproj(x, a_sb, a_ret, w_out, *, tm):
    m = x.shape[0]
    row = lambda i: (i, 0)
    return pl.pallas_call(
        _outproj_kernel,
        out_shape=jax.ShapeDtypeStruct((m, D_MODEL), F32),
        grid=(m // tm,),
        in_specs=[
            pl.BlockSpec((tm, D_MODEL), row),
            pl.BlockSpec((tm, SB_W), row),
            pl.BlockSpec((tm, RET_W), row),
            pl.BlockSpec((SB_W, D_MODEL), lambda i: (0, 0)),
            pl.BlockSpec((RET_W, D_MODEL), lambda i: (1, 0)),
        ],
        out_specs=pl.BlockSpec((tm, D_MODEL), row),
        compiler_params=_cparams(("arbitrary",)),
        name="outproj",
    )(x, a_sb, a_ret, w_out, w_out)


def _layer(x, weights, sb_fn, state0, *, tm, pos_base, pos_mod, chunk):
    (n1, wg1, wu1, wd1, nmix, w_in, qn, kn, ron, w_out, n2, wg2, wu2, wd2, nf, inv_freq) = weights
    x1 = _ffn(x, n1, wg1, wu1, wd1, tm=tm)
    sq, sk, sk16, sv, sv16, rq, rk, rv, gate = _inproj(
        x1, nmix, w_in, qn, kn, inv_freq, tm=tm, pos_base=pos_base, pos_mod=pos_mod)
    a_sb = sb_fn(sq, sk16, sv16)
    a_ret, state = _retention(rq, rk, rv, gate, ron, state0, chunk=chunk)
    x2 = _outproj(x1, a_sb, a_ret, w_out, tm=tm)
    y = _ffn(x2, n2, wg2, wu2, wd2, nf, tm=tm)
    return y, sk, sv, state


def kernel(x_prompt, x_sample, cache_sb_k, cache_sb_v, state_ret, ffn1_norm, ffn1_w_gate, ffn1_w_up, ffn1_w_down, mix_norm, w_in, sb_q_norm, sb_k_norm, ret_out_norm, w_out, ffn2_norm, ffn2_w_gate, ffn2_w_up, ffn2_w_down, final_norm):
    depth = ffn1_norm.shape[0]
    assert depth == 1
    nb_p, seq, _ = x_prompt.shape
    nb_s, dec_seq, _ = x_sample.shape
    past = cache_sb_k.shape[3]
    assert nb_p == 1

    half = DK_RET // 2
    inv_freq = (ROPE_BASE ** (-jnp.arange(half, dtype=F32) / half)).reshape(1, half)
    l = 0
    weights = (ffn1_norm[l][None], ffn1_w_gate[l].astype(BF16), ffn1_w_up[l].astype(BF16),
               ffn1_w_down[l].astype(BF16), mix_norm[l][None], w_in[l].astype(BF16),
               sb_q_norm[l][None], sb_k_norm[l][None], ret_out_norm[l][None],
               w_out[l].astype(BF16), ffn2_norm[l][None], ffn2_w_gate[l].astype(BF16),
               ffn2_w_up[l].astype(BF16), ffn2_w_down[l].astype(BF16), final_norm[l][None],
               inv_freq)

    xp = x_prompt.reshape(seq, D_MODEL)
    zero_state = jnp.zeros((1, H_RET, DK_RET, DV_RET), F32)
    yp, skp, svp, stp = _layer(xp, weights, _sb_prompt, zero_state,
                               tm=512, pos_base=0, pos_mod=seq, chunk=RET_CHUNK)

    xs = x_sample.reshape(nb_s * dec_seq, D_MODEL)
    sb_s = functools.partial(_sb_sample, cache_k=cache_sb_k[l], cache_v=cache_sb_v[l], t=dec_seq)
    ys, sks, svs, sts = _layer(xs, weights, sb_s, state_ret[l],
                               tm=nb_s * dec_seq, pos_base=past, pos_mod=dec_seq, chunk=dec_seq)

    def cache_layout(t):
        return t.reshape(H_SB, nb_s, dec_seq, D_SB).transpose(1, 0, 2, 3)[None]

    return (yp.reshape(1, seq, D_MODEL), ys.reshape(nb_s, dec_seq, D_MODEL),
            skp[None, None], svp[None, None], stp[None],
            cache_layout(sks), cache_layout(svs), sts[None])
```

```python
import functools
import math

import jax
import jax.numpy as jnp
from jax import lax
from jax.experimental import pallas as pl
from jax.experimental.pallas import tpu as pltpu

F32 = jnp.float32
BF16 = jnp.bfloat16

D_MODEL = 2048
D_FF = 5632
H_SB = 8
D_SB = 128
H_RET = 4
DK_RET = 256
DV_RET = 256
SB_W = H_SB * D_SB
RET_W = H_RET * DK_RET
N_SECTIONS = 7
SECTION_W = 1024
ROPE_BASE = 10000.0
EPS = 1e-6

V7X_VMEM_LIMIT_BYTES = 56 * 1024 * 1024
SB_BLOCK = 256
RET_CHUNK = 256
FFN_TF = 512

LOG2E = 1.4426950408889634
LN2 = 0.6931471805599453


def _cparams(sem):
    return pltpu.CompilerParams(dimension_semantics=sem,
                                vmem_limit_bytes=V7X_VMEM_LIMIT_BYTES)


def _rmsnorm_rows(x, w):
    ms = jnp.mean(x * x, axis=-1, keepdims=True)
    return x * lax.rsqrt(ms + EPS) * w


def _dot(a, b):
    return jnp.dot(a, b, preferred_element_type=F32)


def _dot_nt(a, b):
    return lax.dot_general(a, b, (((1,), (1,)), ((), ())), preferred_element_type=F32)


def _dot_tn(a, b):
    return lax.dot_general(a, b, (((0,), (0,)), ((), ())), preferred_element_type=F32)


def _ffn_kernel(*refs, n_steps, final_norm):
    if final_norm:
        x_ref, nw_ref, wg_ref, wu_ref, wd_ref, fw_ref, o_ref, h_ref = refs
    else:
        x_ref, nw_ref, wg_ref, wu_ref, wd_ref, o_ref, h_ref = refs
        fw_ref = None
    j = pl.program_id(1)

    @pl.when(j == 0)
    def _():
        x = x_ref[...]
        h_ref[...] = _rmsnorm_rows(x, nw_ref[...]).astype(BF16)
        o_ref[...] = x

    h = h_ref[...]
    g = _dot(h, wg_ref[...])
    u = _dot(h, wu_ref[...])
    a = (g * jax.nn.sigmoid(g) * (0.5 * u)).astype(BF16)
    o_ref[...] += _dot(a, wd_ref[...])

    if final_norm:
        @pl.when(j == n_steps - 1)
        def _():
            o_ref[...] = _rmsnorm_rows(o_ref[...], fw_ref[...])


def _ffn(x, norm_w, wg, wu, wd, final_w=None, *, tm):
    m = x.shape[0]
    n_steps = D_FF // FFN_TF
    in_specs = [
        pl.BlockSpec((tm, D_MODEL), lambda i, j: (i, 0)),
        pl.BlockSpec((1, D_MODEL), lambda i, j: (0, 0)),
        pl.BlockSpec((D_MODEL, FFN_TF), lambda i, j: (0, j)),
        pl.BlockSpec((D_MODEL, FFN_TF), lambda i, j: (0, j)),
        pl.BlockSpec((FFN_TF, D_MODEL), lambda i, j: (j, 0)),
    ]
    args = [x, norm_w, wg, wu, wd]
    if final_w is not None:
        in_specs.append(pl.BlockSpec((1, D_MODEL), lambda i, j: (0, 0)))
        args.append(final_w)
    return pl.pallas_call(
        functools.partial(_ffn_kernel, n_steps=n_steps, final_norm=final_w is not None),
        out_shape=jax.ShapeDtypeStruct((m, D_MODEL), F32),
        grid=(m // tm, n_steps),
        in_specs=in_specs,
        out_specs=pl.BlockSpec((tm, D_MODEL), lambda i, j: (i, 0)),
        scratch_shapes=[pltpu.VMEM((tm, D_MODEL), BF16)],
        compiler_params=_cparams(("arbitrary", "arbitrary")),
        name="ffn_final" if final_w is not None else "ffn",
    )(*args)


def _inproj_kernel(x_ref, nw_ref, w_ref, qn_ref, kn_ref, inv_ref,
                   sq_ref, sk_ref, sk16_ref, sv_ref, sv16_ref,
                   rq_ref, rk_ref, rv_ref, g_ref,
                   h_ref, cos_ref, sin_ref, cos_row_ref, sin_row_ref, *, tm, tile_stride, pos_base, pos_mod):
    i = pl.program_id(0)
    j = pl.program_id(1)

    @pl.when(jnp.logical_and(i == 0, j == 0))
    def _():
        row = lax.broadcasted_iota(jnp.int32, (tm, DK_RET // 2), 0)
        ang = lax.rem(row, pos_mod).astype(F32) * inv_ref[...]
        cos_row_ref[...] = jnp.cos(ang)
        sin_row_ref[...] = jnp.sin(ang)

    @pl.when(j == 0)
    def _():
        h_ref[...] = _rmsnorm_rows(x_ref[...], nw_ref[...]).astype(BF16)
        ang = (pos_base + i * tile_stride).astype(F32) * inv_ref[...]
        ca, sa = jnp.cos(ang), jnp.sin(ang)
        cb, sb = cos_row_ref[...], sin_row_ref[...]
        cos_ref[...] = ca * cb - sa * sb
        sin_ref[...] = sa * cb + ca * sb

    def proj():
        return _dot(h_ref[...], w_ref[...])

    def head(p, hd, width):
        return p[:, hd * width:(hd + 1) * width]

    @pl.when(j == 0)
    def _():
        p = proj()
        w = qn_ref[...] * (D_SB ** -0.5 * LOG2E)
        for hd in range(H_SB):
            sq_ref[hd] = _rmsnorm_rows(head(p, hd, D_SB), w).astype(BF16)

    @pl.when(j == 1)
    def _():
        p = proj()
        for hd in range(H_SB):
            k = _rmsnorm_rows(head(p, hd, D_SB), kn_ref[...])
            sk_ref[hd] = k
            sk16_ref[hd] = k.astype(BF16)

    @pl.when(j == 2)
    def _():
        p = proj()
        for hd in range(H_SB):
            v = head(p, hd, D_SB)
            sv_ref[hd] = v
            sv16_ref[hd] = v.astype(BF16)

    def rotary(ph):
        half = DK_RET // 2
        x1, x2 = ph[:, :half], ph[:, half:]
        c, s = cos_ref[...], sin_ref[...]
        return jnp.concatenate([x1 * c - x2 * s, x1 * s + x2 * c], axis=-1)

    @pl.when(j == 3)
    def _():
        p = proj()
        for hd in range(H_RET):
            rq_ref[hd] = rotary(head(p, hd, DK_RET)).astype(BF16)

    @pl.when(j == 4)
    def _():
        p = proj()
        for hd in range(H_RET):
            rk_ref[hd] = rotary(head(p, hd, DK_RET)) * (DK_RET ** -0.5)

    @pl.when(j == 5)
    def _():
        p = proj()
        for hd in range(H_RET):
            rv_ref[hd] = head(p, hd, DV_RET).astype(BF16)

    @pl.when(j == 6)
    def _():
        g_ref[...] = proj()


def _inproj(x, norm_w, w_in, qn, kn, inv_freq, *, tm, pos_base, pos_mod):
    m = x.shape[0]
    head_spec_sb = pl.BlockSpec((H_SB, tm, D_SB), lambda i, j: (0, i, 0))
    head_spec_ret = pl.BlockSpec((H_RET, tm, DK_RET), lambda i, j: (0, i, 0))
    sb16 = jax.ShapeDtypeStruct((H_SB, m, D_SB), BF16)
    sb32 = jax.ShapeDtypeStruct((H_SB, m, D_SB), F32)
    ret16 = jax.ShapeDtypeStruct((H_RET, m, DK_RET), BF16)
    ret32 = jax.ShapeDtypeStruct((H_RET, m, DK_RET), F32)
    assert pos_mod >= m or tm % pos_mod == 0
    tile_stride = tm if pos_mod >= m else 0
    half = DK_RET // 2
    return pl.pallas_call(
        functools.partial(_inproj_kernel, tm=tm, tile_stride=tile_stride,
                          pos_base=pos_base, pos_mod=pos_mod),
        out_shape=(sb16, sb32, sb16, sb32, sb16, ret16, ret32, ret16,
                   jax.ShapeDtypeStruct((m, SECTION_W), F32)),
        grid=(m // tm, N_SECTIONS),
        in_specs=[
            pl.BlockSpec((tm, D_MODEL), lambda i, j: (i, 0)),
            pl.BlockSpec((1, D_MODEL), lambda i, j: (0, 0)),
            pl.BlockSpec((D_MODEL, SECTION_W), lambda i, j: (0, j)),
            pl.BlockSpec((1, D_SB), lambda i, j: (0, 0)),
            pl.BlockSpec((1, D_SB), lambda i, j: (0, 0)),
            pl.BlockSpec((1, DK_RET // 2), lambda i, j: (0, 0)),
        ],
        out_specs=(head_spec_sb, head_spec_sb, head_spec_sb, head_spec_sb, head_spec_sb,
                   head_spec_ret, head_spec_ret, head_spec_ret,
                   pl.BlockSpec((tm, SECTION_W), lambda i, j: (i, 0))),
        scratch_shapes=[pltpu.VMEM((tm, D_MODEL), BF16)] + [pltpu.VMEM((tm, half), F32)] * 4,
        compiler_params=_cparams(("arbitrary", "arbitrary")),
        name="inproj",
    )(x, norm_w, w_in, qn, kn, inv_freq)


def _softplus2(z):
    return jnp.maximum(z, 0.0) + jnp.log2(1.0 + jnp.exp2(-jnp.abs(z)))


def _suffix_matrix(n):
    r = lax.broadcasted_iota(jnp.int32, (2 * n, n), 0)
    c = lax.broadcasted_iota(jnp.int32, (2 * n, n), 1)
    return jnp.where(jnp.where(r >= n, r - n, r) >= c, 1.0, 0.0).astype(BF16)


def _sb_block(q, k, v, carry, tri, mask):
    z = _dot_nt(q, k)
    sp = _softplus2(z)
    if mask is not None:
        sp = jnp.where(mask, sp, 0.0)
    hi = sp.astype(BF16)
    lo = (sp - hi.astype(F32)).astype(BF16)
    n = hi.shape[1]
    if n % 128 == 0:
        incl = _dot(jnp.concatenate([hi, lo], axis=1), tri) + carry
    else:
        incl = _dot(hi, tri[:n]) + _dot(lo, tri[:n]) + carry
    a = jnp.exp2(z - incl)
    if mask is not None:
        a = jnp.where(mask, a, 0.0)
    return _dot(a.astype(BF16), v), incl[:, 0:1]


SB_DEAD_CARRY = 150.0


def _sb_prompt_kernel(q_ref, k_ref, v_ref, o_ref):
    qi = pl.program_id(1)
    blk = SB_BLOCK
    tri = _suffix_matrix(blk)
    r = lax.broadcasted_iota(jnp.int32, (blk, blk), 0)
    c = lax.broadcasted_iota(jnp.int32, (blk, blk), 1)
    strict = c < r
    q = q_ref[...]
    q_a, q_b = q[:blk], q[blk:]

    def kv(block):
        s0 = pl.multiple_of(block * blk, blk)
        return k_ref[pl.ds(s0, blk), :], v_ref[pl.ds(s0, blk), :]

    zero = jnp.zeros((blk, 1), F32)
    k0, v0 = kv(2 * qi)
    k1, v1 = kv(2 * qi + 1)
    acc_a, carry_a = _sb_block(q_a, k0, v0, zero, tri, strict)
    acc_b, carry_b = _sb_block(q_b, k1, v1, zero, tri, strict)
    d_b, carry_b = _sb_block(q_b, k0, v0, carry_b, tri, None)
    acc = jnp.concatenate([acc_a, acc_b + d_b], axis=0)
    carry = jnp.concatenate([carry_a, carry_b], axis=0)

    def alive(carry):
        return (jnp.min(carry) < SB_DEAD_CARRY).astype(jnp.int32)

    def cond(state):
        t, live, _, _ = state
        return jnp.logical_and(t >= 0, live > 0)

    def body(state):
        t, _, acc, carry = state
        kb, vb = kv(t)
        d, carry = _sb_block(q, kb, vb, carry, tri, None)
        return t - 1, alive(carry), acc + d, carry

    _, _, acc, _ = lax.while_loop(cond, body, (2 * qi - 1, alive(carry), acc, carry))
    o_ref[...] = acc.astype(o_ref.dtype)


def _sb_prompt(q16, k16, v16):
    h, s, d = q16.shape
    tq = 2 * SB_BLOCK
    return pl.pallas_call(
        _sb_prompt_kernel,
        out_shape=jax.ShapeDtypeStruct((s, h * d), BF16),
        grid=(h, s // tq),
        in_specs=[
            pl.BlockSpec((None, tq, d), lambda hd, qi: (hd, qi, 0)),
            pl.BlockSpec((None, s, d), lambda hd, qi: (hd, 0, 0)),
            pl.BlockSpec((None, s, d), lambda hd, qi: (hd, 0, 0)),
        ],
        out_specs=pl.BlockSpec((tq, d), lambda hd, qi: (qi, hd)),
        compiler_params=_cparams(("arbitrary", "arbitrary")),
        name="sb_prompt",
    )(q16, k16, v16)


def _sb_sample_kernel(q_ref, kn_ref, vn_ref, kc_ref, vc_ref, o_ref, *, past):
    nh, t, d = q_ref.shape
    ht = nh * t
    blk = SB_BLOCK
    q_all = q_ref[...].reshape(ht, d).astype(F32)
    q_head = lax.broadcasted_iota(jnp.int32, (ht, d), 0) // t
    q_masked = [jnp.where(q_head == h, q_all, 0.0).astype(BF16) for h in range(nh)]

    def logits(keys_of_head):
        z = _dot_nt(keys_of_head(0), q_masked[0])
        for h in range(1, nh):
            z = z + _dot_nt(keys_of_head(h), q_masked[h])
        return z

    def suffix_matrix(n):
        r = lax.broadcasted_iota(jnp.int32, (n, 2 * n), 0)
        c = lax.broadcasted_iota(jnp.int32, (n, 2 * n), 1)
        return jnp.where(jnp.where(c >= n, c - n, c) >= r, 1.0, 0.0).astype(BF16)

    def suffix_sum(sp, lmat):
        hi = sp.astype(BF16)
        lo = (sp - hi.astype(F32)).astype(BF16)
        return _dot(lmat, jnp.concatenate([hi, lo], axis=0))

    def emit(a_t, values_of_head, acc):
        a = a_t.T.astype(BF16)
        return [acc[h] + _dot(a[h * t:(h + 1) * t, :], values_of_head(h)) for h in range(nh)]

    z = logits(lambda h: kn_ref[h])
    s_idx = lax.broadcasted_iota(jnp.int32, (t, ht), 0)
    t_idx = lax.broadcasted_iota(jnp.int32, (t, ht), 1) % t
    strict = s_idx < t_idx
    sp = jnp.where(strict, _softplus2(z), 0.0)
    incl = suffix_sum(sp, suffix_matrix(t))
    a_t = jnp.where(strict, jnp.exp2(z - incl), 0.0)
    acc = emit(a_t, lambda h: vn_ref[h], [jnp.zeros((t, d), F32) for _ in range(nh)])
    carry = incl[0:1, :]

    lmat = suffix_matrix(blk)
    for b in range(past // blk - 1, -1, -1):
        rows = slice(b * blk, (b + 1) * blk)
        z = logits(lambda h: kc_ref[h, rows, :].astype(BF16))
        incl = suffix_sum(_softplus2(z), lmat) + carry
        acc = emit(jnp.exp2(z - incl), lambda h: vc_ref[h, rows, :].astype(BF16), acc)
        carry = incl[0:1, :]

    for h in range(nh):
        o_ref[:, h * d:(h + 1) * d] = acc[h].astype(o_ref.dtype)


def _sb_sample(q16, k16, v16, cache_k, cache_v, *, t):
    h, m, d = q16.shape
    nb = m // t
    past = cache_k.shape[2]
    new_spec = pl.BlockSpec((h, t, d), lambda b: (0, b, 0))
    cache_spec = pl.BlockSpec((None, h, past, d), lambda b: (b, 0, 0, 0))
    return pl.pallas_call(
        functools.partial(_sb_sample_kernel, past=past),
        out_shape=jax.ShapeDtypeStruct((m, h * d), BF16),
        grid=(nb,),
        in_specs=[new_spec, new_spec, new_spec, cache_spec, cache_spec],
        out_specs=pl.BlockSpec((t, h * d), lambda b: (b, 0)),
        compiler_params=_cparams(("arbitrary",)),
        name="sb_sample",
    )(q16, k16, v16, cache_k, cache_v)


def _ret_log_decay(hd):
    return math.log(1.0 - 2.0 ** (-5.0 - hd))


def _ret_kernel(q_ref, k_ref, v_ref, g_ref, nw_ref, s0_ref, r_ref, st_ref, *, chunk):
    c = pl.program_id(1)

    @pl.when(c == 0)
    def _():
        st_ref[...] = s0_ref[...]

    row = lax.broadcasted_iota(jnp.int32, (chunk, chunk), 0)
    col = lax.broadcasted_iota(jnp.int32, (chunk, chunk), 1)
    diff = (row - col).astype(F32)
    causal = row >= col
    pos = lax.broadcasted_iota(jnp.int32, (chunk, DK_RET), 0).astype(F32)
    for hd in range(H_RET):
        lg = _ret_log_decay(hd)
        intra = jnp.where(causal, jnp.exp(lg * jnp.maximum(diff, 0.0)), 0.0)
        q_dec = jnp.exp(lg * (pos + 1.0))
        k_dec = jnp.exp(lg * (chunk - 1.0 - pos))
        c_dec = math.exp(lg * chunk)
        q = q_ref[hd]
        k = k_ref[hd]
        v = v_ref[hd]
        state = st_ref[hd]
        scores = (_dot_nt(q, k.astype(BF16)) * intra).astype(BF16)
        o = _dot(scores, v) + _dot(q, state.astype(BF16)) * q_dec
        st_ref[hd] = c_dec * state + _dot_tn((k * k_dec).astype(BF16), v)
        o = o * lax.rsqrt(jnp.mean(o * o, axis=-1, keepdims=True) + EPS)
        sl = slice(hd * DV_RET, (hd + 1) * DV_RET)
        gate = g_ref[:, sl]
        r_ref[:, sl] = (o * nw_ref[:, sl] * (gate * jax.nn.sigmoid(gate))).astype(r_ref.dtype)


def _retention(rq, rk, rv, gate, norm_w, state0, *, chunk):
    h, m, _ = rq.shape
    nb = state0.shape[0]
    nc = m // (nb * chunk)
    qkv_spec = pl.BlockSpec((h, chunk, DK_RET), lambda b, c: (0, b * nc + c, 0))
    st_spec = pl.BlockSpec((None, h, DK_RET, DV_RET), lambda b, c: (b, 0, 0, 0))
    return pl.pallas_call(
        functools.partial(_ret_kernel, chunk=chunk),
        out_shape=(jax.ShapeDtypeStruct((m, h * DV_RET), BF16),
                   jax.ShapeDtypeStruct(state0.shape, F32)),
        grid=(nb, nc),
        in_specs=[qkv_spec, qkv_spec, qkv_spec,
                  pl.BlockSpec((chunk, h * DV_RET), lambda b, c: (b * nc + c, 0)),
                  pl.BlockSpec((1, h * DV_RET), lambda b, c: (0, 0)),
                  st_spec],
        out_specs=(pl.BlockSpec((chunk, h * DV_RET), lambda b, c: (b * nc + c, 0)), st_spec),
        compiler_params=_cparams(("arbitrary", "arbitrary")),
        name="retention",
    )(rq, rk, rv, gate, norm_w, state0)


def _outproj_kernel(x_ref, a_ref, b_ref, wa_ref, wb_ref, o_ref):
    o_ref[...] = x_ref[...] + _dot(a_ref[...], wa_ref[...]) + _dot(b_ref[...], wb_ref[...])


def _outproj(x, a_sb, a_ret, w_out, *, tm):
    m = x.shape[0]
    row = lambda i: (i, 0)
    return pl.pallas_call(
        _outproj_kernel,
        out_shape=jax.ShapeDtypeStruct((m, D_MODEL), F32),
        grid=(m // tm,),
        in_specs=[
            pl.BlockSpec((tm, D_MODEL), row),
            pl.BlockSpec((tm, SB_W), row),
            pl.BlockSpec((tm, RET_W), row),
            pl.BlockSpec((SB_W, D_MODEL), lambda i: (0, 0)),
            pl.BlockSpec((RET_W, D_MODEL), lambda i: (1, 0)),
        ],
        out_specs=pl.BlockSpec((tm, D_MODEL), row),
        compiler_params=_cparams(("arbitrary",)),
        name="outproj",
    )(x, a_sb, a_ret, w_out, w_out)


def _layer(x, weights, sb_fn, state0, *, tm, ffn_tm, pos_base, pos_mod, chunk):
    (n1, wg1, wu1, wd1, nmix, w_in, qn, kn, ron, w_out, n2, wg2, wu2, wd2, nf, inv_freq) = weights
    x1 = _ffn(x, n1, wg1, wu1, wd1, tm=ffn_tm)
    sq, sk, sk16, sv, sv16, rq, rk, rv, gate = _inproj(
        x1, nmix, w_in, qn, kn, inv_freq, tm=tm, pos_base=pos_base, pos_mod=pos_mod)
    a_sb = sb_fn(sq, sk16, sv16)
    a_ret, state = _retention(rq, rk, rv, gate, ron, state0, chunk=chunk)
    x2 = _outproj(x1, a_sb, a_ret, w_out, tm=tm)
    y = _ffn(x2, n2, wg2, wu2, wd2, nf, tm=ffn_tm)
    return y, sk, sv, state


def kernel(x_prompt, x_sample, cache_sb_k, cache_sb_v, state_ret, ffn1_norm, ffn1_w_gate, ffn1_w_up, ffn1_w_down, mix_norm, w_in, sb_q_norm, sb_k_norm, ret_out_norm, w_out, ffn2_norm, ffn2_w_gate, ffn2_w_up, ffn2_w_down, final_norm):
    depth = ffn1_norm.shape[0]
    assert depth == 1
    nb_p, seq, _ = x_prompt.shape
    nb_s, dec_seq, _ = x_sample.shape
    past = cache_sb_k.shape[3]
    assert nb_p == 1

    half = DK_RET // 2
    inv_freq = (ROPE_BASE ** (-jnp.arange(half, dtype=F32) / half)).reshape(1, half)
    l = 0
    weights = (ffn1_norm[l][None], ffn1_w_gate[l].astype(BF16), ffn1_w_up[l].astype(BF16),
               ffn1_w_down[l].astype(BF16), mix_norm[l][None], w_in[l].astype(BF16),
               sb_q_norm[l][None], sb_k_norm[l][None], ret_out_norm[l][None],
               w_out[l].astype(BF16), ffn2_norm[l][None], ffn2_w_gate[l].astype(BF16),
               ffn2_w_up[l].astype(BF16), ffn2_w_down[l].astype(BF16), final_norm[l][None],
               inv_freq)

    xp = x_prompt.reshape(seq, D_MODEL)
    zero_state = jnp.zeros((1, H_RET, DK_RET, DV_RET), F32)
    yp, skp, svp, stp = _layer(xp, weights, _sb_prompt, zero_state,
                               tm=512, ffn_tm=1024, pos_base=0, pos_mod=seq, chunk=RET_CHUNK)

    xs = x_sample.reshape(nb_s * dec_seq, D_MODEL)
    sb_s = functools.partial(_sb_sample, cache_k=cache_sb_k[l], cache_v=cache_sb_v[l], t=dec_seq)
    ys, sks, svs, sts = _layer(xs, weights, sb_s, state_ret[l],
                               tm=nb_s * dec_seq, ffn_tm=nb_s * dec_seq, pos_base=past,
                               pos_mod=dec_seq, chunk=dec_seq)

    def cache_layout(t):
        return t.reshape(H_SB, nb_s, dec_seq, D_SB).transpose(1, 0, 2, 3)[None]

    return (yp.reshape(1, seq, D_MODEL), ys.reshape(nb_s, dec_seq, D_MODEL),
            skp[None, None], svp[None, None], stp[None],
            cache_layout(sks), cache_layout(svs), sts[None])
```

```python
import functools
import math

import jax
import jax.numpy as jnp
from jax import lax
from jax.experimental import pallas as pl
from jax.experimental.pallas import tpu as pltpu

F32 = jnp.float32
BF16 = jnp.bfloat16

D_MODEL = 2048
D_FF = 5632
H_SB = 8
D_SB = 128
H_RET = 4
DK_RET = 256
DV_RET = 256
SB_W = H_SB * D_SB
RET_W = H_RET * DK_RET
N_SECTIONS = 7
SECTION_W = 1024
ROPE_BASE = 10000.0
EPS = 1e-6

V7X_VMEM_LIMIT_BYTES = 56 * 1024 * 1024
SB_BLOCK = 256
RET_CHUNK = 256
FFN_TF = 512

LOG2E = 1.4426950408889634
LN2 = 0.6931471805599453


def _cparams(sem):
    return pltpu.CompilerParams(dimension_semantics=sem,
                                vmem_limit_bytes=V7X_VMEM_LIMIT_BYTES)


def _rmsnorm_rows(x, w):
    ms = jnp.mean(x * x, axis=-1, keepdims=True)
    return x * lax.rsqrt(ms + EPS) * w


def _dot(a, b):
    return jnp.dot(a, b, preferred_element_type=F32)


def _dot_nt(a, b):
    return lax.dot_general(a, b, (((1,), (1,)), ((), ())), preferred_element_type=F32)


def _dot_tn(a, b):
    return lax.dot_general(a, b, (((0,), (0,)), ((), ())), preferred_element_type=F32)


def _ffn_kernel(*refs, n_steps, final_norm, emit_bf16):
    x_ref, nw_ref, wg_ref, wu_ref, wd_ref = refs[:5]
    refs = refs[5:]
    fw_ref = None
    if final_norm:
        fw_ref, refs = refs[0], refs[1:]
    o_ref, refs = refs[0], refs[1:]
    if emit_bf16:
        wg16_ref, wu16_ref, wd16_ref, h_ref = refs
        wg16_ref[...] = wg_ref[...].astype(BF16)
        wu16_ref[...] = wu_ref[...].astype(BF16)
        wd16_ref[...] = wd_ref[...].astype(BF16)
        wg_ref, wu_ref, wd_ref = wg16_ref, wu16_ref, wd16_ref
    else:
        (h_ref,) = refs
    j = pl.program_id(1)

    @pl.when(j == 0)
    def _():
        x = x_ref[...]
        h_ref[...] = _rmsnorm_rows(x, nw_ref[...]).astype(BF16)
        o_ref[...] = x

    h = h_ref[...]
    g = _dot(h, wg_ref[...])
    u = _dot(h, wu_ref[...])
    a = (g * jax.nn.sigmoid(g) * (0.5 * u)).astype(BF16)
    o_ref[...] += _dot(a, wd_ref[...])

    if final_norm:
        @pl.when(j == n_steps - 1)
        def _():
            o_ref[...] = _rmsnorm_rows(o_ref[...], fw_ref[...])


def _ffn(x, norm_w, wg, wu, wd, final_w=None, *, tm):
    m = x.shape[0]
    n_steps = D_FF // FFN_TF
    emit_bf16 = wg.dtype == F32
    assert not emit_bf16 or m == tm
    w_specs = [
        pl.BlockSpec((D_MODEL, FFN_TF), lambda i, j: (0, j)),
        pl.BlockSpec((D_MODEL, FFN_TF), lambda i, j: (0, j)),
        pl.BlockSpec((FFN_TF, D_MODEL), lambda i, j: (j, 0)),
    ]
    in_specs = [
        pl.BlockSpec((tm, D_MODEL), lambda i, j: (i, 0)),
        pl.BlockSpec((1, D_MODEL), lambda i, j: (0, 0)),
    ] + w_specs
    args = [x, norm_w, wg, wu, wd]
    if final_w is not None:
        in_specs.append(pl.BlockSpec((1, D_MODEL), lambda i, j: (0, 0)))
        args.append(final_w)
    out_shape = [jax.ShapeDtypeStruct((m, D_MODEL), F32)]
    out_specs = [pl.BlockSpec((tm, D_MODEL), lambda i, j: (i, 0))]
    if emit_bf16:
        out_shape += [jax.ShapeDtypeStruct(w.shape, BF16) for w in (wg, wu, wd)]
        out_specs += w_specs
    out = pl.pallas_call(
        functools.partial(_ffn_kernel, n_steps=n_steps, final_norm=final_w is not None,
                          emit_bf16=emit_bf16),
        out_shape=out_shape,
        grid=(m // tm, n_steps),
        in_specs=in_specs,
        out_specs=out_specs,
        scratch_shapes=[pltpu.VMEM((tm, D_MODEL), BF16)],
        compiler_params=_cparams(("arbitrary", "arbitrary")),
        name="ffn_final" if final_w is not None else "ffn",
    )(*args)
    return (out[0], tuple(out[1:])) if emit_bf16 else out[0]


def _inproj_kernel(x_ref, nw_ref, w_ref, qn_ref, kn_ref, inv_ref,
                   sq_ref, sk_ref, sk16_ref, sv_ref, sv16_ref,
                   rq_ref, rk_ref, rv_ref, g_ref, *rest,
                   tm, tile_stride, pos_base, pos_mod, emit_bf16):
    if emit_bf16:
        w16_ref, h_ref, cos_ref, sin_ref, cos_row_ref, sin_row_ref = rest
        w16_ref[...] = w_ref[...].astype(BF16)
        w_ref = w16_ref
    else:
        h_ref, cos_ref, sin_ref, cos_row_ref, sin_row_ref = rest
    i = pl.program_id(0)
    j = pl.program_id(1)

    @pl.when(jnp.logical_and(i == 0, j == 0))
    def _():
        row = lax.broadcasted_iota(jnp.int32, (tm, DK_RET // 2), 0)
        ang = lax.rem(row, pos_mod).astype(F32) * inv_ref[...]
        cos_row_ref[...] = jnp.cos(ang)
        sin_row_ref[...] = jnp.sin(ang)

    @pl.when(j == 0)
    def _():
        h_ref[...] = _rmsnorm_rows(x_ref[...], nw_ref[...]).astype(BF16)
        ang = (pos_base + i * tile_stride).astype(F32) * inv_ref[...]
        ca, sa = jnp.cos(ang), jnp.sin(ang)
        cb, sb = cos_row_ref[...], sin_row_ref[...]
        cos_ref[...] = ca * cb - sa * sb
        sin_ref[...] = sa * cb + ca * sb

    def proj():
        return _dot(h_ref[...], w_ref[...])

    def head(p, hd, width):
        return p[:, hd * width:(hd + 1) * width]

    @pl.when(j == 0)
    def _():
        p = proj()
        w = qn_ref[...] * (D_SB ** -0.5 * LOG2E)
        for hd in range(H_SB):
            sq_ref[hd] = _rmsnorm_rows(head(p, hd, D_SB), w).astype(BF16)

    @pl.when(j == 1)
    def _():
        p = proj()
        for hd in range(H_SB):
            k = _rmsnorm_rows(head(p, hd, D_SB), kn_ref[...])
            sk_ref[hd] = k
            sk16_ref[hd] = k.astype(BF16)

    @pl.when(j == 2)
    def _():
        p = proj()
        for hd in range(H_SB):
            v = head(p, hd, D_SB)
            sv_ref[hd] = v
            sv16_ref[hd] = v.astype(BF16)

    def rotary(ph):
        half = DK_RET // 2
        x1, x2 = ph[:, :half], ph[:, half:]
        c, s = cos_ref[...], sin_ref[...]
        return jnp.concatenate([x1 * c - x2 * s, x1 * s + x2 * c], axis=-1)

    @pl.when(j == 3)
    def _():
        p = proj()
        for hd in range(H_RET):
            rq_ref[hd] = rotary(head(p, hd, DK_RET)).astype(BF16)

    @pl.when(j == 4)
    def _():
        p = proj()
        for hd in range(H_RET):
            rk_ref[hd] = rotary(head(p, hd, DK_RET)) * (DK_RET ** -0.5)

    @pl.when(j == 5)
    def _():
        p = proj()
        for hd in range(H_RET):
            rv_ref[hd] = head(p, hd, DV_RET).astype(BF16)

    @pl.when(j == 6)
    def _():
        g_ref[...] = proj()


def _inproj(x, norm_w, w_in, qn, kn, inv_freq, *, tm, pos_base, pos_mod):
    m = x.shape[0]
    emit_bf16 = w_in.dtype == F32
    assert not emit_bf16 or m == tm

    def row_tile(section):
        return lambda i, j: jnp.where(j >= section, i, jnp.maximum(i - 1, 0))

    def head_spec(heads, width, section):
        rt = row_tile(section)
        return pl.BlockSpec((heads, tm, width), lambda i, j: (0, rt(i, j), 0))

    gate_rt = row_tile(6)
    sb16 = jax.ShapeDtypeStruct((H_SB, m, D_SB), BF16)
    sb32 = jax.ShapeDtypeStruct((H_SB, m, D_SB), F32)
    ret16 = jax.ShapeDtypeStruct((H_RET, m, DK_RET), BF16)
    ret32 = jax.ShapeDtypeStruct((H_RET, m, DK_RET), F32)
    assert pos_mod >= m or tm % pos_mod == 0
    tile_stride = tm if pos_mod >= m else 0
    half = DK_RET // 2
    w_spec = pl.BlockSpec((D_MODEL, SECTION_W), lambda i, j: (0, j))
    out_shape = [sb16, sb32, sb16, sb32, sb16, ret16, ret32, ret16,
                 jax.ShapeDtypeStruct((m, SECTION_W), F32)]
    out_specs = [head_spec(H_SB, D_SB, 0), head_spec(H_SB, D_SB, 1), head_spec(H_SB, D_SB, 1),
                 head_spec(H_SB, D_SB, 2), head_spec(H_SB, D_SB, 2),
                 head_spec(H_RET, DK_RET, 3), head_spec(H_RET, DK_RET, 4),
                 head_spec(H_RET, DK_RET, 5),
                 pl.BlockSpec((tm, SECTION_W), lambda i, j: (gate_rt(i, j), 0))]
    if emit_bf16:
        out_shape.append(jax.ShapeDtypeStruct(w_in.shape, BF16))
        out_specs.append(w_spec)
    out = pl.pallas_call(
        functools.partial(_inproj_kernel, tm=tm, tile_stride=tile_stride,
                          pos_base=pos_base, pos_mod=pos_mod, emit_bf16=emit_bf16),
        out_shape=out_shape,
        grid=(m // tm, N_SECTIONS),
        in_specs=[
            pl.BlockSpec((tm, D_MODEL), lambda i, j: (i, 0)),
            pl.BlockSpec((1, D_MODEL), lambda i, j: (0, 0)),
            w_spec,
            pl.BlockSpec((1, D_SB), lambda i, j: (0, 0)),
            pl.BlockSpec((1, D_SB), lambda i, j: (0, 0)),
            pl.BlockSpec((1, half), lambda i, j: (0, 0)),
        ],
        out_specs=out_specs,
        scratch_shapes=[pltpu.VMEM((tm, D_MODEL), BF16)] + [pltpu.VMEM((tm, half), F32)] * 4,
        compiler_params=_cparams(("arbitrary", "arbitrary")),
        name="inproj",
    )(x, norm_w, w_in, qn, kn, inv_freq)
    return (tuple(out[:9]), out[9]) if emit_bf16 else tuple(out)


def _softplus2(z):
    return jnp.maximum(z, 0.0) + jnp.log2(1.0 + jnp.exp2(-jnp.abs(z)))


def _suffix_matrix(n):
    r = lax.broadcasted_iota(jnp.int32, (2 * n, n), 0)
    c = lax.broadcasted_iota(jnp.int32, (2 * n, n), 1)
    return jnp.where(jnp.where(r >= n, r - n, r) >= c, 1.0, 0.0).astype(BF16)


def _sb_block(q, k, v, carry, tri, mask):
    z = _dot_nt(q, k)
    sp = _softplus2(z)
    if mask is not None:
        sp = jnp.where(mask, sp, 0.0)
    hi = sp.astype(BF16)
    lo = (sp - hi.astype(F32)).astype(BF16)
    n = hi.shape[1]
    if n % 128 == 0:
        incl = _dot(jnp.concatenate([hi, lo], axis=1), tri) + carry
    else:
        incl = _dot(hi, tri[:n]) + _dot(lo, tri[:n]) + carry
    a = jnp.exp2(z - incl)
    if mask is not None:
        a = jnp.where(mask, a, 0.0)
    return _dot(a.astype(BF16), v), incl[:, 0:1]


SB_DEAD_CARRY = 150.0


SB_HEADS_PER_STEP = 2


def _sb_prompt_kernel(q_ref, k_ref, v_ref, o_ref):
    qi = pl.program_id(1)
    heads, _, d = q_ref.shape
    blk = SB_BLOCK
    tri = _suffix_matrix(blk)
    r = lax.broadcasted_iota(jnp.int32, (blk, blk), 0)
    c = lax.broadcasted_iota(jnp.int32, (blk, blk), 1)
    strict = c < r
    zero = jnp.zeros((blk, 1), F32)
    has_prev = qi > 0

    def kv(g, block):
        s0 = pl.multiple_of(block * blk, blk)
        return k_ref[g, pl.ds(s0, blk), :], v_ref[g, pl.ds(s0, blk), :]

    def start(g):
        q = q_ref[g]
        q_a, q_b = q[:blk], q[blk:]
        k0, v0 = kv(g, 2 * qi)
        k1, v1 = kv(g, 2 * qi + 1)
        acc_a, carry_a = _sb_block(q_a, k0, v0, zero, tri, strict)
        acc_b, carry_b = _sb_block(q_b, k1, v1, zero, tri, strict)
        d_b, carry_b = _sb_block(q_b, k0, v0, carry_b, tri, None)
        acc = jnp.concatenate([acc_a, acc_b + d_b], axis=0)
        carry = jnp.concatenate([carry_a, carry_b], axis=0)
        kp, vp = kv(g, jnp.maximum(2 * qi - 1, 0))
        d_p, carry_p = _sb_block(q, kp, vp, carry, tri, None)
        return q, acc + jnp.where(has_prev, d_p, 0.0), jnp.where(has_prev, carry_p, carry)

    qs, accs, carries = zip(*[start(g) for g in range(heads)])

    def alive(carries):
        low = functools.reduce(jnp.minimum, carries)
        return (jnp.min(low) < SB_DEAD_CARRY).astype(jnp.int32)

    def cond(state):
        t, live, _, _ = state
        return jnp.logical_and(t >= 0, live > 0)

    def body(state):
        t, _, accs, carries = state
        new_accs, new_carries = [], []
        for g in range(heads):
            kb, vb = kv(g, t)
            d_g, carry_g = _sb_block(qs[g], kb, vb, carries[g], tri, None)
            new_accs.append(accs[g] + d_g)
            new_carries.append(carry_g)
        return t - 1, alive(new_carries), tuple(new_accs), tuple(new_carries)

    _, _, accs, _ = lax.while_loop(cond, body, (2 * qi - 2, alive(carries), accs, carries))
    for g in range(heads):
        o_ref[:, g * d:(g + 1) * d] = accs[g].astype(o_ref.dtype)


def _sb_prompt(q16, k16, v16):
    h, s, d = q16.shape
    tq = 2 * SB_BLOCK
    g = SB_HEADS_PER_STEP
    return pl.pallas_call(
        _sb_prompt_kernel,
        out_shape=jax.ShapeDtypeStruct((s, h * d), BF16),
        grid=(h // g, s // tq),
        in_specs=[
            pl.BlockSpec((g, tq, d), lambda hg, qi: (hg, qi, 0)),
            pl.BlockSpec((g, s, d), lambda hg, qi: (hg, 0, 0)),
            pl.BlockSpec((g, s, d), lambda hg, qi: (hg, 0, 0)),
        ],
        out_specs=pl.BlockSpec((tq, g * d), lambda hg, qi: (qi, hg)),
        compiler_params=_cparams(("arbitrary", "arbitrary")),
        name="sb_prompt",
    )(q16, k16, v16)


def _sb_sample_kernel(q_ref, kn_ref, vn_ref, kc_ref, vc_ref, o_ref, *, past):
    nh, t, d = q_ref.shape
    ht = nh * t
    blk = SB_BLOCK
    q_all = q_ref[...].reshape(ht, d).astype(F32)
    q_head = lax.broadcasted_iota(jnp.int32, (ht, d), 0) // t
    q_masked = [jnp.where(q_head == h, q_all, 0.0).astype(BF16) for h in range(nh)]

    def logits(keys_of_head):
        z = _dot_nt(keys_of_head(0), q_masked[0])
        for h in range(1, nh):
            z = z + _dot_nt(keys_of_head(h), q_masked[h])
        return z

    def suffix_matrix(n):
        r = lax.broadcasted_iota(jnp.int32, (n, 2 * n), 0)
        c = lax.broadcasted_iota(jnp.int32, (n, 2 * n), 1)
        return jnp.where(jnp.where(c >= n, c - n, c) >= r, 1.0, 0.0).astype(BF16)

    def suffix_sum(sp, lmat):
        hi = sp.astype(BF16)
        lo = (sp - hi.astype(F32)).astype(BF16)
        return _dot(lmat, jnp.concatenate([hi, lo], axis=0))

    def emit(a_t, values_of_head, acc):
        a = a_t.T.astype(BF16)
        return [acc[h] + _dot(a[h * t:(h + 1) * t, :], values_of_head(h)) for h in range(nh)]

    z = logits(lambda h: kn_ref[h])
    s_idx = lax.broadcasted_iota(jnp.int32, (t, ht), 0)
    t_idx = lax.broadcasted_iota(jnp.int32, (t, ht), 1) % t
    strict = s_idx < t_idx
    sp = jnp.where(strict, _softplus2(z), 0.0)
    incl = suffix_sum(sp, suffix_matrix(t))
    a_t = jnp.where(strict, jnp.exp2(z - incl), 0.0)
    acc = emit(a_t, lambda h: vn_ref[h], [jnp.zeros((t, d), F32) for _ in range(nh)])
    carry = incl[0:1, :]

    lmat = suffix_matrix(blk)
    for b in range(past // blk - 1, -1, -1):
        rows = slice(b * blk, (b + 1) * blk)
        z = logits(lambda h: kc_ref[h, rows, :].astype(BF16))
        incl = suffix_sum(_softplus2(z), lmat) + carry
        acc = emit(jnp.exp2(z - incl), lambda h: vc_ref[h, rows, :].astype(BF16), acc)
        carry = incl[0:1, :]

    for h in range(nh):
        o_ref[:, h * d:(h + 1) * d] = acc[h].astype(o_ref.dtype)


def _sb_sample(q16, k16, v16, cache_k, cache_v, *, t):
    h, m, d = q16.shape
    nb = m // t
    past = cache_k.shape[2]
    new_spec = pl.BlockSpec((h, t, d), lambda b: (0, b, 0))
    cache_spec = pl.BlockSpec((None, h, past, d), lambda b: (b, 0, 0, 0))
    return pl.pallas_call(
        functools.partial(_sb_sample_kernel, past=past),
        out_shape=jax.ShapeDtypeStruct((m, h * d), BF16),
        grid=(nb,),
        in_specs=[new_spec, new_spec, new_spec, cache_spec, cache_spec],
        out_specs=pl.BlockSpec((t, h * d), lambda b: (b, 0)),
        compiler_params=_cparams(("arbitrary",)),
        name="sb_sample",
    )(q16, k16, v16, cache_k, cache_v)


def _ret_log_decay(hd):
    return math.log(1.0 - 2.0 ** (-5.0 - hd))


def _ret_kernel(q_ref, k_ref, v_ref, g_ref, nw_ref, s0_ref, r_ref, st_ref, *, chunk):
    c = pl.program_id(1)

    @pl.when(c == 0)
    def _():
        st_ref[...] = s0_ref[...]

    row = lax.broadcasted_iota(jnp.int32, (chunk, chunk), 0)
    col = lax.broadcasted_iota(jnp.int32, (chunk, chunk), 1)
    diff = (row - col).astype(F32)
    causal = row >= col
    pos = lax.broadcasted_iota(jnp.int32, (chunk, DK_RET), 0).astype(F32)
    for hd in range(H_RET):
        lg = _ret_log_decay(hd)
        intra = jnp.where(causal, jnp.exp(lg * jnp.maximum(diff, 0.0)), 0.0)
        q_dec = jnp.exp(lg * (pos + 1.0))
        k_dec = jnp.exp(lg * (chunk - 1.0 - pos))
        c_dec = math.exp(lg * chunk)
        q = q_ref[hd]
        k = k_ref[hd]
        v = v_ref[hd]
        state = st_ref[hd]
        scores = (_dot_nt(q, k.astype(BF16)) * intra).astype(BF16)
        o = _dot(scores, v) + _dot(q, state.astype(BF16)) * q_dec
        st_ref[hd] = c_dec * state + _dot_tn((k * k_dec).astype(BF16), v)
        o = o * lax.rsqrt(jnp.mean(o * o, axis=-1, keepdims=True) + EPS)
        sl = slice(hd * DV_RET, (hd + 1) * DV_RET)
        gate = g_ref[:, sl]
        r_ref[:, sl] = (o * nw_ref[:, sl] * (gate * jax.nn.sigmoid(gate))).astype(r_ref.dtype)


def _retention(rq, rk, rv, gate, norm_w, state0, *, chunk):
    h, m, _ = rq.shape
    nb = state0.shape[0]
    nc = m // (nb * chunk)
    qkv_spec = pl.BlockSpec((h, chunk, DK_RET), lambda b, c: (0, b * nc + c, 0))
    st_spec = pl.BlockSpec((None, h, DK_RET, DV_RET), lambda b, c: (b, 0, 0, 0))
    return pl.pallas_call(
        functools.partial(_ret_kernel, chunk=chunk),
        out_shape=(jax.ShapeDtypeStruct((m, h * DV_RET), BF16),
                   jax.ShapeDtypeStruct(state0.shape, F32)),
        grid=(nb, nc),
        in_specs=[qkv_spec, qkv_spec, qkv_spec,
                  pl.BlockSpec((chunk, h * DV_RET), lambda b, c: (b * nc + c, 0)),
                  pl.BlockSpec((1, h * DV_RET), lambda b, c: (0, 0)),
                  st_spec],
        out_specs=(pl.BlockSpec((chunk, h * DV_RET), lambda b, c: (b * nc + c, 0)), st_spec),
        compiler_params=_cparams(("arbitrary", "arbitrary")),
        name="retention",
    )(rq, rk, rv, gate, norm_w, state0)


def _outproj_kernel(x_ref, a_ref, b_ref, wa_ref, wb_ref, o_ref):
    o_ref[...] = x_ref[...] + _dot(a_ref[...], wa_ref[...]) + _dot(b_ref[...], wb_ref[...])


def _outproj(x, a_sb, a_ret, w_out, *, tm):
    m = x.shape[0]
    row = lambda i: (i, 0)
    return pl.pallas_call(
        _outproj_kernel,
        out_shape=jax.ShapeDtypeStruct((m, D_MODEL), F32),
        grid=(m // tm,),
        in_specs=[
            pl.BlockSpec((tm, D_MODEL), row),
            pl.BlockSpec((tm, SB_W), row),
            pl.BlockSpec((tm, RET_W), row),
            pl.BlockSpec((SB_W, D_MODEL), lambda i: (0, 0)),
            pl.BlockSpec((RET_W, D_MODEL), lambda i: (1, 0)),
        ],
        out_specs=pl.BlockSpec((tm, D_MODEL), row),
        compiler_params=_cparams(("arbitrary",)),
        name="outproj",
    )(x, a_sb, a_ret, w_out, w_out)


OUTPROJ_TK = 512


def _outproj_f32w_kernel(x_ref, a_ref, w_ref, o_ref, w16_ref):
    @pl.when(pl.program_id(0) == 0)
    def _():
        o_ref[...] = x_ref[...]

    w16_ref[...] = w_ref[...].astype(BF16)
    o_ref[...] += _dot(a_ref[...], w16_ref[...])


def _outproj_f32w(x, a, w_out):
    m = x.shape[0]
    d_mix = w_out.shape[0]
    return pl.pallas_call(
        _outproj_f32w_kernel,
        out_shape=(jax.ShapeDtypeStruct((m, D_MODEL), F32),
                   jax.ShapeDtypeStruct(w_out.shape, BF16)),
        grid=(d_mix // OUTPROJ_TK,),
        in_specs=[
            pl.BlockSpec((m, D_MODEL), lambda k: (0, 0)),
            pl.BlockSpec((m, OUTPROJ_TK), lambda k: (0, k)),
            pl.BlockSpec((OUTPROJ_TK, D_MODEL), lambda k: (k, 0)),
        ],
        out_specs=(pl.BlockSpec((m, D_MODEL), lambda k: (0, 0)),
                   pl.BlockSpec((OUTPROJ_TK, D_MODEL), lambda k: (k, 0))),
        compiler_params=_cparams(("arbitrary",)),
        name="outproj_f32w",
    )(x, a, w_out)


def _layer(x, weights, sb_fn, state0, *, tm, ffn_tm, pos_base, pos_mod, chunk):
    (n1, wg1, wu1, wd1, nmix, w_in, qn, kn, ron, w_out, n2, wg2, wu2, wd2, nf, inv_freq) = weights
    f32w = w_in.dtype == F32
    x1 = _ffn(x, n1, wg1, wu1, wd1, tm=ffn_tm)
    if f32w:
        x1, (wg1, wu1, wd1) = x1
    proj = _inproj(x1, nmix, w_in, qn, kn, inv_freq, tm=tm, pos_base=pos_base, pos_mod=pos_mod)
    if f32w:
        proj, w_in = proj
    sq, sk, sk16, sv, sv16, rq, rk, rv, gate = proj
    a_sb = sb_fn(sq, sk16, sv16)
    a_ret, state = _retention(rq, rk, rv, gate, ron, state0, chunk=chunk)
    if f32w:
        x2, w_out = _outproj_f32w(x1, jnp.concatenate([a_sb, a_ret], axis=1), w_out)
    else:
        x2 = _outproj(x1, a_sb, a_ret, w_out, tm=tm)
    y = _ffn(x2, n2, wg2, wu2, wd2, nf, tm=ffn_tm)
    if f32w:
        y, (wg2, wu2, wd2) = y
    weights16 = (n1, wg1, wu1, wd1, nmix, w_in, qn, kn, ron, w_out, n2, wg2, wu2, wd2, nf, inv_freq)
    return y, sk, sv, state, weights16


def kernel(x_prompt, x_sample, cache_sb_k, cache_sb_v, state_ret, ffn1_norm, ffn1_w_gate, ffn1_w_up, ffn1_w_down, mix_norm, w_in, sb_q_norm, sb_k_norm, ret_out_norm, w_out, ffn2_norm, ffn2_w_gate, ffn2_w_up, ffn2_w_down, final_norm):
    depth = ffn1_norm.shape[0]
    assert depth == 1
    nb_p, seq, _ = x_prompt.shape
    nb_s, dec_seq, _ = x_sample.shape
    past = cache_sb_k.shape[3]
    assert nb_p == 1

    half = DK_RET // 2
    inv_freq = (ROPE_BASE ** (-jnp.arange(half, dtype=F32) / half)).reshape(1, half)
    l = 0
    weights = (ffn1_norm[l][None], ffn1_w_gate[l], ffn1_w_up[l], ffn1_w_down[l],
               mix_norm[l][None], w_in[l], sb_q_norm[l][None], sb_k_norm[l][None],
               ret_out_norm[l][None], w_out[l], ffn2_norm[l][None], ffn2_w_gate[l],
               ffn2_w_up[l], ffn2_w_down[l], final_norm[l][None], inv_freq)

    xs = x_sample.reshape(nb_s * dec_seq, D_MODEL)
    sb_s = functools.partial(_sb_sample, cache_k=cache_sb_k[l], cache_v=cache_sb_v[l], t=dec_seq)
    ys, sks, svs, sts, weights16 = _layer(xs, weights, sb_s, state_ret[l],
                                          tm=nb_s * dec_seq, ffn_tm=nb_s * dec_seq, pos_base=past,
                                          pos_mod=dec_seq, chunk=dec_seq)

    xp = x_prompt.reshape(seq, D_MODEL)
    zero_state = jnp.zeros((1, H_RET, DK_RET, DV_RET), F32)
    yp, skp, svp, stp, _ = _layer(xp, weights16, _sb_prompt, zero_state,
                                  tm=512, ffn_tm=1024, pos_base=0, pos_mod=seq, chunk=RET_CHUNK)

    def cache_layout(t):
        return t.reshape(H_SB, nb_s, dec_seq, D_SB).transpose(1, 0, 2, 3)[None]

    return (yp.reshape(1, seq, D_MODEL), ys.reshape(nb_s, dec_seq, D_MODEL),
            skp[None, None], svp[None, None], stp[None],
            cache_layout(sks), cache_layout(svs), sts[None])
```

```python
import functools
import math

import jax
import jax.numpy as jnp
from jax import lax
from jax.experimental import pallas as pl
from jax.experimental.pallas import tpu as pltpu

F32 = jnp.float32
BF16 = jnp.bfloat16

D_MODEL = 2048
D_FF = 5632
H_SB = 8
D_SB = 128
H_RET = 4
DK_RET = 256
DV_RET = 256
SB_W = H_SB * D_SB
RET_W = H_RET * DK_RET
N_SECTIONS = 7
SECTION_W = 1024
ROPE_BASE = 10000.0
EPS = 1e-6

V7X_VMEM_LIMIT_BYTES = 56 * 1024 * 1024
SB_BLOCK = 256
RET_CHUNK = 256
FFN_TF = 512

LOG2E = 1.4426950408889634
LN2 = 0.6931471805599453


def _cparams(sem):
    return pltpu.CompilerParams(dimension_semantics=sem,
                                vmem_limit_bytes=V7X_VMEM_LIMIT_BYTES)


def _rmsnorm_rows(x, w):
    ms = jnp.mean(x * x, axis=-1, keepdims=True)
    return x * lax.rsqrt(ms + EPS) * w


def _dot(a, b):
    return jnp.dot(a, b, preferred_element_type=F32)


def _dot_nt(a, b):
    return lax.dot_general(a, b, (((1,), (1,)), ((), ())), preferred_element_type=F32)


def _dot_tn(a, b):
    return lax.dot_general(a, b, (((0,), (0,)), ((), ())), preferred_element_type=F32)


def _ffn_kernel(*refs, n_steps, final_norm, emit_bf16):
    x_ref, nw_ref, wg_ref, wu_ref, wd_ref = refs[:5]
    refs = refs[5:]
    fw_ref = None
    if final_norm:
        fw_ref, refs = refs[0], refs[1:]
    o_ref, refs = refs[0], refs[1:]
    if emit_bf16:
        wg16_ref, wu16_ref, wd16_ref, h_ref = refs
        wg16_ref[...] = wg_ref[...].astype(BF16)
        wu16_ref[...] = wu_ref[...].astype(BF16)
        wd16_ref[...] = wd_ref[...].astype(BF16)
        wg_ref, wu_ref, wd_ref = wg16_ref, wu16_ref, wd16_ref
    else:
        (h_ref,) = refs
    j = pl.program_id(1)

    @pl.when(j == 0)
    def _():
        x = x_ref[...]
        h_ref[...] = _rmsnorm_rows(x, nw_ref[...]).astype(BF16)
        o_ref[...] = x

    h = h_ref[...]
    g = _dot(h, wg_ref[...])
    u = _dot(h, wu_ref[...])
    a = (g * jax.nn.sigmoid(g) * (0.5 * u)).astype(BF16)
    o_ref[...] += _dot(a, wd_ref[...])

    if final_norm:
        @pl.when(j == n_steps - 1)
        def _():
            o_ref[...] = _rmsnorm_rows(o_ref[...], fw_ref[...])


def _ffn(x, norm_w, wg, wu, wd, final_w=None, *, tm):
    m = x.shape[0]
    n_steps = D_FF // FFN_TF
    emit_bf16 = wg.dtype == F32
    assert not emit_bf16 or m == tm
    w_specs = [
        pl.BlockSpec((D_MODEL, FFN_TF), lambda i, j: (0, j)),
        pl.BlockSpec((D_MODEL, FFN_TF), lambda i, j: (0, j)),
        pl.BlockSpec((FFN_TF, D_MODEL), lambda i, j: (j, 0)),
    ]
    in_specs = [
        pl.BlockSpec((tm, D_MODEL), lambda i, j: (i, 0)),
        pl.BlockSpec((1, D_MODEL), lambda i, j: (0, 0)),
    ] + w_specs
    args = [x, norm_w, wg, wu, wd]
    if final_w is not None:
        in_specs.append(pl.BlockSpec((1, D_MODEL), lambda i, j: (0, 0)))
        args.append(final_w)
    out_shape = [jax.ShapeDtypeStruct((m, D_MODEL), F32)]
    out_specs = [pl.BlockSpec((tm, D_MODEL), lambda i, j: (i, 0))]
    if emit_bf16:
        out_shape += [jax.ShapeDtypeStruct(w.shape, BF16) for w in (wg, wu, wd)]
        out_specs += w_specs
    out = pl.pallas_call(
        functools.partial(_ffn_kernel, n_steps=n_steps, final_norm=final_w is not None,
                          emit_bf16=emit_bf16),
        out_shape=out_shape,
        grid=(m // tm, n_steps),
        in_specs=in_specs,
        out_specs=out_specs,
        scratch_shapes=[pltpu.VMEM((tm, D_MODEL), BF16)],
        compiler_params=_cparams(("arbitrary", "arbitrary")),
        name="ffn_final" if final_w is not None else "ffn",
    )(*args)
    return (out[0], tuple(out[1:])) if emit_bf16 else out[0]


def _inproj_kernel(x_ref, nw_ref, w_ref, qn_ref, kn_ref, inv_ref,
                   sq_ref, sk_ref, sk16_ref, sv_ref, sv16_ref,
                   rq_ref, rk_ref, rv_ref, g_ref, *rest,
                   tm, n_split, tile_stride, pos_base, pos_mod, emit_bf16):
    if emit_bf16:
        w16_ref, h_ref, cos_ref, sin_ref, cos_row_ref, sin_row_ref = rest
        w16_ref[...] = w_ref[...].astype(BF16)
        w_ref = w16_ref
    else:
        h_ref, cos_ref, sin_ref, cos_row_ref, sin_row_ref = rest
    i = pl.program_id(0)
    j = pl.program_id(1)

    @pl.when(jnp.logical_and(i == 0, j == 0))
    def _():
        row = lax.broadcasted_iota(jnp.int32, (tm, DK_RET // 2), 0)
        ang = lax.rem(row, pos_mod).astype(F32) * inv_ref[...]
        cos_row_ref[...] = jnp.cos(ang)
        sin_row_ref[...] = jnp.sin(ang)

    @pl.when(j == 0)
    def _():
        h_ref[...] = _rmsnorm_rows(x_ref[...], nw_ref[...]).astype(BF16)
        ang = (pos_base + i * tile_stride).astype(F32) * inv_ref[...]
        ca, sa = jnp.cos(ang), jnp.sin(ang)
        cb, sb = cos_row_ref[...], sin_row_ref[...]
        cos_ref[...] = ca * cb - sa * sb
        sin_ref[...] = sa * cb + ca * sb

    section = j // n_split
    sb_heads = H_SB // n_split
    ret_heads = H_RET // n_split

    def proj():
        return _dot(h_ref[...], w_ref[...])

    def head(p, hd, width):
        return p[:, hd * width:(hd + 1) * width]

    @pl.when(section == 0)
    def _():
        p = proj()
        w = qn_ref[...] * (D_SB ** -0.5 * LOG2E)
        for hd in range(sb_heads):
            sq_ref[hd] = _rmsnorm_rows(head(p, hd, D_SB), w).astype(BF16)

    @pl.when(section == 1)
    def _():
        p = proj()
        for hd in range(sb_heads):
            k = _rmsnorm_rows(head(p, hd, D_SB), kn_ref[...])
            sk_ref[hd] = k
            sk16_ref[hd] = k.astype(BF16)

    @pl.when(section == 2)
    def _():
        p = proj()
        for hd in range(sb_heads):
            v = head(p, hd, D_SB)
            sv_ref[hd] = v
            sv16_ref[hd] = v.astype(BF16)

    def rotary(ph):
        half = DK_RET // 2
        x1, x2 = ph[:, :half], ph[:, half:]
        c, s = cos_ref[...], sin_ref[...]
        return jnp.concatenate([x1 * c - x2 * s, x1 * s + x2 * c], axis=-1)

    @pl.when(section == 3)
    def _():
        p = proj()
        for hd in range(ret_heads):
            rq_ref[hd] = rotary(head(p, hd, DK_RET)).astype(BF16)

    @pl.when(section == 4)
    def _():
        p = proj()
        for hd in range(ret_heads):
            rk_ref[hd] = rotary(head(p, hd, DK_RET)) * (DK_RET ** -0.5)

    @pl.when(section == 5)
    def _():
        p = proj()
        for hd in range(ret_heads):
            rv_ref[hd] = head(p, hd, DV_RET).astype(BF16)

    @pl.when(section == 6)
    def _():
        g_ref[...] = proj()


def _inproj(x, norm_w, w_in, qn, kn, inv_freq, *, tm, n_split, pos_base, pos_mod):
    m = x.shape[0]
    emit_bf16 = w_in.dtype == F32
    assert not emit_bf16 or m == tm
    tn = SECTION_W // n_split

    def part(section):
        return lambda j: jnp.clip(j - section * n_split, 0, n_split - 1)

    def head_spec(heads, width, section):
        pt = part(section)
        return pl.BlockSpec((heads // n_split, tm, width), lambda i, j: (pt(j), i, 0))

    gate_part = part(6)
    sb16 = jax.ShapeDtypeStruct((H_SB, m, D_SB), BF16)
    sb32 = jax.ShapeDtypeStruct((H_SB, m, D_SB), F32)
    ret16 = jax.ShapeDtypeStruct((H_RET, m, DK_RET), BF16)
    ret32 = jax.ShapeDtypeStruct((H_RET, m, DK_RET), F32)
    assert pos_mod >= m or tm % pos_mod == 0
    tile_stride = tm if pos_mod >= m else 0
    half = DK_RET // 2
    w_spec = pl.BlockSpec((D_MODEL, tn), lambda i, j: (0, j))
    out_shape = [sb16, sb32, sb16, sb32, sb16, ret16, ret32, ret16,
                 jax.ShapeDtypeStruct((m, SECTION_W), F32)]
    out_specs = [head_spec(H_SB, D_SB, 0), head_spec(H_SB, D_SB, 1), head_spec(H_SB, D_SB, 1),
                 head_spec(H_SB, D_SB, 2), head_spec(H_SB, D_SB, 2),
                 head_spec(H_RET, DK_RET, 3), head_spec(H_RET, DK_RET, 4),
                 head_spec(H_RET, DK_RET, 5),
                 pl.BlockSpec((tm, tn), lambda i, j: (i, gate_part(j)))]
    if emit_bf16:
        out_shape.append(jax.ShapeDtypeStruct(w_in.shape, BF16))
        out_specs.append(w_spec)
    out = pl.pallas_call(
        functools.partial(_inproj_kernel, tm=tm, n_split=n_split, tile_stride=tile_stride,
                          pos_base=pos_base, pos_mod=pos_mod, emit_bf16=emit_bf16),
        out_shape=out_shape,
        grid=(m // tm, N_SECTIONS * n_split),
        in_specs=[
            pl.BlockSpec((tm, D_MODEL), lambda i, j: (i, 0)),
            pl.BlockSpec((1, D_MODEL), lambda i, j: (0, 0)),
            w_spec,
            pl.BlockSpec((1, D_SB), lambda i, j: (0, 0)),
            pl.BlockSpec((1, D_SB), lambda i, j: (0, 0)),
            pl.BlockSpec((1, half), lambda i, j: (0, 0)),
        ],
        out_specs=out_specs,
        scratch_shapes=[pltpu.VMEM((tm, D_MODEL), BF16)] + [pltpu.VMEM((tm, half), F32)] * 4,
        compiler_params=_cparams(("arbitrary", "arbitrary")),
        name="inproj",
    )(x, norm_w, w_in, qn, kn, inv_freq)
    return (tuple(out[:9]), out[9]) if emit_bf16 else tuple(out)


def _softplus2(z):
    return jnp.maximum(z, 0.0) + jnp.log2(1.0 + jnp.exp2(-jnp.abs(z)))


def _suffix_matrix(n):
    r = lax.broadcasted_iota(jnp.int32, (2 * n, n), 0)
    c = lax.broadcasted_iota(jnp.int32, (2 * n, n), 1)
    return jnp.where(jnp.where(r >= n, r - n, r) >= c, 1.0, 0.0).astype(BF16)


def _sb_block(q, k, v, carry, tri, mask):
    z = _dot_nt(q, k)
    sp = _softplus2(z)
    if mask is not None:
        sp = jnp.where(mask, sp, 0.0)
    hi = sp.astype(BF16)
    lo = (sp - hi.astype(F32)).astype(BF16)
    n = hi.shape[1]
    if n % 128 == 0:
        incl = _dot(jnp.concatenate([hi, lo], axis=1), tri) + carry
    else:
        incl = _dot(hi, tri[:n]) + _dot(lo, tri[:n]) + carry
    a = jnp.exp2(z - incl)
    if mask is not None:
        a = jnp.where(mask, a, 0.0)
    return _dot(a.astype(BF16), v), incl[:, 0:1]


SB_DEAD_CARRY = 150.0


SB_HEADS_PER_STEP = 4


def _sb_prompt_kernel(q_ref, k_ref, v_ref, o_ref):
    qi = pl.program_id(1)
    heads, _, d = q_ref.shape
    blk = SB_BLOCK
    tri = _suffix_matrix(blk)
    r = lax.broadcasted_iota(jnp.int32, (blk, blk), 0)
    c = lax.broadcasted_iota(jnp.int32, (blk, blk), 1)
    strict = c < r
    zero = jnp.zeros((blk, 1), F32)
    has_prev = qi > 0

    def kv(g, block):
        s0 = pl.multiple_of(block * blk, blk)
        return k_ref[g, pl.ds(s0, blk), :], v_ref[g, pl.ds(s0, blk), :]

    def start(g):
        q = q_ref[g]
        k0, v0 = kv(g, qi)
        acc, carry = _sb_block(q, k0, v0, zero, tri, strict)
        kp, vp = kv(g, jnp.maximum(qi - 1, 0))
        d_p, carry_p = _sb_block(q, kp, vp, carry, tri, None)
        return q, acc + jnp.where(has_prev, d_p, 0.0), jnp.where(has_prev, carry_p, carry)

    qs, accs, carries = zip(*[start(g) for g in range(heads)])

    def alive(carries):
        low = functools.reduce(jnp.minimum, carries)
        return (jnp.min(low) < SB_DEAD_CARRY).astype(jnp.int32)

    def cond(state):
        t, live, _, _ = state
        return jnp.logical_and(t >= 0, live > 0)

    def body(state):
        t, _, accs, carries = state
        new_accs, new_carries = [], []
        for g in range(heads):
            kb, vb = kv(g, t)
            d_g, carry_g = _sb_block(qs[g], kb, vb, carries[g], tri, None)
            new_accs.append(accs[g] + d_g)
            new_carries.append(carry_g)
        return t - 1, alive(new_carries), tuple(new_accs), tuple(new_carries)

    _, _, accs, _ = lax.while_loop(cond, body, (qi - 2, alive(carries), accs, carries))
    for g in range(heads):
        o_ref[:, g * d:(g + 1) * d] = accs[g].astype(o_ref.dtype)


def _sb_prompt(q16, k16, v16):
    h, s, d = q16.shape
    tq = SB_BLOCK
    g = SB_HEADS_PER_STEP
    return pl.pallas_call(
        _sb_prompt_kernel,
        out_shape=jax.ShapeDtypeStruct((s, h * d), BF16),
        grid=(h // g, s // tq),
        in_specs=[
            pl.BlockSpec((g, tq, d), lambda hg, qi: (hg, qi, 0)),
            pl.BlockSpec((g, s, d), lambda hg, qi: (hg, 0, 0)),
            pl.BlockSpec((g, s, d), lambda hg, qi: (hg, 0, 0)),
        ],
        out_specs=pl.BlockSpec((tq, g * d), lambda hg, qi: (qi, hg)),
        compiler_params=_cparams(("arbitrary", "arbitrary")),
        name="sb_prompt",
    )(q16, k16, v16)


def _sb_sample_kernel(q_ref, kn_ref, vn_ref, kc_ref, vc_ref, o_ref, *, past):
    nh, t, d = q_ref.shape
    ht = nh * t
    blk = SB_BLOCK
    q_all = q_ref[...].reshape(ht, d).astype(F32)
    q_head = lax.broadcasted_iota(jnp.int32, (ht, d), 0) // t
    q_masked = [jnp.where(q_head == h, q_all, 0.0).astype(BF16) for h in range(nh)]

    def logits(keys_of_head):
        z = _dot_nt(keys_of_head(0), q_masked[0])
        for h in range(1, nh):
            z = z + _dot_nt(keys_of_head(h), q_masked[h])
        return z

    def suffix_matrix(n):
        r = lax.broadcasted_iota(jnp.int32, (n, 2 * n), 0)
        c = lax.broadcasted_iota(jnp.int32, (n, 2 * n), 1)
        return jnp.where(jnp.where(c >= n, c - n, c) >= r, 1.0, 0.0).astype(BF16)

    def suffix_sum(sp, lmat):
        hi = sp.astype(BF16)
        lo = (sp - hi.astype(F32)).astype(BF16)
        return _dot(lmat, jnp.concatenate([hi, lo], axis=0))

    def emit(a_t, values_of_head, acc):
        a = a_t.T.astype(BF16)
        return [acc[h] + _dot(a[h * t:(h + 1) * t, :], values_of_head(h)) for h in range(nh)]

    z = logits(lambda h: kn_ref[h])
    s_idx = lax.broadcasted_iota(jnp.int32, (t, ht), 0)
    t_idx = lax.broadcasted_iota(jnp.int32, (t, ht), 1) % t
    strict = s_idx < t_idx
    sp = jnp.where(strict, _softplus2(z), 0.0)
    incl = suffix_sum(sp, suffix_matrix(t))
    a_t = jnp.where(strict, jnp.exp2(z - incl), 0.0)
    acc = emit(a_t, lambda h: vn_ref[h], [jnp.zeros((t, d), F32) for _ in range(nh)])
    carry = incl[0:1, :]

    lmat = suffix_matrix(blk)
    for b in range(past // blk - 1, -1, -1):
        rows = slice(b * blk, (b + 1) * blk)
        z = logits(lambda h: kc_ref[h, rows, :].astype(BF16))
        incl = suffix_sum(_softplus2(z), lmat) + carry
        acc = emit(jnp.exp2(z - incl), lambda h: vc_ref[h, rows, :].astype(BF16), acc)
        carry = incl[0:1, :]

    for h in range(nh):
        o_ref[:, h * d:(h + 1) * d] = acc[h].astype(o_ref.dtype)


def _sb_sample(q16, k16, v16, cache_k, cache_v, *, t):
    h, m, d = q16.shape
    nb = m // t
    past = cache_k.shape[2]
    new_spec = pl.BlockSpec((h, t, d), lambda b: (0, b, 0))
    cache_spec = pl.BlockSpec((None, h, past, d), lambda b: (b, 0, 0, 0))
    return pl.pallas_call(
        functools.partial(_sb_sample_kernel, past=past),
        out_shape=jax.ShapeDtypeStruct((m, h * d), BF16),
        grid=(nb,),
        in_specs=[new_spec, new_spec, new_spec, cache_spec, cache_spec],
        out_specs=pl.BlockSpec((t, h * d), lambda b: (b, 0)),
        compiler_params=_cparams(("arbitrary",)),
        name="sb_sample",
    )(q16, k16, v16, cache_k, cache_v)


def _ret_log_decay(hd):
    return math.log(1.0 - 2.0 ** (-5.0 - hd))


def _ret_kernel(q_ref, k_ref, v_ref, g_ref, nw_ref, s0_ref, r_ref, st_ref, *, chunk):
    c = pl.program_id(1)

    @pl.when(c == 0)
    def _():
        st_ref[...] = s0_ref[...]

    row = lax.broadcasted_iota(jnp.int32, (chunk, chunk), 0)
    col = lax.broadcasted_iota(jnp.int32, (chunk, chunk), 1)
    diff = (row - col).astype(F32)
    causal = row >= col
    pos = lax.broadcasted_iota(jnp.int32, (chunk, DK_RET), 0).astype(F32)
    for hd in range(H_RET):
        lg = _ret_log_decay(hd)
        intra = jnp.where(causal, jnp.exp(lg * jnp.maximum(diff, 0.0)), 0.0)
        q_dec = jnp.exp(lg * (pos + 1.0))
        k_dec = jnp.exp(lg * (chunk - 1.0 - pos))
        c_dec = math.exp(lg * chunk)
        q = q_ref[hd]
        k = k_ref[hd]
        v = v_ref[hd]
        state = st_ref[hd]
        scores = (_dot_nt(q, k.astype(BF16)) * intra).astype(BF16)
        o = _dot(scores, v) + _dot(q, state.astype(BF16)) * q_dec
        st_ref[hd] = c_dec * state + _dot_tn((k * k_dec).astype(BF16), v)
        o = o * lax.rsqrt(jnp.mean(o * o, axis=-1, keepdims=True) + EPS)
        sl = slice(hd * DV_RET, (hd + 1) * DV_RET)
        gate = g_ref[:, sl]
        r_ref[:, sl] = (o * nw_ref[:, sl] * (gate * jax.nn.sigmoid(gate))).astype(r_ref.dtype)


def _retention(rq, rk, rv, gate, norm_w, state0, *, chunk):
    h, m, _ = rq.shape
    nb = state0.shape[0]
    nc = m // (nb * chunk)
    qkv_spec = pl.BlockSpec((h, chunk, DK_RET), lambda b, c: (0, b * nc + c, 0))
    st_spec = pl.BlockSpec((None, h, DK_RET, DV_RET), lambda b, c: (b, 0, 0, 0))
    return pl.pallas_call(
        functools.partial(_ret_kernel, chunk=chunk),
        out_shape=(jax.ShapeDtypeStruct((m, h * DV_RET), BF16),
                   jax.ShapeDtypeStruct(state0.shape, F32)),
        grid=(nb, nc),
        in_specs=[qkv_spec, qkv_spec, qkv_spec,
                  pl.BlockSpec((chunk, h * DV_RET), lambda b, c: (b * nc + c, 0)),
                  pl.BlockSpec((1, h * DV_RET), lambda b, c: (0, 0)),
                  st_spec],
        out_specs=(pl.BlockSpec((chunk, h * DV_RET), lambda b, c: (b * nc + c, 0)), st_spec),
        compiler_params=_cparams(("arbitrary", "arbitrary")),
        name="retention",
    )(rq, rk, rv, gate, norm_w, state0)


def _outproj_kernel(x_ref, a_ref, b_ref, wa_ref, wb_ref, o_ref):
    o_ref[...] = x_ref[...] + _dot(a_ref[...], wa_ref[...]) + _dot(b_ref[...], wb_ref[...])


def _outproj(x, a_sb, a_ret, w_out, *, tm):
    m = x.shape[0]
    row = lambda i: (i, 0)
    return pl.pallas_call(
        _outproj_kernel,
        out_shape=jax.ShapeDtypeStruct((m, D_MODEL), F32),
        grid=(m // tm,),
        in_specs=[
            pl.BlockSpec((tm, D_MODEL), row),
            pl.BlockSpec((tm, SB_W), row),
            pl.BlockSpec((tm, RET_W), row),
            pl.BlockSpec((SB_W, D_MODEL), lambda i: (0, 0)),
            pl.BlockSpec((RET_W, D_MODEL), lambda i: (1, 0)),
        ],
        out_specs=pl.BlockSpec((tm, D_MODEL), row),
        compiler_params=_cparams(("arbitrary",)),
        name="outproj",
    )(x, a_sb, a_ret, w_out, w_out)


OUTPROJ_TK = 512


def _outproj_f32w_kernel(x_ref, a_ref, w_ref, o_ref, w16_ref):
    @pl.when(pl.program_id(0) == 0)
    def _():
        o_ref[...] = x_ref[...]

    w16_ref[...] = w_ref[...].astype(BF16)
    o_ref[...] += _dot(a_ref[...], w16_ref[...])


def _outproj_f32w(x, a, w_out):
    m = x.shape[0]
    d_mix = w_out.shape[0]
    return pl.pallas_call(
        _outproj_f32w_kernel,
        out_shape=(jax.ShapeDtypeStruct((m, D_MODEL), F32),
                   jax.ShapeDtypeStruct(w_out.shape, BF16)),
        grid=(d_mix // OUTPROJ_TK,),
        in_specs=[
            pl.BlockSpec((m, D_MODEL), lambda k: (0, 0)),
            pl.BlockSpec((m, OUTPROJ_TK), lambda k: (0, k)),
            pl.BlockSpec((OUTPROJ_TK, D_MODEL), lambda k: (k, 0)),
        ],
        out_specs=(pl.BlockSpec((m, D_MODEL), lambda k: (0, 0)),
                   pl.BlockSpec((OUTPROJ_TK, D_MODEL), lambda k: (k, 0))),
        compiler_params=_cparams(("arbitrary",)),
        name="outproj_f32w",
    )(x, a, w_out)


def _layer(x, weights, sb_fn, state0, *, tm, ffn_tm, proj_tm, proj_split, pos_base, pos_mod, chunk):
    (n1, wg1, wu1, wd1, nmix, w_in, qn, kn, ron, w_out, n2, wg2, wu2, wd2, nf, inv_freq) = weights
    f32w = w_in.dtype == F32
    x1 = _ffn(x, n1, wg1, wu1, wd1, tm=ffn_tm)
    if f32w:
        x1, (wg1, wu1, wd1) = x1
    proj = _inproj(x1, nmix, w_in, qn, kn, inv_freq, tm=proj_tm, n_split=proj_split,
                   pos_base=pos_base, pos_mod=pos_mod)
    if f32w:
        proj, w_in = proj
    sq, sk, sk16, sv, sv16, rq, rk, rv, gate = proj
    a_sb = sb_fn(sq, sk16, sv16)
    a_ret, state = _retention(rq, rk, rv, gate, ron, state0, chunk=chunk)
    if f32w:
        x2, w_out = _outproj_f32w(x1, jnp.concatenate([a_sb, a_ret], axis=1), w_out)
    else:
        x2 = _outproj(x1, a_sb, a_ret, w_out, tm=tm)
    y = _ffn(x2, n2, wg2, wu2, wd2, nf, tm=ffn_tm)
    if f32w:
        y, (wg2, wu2, wd2) = y
    weights16 = (n1, wg1, wu1, wd1, nmix, w_in, qn, kn, ron, w_out, n2, wg2, wu2, wd2, nf, inv_freq)
    return y, sk, sv, state, weights16


def kernel(x_prompt, x_sample, cache_sb_k, cache_sb_v, state_ret, ffn1_norm, ffn1_w_gate, ffn1_w_up, ffn1_w_down, mix_norm, w_in, sb_q_norm, sb_k_norm, ret_out_norm, w_out, ffn2_norm, ffn2_w_gate, ffn2_w_up, ffn2_w_down, final_norm):
    depth = ffn1_norm.shape[0]
    assert depth == 1
    nb_p, seq, _ = x_prompt.shape
    nb_s, dec_seq, _ = x_sample.shape
    past = cache_sb_k.shape[3]
    assert nb_p == 1

    half = DK_RET // 2
    inv_freq = (ROPE_BASE ** (-jnp.arange(half, dtype=F32) / half)).reshape(1, half)
    l = 0
    weights = (ffn1_norm[l][None], ffn1_w_gate[l], ffn1_w_up[l], ffn1_w_down[l],
               mix_norm[l][None], w_in[l], sb_q_norm[l][None], sb_k_norm[l][None],
               ret_out_norm[l][None], w_out[l], ffn2_norm[l][None], ffn2_w_gate[l],
               ffn2_w_up[l], ffn2_w_down[l], final_norm[l][None], inv_freq)

    xs = x_sample.reshape(nb_s * dec_seq, D_MODEL)
    sb_s = functools.partial(_sb_sample, cache_k=cache_sb_k[l], cache_v=cache_sb_v[l], t=dec_seq)
    ys, sks, svs, sts, weights16 = _layer(xs, weights, sb_s, state_ret[l],
                                          tm=nb_s * dec_seq, ffn_tm=nb_s * dec_seq,
                                          proj_tm=nb_s * dec_seq, proj_split=1, pos_base=past,
                                          pos_mod=dec_seq, chunk=dec_seq)

    xp = x_prompt.reshape(seq, D_MODEL)
    zero_state = jnp.zeros((1, H_RET, DK_RET, DV_RET), F32)
    yp, skp, svp, stp, _ = _layer(xp, weights16, _sb_prompt, zero_state,
                                  tm=512, ffn_tm=1024, proj_tm=1024, proj_split=2,
                                  pos_base=0, pos_mod=seq, chunk=RET_CHUNK)

    def cache_layout(t):
        return t.reshape(H_SB, nb_s, dec_seq, D_SB).transpose(1, 0, 2, 3)[None]

    return (yp.reshape(1, seq, D_MODEL), ys.reshape(nb_s, dec_seq, D_MODEL),
            skp[None, None], svp[None, None], stp[None],
            cache_layout(sks), cache_layout(svs), sts[None])
```

```python
import functools
import math

import jax
import jax.numpy as jnp
from jax import lax
from jax.experimental import pallas as pl
from jax.experimental.pallas import tpu as pltpu

F32 = jnp.float32
BF16 = jnp.bfloat16

D_MODEL = 2048
D_FF = 5632
H_SB = 8
D_SB = 128
H_RET = 4
DK_RET = 256
DV_RET = 256
SB_W = H_SB * D_SB
RET_W = H_RET * DK_RET
N_SECTIONS = 7
SECTION_W = 1024
ROPE_BASE = 10000.0
EPS = 1e-6

V7X_VMEM_LIMIT_BYTES = 56 * 1024 * 1024
SB_BLOCK = 256
RET_CHUNK = 256
FFN_TF = 512

LOG2E = 1.4426950408889634
LN2 = 0.6931471805599453


def _cparams(sem):
    return pltpu.CompilerParams(dimension_semantics=sem,
                                vmem_limit_bytes=V7X_VMEM_LIMIT_BYTES)


def _rmsnorm_rows(x, w):
    ms = jnp.mean(x * x, axis=-1, keepdims=True)
    return x * lax.rsqrt(ms + EPS) * w


def _dot(a, b):
    return jnp.dot(a, b, preferred_element_type=F32)


def _dot_nt(a, b):
    return lax.dot_general(a, b, (((1,), (1,)), ((), ())), preferred_element_type=F32)


def _dot_tn(a, b):
    return lax.dot_general(a, b, (((0,), (0,)), ((), ())), preferred_element_type=F32)


def _ffn_kernel(*refs, n_steps, final_norm, emit_bf16):
    x_ref, nw_ref, wg_ref, wu_ref, wd_ref = refs[:5]
    refs = refs[5:]
    fw_ref = None
    if final_norm:
        fw_ref, refs = refs[0], refs[1:]
    o_ref, refs = refs[0], refs[1:]
    if emit_bf16:
        wg16_ref, wu16_ref, wd16_ref, h_ref = refs
        wg16_ref[...] = wg_ref[...].astype(BF16)
        wu16_ref[...] = wu_ref[...].astype(BF16)
        wd16_ref[...] = wd_ref[...].astype(BF16)
        wg_ref, wu_ref, wd_ref = wg16_ref, wu16_ref, wd16_ref
    else:
        (h_ref,) = refs
    j = pl.program_id(1)

    @pl.when(j == 0)
    def _():
        x = x_ref[...]
        h_ref[...] = _rmsnorm_rows(x, nw_ref[...]).astype(BF16)
        o_ref[...] = x

    h = h_ref[...]
    g = _dot(h, wg_ref[...])
    u = _dot(h, wu_ref[...])
    a = (g * jax.nn.sigmoid(g) * (0.5 * u)).astype(BF16)
    o_ref[...] += _dot(a, wd_ref[...])

    if final_norm:
        @pl.when(j == n_steps - 1)
        def _():
            o_ref[...] = _rmsnorm_rows(o_ref[...], fw_ref[...])


def _ffn(x, norm_w, wg, wu, wd, final_w=None, *, tm):
    m = x.shape[0]
    n_steps = D_FF // FFN_TF
    emit_bf16 = wg.dtype == F32
    assert not emit_bf16 or m == tm
    w_specs = [
        pl.BlockSpec((D_MODEL, FFN_TF), lambda i, j: (0, j)),
        pl.BlockSpec((D_MODEL, FFN_TF), lambda i, j: (0, j)),
        pl.BlockSpec((FFN_TF, D_MODEL), lambda i, j: (j, 0)),
    ]
    in_specs = [
        pl.BlockSpec((tm, D_MODEL), lambda i, j: (i, 0)),
        pl.BlockSpec((1, D_MODEL), lambda i, j: (0, 0)),
    ] + w_specs
    args = [x, norm_w, wg, wu, wd]
    if final_w is not None:
        in_specs.append(pl.BlockSpec((1, D_MODEL), lambda i, j: (0, 0)))
        args.append(final_w)
    out_shape = [jax.ShapeDtypeStruct((m, D_MODEL), F32)]
    out_specs = [pl.BlockSpec((tm, D_MODEL), lambda i, j: (i, 0))]
    if emit_bf16:
        out_shape += [jax.ShapeDtypeStruct(w.shape, BF16) for w in (wg, wu, wd)]
        out_specs += w_specs
    out = pl.pallas_call(
        functools.partial(_ffn_kernel, n_steps=n_steps, final_norm=final_w is not None,
                          emit_bf16=emit_bf16),
        out_shape=out_shape,
        grid=(m // tm, n_steps),
        in_specs=in_specs,
        out_specs=out_specs,
        scratch_shapes=[pltpu.VMEM((tm, D_MODEL), BF16)],
        compiler_params=_cparams(("arbitrary", "arbitrary")),
        name="ffn_final" if final_w is not None else "ffn",
    )(*args)
    return (out[0], tuple(out[1:])) if emit_bf16 else out[0]


def _inproj_kernel(x_ref, nw_ref, w_ref, qn_ref, kn_ref, inv_ref,
                   sq_ref, sk_ref, sk16_ref, sv_ref, sv16_ref,
                   rq_ref, rk_ref, rv_ref, g_ref, *rest,
                   tm, n_split, tile_stride, pos_base, pos_mod, emit_bf16):
    if emit_bf16:
        w16_ref, h_ref, cos_ref, sin_ref, cos_row_ref, sin_row_ref = rest
        w16_ref[...] = w_ref[...].astype(BF16)
        w_ref = w16_ref
    else:
        h_ref, cos_ref, sin_ref, cos_row_ref, sin_row_ref = rest
    i = pl.program_id(0)
    j = pl.program_id(1)

    @pl.when(jnp.logical_and(i == 0, j == 0))
    def _():
        row = lax.broadcasted_iota(jnp.int32, (tm, DK_RET // 2), 0)
        ang = lax.rem(row, pos_mod).astype(F32) * inv_ref[...]
        cos_row_ref[...] = jnp.cos(ang)
        sin_row_ref[...] = jnp.sin(ang)

    @pl.when(j == 0)
    def _():
        h_ref[...] = _rmsnorm_rows(x_ref[...], nw_ref[...]).astype(BF16)
        ang = (pos_base + i * tile_stride).astype(F32) * inv_ref[...]
        ca, sa = jnp.cos(ang), jnp.sin(ang)
        cb, sb = cos_row_ref[...], sin_row_ref[...]
        cos_ref[...] = ca * cb - sa * sb
        sin_ref[...] = sa * cb + ca * sb

    section = j // n_split
    sb_heads = H_SB // n_split
    ret_heads = H_RET // n_split

    def proj():
        return _dot(h_ref[...], w_ref[...])

    def head(p, hd, width):
        return p[:, hd * width:(hd + 1) * width]

    @pl.when(section == 0)
    def _():
        p = proj()
        w = qn_ref[...] * (D_SB ** -0.5 * LOG2E)
        for hd in range(sb_heads):
            sq_ref[hd] = _rmsnorm_rows(head(p, hd, D_SB), w).astype(BF16)

    @pl.when(section == 1)
    def _():
        p = proj()
        for hd in range(sb_heads):
            k = _rmsnorm_rows(head(p, hd, D_SB), kn_ref[...])
            sk_ref[hd] = k
            sk16_ref[hd] = k.astype(BF16)

    @pl.when(section == 2)
    def _():
        p = proj()
        for hd in range(sb_heads):
            v = head(p, hd, D_SB)
            sv_ref[hd] = v
            sv16_ref[hd] = v.astype(BF16)

    def rotary(ph):
        half = DK_RET // 2
        x1, x2 = ph[:, :half], ph[:, half:]
        c, s = cos_ref[...], sin_ref[...]
        return jnp.concatenate([x1 * c - x2 * s, x1 * s + x2 * c], axis=-1)

    @pl.when(section == 3)
    def _():
        p = proj()
        for hd in range(ret_heads):
            rq_ref[hd] = rotary(head(p, hd, DK_RET)).astype(BF16)

    @pl.when(section == 4)
    def _():
        p = proj()
        for hd in range(ret_heads):
            rk_ref[hd] = rotary(head(p, hd, DK_RET)) * (DK_RET ** -0.5)

    @pl.when(section == 5)
    def _():
        p = proj()
        for hd in range(ret_heads):
            rv_ref[hd] = head(p, hd, DV_RET).astype(BF16)

    @pl.when(section == 6)
    def _():
        g_ref[...] = proj()


def _inproj(x, norm_w, w_in, qn, kn, inv_freq, *, tm, n_split, pos_base, pos_mod):
    m = x.shape[0]
    emit_bf16 = w_in.dtype == F32
    assert not emit_bf16 or m == tm
    tn = SECTION_W // n_split

    def part(section):
        return lambda j: jnp.clip(j - section * n_split, 0, n_split - 1)

    def head_spec(heads, width, section):
        pt = part(section)
        return pl.BlockSpec((heads // n_split, tm, width), lambda i, j: (pt(j), i, 0))

    gate_part = part(6)
    sb16 = jax.ShapeDtypeStruct((H_SB, m, D_SB), BF16)
    sb32 = jax.ShapeDtypeStruct((H_SB, m, D_SB), F32)
    ret16 = jax.ShapeDtypeStruct((H_RET, m, DK_RET), BF16)
    ret32 = jax.ShapeDtypeStruct((H_RET, m, DK_RET), F32)
    assert pos_mod >= m or tm % pos_mod == 0
    tile_stride = tm if pos_mod >= m else 0
    half = DK_RET // 2
    w_spec = pl.BlockSpec((D_MODEL, tn), lambda i, j: (0, j))
    out_shape = [sb16, sb32, sb16, sb32, sb16, ret16, ret32, ret16,
                 jax.ShapeDtypeStruct((m, SECTION_W), F32)]
    out_specs = [head_spec(H_SB, D_SB, 0), head_spec(H_SB, D_SB, 1), head_spec(H_SB, D_SB, 1),
                 head_spec(H_SB, D_SB, 2), head_spec(H_SB, D_SB, 2),
                 head_spec(H_RET, DK_RET, 3), head_spec(H_RET, DK_RET, 4),
                 head_spec(H_RET, DK_RET, 5),
                 pl.BlockSpec((tm, tn), lambda i, j: (i, gate_part(j)))]
    if emit_bf16:
        out_shape.append(jax.ShapeDtypeStruct(w_in.shape, BF16))
        out_specs.append(w_spec)
    out = pl.pallas_call(
        functools.partial(_inproj_kernel, tm=tm, n_split=n_split, tile_stride=tile_stride,
                          pos_base=pos_base, pos_mod=pos_mod, emit_bf16=emit_bf16),
        out_shape=out_shape,
        grid=(m // tm, N_SECTIONS * n_split),
        in_specs=[
            pl.BlockSpec((tm, D_MODEL), lambda i, j: (i, 0)),
            pl.BlockSpec((1, D_MODEL), lambda i, j: (0, 0)),
            w_spec,
            pl.BlockSpec((1, D_SB), lambda i, j: (0, 0)),
            pl.BlockSpec((1, D_SB), lambda i, j: (0, 0)),
            pl.BlockSpec((1, half), lambda i, j: (0, 0)),
        ],
        out_specs=out_specs,
        scratch_shapes=[pltpu.VMEM((tm, D_MODEL), BF16)] + [pltpu.VMEM((tm, half), F32)] * 4,
        compiler_params=_cparams(("arbitrary", "arbitrary")),
        name="inproj",
    )(x, norm_w, w_in, qn, kn, inv_freq)
    return (tuple(out[:9]), out[9]) if emit_bf16 else tuple(out)


def _softplus2(z):
    return jnp.maximum(z, 0.0) + jnp.log2(1.0 + jnp.exp2(-jnp.abs(z)))


def _suffix_matrix(n):
    r = lax.broadcasted_iota(jnp.int32, (2 * n, n), 0)
    c = lax.broadcasted_iota(jnp.int32, (2 * n, n), 1)
    return jnp.where(jnp.where(r >= n, r - n, r) >= c, 1.0, 0.0).astype(BF16)


def _sb_block(q, k, v, carry, tri, mask):
    z = _dot_nt(q, k)
    sp = _softplus2(z)
    if mask is not None:
        sp = jnp.where(mask, sp, 0.0)
    hi = sp.astype(BF16)
    lo = (sp - hi.astype(F32)).astype(BF16)
    n = hi.shape[1]
    if n % 128 == 0:
        incl = _dot(jnp.concatenate([hi, lo], axis=1), tri) + carry
    else:
        incl = _dot(hi, tri[:n]) + _dot(lo, tri[:n]) + carry
    a = jnp.exp2(z - incl)
    if mask is not None:
        a = jnp.where(mask, a, 0.0)
    return _dot(a.astype(BF16), v), incl[:, 0:1]


SB_DEAD_CARRY = 150.0


SB_HEADS_PER_STEP = 4


def _sb_prompt_kernel(q_ref, k_ref, v_ref, o_ref):
    qi = pl.program_id(1)
    heads, _, d = q_ref.shape
    blk = SB_BLOCK
    tri = _suffix_matrix(blk)
    r = lax.broadcasted_iota(jnp.int32, (blk, blk), 0)
    c = lax.broadcasted_iota(jnp.int32, (blk, blk), 1)
    strict = c < r
    zero = jnp.zeros((blk, 1), F32)
    has_prev = qi > 0

    def kv(g, block):
        s0 = pl.multiple_of(block * blk, blk)
        return k_ref[g, pl.ds(s0, blk), :], v_ref[g, pl.ds(s0, blk), :]

    def start(g):
        q = q_ref[g]
        k0, v0 = kv(g, qi)
        acc, carry = _sb_block(q, k0, v0, zero, tri, strict)
        kp, vp = kv(g, jnp.maximum(qi - 1, 0))
        d_p, carry_p = _sb_block(q, kp, vp, carry, tri, None)
        return q, acc + jnp.where(has_prev, d_p, 0.0), jnp.where(has_prev, carry_p, carry)

    qs, accs, carries = zip(*[start(g) for g in range(heads)])

    def alive(carries):
        low = functools.reduce(jnp.minimum, carries)
        return (jnp.min(low) < SB_DEAD_CARRY).astype(jnp.int32)

    def cond(state):
        t, live, _, _ = state
        return jnp.logical_and(t >= 0, live > 0)

    def body(state):
        t, _, accs, carries = state
        new_accs, new_carries = [], []
        for g in range(heads):
            kb, vb = kv(g, t)
            d_g, carry_g = _sb_block(qs[g], kb, vb, carries[g], tri, None)
            new_accs.append(accs[g] + d_g)
            new_carries.append(carry_g)
        return t - 1, alive(new_carries), tuple(new_accs), tuple(new_carries)

    _, _, accs, _ = lax.while_loop(cond, body, (qi - 2, alive(carries), accs, carries))
    for g in range(heads):
        o_ref[:, g * d:(g + 1) * d] = accs[g].astype(o_ref.dtype)


def _sb_prompt(q16, k16, v16):
    h, s, d = q16.shape
    tq = SB_BLOCK
    g = SB_HEADS_PER_STEP
    return pl.pallas_call(
        _sb_prompt_kernel,
        out_shape=jax.ShapeDtypeStruct((s, h * d), BF16),
        grid=(h // g, s // tq),
        in_specs=[
            pl.BlockSpec((g, tq, d), lambda hg, qi: (hg, qi, 0)),
            pl.BlockSpec((g, s, d), lambda hg, qi: (hg, 0, 0)),
            pl.BlockSpec((g, s, d), lambda hg, qi: (hg, 0, 0)),
        ],
        out_specs=pl.BlockSpec((tq, g * d), lambda hg, qi: (qi, hg)),
        compiler_params=_cparams(("arbitrary", "arbitrary")),
        name="sb_prompt",
    )(q16, k16, v16)


def _sb_sample_kernel(q_ref, kn_ref, vn_ref, kc_hbm, vc_hbm, o_ref, kbuf, vbuf, sem, *, past):
    nh, t, d = q_ref.shape
    ht = nh * t
    blk = SB_BLOCK
    n_blocks = past // blk
    b = pl.program_id(0)
    slot = lax.rem(b, 2)

    def cache_copies(batch, block, to_slot):
        rows = pl.ds(pl.multiple_of(block * blk, blk), blk)
        return (pltpu.make_async_copy(kc_hbm.at[batch, :, rows, :], kbuf.at[to_slot], sem.at[0, to_slot]),
                pltpu.make_async_copy(vc_hbm.at[batch, :, rows, :], vbuf.at[to_slot], sem.at[1, to_slot]))

    @pl.when(b == 0)
    def _():
        for cp in cache_copies(0, n_blocks - 1, 0):
            cp.start()

    @pl.when(b + 1 < pl.num_programs(0))
    def _():
        for cp in cache_copies(b + 1, n_blocks - 1, 1 - slot):
            cp.start()

    q_all = q_ref[...].reshape(ht, d).astype(F32)
    q_head = lax.broadcasted_iota(jnp.int32, (ht, d), 0) // t
    q_masked = [jnp.where(q_head == h, q_all, 0.0).astype(BF16) for h in range(nh)]

    def logits(keys_of_head):
        z = _dot_nt(keys_of_head(0), q_masked[0])
        for h in range(1, nh):
            z = z + _dot_nt(keys_of_head(h), q_masked[h])
        return z

    def suffix_matrix(n):
        r = lax.broadcasted_iota(jnp.int32, (n, 2 * n), 0)
        c = lax.broadcasted_iota(jnp.int32, (n, 2 * n), 1)
        return jnp.where(jnp.where(c >= n, c - n, c) >= r, 1.0, 0.0).astype(BF16)

    def suffix_sum(sp, lmat):
        hi = sp.astype(BF16)
        lo = (sp - hi.astype(F32)).astype(BF16)
        return _dot(lmat, jnp.concatenate([hi, lo], axis=0))

    def emit(a_t, values_of_head, acc):
        a = a_t.T.astype(BF16)
        return [acc[h] + _dot(a[h * t:(h + 1) * t, :], values_of_head(h)) for h in range(nh)]

    z = logits(lambda h: kn_ref[h])
    s_idx = lax.broadcasted_iota(jnp.int32, (t, ht), 0)
    t_idx = lax.broadcasted_iota(jnp.int32, (t, ht), 1) % t
    strict = s_idx < t_idx
    sp = jnp.where(strict, _softplus2(z), 0.0)
    incl = suffix_sum(sp, suffix_matrix(t))
    a_t = jnp.where(strict, jnp.exp2(z - incl), 0.0)
    acc = emit(a_t, lambda h: vn_ref[h], [jnp.zeros((t, d), F32) for _ in range(nh)])
    carry = incl[0:1, :]

    lmat = suffix_matrix(blk)

    def cache_block(from_slot, acc, carry):
        z = logits(lambda h: kbuf[from_slot, h].astype(BF16))
        incl = suffix_sum(_softplus2(z), lmat) + carry
        acc = emit(jnp.exp2(z - incl), lambda h: vbuf[from_slot, h].astype(BF16), acc)
        return acc, incl[0:1, :]

    def alive(carry):
        return (jnp.min(carry) < SB_DEAD_CARRY).astype(jnp.int32)

    for cp in cache_copies(b, n_blocks - 1, slot):
        cp.wait()
    acc, carry = cache_block(slot, acc, carry)

    def cond(state):
        block, live, _, _ = state
        return jnp.logical_and(block >= 0, live > 0)

    def body(state):
        block, _, acc, carry = state
        copies = cache_copies(b, block, 2)
        for cp in copies:
            cp.start()
        for cp in copies:
            cp.wait()
        acc, carry = cache_block(2, list(acc), carry)
        return block - 1, alive(carry), tuple(acc), carry

    _, _, acc, _ = lax.while_loop(cond, body, (n_blocks - 2, alive(carry), tuple(acc), carry))
    for h in range(nh):
        o_ref[:, h * d:(h + 1) * d] = acc[h].astype(o_ref.dtype)


def _sb_sample(q16, k16, v16, cache_k, cache_v, *, t):
    h, m, d = q16.shape
    nb = m // t
    past = cache_k.shape[2]
    assert past % SB_BLOCK == 0
    new_spec = pl.BlockSpec((h, t, d), lambda b: (0, b, 0))
    cache_spec = pl.BlockSpec(memory_space=pl.ANY)
    return pl.pallas_call(
        functools.partial(_sb_sample_kernel, past=past),
        out_shape=jax.ShapeDtypeStruct((m, h * d), BF16),
        grid=(nb,),
        in_specs=[new_spec, new_spec, new_spec, cache_spec, cache_spec],
        out_specs=pl.BlockSpec((t, h * d), lambda b: (b, 0)),
        scratch_shapes=[pltpu.VMEM((3, h, SB_BLOCK, d), F32),
                        pltpu.VMEM((3, h, SB_BLOCK, d), F32),
                        pltpu.SemaphoreType.DMA((2, 3))],
        compiler_params=_cparams(("arbitrary",)),
        name="sb_sample",
    )(q16, k16, v16, cache_k, cache_v)


def _ret_log_decay(hd):
    return math.log(1.0 - 2.0 ** (-5.0 - hd))


def _ret_kernel(q_ref, k_ref, v_ref, g_ref, nw_ref, s0_ref, r_ref, st_ref,
                intra_ref, qdec_ref, kdec_ref, *, chunk):
    c = pl.program_id(1)

    @pl.when(jnp.logical_and(pl.program_id(0) == 0, c == 0))
    def _():
        row = lax.broadcasted_iota(jnp.int32, (chunk, chunk), 0)
        col = lax.broadcasted_iota(jnp.int32, (chunk, chunk), 1)
        diff = (row - col).astype(F32)
        pos = lax.broadcasted_iota(jnp.int32, (chunk, DK_RET), 0).astype(F32)
        for hd in range(H_RET):
            lg = _ret_log_decay(hd)
            intra_ref[hd] = jnp.where(row >= col, jnp.exp(lg * jnp.maximum(diff, 0.0)), 0.0)
            qdec_ref[hd] = jnp.exp(lg * (pos + 1.0))
            kdec_ref[hd] = jnp.exp(lg * (chunk - 1.0 - pos))

    @pl.when(c == 0)
    def _():
        st_ref[...] = s0_ref[...]

    for hd in range(H_RET):
        intra, q_dec, k_dec = intra_ref[hd], qdec_ref[hd], kdec_ref[hd]
        c_dec = math.exp(_ret_log_decay(hd) * chunk)
        q = q_ref[hd]
        k = k_ref[hd]
        v = v_ref[hd]
        state = st_ref[hd]
        scores = (_dot_nt(q, k.astype(BF16)) * intra).astype(BF16)
        o = _dot(scores, v) + _dot(q, state.astype(BF16)) * q_dec
        st_ref[hd] = c_dec * state + _dot_tn((k * k_dec).astype(BF16), v)
        o = o * lax.rsqrt(jnp.mean(o * o, axis=-1, keepdims=True) + EPS)
        sl = slice(hd * DV_RET, (hd + 1) * DV_RET)
        gate = g_ref[:, sl]
        r_ref[:, sl] = (o * nw_ref[:, sl] * (gate * jax.nn.sigmoid(gate))).astype(r_ref.dtype)


def _retention(rq, rk, rv, gate, norm_w, state0, *, chunk):
    h, m, _ = rq.shape
    nb = state0.shape[0]
    nc = m // (nb * chunk)
    qkv_spec = pl.BlockSpec((h, chunk, DK_RET), lambda b, c: (0, b * nc + c, 0))
    st_spec = pl.BlockSpec((None, h, DK_RET, DV_RET), lambda b, c: (b, 0, 0, 0))
    return pl.pallas_call(
        functools.partial(_ret_kernel, chunk=chunk),
        out_shape=(jax.ShapeDtypeStruct((m, h * DV_RET), BF16),
                   jax.ShapeDtypeStruct(state0.shape, F32)),
        grid=(nb, nc),
        in_specs=[qkv_spec, qkv_spec, qkv_spec,
                  pl.BlockSpec((chunk, h * DV_RET), lambda b, c: (b * nc + c, 0)),
                  pl.BlockSpec((1, h * DV_RET), lambda b, c: (0, 0)),
                  st_spec],
        out_specs=(pl.BlockSpec((chunk, h * DV_RET), lambda b, c: (b * nc + c, 0)), st_spec),
        scratch_shapes=[pltpu.VMEM((h, chunk, chunk), F32),
                        pltpu.VMEM((h, chunk, DK_RET), F32),
                        pltpu.VMEM((h, chunk, DK_RET), F32)],
        compiler_params=_cparams(("arbitrary", "arbitrary")),
        name="retention",
    )(rq, rk, rv, gate, norm_w, state0)


def _outproj_kernel(x_ref, a_ref, b_ref, wa_ref, wb_ref, o_ref):
    o_ref[...] = x_ref[...] + _dot(a_ref[...], wa_ref[...]) + _dot(b_ref[...], wb_ref[...])


def _outproj(x, a_sb, a_ret, w_out, *, tm):
    m = x.shape[0]
    row = lambda i: (i, 0)
    return pl.pallas_call(
        _outproj_kernel,
        out_shape=jax.ShapeDtypeStruct((m, D_MODEL), F32),
        grid=(m // tm,),
        in_specs=[
            pl.BlockSpec((tm, D_MODEL), row),
            pl.BlockSpec((tm, SB_W), row),
            pl.BlockSpec((tm, RET_W), row),
            pl.BlockSpec((SB_W, D_MODEL), lambda i: (0, 0)),
            pl.BlockSpec((RET_W, D_MODEL), lambda i: (1, 0)),
        ],
        out_specs=pl.BlockSpec((tm, D_MODEL), row),
        compiler_params=_cparams(("arbitrary",)),
        name="outproj",
    )(x, a_sb, a_ret, w_out, w_out)


OUTPROJ_TK = 512


def _outproj_f32w_kernel(x_ref, a_ref, w_ref, o_ref, w16_ref):
    @pl.when(pl.program_id(0) == 0)
    def _():
        o_ref[...] = x_ref[...]

    w16_ref[...] = w_ref[...].astype(BF16)
    o_ref[...] += _dot(a_ref[...], w16_ref[...])


def _outproj_f32w(x, a, w_out):
    m = x.shape[0]
    d_mix = w_out.shape[0]
    return pl.pallas_call(
        _outproj_f32w_kernel,
        out_shape=(jax.ShapeDtypeStruct((m, D_MODEL), F32),
                   jax.ShapeDtypeStruct(w_out.shape, BF16)),
        grid=(d_mix // OUTPROJ_TK,),
        in_specs=[
            pl.BlockSpec((m, D_MODEL), lambda k: (0, 0)),
            pl.BlockSpec((m, OUTPROJ_TK), lambda k: (0, k)),
            pl.BlockSpec((OUTPROJ_TK, D_MODEL), lambda k: (k, 0)),
        ],
        out_specs=(pl.BlockSpec((m, D_MODEL), lambda k: (0, 0)),
                   pl.BlockSpec((OUTPROJ_TK, D_MODEL), lambda k: (k, 0))),
        compiler_params=_cparams(("arbitrary",)),
        name="outproj_f32w",
    )(x, a, w_out)


def _layer(x, weights, sb_fn, state0, *, tm, ffn_tm, proj_tm, proj_split, pos_base, pos_mod, chunk):
    (n1, wg1, wu1, wd1, nmix, w_in, qn, kn, ron, w_out, n2, wg2, wu2, wd2, nf, inv_freq) = weights
    f32w = w_in.dtype == F32
    x1 = _ffn(x, n1, wg1, wu1, wd1, tm=ffn_tm)
    if f32w:
        x1, (wg1, wu1, wd1) = x1
    proj = _inproj(x1, nmix, w_in, qn, kn, inv_freq, tm=proj_tm, n_split=proj_split,
                   pos_base=pos_base, pos_mod=pos_mod)
    if f32w:
        proj, w_in = proj
    sq, sk, sk16, sv, sv16, rq, rk, rv, gate = proj
    a_sb = sb_fn(sq, sk16, sv16)
    a_ret, state = _retention(rq, rk, rv, gate, ron, state0, chunk=chunk)
    if f32w:
        x2, w_out = _outproj_f32w(x1, jnp.concatenate([a_sb, a_ret], axis=1), w_out)
    else:
        x2 = _outproj(x1, a_sb, a_ret, w_out, tm=tm)
    y = _ffn(x2, n2, wg2, wu2, wd2, nf, tm=ffn_tm)
    if f32w:
        y, (wg2, wu2, wd2) = y
    weights16 = (n1, wg1, wu1, wd1, nmix, w_in, qn, kn, ron, w_out, n2, wg2, wu2, wd2, nf, inv_freq)
    return y, sk, sv, state, weights16


def kernel(x_prompt, x_sample, cache_sb_k, cache_sb_v, state_ret, ffn1_norm, ffn1_w_gate, ffn1_w_up, ffn1_w_down, mix_norm, w_in, sb_q_norm, sb_k_norm, ret_out_norm, w_out, ffn2_norm, ffn2_w_gate, ffn2_w_up, ffn2_w_down, final_norm):
    depth = ffn1_norm.shape[0]
    assert depth == 1
    nb_p, seq, _ = x_prompt.shape
    nb_s, dec_seq, _ = x_sample.shape
    past = cache_sb_k.shape[3]
    assert nb_p == 1

    half = DK_RET // 2
    inv_freq = (ROPE_BASE ** (-jnp.arange(half, dtype=F32) / half)).reshape(1, half)
    l = 0
    weights = (ffn1_norm[l][None], ffn1_w_gate[l], ffn1_w_up[l], ffn1_w_down[l],
               mix_norm[l][None], w_in[l], sb_q_norm[l][None], sb_k_norm[l][None],
               ret_out_norm[l][None], w_out[l], ffn2_norm[l][None], ffn2_w_gate[l],
               ffn2_w_up[l], ffn2_w_down[l], final_norm[l][None], inv_freq)

    xs = x_sample.reshape(nb_s * dec_seq, D_MODEL)
    sb_s = functools.partial(_sb_sample, cache_k=cache_sb_k[l], cache_v=cache_sb_v[l], t=dec_seq)
    ys, sks, svs, sts, weights16 = _layer(xs, weights, sb_s, state_ret[l],
                                          tm=nb_s * dec_seq, ffn_tm=nb_s * dec_seq,
                                          proj_tm=nb_s * dec_seq, proj_split=1, pos_base=past,
                                          pos_mod=dec_seq, chunk=dec_seq)

    xp = x_prompt.reshape(seq, D_MODEL)
    zero_state = jnp.zeros((1, H_RET, DK_RET, DV_RET), F32)
    yp, skp, svp, stp, _ = _layer(xp, weights16, _sb_prompt, zero_state,
                                  tm=512, ffn_tm=1024, proj_tm=1024, proj_split=2,
                                  pos_base=0, pos_mod=seq, chunk=RET_CHUNK)

    def cache_layout(t):
        return t.reshape(H_SB, nb_s, dec_seq, D_SB).transpose(1, 0, 2, 3)[None]

    return (yp.reshape(1, seq, D_MODEL), ys.reshape(nb_s, dec_seq, D_MODEL),
            skp[None, None], svp[None, None], stp[None],
            cache_layout(sks), cache_layout(svs), sts[None])
```

```python
import functools
import math

import jax
import jax.numpy as jnp
from jax import lax
from jax.experimental import pallas as pl
from jax.experimental.pallas import tpu as pltpu

F32 = jnp.float32
BF16 = jnp.bfloat16

D_MODEL = 2048
D_FF = 5632
H_SB = 8
D_SB = 128
H_RET = 4
DK_RET = 256
DV_RET = 256
SB_W = H_SB * D_SB
RET_W = H_RET * DK_RET
N_SECTIONS = 7
SECTION_W = 1024
ROPE_BASE = 10000.0
EPS = 1e-6

V7X_VMEM_LIMIT_BYTES = 58 * 1024 * 1024
INPROJ_RING = 3
SB_BLOCK = 256
RET_CHUNK = 256
FFN_TF = 512

LOG2E = 1.4426950408889634
LN2 = 0.6931471805599453


def _cparams(sem):
    return pltpu.CompilerParams(dimension_semantics=sem,
                                vmem_limit_bytes=V7X_VMEM_LIMIT_BYTES)


def _rmsnorm_rows(x, w):
    ms = jnp.mean(x * x, axis=-1, keepdims=True)
    return x * lax.rsqrt(ms + EPS) * w


def _dot(a, b):
    return jnp.dot(a, b, preferred_element_type=F32)


def _dot_nt(a, b):
    return lax.dot_general(a, b, (((1,), (1,)), ((), ())), preferred_element_type=F32)


def _dot_tn(a, b):
    return lax.dot_general(a, b, (((0,), (0,)), ((), ())), preferred_element_type=F32)


def _ffn_kernel(*refs, n_steps, final_norm, emit_bf16, lookahead):
    x_ref, nw_ref, wg_ref, wu_ref, wd_ref = refs[:5]
    refs = refs[5:]
    fw_ref = None
    if final_norm:
        fw_ref, refs = refs[0], refs[1:]
    o_ref, refs = refs[0], refs[1:]
    if emit_bf16:
        wg16_ref, wu16_ref, wd16_ref, h_ref = refs
        wg16_ref[...] = wg_ref[...].astype(BF16)
        wu16_ref[...] = wu_ref[...].astype(BF16)
        wd16_ref[...] = wd_ref[...].astype(BF16)
        wg_ref, wu_ref, wd_ref = wg16_ref, wu16_ref, wd16_ref
    else:
        (h_ref,) = refs
    i = pl.program_id(0)
    j = pl.program_id(1)
    last = n_steps - 1
    slot = lax.rem(i, 2) if lookahead else 0

    def normed_input():
        return _rmsnorm_rows(x_ref[...], nw_ref[...]).astype(BF16)

    @pl.when(jnp.logical_and(j == 0, i == 0) if lookahead else j == 0)
    def _():
        h_ref[slot] = normed_input()

    def down_proj():
        h = h_ref[slot]
        g = _dot(h, wg_ref[...])
        u = _dot(h, wu_ref[...])
        a = (g * jax.nn.sigmoid(g) * (0.5 * u)).astype(BF16)
        return _dot(a, wd_ref[...])

    @pl.when(j == 0)
    def _():
        o_ref[...] = x_ref[...] + down_proj()

    @pl.when(jnp.logical_and(j > 0, j < last))
    def _():
        o_ref[...] += down_proj()

    @pl.when(j == last)
    def _():
        if lookahead:
            h_ref[1 - slot] = normed_input()
        o_ref[...] += down_proj()
        if final_norm:
            o_ref[...] = _rmsnorm_rows(o_ref[...], fw_ref[...])


def _ffn(x, norm_w, wg, wu, wd, final_w=None, *, tm):
    m = x.shape[0]
    n_steps = D_FF // FFN_TF
    n_tiles = m // tm
    assert n_steps >= 2
    emit_bf16 = wg.dtype == F32
    assert not emit_bf16 or m == tm
    lookahead = n_tiles > 1
    w_specs = [
        pl.BlockSpec((D_MODEL, FFN_TF), lambda i, j: (0, j)),
        pl.BlockSpec((D_MODEL, FFN_TF), lambda i, j: (0, j)),
        pl.BlockSpec((FFN_TF, D_MODEL), lambda i, j: (j, 0)),
    ]
    if lookahead:
        x_spec = pl.BlockSpec(
            (tm, D_MODEL),
            lambda i, j: (jnp.minimum(i + (j == n_steps - 1).astype(jnp.int32), n_tiles - 1), 0))
    else:
        x_spec = pl.BlockSpec((tm, D_MODEL), lambda i, j: (i, 0))
    in_specs = [x_spec, pl.BlockSpec((1, D_MODEL), lambda i, j: (0, 0))] + w_specs
    args = [x, norm_w, wg, wu, wd]
    if final_w is not None:
        in_specs.append(pl.BlockSpec((1, D_MODEL), lambda i, j: (0, 0)))
        args.append(final_w)
    out_shape = [jax.ShapeDtypeStruct((m, D_MODEL), F32)]
    out_specs = [pl.BlockSpec((tm, D_MODEL), lambda i, j: (i, 0))]
    if emit_bf16:
        out_shape += [jax.ShapeDtypeStruct(w.shape, BF16) for w in (wg, wu, wd)]
        out_specs += w_specs
    out = pl.pallas_call(
        functools.partial(_ffn_kernel, n_steps=n_steps, final_norm=final_w is not None,
                          emit_bf16=emit_bf16, lookahead=lookahead),
        out_shape=out_shape,
        grid=(n_tiles, n_steps),
        in_specs=in_specs,
        out_specs=out_specs,
        scratch_shapes=[pltpu.VMEM((2 if lookahead else 1, tm, D_MODEL), BF16)],
        compiler_params=_cparams(("arbitrary", "arbitrary")),
        name="ffn_final" if final_w is not None else "ffn",
    )(*args)
    return (out[0], tuple(out[1:])) if emit_bf16 else out[0]


def _inproj_kernel(x_ref, nw_ref, w_ref, qn_ref, kn_ref, inv_ref,
                   sq_ref, sk_ref, sk16_ref, sv_ref, sv16_ref,
                   rq_ref, rk_ref, rv_ref, g_ref, *rest,
                   tm, n_split, tile_stride, pos_base, pos_mod, emit_bf16):
    i = pl.program_id(0)
    j = pl.program_id(1)
    if emit_bf16:
        w16_ref, h_ref, cos_ref, sin_ref, cos_row_ref, sin_row_ref = rest
        w16_ref[...] = w_ref[...].astype(BF16)
        w_tile = w16_ref
    else:
        h_ref, cos_ref, sin_ref, cos_row_ref, sin_row_ref, wbuf, wsem = rest
        n_col = pl.num_programs(1)
        step = i * n_col + j
        tn = wbuf.shape[2]

        def tile_copy(s):
            col = pl.multiple_of(lax.rem(s, n_col) * tn, tn)
            ring = lax.rem(s, INPROJ_RING)
            return pltpu.make_async_copy(w_ref.at[:, pl.ds(col, tn)], wbuf.at[ring], wsem.at[ring])

        @pl.when(step == 0)
        def _():
            for s in range(INPROJ_RING - 1):
                tile_copy(s).start()

        @pl.when(step + INPROJ_RING - 1 < pl.num_programs(0) * n_col)
        def _():
            tile_copy(step + INPROJ_RING - 1).start()

        tile_copy(step).wait()
        w_tile = wbuf.at[lax.rem(step, INPROJ_RING)]

    @pl.when(jnp.logical_and(i == 0, j == 0))
    def _():
        row = lax.broadcasted_iota(jnp.int32, (tm, DK_RET // 2), 0)
        ang = lax.rem(row, pos_mod).astype(F32) * inv_ref[...]
        cos_row_ref[...] = jnp.cos(ang)
        sin_row_ref[...] = jnp.sin(ang)

    @pl.when(j == 0)
    def _():
        h_ref[...] = _rmsnorm_rows(x_ref[...], nw_ref[...]).astype(BF16)
        ang = (pos_base + i * tile_stride).astype(F32) * inv_ref[...]
        ca, sa = jnp.cos(ang), jnp.sin(ang)
        cb, sb = cos_row_ref[...], sin_row_ref[...]
        cos_ref[...] = ca * cb - sa * sb
        sin_ref[...] = sa * cb + ca * sb

    section = j // n_split
    sb_heads = H_SB // n_split
    ret_heads = H_RET // n_split

    def proj():
        return _dot(h_ref[...], w_tile[...])

    def head(p, hd, width):
        return p[:, hd * width:(hd + 1) * width]

    @pl.when(section == 0)
    def _():
        p = proj()
        w = qn_ref[...] * (D_SB ** -0.5 * LOG2E)
        for hd in range(sb_heads):
            sq_ref[hd] = _rmsnorm_rows(head(p, hd, D_SB), w).astype(BF16)

    @pl.when(section == 1)
    def _():
        p = proj()
        for hd in range(sb_heads):
            k = _rmsnorm_rows(head(p, hd, D_SB), kn_ref[...])
            sk_ref[hd] = k
            sk16_ref[hd] = k.astype(BF16)

    @pl.when(section == 2)
    def _():
        p = proj()
        for hd in range(sb_heads):
            v = head(p, hd, D_SB)
            sv_ref[hd] = v
            sv16_ref[hd] = v.astype(BF16)

    def rotary(ph):
        half = DK_RET // 2
        x1, x2 = ph[:, :half], ph[:, half:]
        c, s = cos_ref[...], sin_ref[...]
        return jnp.concatenate([x1 * c - x2 * s, x1 * s + x2 * c], axis=-1)

    @pl.when(section == 3)
    def _():
        p = proj()
        for hd in range(ret_heads):
            rq_ref[hd] = rotary(head(p, hd, DK_RET)).astype(BF16)

    @pl.when(section == 4)
    def _():
        p = proj()
        for hd in range(ret_heads):
            rk_ref[hd] = rotary(head(p, hd, DK_RET)) * (DK_RET ** -0.5)

    @pl.when(section == 5)
    def _():
        p = proj()
        for hd in range(ret_heads):
            rv_ref[hd] = head(p, hd, DV_RET).astype(BF16)

    @pl.when(section == 6)
    def _():
        g_ref[...] = proj()


def _inproj(x, norm_w, w_in, qn, kn, inv_freq, *, tm, n_split, pos_base, pos_mod):
    m = x.shape[0]
    emit_bf16 = w_in.dtype == F32
    assert not emit_bf16 or m == tm
    tn = SECTION_W // n_split

    def part(section):
        return lambda j: jnp.clip(j - section * n_split, 0, n_split - 1)

    def head_spec(heads, width, section):
        pt = part(section)
        return pl.BlockSpec((heads // n_split, tm, width), lambda i, j: (pt(j), i, 0))

    gate_part = part(6)
    sb16 = jax.ShapeDtypeStruct((H_SB, m, D_SB), BF16)
    sb32 = jax.ShapeDtypeStruct((H_SB, m, D_SB), F32)
    ret16 = jax.ShapeDtypeStruct((H_RET, m, DK_RET), BF16)
    ret32 = jax.ShapeDtypeStruct((H_RET, m, DK_RET), F32)
    assert pos_mod >= m or tm % pos_mod == 0
    tile_stride = tm if pos_mod >= m else 0
    half = DK_RET // 2
    w_spec = pl.BlockSpec((D_MODEL, tn), lambda i, j: (0, j))
    out_shape = [sb16, sb32, sb16, sb32, sb16, ret16, ret32, ret16,
                 jax.ShapeDtypeStruct((m, SECTION_W), F32)]
    out_specs = [head_spec(H_SB, D_SB, 0), head_spec(H_SB, D_SB, 1), head_spec(H_SB, D_SB, 1),
                 head_spec(H_SB, D_SB, 2), head_spec(H_SB, D_SB, 2),
                 head_spec(H_RET, DK_RET, 3), head_spec(H_RET, DK_RET, 4),
                 head_spec(H_RET, DK_RET, 5),
                 pl.BlockSpec((tm, tn), lambda i, j: (i, gate_part(j)))]
    scratch = [pltpu.VMEM((tm, D_MODEL), BF16)] + [pltpu.VMEM((tm, half), F32)] * 4
    if emit_bf16:
        out_shape.append(jax.ShapeDtypeStruct(w_in.shape, BF16))
        out_specs.append(w_spec)
    else:
        w_spec = pl.BlockSpec(memory_space=pl.ANY)
        scratch += [pltpu.VMEM((INPROJ_RING, D_MODEL, tn), BF16),
                    pltpu.SemaphoreType.DMA((INPROJ_RING,))]
    out = pl.pallas_call(
        functools.partial(_inproj_kernel, tm=tm, n_split=n_split, tile_stride=tile_stride,
                          pos_base=pos_base, pos_mod=pos_mod, emit_bf16=emit_bf16),
        out_shape=out_shape,
        grid=(m // tm, N_SECTIONS * n_split),
        in_specs=[
            pl.BlockSpec((tm, D_MODEL), lambda i, j: (i, 0)),
            pl.BlockSpec((1, D_MODEL), lambda i, j: (0, 0)),
            w_spec,
            pl.BlockSpec((1, D_SB), lambda i, j: (0, 0)),
            pl.BlockSpec((1, D_SB), lambda i, j: (0, 0)),
            pl.BlockSpec((1, half), lambda i, j: (0, 0)),
        ],
        out_specs=out_specs,
        scratch_shapes=scratch,
        compiler_params=_cparams(("arbitrary", "arbitrary")),
        name="inproj",
    )(x, norm_w, w_in, qn, kn, inv_freq)
    return (tuple(out[:9]), out[9]) if emit_bf16 else tuple(out)


def _softplus2(z):
    return jnp.maximum(z, 0.0) + jnp.log2(1.0 + jnp.exp2(-jnp.abs(z)))


def _suffix_matrix(n):
    r = lax.broadcasted_iota(jnp.int32, (2 * n, n), 0)
    c = lax.broadcasted_iota(jnp.int32, (2 * n, n), 1)
    return jnp.where(jnp.where(r >= n, r - n, r) >= c, 1.0, 0.0).astype(BF16)


def _sb_block(q, k, v, carry, tri, mask):
    z = _dot_nt(q, k)
    sp = _softplus2(z)
    if mask is not None:
        sp = jnp.where(mask, sp, 0.0)
    hi = sp.astype(BF16)
    lo = (sp - hi.astype(F32)).astype(BF16)
    n = hi.shape[1]
    if n % 128 == 0:
        incl = _dot(jnp.concatenate([hi, lo], axis=1), tri) + carry
    else:
        incl = _dot(hi, tri[:n]) + _dot(lo, tri[:n]) + carry
    a = jnp.exp2(z - incl)
    if mask is not None:
        a = jnp.where(mask, a, 0.0)
    return _dot(a.astype(BF16), v), incl[:, 0:1]


SB_DEAD_CARRY = 150.0


SB_HEADS_PER_STEP = 4


def _sb_prompt_kernel(q_ref, k_ref, v_ref, o_ref):
    qi = pl.program_id(1)
    heads, _, d = q_ref.shape
    blk = SB_BLOCK
    tri = _suffix_matrix(blk)
    r = lax.broadcasted_iota(jnp.int32, (blk, blk), 0)
    c = lax.broadcasted_iota(jnp.int32, (blk, blk), 1)
    strict = c < r
    zero = jnp.zeros((blk, 1), F32)
    has_prev = qi > 0

    def kv(g, block):
        s0 = pl.multiple_of(block * blk, blk)
        return k_ref[g, pl.ds(s0, blk), :], v_ref[g, pl.ds(s0, blk), :]

    def start(g):
        q = q_ref[g]
        k0, v0 = kv(g, qi)
        acc, carry = _sb_block(q, k0, v0, zero, tri, strict)
        kp, vp = kv(g, jnp.maximum(qi - 1, 0))
        d_p, carry_p = _sb_block(q, kp, vp, carry, tri, None)
        return q, acc + jnp.where(has_prev, d_p, 0.0), jnp.where(has_prev, carry_p, carry)

    qs, accs, carries = zip(*[start(g) for g in range(heads)])

    def alive(carries):
        low = functools.reduce(jnp.minimum, carries)
        return (jnp.min(low) < SB_DEAD_CARRY).astype(jnp.int32)

    def cond(state):
        t, live, _, _ = state
        return jnp.logical_and(t >= 0, live > 0)

    def body(state):
        t, _, accs, carries = state
        new_accs, new_carries = [], []
        for g in range(heads):
            kb, vb = kv(g, t)
            d_g, carry_g = _sb_block(qs[g], kb, vb, carries[g], tri, None)
            new_accs.append(accs[g] + d_g)
            new_carries.append(carry_g)
        return t - 1, alive(new_carries), tuple(new_accs), tuple(new_carries)

    _, _, accs, _ = lax.while_loop(cond, body, (qi - 2, alive(carries), accs, carries))
    for g in range(heads):
        o_ref[:, g * d:(g + 1) * d] = accs[g].astype(o_ref.dtype)


def _sb_prompt(q16, k16, v16):
    h, s, d = q16.shape
    tq = SB_BLOCK
    g = SB_HEADS_PER_STEP
    return pl.pallas_call(
        _sb_prompt_kernel,
        out_shape=jax.ShapeDtypeStruct((s, h * d), BF16),
        grid=(h // g, s // tq),
        in_specs=[
            pl.BlockSpec((g, tq, d), lambda hg, qi: (hg, qi, 0)),
            pl.BlockSpec((g, s, d), lambda hg, qi: (hg, 0, 0)),
            pl.BlockSpec((g, s, d), lambda hg, qi: (hg, 0, 0)),
        ],
        out_specs=pl.BlockSpec((tq, g * d), lambda hg, qi: (qi, hg)),
        compiler_params=_cparams(("arbitrary", "arbitrary")),
        name="sb_prompt",
    )(q16, k16, v16)


def _sb_sample_kernel(q_ref, kn_ref, vn_ref, kc_hbm, vc_hbm, o_ref, kbuf, vbuf, sem, *, past):
    nh, t, d = q_ref.shape
    ht = nh * t
    blk = SB_BLOCK
    n_blocks = past // blk
    b = pl.program_id(0)
    slot = lax.rem(b, 2)

    def cache_copies(batch, block, to_slot):
        rows = pl.ds(pl.multiple_of(block * blk, blk), blk)
        return (pltpu.make_async_copy(kc_hbm.at[batch, :, rows, :], kbuf.at[to_slot], sem.at[0, to_slot]),
                pltpu.make_async_copy(vc_hbm.at[batch, :, rows, :], vbuf.at[to_slot], sem.at[1, to_slot]))

    @pl.when(b == 0)
    def _():
        for cp in cache_copies(0, n_blocks - 1, 0):
            cp.start()

    @pl.when(b + 1 < pl.num_programs(0))
    def _():
        for cp in cache_copies(b + 1, n_blocks - 1, 1 - slot):
            cp.start()

    q_all = q_ref[...].reshape(ht, d).astype(F32)
    q_head = lax.broadcasted_iota(jnp.int32, (ht, d), 0) // t
    q_masked = [jnp.where(q_head == h, q_all, 0.0).astype(BF16) for h in range(nh)]

    def logits(keys_of_head):
        z = _dot_nt(keys_of_head(0), q_masked[0])
        for h in range(1, nh):
            z = z + _dot_nt(keys_of_head(h), q_masked[h])
        return z

    def suffix_matrix(n):
        r = lax.broadcasted_iota(jnp.int32, (n, 2 * n), 0)
        c = lax.broadcasted_iota(jnp.int32, (n, 2 * n), 1)
        return jnp.where(jnp.where(c >= n, c - n, c) >= r, 1.0, 0.0).astype(BF16)

    def suffix_sum(sp, lmat):
        hi = sp.astype(BF16)
        lo = (sp - hi.astype(F32)).astype(BF16)
        return _dot(lmat, jnp.concatenate([hi, lo], axis=0))

    def emit(a_t, values_of_head, acc):
        a = a_t.T.astype(BF16)
        return [acc[h] + _dot(a[h * t:(h + 1) * t, :], values_of_head(h)) for h in range(nh)]

    z = logits(lambda h: kn_ref[h])
    s_idx = lax.broadcasted_iota(jnp.int32, (t, ht), 0)
    t_idx = lax.broadcasted_iota(jnp.int32, (t, ht), 1) % t
    strict = s_idx < t_idx
    sp = jnp.where(strict, _softplus2(z), 0.0)
    incl = suffix_sum(sp, suffix_matrix(t))
    a_t = jnp.where(strict, jnp.exp2(z - incl), 0.0)
    acc = emit(a_t, lambda h: vn_ref[h], [jnp.zeros((t, d), F32) for _ in range(nh)])
    carry = incl[0:1, :]

    lmat = suffix_matrix(blk)

    def cache_block(from_slot, acc, carry):
        z = logits(lambda h: kbuf[from_slot, h].astype(BF16))
        incl = suffix_sum(_softplus2(z), lmat) + carry
        acc = emit(jnp.exp2(z - incl), lambda h: vbuf[from_slot, h].astype(BF16), acc)
        return acc, incl[0:1, :]

    def alive(carry):
        return (jnp.min(carry) < SB_DEAD_CARRY).astype(jnp.int32)

    for cp in cache_copies(b, n_blocks - 1, slot):
        cp.wait()
    acc, carry = cache_block(slot, acc, carry)

    def cond(state):
        block, live, _, _ = state
        return jnp.logical_and(block >= 0, live > 0)

    def body(state):
        block, _, acc, carry = state
        copies = cache_copies(b, block, 2)
        for cp in copies:
            cp.start()
        for cp in copies:
            cp.wait()
        acc, carry = cache_block(2, list(acc), carry)
        return block - 1, alive(carry), tuple(acc), carry

    _, _, acc, _ = lax.while_loop(cond, body, (n_blocks - 2, alive(carry), tuple(acc), carry))
    for h in range(nh):
        o_ref[:, h * d:(h + 1) * d] = acc[h].astype(o_ref.dtype)


def _sb_sample(q16, k16, v16, cache_k, cache_v, *, t):
    h, m, d = q16.shape
    nb = m // t
    past = cache_k.shape[2]
    assert past % SB_BLOCK == 0
    new_spec = pl.BlockSpec((h, t, d), lambda b: (0, b, 0))
    cache_spec = pl.BlockSpec(memory_space=pl.ANY)
    return pl.pallas_call(
        functools.partial(_sb_sample_kernel, past=past),
        out_shape=jax.ShapeDtypeStruct((m, h * d), BF16),
        grid=(nb,),
        in_specs=[new_spec, new_spec, new_spec, cache_spec, cache_spec],
        out_specs=pl.BlockSpec((t, h * d), lambda b: (b, 0)),
        scratch_shapes=[pltpu.VMEM((3, h, SB_BLOCK, d), F32),
                        pltpu.VMEM((3, h, SB_BLOCK, d), F32),
                        pltpu.SemaphoreType.DMA((2, 3))],
        compiler_params=_cparams(("arbitrary",)),
        name="sb_sample",
    )(q16, k16, v16, cache_k, cache_v)


def _ret_log_decay(hd):
    return math.log(1.0 - 2.0 ** (-5.0 - hd))


def _ret_kernel(q_ref, k_ref, v_ref, g_ref, nw_ref, s0_ref, r_ref, st_ref,
                intra_ref, qdec_ref, kdec_ref, *, chunk):
    c = pl.program_id(1)

    @pl.when(jnp.logical_and(pl.program_id(0) == 0, c == 0))
    def _():
        row = lax.broadcasted_iota(jnp.int32, (chunk, chunk), 0)
        col = lax.broadcasted_iota(jnp.int32, (chunk, chunk), 1)
        diff = (row - col).astype(F32)
        pos = lax.broadcasted_iota(jnp.int32, (chunk, DK_RET), 0).astype(F32)
        for hd in range(H_RET):
            lg = _ret_log_decay(hd)
            intra_ref[hd] = jnp.where(row >= col, jnp.exp(lg * jnp.maximum(diff, 0.0)), 0.0)
            qdec_ref[hd] = jnp.exp(lg * (pos + 1.0))
            kdec_ref[hd] = jnp.exp(lg * (chunk - 1.0 - pos))

    @pl.when(c == 0)
    def _():
        st_ref[...] = s0_ref[...]

    for hd in range(H_RET):
        intra, q_dec, k_dec = intra_ref[hd], qdec_ref[hd], kdec_ref[hd]
        c_dec = math.exp(_ret_log_decay(hd) * chunk)
        q = q_ref[hd]
        k = k_ref[hd]
        v = v_ref[hd]
        state = st_ref[hd]
        scores = (_dot_nt(q, k.astype(BF16)) * intra).astype(BF16)
        o = _dot(scores, v) + _dot(q, state.astype(BF16)) * q_dec
        st_ref[hd] = c_dec * state + _dot_tn((k * k_dec).astype(BF16), v)
        o = o * lax.rsqrt(jnp.mean(o * o, axis=-1, keepdims=True) + EPS)
        sl = slice(hd * DV_RET, (hd + 1) * DV_RET)
        gate = g_ref[:, sl]
        r_ref[:, sl] = (o * nw_ref[:, sl] * (gate * jax.nn.sigmoid(gate))).astype(r_ref.dtype)


def _retention(rq, rk, rv, gate, norm_w, state0, *, chunk):
    h, m, _ = rq.shape
    nb = state0.shape[0]
    nc = m // (nb * chunk)
    qkv_spec = pl.BlockSpec((h, chunk, DK_RET), lambda b, c: (0, b * nc + c, 0))
    st_spec = pl.BlockSpec((None, h, DK_RET, DV_RET), lambda b, c: (b, 0, 0, 0))
    return pl.pallas_call(
        functools.partial(_ret_kernel, chunk=chunk),
        out_shape=(jax.ShapeDtypeStruct((m, h * DV_RET), BF16),
                   jax.ShapeDtypeStruct(state0.shape, F32)),
        grid=(nb, nc),
        in_specs=[qkv_spec, qkv_spec, qkv_spec,
                  pl.BlockSpec((chunk, h * DV_RET), lambda b, c: (b * nc + c, 0)),
                  pl.BlockSpec((1, h * DV_RET), lambda b, c: (0, 0)),
                  st_spec],
        out_specs=(pl.BlockSpec((chunk, h * DV_RET), lambda b, c: (b * nc + c, 0)), st_spec),
        scratch_shapes=[pltpu.VMEM((h, chunk, chunk), F32),
                        pltpu.VMEM((h, chunk, DK_RET), F32),
                        pltpu.VMEM((h, chunk, DK_RET), F32)],
        compiler_params=_cparams(("arbitrary", "arbitrary")),
        name="retention",
    )(rq, rk, rv, gate, norm_w, state0)


def _outproj_kernel(x_ref, a_ref, b_ref, wa_ref, wb_ref, o_ref):
    o_ref[...] = x_ref[...] + _dot(a_ref[...], wa_ref[...]) + _dot(b_ref[...], wb_ref[...])


def _outproj(x, a_sb, a_ret, w_out, *, tm):
    m = x.shape[0]
    row = lambda i: (i, 0)
    return pl.pallas_call(
        _outproj_kernel,
        out_shape=jax.ShapeDtypeStruct((m, D_MODEL), F32),
        grid=(m // tm,),
        in_specs=[
            pl.BlockSpec((tm, D_MODEL), row),
            pl.BlockSpec((tm, SB_W), row),
            pl.BlockSpec((tm, RET_W), row),
            pl.BlockSpec((SB_W, D_MODEL), lambda i: (0, 0)),
            pl.BlockSpec((RET_W, D_MODEL), lambda i: (1, 0)),
        ],
        out_specs=pl.BlockSpec((tm, D_MODEL), row),
        compiler_params=_cparams(("arbitrary",)),
        name="outproj",
    )(x, a_sb, a_ret, w_out, w_out)


OUTPROJ_TK = 512


def _outproj_f32w_kernel(x_ref, a_ref, w_ref, o_ref, w16_ref):
    @pl.when(pl.program_id(0) == 0)
    def _():
        o_ref[...] = x_ref[...]

    w16_ref[...] = w_ref[...].astype(BF16)
    o_ref[...] += _dot(a_ref[...], w16_ref[...])


def _outproj_f32w(x, a, w_out):
    m = x.shape[0]
    d_mix = w_out.shape[0]
    return pl.pallas_call(
        _outproj_f32w_kernel,
        out_shape=(jax.ShapeDtypeStruct((m, D_MODEL), F32),
                   jax.ShapeDtypeStruct(w_out.shape, BF16)),
        grid=(d_mix // OUTPROJ_TK,),
        in_specs=[
            pl.BlockSpec((m, D_MODEL), lambda k: (0, 0)),
            pl.BlockSpec((m, OUTPROJ_TK), lambda k: (0, k)),
            pl.BlockSpec((OUTPROJ_TK, D_MODEL), lambda k: (k, 0)),
        ],
        out_specs=(pl.BlockSpec((m, D_MODEL), lambda k: (0, 0)),
                   pl.BlockSpec((OUTPROJ_TK, D_MODEL), lambda k: (k, 0))),
        compiler_params=_cparams(("arbitrary",)),
        name="outproj_f32w",
    )(x, a, w_out)


def _layer(x, weights, sb_fn, state0, *, tm, ffn_tm, proj_tm, proj_split, pos_base, pos_mod, chunk):
    (n1, wg1, wu1, wd1, nmix, w_in, qn, kn, ron, w_out, n2, wg2, wu2, wd2, nf, inv_freq) = weights
    f32w = w_in.dtype == F32
    x1 = _ffn(x, n1, wg1, wu1, wd1, tm=ffn_tm)
    if f32w:
        x1, (wg1, wu1, wd1) = x1
    proj = _inproj(x1, nmix, w_in, qn, kn, inv_freq, tm=proj_tm, n_split=proj_split,
                   pos_base=pos_base, pos_mod=pos_mod)
    if f32w:
        proj, w_in = proj
    sq, sk, sk16, sv, sv16, rq, rk, rv, gate = proj
    a_sb = sb_fn(sq, sk16, sv16)
    a_ret, state = _retention(rq, rk, rv, gate, ron, state0, chunk=chunk)
    if f32w:
        x2, w_out = _outproj_f32w(x1, jnp.concatenate([a_sb, a_ret], axis=1), w_out)
    else:
        x2 = _outproj(x1, a_sb, a_ret, w_out, tm=tm)
    y = _ffn(x2, n2, wg2, wu2, wd2, nf, tm=ffn_tm)
    if f32w:
        y, (wg2, wu2, wd2) = y
    weights16 = (n1, wg1, wu1, wd1, nmix, w_in, qn, kn, ron, w_out, n2, wg2, wu2, wd2, nf, inv_freq)
    return y, sk, sv, state, weights16


def kernel(x_prompt, x_sample, cache_sb_k, cache_sb_v, state_ret, ffn1_norm, ffn1_w_gate, ffn1_w_up, ffn1_w_down, mix_norm, w_in, sb_q_norm, sb_k_norm, ret_out_norm, w_out, ffn2_norm, ffn2_w_gate, ffn2_w_up, ffn2_w_down, final_norm):
    depth = ffn1_norm.shape[0]
    assert depth == 1
    nb_p, seq, _ = x_prompt.shape
    nb_s, dec_seq, _ = x_sample.shape
    past = cache_sb_k.shape[3]
    assert nb_p == 1

    half = DK_RET // 2
    inv_freq = (ROPE_BASE ** (-jnp.arange(half, dtype=F32) / half)).reshape(1, half)
    l = 0
    weights = (ffn1_norm[l][None], ffn1_w_gate[l], ffn1_w_up[l], ffn1_w_down[l],
               mix_norm[l][None], w_in[l], sb_q_norm[l][None], sb_k_norm[l][None],
               ret_out_norm[l][None], w_out[l], ffn2_norm[l][None], ffn2_w_gate[l],
               ffn2_w_up[l], ffn2_w_down[l], final_norm[l][None], inv_freq)

    xs = x_sample.reshape(nb_s * dec_seq, D_MODEL)
    sb_s = functools.partial(_sb_sample, cache_k=cache_sb_k[l], cache_v=cache_sb_v[l], t=dec_seq)
    ys, sks, svs, sts, weights16 = _layer(xs, weights, sb_s, state_ret[l],
                                          tm=nb_s * dec_seq, ffn_tm=nb_s * dec_seq,
                                          proj_tm=nb_s * dec_seq, proj_split=1, pos_base=past,
                                          pos_mod=dec_seq, chunk=dec_seq)

    xp = x_prompt.reshape(seq, D_MODEL)
    zero_state = jnp.zeros((1, H_RET, DK_RET, DV_RET), F32)
    yp, skp, svp, stp, _ = _layer(xp, weights16, _sb_prompt, zero_state,
                                  tm=512, ffn_tm=1024, proj_tm=1024, proj_split=2,
                                  pos_base=0, pos_mod=seq, chunk=RET_CHUNK)

    def cache_layout(t):
        return t.reshape(H_SB, nb_s, dec_seq, D_SB).transpose(1, 0, 2, 3)[None]

    return (yp.reshape(1, seq, D_MODEL), ys.reshape(nb_s, dec_seq, D_MODEL),
            skp[None, None], svp[None, None], stp[None],
            cache_layout(sks), cache_layout(svs), sts[None])
```

```python
import functools
import math

import jax
import jax.numpy as jnp
from jax import lax
from jax.experimental import pallas as pl
from jax.experimental.pallas import tpu as pltpu

F32 = jnp.float32
BF16 = jnp.bfloat16

D_MODEL = 2048
D_FF = 5632
H_SB = 8
D_SB = 128
H_RET = 4
DK_RET = 256
DV_RET = 256
SB_W = H_SB * D_SB
RET_W = H_RET * DK_RET
N_SECTIONS = 7
SECTION_W = 1024
ROPE_BASE = 10000.0
EPS = 1e-6

V7X_VMEM_LIMIT_BYTES = 58 * 1024 * 1024
SB_BLOCK = 256
RET_CHUNK = 256
FFN_TF = 512

LOG2E = 1.4426950408889634
LN2 = 0.6931471805599453


def _cparams(sem):
    return pltpu.CompilerParams(dimension_semantics=sem,
                                vmem_limit_bytes=V7X_VMEM_LIMIT_BYTES)


def _rmsnorm_rows(x, w):
    ms = jnp.mean(x * x, axis=-1, keepdims=True)
    return x * lax.rsqrt(ms + EPS) * w


def _dot(a, b):
    return jnp.dot(a, b, preferred_element_type=F32)


def _dot_nt(a, b):
    return lax.dot_general(a, b, (((1,), (1,)), ((), ())), preferred_element_type=F32)


def _dot_tn(a, b):
    return lax.dot_general(a, b, (((0,), (0,)), ((), ())), preferred_element_type=F32)


def _ffn_kernel(*refs, n_steps, final_norm, emit_bf16):
    x_ref, nw_ref, wg_ref, wu_ref, wd_ref = refs[:5]
    refs = refs[5:]
    fw_ref = None
    if final_norm:
        fw_ref, refs = refs[0], refs[1:]
    o_ref, refs = refs[0], refs[1:]
    if emit_bf16:
        wg16_ref, wu16_ref, wd16_ref, h_ref = refs
        wg16_ref[...] = wg_ref[...].astype(BF16)
        wu16_ref[...] = wu_ref[...].astype(BF16)
        wd16_ref[...] = wd_ref[...].astype(BF16)
        wg_ref, wu_ref, wd_ref = wg16_ref, wu16_ref, wd16_ref
    else:
        (h_ref,) = refs
    j = pl.program_id(1)

    @pl.when(j == 0)
    def _():
        x = x_ref[...]
        h_ref[...] = _rmsnorm_rows(x, nw_ref[...]).astype(BF16)
        o_ref[...] = x

    h = h_ref[...]
    g = _dot(h, wg_ref[...])
    u = _dot(h, wu_ref[...])
    a = (g * jax.nn.sigmoid(g) * (0.5 * u)).astype(BF16)
    o_ref[...] += _dot(a, wd_ref[...])

    if final_norm:
        @pl.when(j == n_steps - 1)
        def _():
            o_ref[...] = _rmsnorm_rows(o_ref[...], fw_ref[...])


def _ffn(x, norm_w, wg, wu, wd, final_w=None, *, tm):
    m = x.shape[0]
    n_steps = D_FF // FFN_TF
    emit_bf16 = wg.dtype == F32
    assert not emit_bf16 or m == tm
    w_specs = [
        pl.BlockSpec((D_MODEL, FFN_TF), lambda i, j: (0, j)),
        pl.BlockSpec((D_MODEL, FFN_TF), lambda i, j: (0, j)),
        pl.BlockSpec((FFN_TF, D_MODEL), lambda i, j: (j, 0)),
    ]
    in_specs = [
        pl.BlockSpec((tm, D_MODEL), lambda i, j: (i, 0)),
        pl.BlockSpec((1, D_MODEL), lambda i, j: (0, 0)),
    ] + w_specs
    args = [x, norm_w, wg, wu, wd]
    if final_w is not None:
        in_specs.append(pl.BlockSpec((1, D_MODEL), lambda i, j: (0, 0)))
        args.append(final_w)
    out_shape = [jax.ShapeDtypeStruct((m, D_MODEL), F32)]
    out_specs = [pl.BlockSpec((tm, D_MODEL), lambda i, j: (i, 0))]
    if emit_bf16:
        out_shape += [jax.ShapeDtypeStruct(w.shape, BF16) for w in (wg, wu, wd)]
        out_specs += w_specs
    out = pl.pallas_call(
        functools.partial(_ffn_kernel, n_steps=n_steps, final_norm=final_w is not None,
                          emit_bf16=emit_bf16),
        out_shape=out_shape,
        grid=(m // tm, n_steps),
        in_specs=in_specs,
        out_specs=out_specs,
        scratch_shapes=[pltpu.VMEM((tm, D_MODEL), BF16)],
        compiler_params=_cparams(("arbitrary", "arbitrary")),
        name="ffn_final" if final_w is not None else "ffn",
    )(*args)
    return (out[0], tuple(out[1:])) if emit_bf16 else out[0]


def _inproj_kernel(x_ref, nw_ref, w_ref, qn_ref, kn_ref, inv_ref,
                   sq_ref, sk_ref, sk16_ref, sv_ref, sv16_ref,
                   rq_ref, rk_ref, rv_ref, g_ref, *rest,
                   tm, n_split, tile_stride, pos_base, pos_mod, emit_bf16):
    i = pl.program_id(0)
    j = pl.program_id(1)
    if emit_bf16:
        w16_ref, h_ref, p_ref, cos_ref, sin_ref, cos_row_ref, sin_row_ref = rest
        w16_ref[...] = w_ref[...].astype(BF16)
        w_ref = w16_ref
    else:
        h_ref, p_ref, cos_ref, sin_ref, cos_row_ref, sin_row_ref = rest

    @pl.when(jnp.logical_and(i == 0, j == 0))
    def _():
        row = lax.broadcasted_iota(jnp.int32, (tm, DK_RET // 2), 0)
        ang = lax.rem(row, pos_mod).astype(F32) * inv_ref[...]
        cos_row_ref[...] = jnp.cos(ang)
        sin_row_ref[...] = jnp.sin(ang)

    @pl.when(j == 0)
    def _():
        h_ref[...] = _rmsnorm_rows(x_ref[...], nw_ref[...]).astype(BF16)
        ang = (pos_base + i * tile_stride).astype(F32) * inv_ref[...]
        ca, sa = jnp.cos(ang), jnp.sin(ang)
        cb, sb = cos_row_ref[...], sin_row_ref[...]
        cos_ref[...] = ca * cb - sa * sb
        sin_ref[...] = sa * cb + ca * sb

    section = j // n_split
    sb_heads = H_SB // n_split
    ret_heads = H_RET // n_split

    p_ref[...] = _dot(h_ref[...], w_ref[...])

    def head(hd, width):
        return p_ref[:, hd * width:(hd + 1) * width]

    @pl.when(section == 0)
    def _():
        w = qn_ref[...] * (D_SB ** -0.5 * LOG2E)
        for hd in range(sb_heads):
            sq_ref[hd] = _rmsnorm_rows(head(hd, D_SB), w).astype(BF16)

    @pl.when(section == 1)
    def _():
        for hd in range(sb_heads):
            k = _rmsnorm_rows(head(hd, D_SB), kn_ref[...])
            sk_ref[hd] = k
            sk16_ref[hd] = k.astype(BF16)

    @pl.when(section == 2)
    def _():
        for hd in range(sb_heads):
            v = head(hd, D_SB)
            sv_ref[hd] = v
            sv16_ref[hd] = v.astype(BF16)

    def rotary(ph):
        half = DK_RET // 2
        x1, x2 = ph[:, :half], ph[:, half:]
        c, s = cos_ref[...], sin_ref[...]
        return jnp.concatenate([x1 * c - x2 * s, x1 * s + x2 * c], axis=-1)

    @pl.when(section == 3)
    def _():
        for hd in range(ret_heads):
            rq_ref[hd] = rotary(head(hd, DK_RET)).astype(BF16)

    @pl.when(section == 4)
    def _():
        for hd in range(ret_heads):
            rk_ref[hd] = rotary(head(hd, DK_RET)) * (DK_RET ** -0.5)

    @pl.when(section == 5)
    def _():
        for hd in range(ret_heads):
            rv_ref[hd] = head(hd, DV_RET).astype(BF16)

    @pl.when(section == 6)
    def _():
        g_ref[...] = p_ref[...]


def _inproj(x, norm_w, w_in, qn, kn, inv_freq, *, tm, n_split, pos_base, pos_mod):
    m = x.shape[0]
    emit_bf16 = w_in.dtype == F32
    assert not emit_bf16 or m == tm
    tn = SECTION_W // n_split

    def part(section):
        return lambda j: jnp.clip(j - section * n_split, 0, n_split - 1)

    def head_spec(heads, width, section):
        pt = part(section)
        return pl.BlockSpec((heads // n_split, tm, width), lambda i, j: (pt(j), i, 0))

    gate_part = part(6)
    sb16 = jax.ShapeDtypeStruct((H_SB, m, D_SB), BF16)
    sb32 = jax.ShapeDtypeStruct((H_SB, m, D_SB), F32)
    ret16 = jax.ShapeDtypeStruct((H_RET, m, DK_RET), BF16)
    ret32 = jax.ShapeDtypeStruct((H_RET, m, DK_RET), F32)
    assert pos_mod >= m or tm % pos_mod == 0
    tile_stride = tm if pos_mod >= m else 0
    half = DK_RET // 2
    w_spec = pl.BlockSpec((D_MODEL, tn), lambda i, j: (0, j))
    out_shape = [sb16, sb32, sb16, sb32, sb16, ret16, ret32, ret16,
                 jax.ShapeDtypeStruct((m, SECTION_W), F32)]
    out_specs = [head_spec(H_SB, D_SB, 0), head_spec(H_SB, D_SB, 1), head_spec(H_SB, D_SB, 1),
                 head_spec(H_SB, D_SB, 2), head_spec(H_SB, D_SB, 2),
                 head_spec(H_RET, DK_RET, 3), head_spec(H_RET, DK_RET, 4),
                 head_spec(H_RET, DK_RET, 5),
                 pl.BlockSpec((tm, tn), lambda i, j: (i, gate_part(j)))]
    scratch = ([pltpu.VMEM((tm, D_MODEL), BF16), pltpu.VMEM((tm, tn), F32)]
               + [pltpu.VMEM((tm, half), F32)] * 4)
    if emit_bf16:
        out_shape.append(jax.ShapeDtypeStruct(w_in.shape, BF16))
        out_specs.append(w_spec)
    out = pl.pallas_call(
        functools.partial(_inproj_kernel, tm=tm, n_split=n_split, tile_stride=tile_stride,
                          pos_base=pos_base, pos_mod=pos_mod, emit_bf16=emit_bf16),
        out_shape=out_shape,
        grid=(m // tm, N_SECTIONS * n_split),
        in_specs=[
            pl.BlockSpec((tm, D_MODEL), lambda i, j: (i, 0)),
            pl.BlockSpec((1, D_MODEL), lambda i, j: (0, 0)),
            w_spec,
            pl.BlockSpec((1, D_SB), lambda i, j: (0, 0)),
            pl.BlockSpec((1, D_SB), lambda i, j: (0, 0)),
            pl.BlockSpec((1, half), lambda i, j: (0, 0)),
        ],
        out_specs=out_specs,
        scratch_shapes=scratch,
        compiler_params=_cparams(("arbitrary", "arbitrary")),
        name="inproj",
    )(x, norm_w, w_in, qn, kn, inv_freq)
    return (tuple(out[:9]), out[9]) if emit_bf16 else tuple(out)


def _softplus2(z):
    return jnp.maximum(z, 0.0) + jnp.log2(1.0 + jnp.exp2(-jnp.abs(z)))


def _suffix_matrix(n):
    r = lax.broadcasted_iota(jnp.int32, (2 * n, n), 0)
    c = lax.broadcasted_iota(jnp.int32, (2 * n, n), 1)
    return jnp.where(jnp.where(r >= n, r - n, r) >= c, 1.0, 0.0).astype(BF16)


def _sb_block(q, k, v, carry, tri, mask):
    z = _dot_nt(q, k)
    sp = _softplus2(z)
    if mask is not None:
        sp = jnp.where(mask, sp, 0.0)
    hi = sp.astype(BF16)
    lo = (sp - hi.astype(F32)).astype(BF16)
    n = hi.shape[1]
    if n % 128 == 0:
        incl = _dot(jnp.concatenate([hi, lo], axis=1), tri) + carry
    else:
        incl = _dot(hi, tri[:n]) + _dot(lo, tri[:n]) + carry
    a = jnp.exp2(z - incl)
    if mask is not None:
        a = jnp.where(mask, a, 0.0)
    return _dot(a.astype(BF16), v), incl[:, 0:1]


SB_DEAD_CARRY = 150.0


SB_HEADS_PER_STEP = 4


def _sb_prompt_kernel(q_ref, k_ref, v_ref, o_ref):
    qi = pl.program_id(1)
    heads, _, d = q_ref.shape
    blk = SB_BLOCK
    tri = _suffix_matrix(blk)
    r = lax.broadcasted_iota(jnp.int32, (blk, blk), 0)
    c = lax.broadcasted_iota(jnp.int32, (blk, blk), 1)
    strict = c < r
    zero = jnp.zeros((blk, 1), F32)
    has_prev = qi > 0

    def kv(g, block):
        s0 = pl.multiple_of(block * blk, blk)
        return k_ref[g, pl.ds(s0, blk), :], v_ref[g, pl.ds(s0, blk), :]

    def start(g):
        q = q_ref[g]
        k0, v0 = kv(g, qi)
        acc, carry = _sb_block(q, k0, v0, zero, tri, strict)
        kp, vp = kv(g, jnp.maximum(qi - 1, 0))
        d_p, carry_p = _sb_block(q, kp, vp, carry, tri, None)
        return q, acc + jnp.where(has_prev, d_p, 0.0), jnp.where(has_prev, carry_p, carry)

    qs, accs, carries = zip(*[start(g) for g in range(heads)])

    def alive(carries):
        low = functools.reduce(jnp.minimum, carries)
        return (jnp.min(low) < SB_DEAD_CARRY).astype(jnp.int32)

    def cond(state):
        t, live, _, _ = state
        return jnp.logical_and(t >= 0, live > 0)

    def body(state):
        t, _, accs, carries = state
        new_accs, new_carries = [], []
        for g in range(heads):
            kb, vb = kv(g, t)
            d_g, carry_g = _sb_block(qs[g], kb, vb, carries[g], tri, None)
            new_accs.append(accs[g] + d_g)
            new_carries.append(carry_g)
        return t - 1, alive(new_carries), tuple(new_accs), tuple(new_carries)

    _, _, accs, _ = lax.while_loop(cond, body, (qi - 2, alive(carries), accs, carries))
    for g in range(heads):
        o_ref[:, g * d:(g + 1) * d] = accs[g].astype(o_ref.dtype)


def _sb_prompt(q16, k16, v16):
    h, s, d = q16.shape
    tq = SB_BLOCK
    g = SB_HEADS_PER_STEP
    return pl.pallas_call(
        _sb_prompt_kernel,
        out_shape=jax.ShapeDtypeStruct((s, h * d), BF16),
        grid=(h // g, s // tq),
        in_specs=[
            pl.BlockSpec((g, tq, d), lambda hg, qi: (hg, qi, 0)),
            pl.BlockSpec((g, s, d), lambda hg, qi: (hg, 0, 0)),
            pl.BlockSpec((g, s, d), lambda hg, qi: (hg, 0, 0)),
        ],
        out_specs=pl.BlockSpec((tq, g * d), lambda hg, qi: (qi, hg)),
        compiler_params=_cparams(("arbitrary", "arbitrary")),
        name="sb_prompt",
    )(q16, k16, v16)


def _sb_sample_kernel(q_ref, kn_ref, vn_ref, kc_hbm, vc_hbm, o_ref, kbuf, vbuf, sem, *, past):
    nh, t, d = q_ref.shape
    ht = nh * t
    blk = SB_BLOCK
    n_blocks = past // blk
    b = pl.program_id(0)
    slot = lax.rem(b, 2)

    def cache_copies(batch, block, to_slot):
        rows = pl.ds(pl.multiple_of(block * blk, blk), blk)
        return (pltpu.make_async_copy(kc_hbm.at[batch, :, rows, :], kbuf.at[to_slot], sem.at[0, to_slot]),
                pltpu.make_async_copy(vc_hbm.at[batch, :, rows, :], vbuf.at[to_slot], sem.at[1, to_slot]))

    @pl.when(b == 0)
    def _():
        for cp in cache_copies(0, n_blocks - 1, 0):
            cp.start()

    @pl.when(b + 1 < pl.num_programs(0))
    def _():
        for cp in cache_copies(b + 1, n_blocks - 1, 1 - slot):
            cp.start()

    q_all = q_ref[...].reshape(ht, d).astype(F32)
    q_head = lax.broadcasted_iota(jnp.int32, (ht, d), 0) // t
    q_masked = [jnp.where(q_head == h, q_all, 0.0).astype(BF16) for h in range(nh)]

    def logits(keys_of_head):
        z = _dot_nt(keys_of_head(0), q_masked[0])
        for h in range(1, nh):
            z = z + _dot_nt(keys_of_head(h), q_masked[h])
        return z

    def suffix_matrix(n):
        r = lax.broadcasted_iota(jnp.int32, (n, 2 * n), 0)
        c = lax.broadcasted_iota(jnp.int32, (n, 2 * n), 1)
        return jnp.where(jnp.where(c >= n, c - n, c) >= r, 1.0, 0.0).astype(BF16)

    def suffix_sum(sp, lmat):
        hi = sp.astype(BF16)
        lo = (sp - hi.astype(F32)).astype(BF16)
        return _dot(lmat, jnp.concatenate([hi, lo], axis=0))

    def emit(a_t, values_of_head, acc):
        a = a_t.T.astype(BF16)
        return [acc[h] + _dot(a[h * t:(h + 1) * t, :], values_of_head(h)) for h in range(nh)]

    z = logits(lambda h: kn_ref[h])
    s_idx = lax.broadcasted_iota(jnp.int32, (t, ht), 0)
    t_idx = lax.broadcasted_iota(jnp.int32, (t, ht), 1) % t
    strict = s_idx < t_idx
    sp = jnp.where(strict, _softplus2(z), 0.0)
    incl = suffix_sum(sp, suffix_matrix(t))
    a_t = jnp.where(strict, jnp.exp2(z - incl), 0.0)
    acc = emit(a_t, lambda h: vn_ref[h], [jnp.zeros((t, d), F32) for _ in range(nh)])
    carry = incl[0:1, :]

    lmat = suffix_matrix(blk)

    def cache_block(from_slot, acc, carry):
        z = logits(lambda h: kbuf[from_slot, h].astype(BF16))
        incl = suffix_sum(_softplus2(z), lmat) + carry
        acc = emit(jnp.exp2(z - incl), lambda h: vbuf[from_slot, h].astype(BF16), acc)
        return acc, incl[0:1, :]

    def alive(carry):
        return (jnp.min(carry) < SB_DEAD_CARRY).astype(jnp.int32)

    for cp in cache_copies(b, n_blocks - 1, slot):
        cp.wait()
    acc, carry = cache_block(slot, acc, carry)

    def cond(state):
        block, live, _, _ = state
        return jnp.logical_and(block >= 0, live > 0)

    def body(state):
        block, _, acc, carry = state
        copies = cache_copies(b, block, 2)
        for cp in copies:
            cp.start()
        for cp in copies:
            cp.wait()
        acc, carry = cache_block(2, list(acc), carry)
        return block - 1, alive(carry), tuple(acc), carry

    _, _, acc, _ = lax.while_loop(cond, body, (n_blocks - 2, alive(carry), tuple(acc), carry))
    for h in range(nh):
        o_ref[:, h * d:(h + 1) * d] = acc[h].astype(o_ref.dtype)


def _sb_sample(q16, k16, v16, cache_k, cache_v, *, t):
    h, m, d = q16.shape
    nb = m // t
    past = cache_k.shape[2]
    assert past % SB_BLOCK == 0
    new_spec = pl.BlockSpec((h, t, d), lambda b: (0, b, 0))
    cache_spec = pl.BlockSpec(memory_space=pl.ANY)
    return pl.pallas_call(
        functools.partial(_sb_sample_kernel, past=past),
        out_shape=jax.ShapeDtypeStruct((m, h * d), BF16),
        grid=(nb,),
        in_specs=[new_spec, new_spec, new_spec, cache_spec, cache_spec],
        out_specs=pl.BlockSpec((t, h * d), lambda b: (b, 0)),
        scratch_shapes=[pltpu.VMEM((3, h, SB_BLOCK, d), F32),
                        pltpu.VMEM((3, h, SB_BLOCK, d), F32),
                        pltpu.SemaphoreType.DMA((2, 3))],
        compiler_params=_cparams(("arbitrary",)),
        name="sb_sample",
    )(q16, k16, v16, cache_k, cache_v)


def _ret_log_decay(hd):
    return math.log(1.0 - 2.0 ** (-5.0 - hd))


def _ret_kernel(q_ref, k_ref, v_ref, g_ref, nw_ref, s0_ref, r_ref, st_ref,
                intra_ref, qdec_ref, kdec_ref, *, chunk):
    c = pl.program_id(1)

    @pl.when(jnp.logical_and(pl.program_id(0) == 0, c == 0))
    def _():
        row = lax.broadcasted_iota(jnp.int32, (chunk, chunk), 0)
        col = lax.broadcasted_iota(jnp.int32, (chunk, chunk), 1)
        diff = (row - col).astype(F32)
        pos = lax.broadcasted_iota(jnp.int32, (chunk, DK_RET), 0).astype(F32)
        for hd in range(H_RET):
            lg = _ret_log_decay(hd)
            intra_ref[hd] = jnp.where(row >= col, jnp.exp(lg * jnp.maximum(diff, 0.0)), 0.0)
            qdec_ref[hd] = jnp.exp(lg * (pos + 1.0))
            kdec_ref[hd] = jnp.exp(lg * (chunk - 1.0 - pos))

    @pl.when(c == 0)
    def _():
        st_ref[...] = s0_ref[...]

    for hd in range(H_RET):
        intra, q_dec, k_dec = intra_ref[hd], qdec_ref[hd], kdec_ref[hd]
        c_dec = math.exp(_ret_log_decay(hd) * chunk)
        q = q_ref[hd]
        k = k_ref[hd]
        v = v_ref[hd]
        state = st_ref[hd]
        scores = (_dot_nt(q, k.astype(BF16)) * intra).astype(BF16)
        o = _dot(scores, v) + _dot(q, state.astype(BF16)) * q_dec
        st_ref[hd] = c_dec * state + _dot_tn((k * k_dec).astype(BF16), v)
        o = o * lax.rsqrt(jnp.mean(o * o, axis=-1, keepdims=True) + EPS)
        sl = slice(hd * DV_RET, (hd + 1) * DV_RET)
        gate = g_ref[:, sl]
        r_ref[:, sl] = (o * nw_ref[:, sl] * (gate * jax.nn.sigmoid(gate))).astype(r_ref.dtype)


def _retention(rq, rk, rv, gate, norm_w, state0, *, chunk):
    h, m, _ = rq.shape
    nb = state0.shape[0]
    nc = m // (nb * chunk)
    qkv_spec = pl.BlockSpec((h, chunk, DK_RET), lambda b, c: (0, b * nc + c, 0))
    st_spec = pl.BlockSpec((None, h, DK_RET, DV_RET), lambda b, c: (b, 0, 0, 0))
    return pl.pallas_call(
        functools.partial(_ret_kernel, chunk=chunk),
        out_shape=(jax.ShapeDtypeStruct((m, h * DV_RET), BF16),
                   jax.ShapeDtypeStruct(state0.shape, F32)),
        grid=(nb, nc),
        in_specs=[qkv_spec, qkv_spec, qkv_spec,
                  pl.BlockSpec((chunk, h * DV_RET), lambda b, c: (b * nc + c, 0)),
                  pl.BlockSpec((1, h * DV_RET), lambda b, c: (0, 0)),
                  st_spec],
        out_specs=(pl.BlockSpec((chunk, h * DV_RET), lambda b, c: (b * nc + c, 0)), st_spec),
        scratch_shapes=[pltpu.VMEM((h, chunk, chunk), F32),
                        pltpu.VMEM((h, chunk, DK_RET), F32),
                        pltpu.VMEM((h, chunk, DK_RET), F32)],
        compiler_params=_cparams(("arbitrary", "arbitrary")),
        name="retention",
    )(rq, rk, rv, gate, norm_w, state0)


def _outproj_kernel(x_ref, a_ref, b_ref, wa_ref, wb_ref, o_ref):
    o_ref[...] = x_ref[...] + _dot(a_ref[...], wa_ref[...]) + _dot(b_ref[...], wb_ref[...])


def _outproj(x, a_sb, a_ret, w_out, *, tm):
    m = x.shape[0]
    row = lambda i: (i, 0)
    return pl.pallas_call(
        _outproj_kernel,
        out_shape=jax.ShapeDtypeStruct((m, D_MODEL), F32),
        grid=(m // tm,),
        in_specs=[
            pl.BlockSpec((tm, D_MODEL), row),
            pl.BlockSpec((tm, SB_W), row),
            pl.BlockSpec((tm, RET_W), row),
            pl.BlockSpec((SB_W, D_MODEL), lambda i: (0, 0)),
            pl.BlockSpec((RET_W, D_MODEL), lambda i: (1, 0)),
        ],
        out_specs=pl.BlockSpec((tm, D_MODEL), row),
        compiler_params=_cparams(("arbitrary",)),
        name="outproj",
    )(x, a_sb, a_ret, w_out, w_out)


OUTPROJ_TK = 512


def _outproj_f32w_kernel(x_ref, a_ref, w_ref, o_ref, w16_ref):
    @pl.when(pl.program_id(0) == 0)
    def _():
        o_ref[...] = x_ref[...]

    w16_ref[...] = w_ref[...].astype(BF16)
    o_ref[...] += _dot(a_ref[...], w16_ref[...])


def _outproj_f32w(x, a, w_out):
    m = x.shape[0]
    d_mix = w_out.shape[0]
    return pl.pallas_call(
        _outproj_f32w_kernel,
        out_shape=(jax.ShapeDtypeStruct((m, D_MODEL), F32),
                   jax.ShapeDtypeStruct(w_out.shape, BF16)),
        grid=(d_mix // OUTPROJ_TK,),
        in_specs=[
            pl.BlockSpec((m, D_MODEL), lambda k: (0, 0)),
            pl.BlockSpec((m, OUTPROJ_TK), lambda k: (0, k)),
            pl.BlockSpec((OUTPROJ_TK, D_MODEL), lambda k: (k, 0)),
        ],
        out_specs=(pl.BlockSpec((m, D_MODEL), lambda k: (0, 0)),
                   pl.BlockSpec((OUTPROJ_TK, D_MODEL), lambda k: (k, 0))),
        compiler_params=_cparams(("arbitrary",)),
        name="outproj_f32w",
    )(x, a, w_out)


def _layer(x, weights, sb_fn, state0, *, tm, ffn_tm, proj_tm, proj_split, pos_base, pos_mod, chunk):
    (n1, wg1, wu1, wd1, nmix, w_in, qn, kn, ron, w_out, n2, wg2, wu2, wd2, nf, inv_freq) = weights
    f32w = w_in.dtype == F32
    x1 = _ffn(x, n1, wg1, wu1, wd1, tm=ffn_tm)
    if f32w:
        x1, (wg1, wu1, wd1) = x1
    proj = _inproj(x1, nmix, w_in, qn, kn, inv_freq, tm=proj_tm, n_split=proj_split,
                   pos_base=pos_base, pos_mod=pos_mod)
    if f32w:
        proj, w_in = proj
    sq, sk, sk16, sv, sv16, rq, rk, rv, gate = proj
    a_sb = sb_fn(sq, sk16, sv16)
    a_ret, state = _retention(rq, rk, rv, gate, ron, state0, chunk=chunk)
    if f32w:
        x2, w_out = _outproj_f32w(x1, jnp.concatenate([a_sb, a_ret], axis=1), w_out)
    else:
        x2 = _outproj(x1, a_sb, a_ret, w_out, tm=tm)
    y = _ffn(x2, n2, wg2, wu2, wd2, nf, tm=ffn_tm)
    if f32w:
        y, (wg2, wu2, wd2) = y
    weights16 = (n1, wg1, wu1, wd1, nmix, w_in, qn, kn, ron, w_out, n2, wg2, wu2, wd2, nf, inv_freq)
    return y, sk, sv, state, weights16


def kernel(x_prompt, x_sample, cache_sb_k, cache_sb_v, state_ret, ffn1_norm, ffn1_w_gate, ffn1_w_up, ffn1_w_down, mix_norm, w_in, sb_q_norm, sb_k_norm, ret_out_norm, w_out, ffn2_norm, ffn2_w_gate, ffn2_w_up, ffn2_w_down, final_norm):
    depth = ffn1_norm.shape[0]
    assert depth == 1
    nb_p, seq, _ = x_prompt.shape
    nb_s, dec_seq, _ = x_sample.shape
    past = cache_sb_k.shape[3]
    assert nb_p == 1

    half = DK_RET // 2
    inv_freq = (ROPE_BASE ** (-jnp.arange(half, dtype=F32) / half)).reshape(1, half)
    l = 0
    weights = (ffn1_norm[l][None], ffn1_w_gate[l], ffn1_w_up[l], ffn1_w_down[l],
               mix_norm[l][None], w_in[l], sb_q_norm[l][None], sb_k_norm[l][None],
               ret_out_norm[l][None], w_out[l], ffn2_norm[l][None], ffn2_w_gate[l],
               ffn2_w_up[l], ffn2_w_down[l], final_norm[l][None], inv_freq)

    xs = x_sample.reshape(nb_s * dec_seq, D_MODEL)
    sb_s = functools.partial(_sb_sample, cache_k=cache_sb_k[l], cache_v=cache_sb_v[l], t=dec_seq)
    ys, sks, svs, sts, weights16 = _layer(xs, weights, sb_s, state_ret[l],
                                          tm=nb_s * dec_seq, ffn_tm=nb_s * dec_seq,
                                          proj_tm=nb_s * dec_seq, proj_split=1, pos_base=past,
                                          pos_mod=dec_seq, chunk=dec_seq)

    xp = x_prompt.reshape(seq, D_MODEL)
    zero_state = jnp.zeros((1, H_RET, DK_RET, DV_RET), F32)
    yp, skp, svp, stp, _ = _layer(xp, weights16, _sb_prompt, zero_state,
                                  tm=512, ffn_tm=1024, proj_tm=1024, proj_split=2,
                                  pos_base=0, pos_mod=seq, chunk=RET_CHUNK)

    def cache_layout(t):
        return t.reshape(H_SB, nb_s, dec_seq, D_SB).transpose(1, 0, 2, 3)[None]

    return (yp.reshape(1, seq, D_MODEL), ys.reshape(nb_s, dec_seq, D_MODEL),
            skp[None, None], svp[None, None], stp[None],
            cache_layout(sks), cache_layout(svs), sts[None])
```

```python
import functools
import math

import jax
import jax.numpy as jnp
from jax import lax
from jax.experimental import pallas as pl
from jax.experimental.pallas import tpu as pltpu

F32 = jnp.float32
BF16 = jnp.bfloat16

D_MODEL = 2048
D_FF = 5632
H_SB = 8
D_SB = 128
H_RET = 4
DK_RET = 256
DV_RET = 256
SB_W = H_SB * D_SB
RET_W = H_RET * DK_RET
N_SECTIONS = 7
SECTION_W = 1024
ROPE_BASE = 10000.0
EPS = 1e-6

V7X_VMEM_LIMIT_BYTES = 56 * 1024 * 1024
SB_BLOCK = 256
RET_CHUNK = 256
FFN_TF = 512

LOG2E = 1.4426950408889634
LN2 = 0.6931471805599453


def _cparams(sem):
    return pltpu.CompilerParams(dimension_semantics=sem,
                                vmem_limit_bytes=V7X_VMEM_LIMIT_BYTES)


def _rmsnorm_rows(x, w):
    ms = jnp.mean(x * x, axis=-1, keepdims=True)
    return x * lax.rsqrt(ms + EPS) * w


def _dot(a, b):
    return jnp.dot(a, b, preferred_element_type=F32)


def _dot_nt(a, b):
    return lax.dot_general(a, b, (((1,), (1,)), ((), ())), preferred_element_type=F32)


def _dot_tn(a, b):
    return lax.dot_general(a, b, (((0,), (0,)), ((), ())), preferred_element_type=F32)


def _ffn_kernel(*refs, n_steps, final_norm, emit_bf16):
    x_ref, nw_ref, wg_ref, wu_ref, wd_ref = refs[:5]
    refs = refs[5:]
    fw_ref = None
    if final_norm:
        fw_ref, refs = refs[0], refs[1:]
    o_ref, refs = refs[0], refs[1:]
    if emit_bf16:
        wg16_ref, wu16_ref, wd16_ref, h_ref = refs
        wg16_ref[...] = wg_ref[...].astype(BF16)
        wu16_ref[...] = wu_ref[...].astype(BF16)
        wd16_ref[...] = wd_ref[...].astype(BF16)
        wg_ref, wu_ref, wd_ref = wg16_ref, wu16_ref, wd16_ref
    else:
        (h_ref,) = refs
    j = pl.program_id(1)

    @pl.when(j == 0)
    def _():
        x = x_ref[...]
        h_ref[...] = _rmsnorm_rows(x, nw_ref[...]).astype(BF16)
        o_ref[...] = x

    h = h_ref[...]
    g = _dot(h, wg_ref[...])
    u = _dot(h, wu_ref[...])
    a = (g * jax.nn.sigmoid(g) * (0.5 * u)).astype(BF16)
    o_ref[...] += _dot(a, wd_ref[...])

    if final_norm:
        @pl.when(j == n_steps - 1)
        def _():
            o_ref[...] = _rmsnorm_rows(o_ref[...], fw_ref[...])


def _ffn(x, norm_w, wg, wu, wd, final_w=None, *, tm):
    m = x.shape[0]
    n_steps = D_FF // FFN_TF
    emit_bf16 = wg.dtype == F32
    assert not emit_bf16 or m == tm
    w_specs = [
        pl.BlockSpec((D_MODEL, FFN_TF), lambda i, j: (0, j)),
        pl.BlockSpec((D_MODEL, FFN_TF), lambda i, j: (0, j)),
        pl.BlockSpec((FFN_TF, D_MODEL), lambda i, j: (j, 0)),
    ]
    in_specs = [
        pl.BlockSpec((tm, D_MODEL), lambda i, j: (i, 0)),
        pl.BlockSpec((1, D_MODEL), lambda i, j: (0, 0)),
    ] + w_specs
    args = [x, norm_w, wg, wu, wd]
    if final_w is not None:
        in_specs.append(pl.BlockSpec((1, D_MODEL), lambda i, j: (0, 0)))
        args.append(final_w)
    out_shape = [jax.ShapeDtypeStruct((m, D_MODEL), F32)]
    out_specs = [pl.BlockSpec((tm, D_MODEL), lambda i, j: (i, 0))]
    if emit_bf16:
        out_shape += [jax.ShapeDtypeStruct(w.shape, BF16) for w in (wg, wu, wd)]
        out_specs += w_specs
    out = pl.pallas_call(
        functools.partial(_ffn_kernel, n_steps=n_steps, final_norm=final_w is not None,
                          emit_bf16=emit_bf16),
        out_shape=out_shape,
        grid=(m // tm, n_steps),
        in_specs=in_specs,
        out_specs=out_specs,
        scratch_shapes=[pltpu.VMEM((tm, D_MODEL), BF16)],
        compiler_params=_cparams(("arbitrary", "arbitrary")),
        name="ffn_final" if final_w is not None else "ffn",
    )(*args)
    return (out[0], tuple(out[1:])) if emit_bf16 else out[0]


def _inproj_kernel(x_ref, nw_ref, w_ref, qn_ref, kn_ref, inv_ref,
                   sq_ref, sk_ref, sk16_ref, sv_ref, sv16_ref,
                   rq_ref, rk_ref, rv_ref, g_ref, *rest,
                   tm, n_split, tile_stride, pos_base, pos_mod, emit_bf16):
    if emit_bf16:
        w16_ref, h_ref, cos_ref, sin_ref, cos_row_ref, sin_row_ref = rest
        w16_ref[...] = w_ref[...].astype(BF16)
        w_ref = w16_ref
    else:
        h_ref, cos_ref, sin_ref, cos_row_ref, sin_row_ref = rest
    i = pl.program_id(0)
    j = pl.program_id(1)

    @pl.when(jnp.logical_and(i == 0, j == 0))
    def _():
        row = lax.broadcasted_iota(jnp.int32, (tm, DK_RET // 2), 0)
        ang = lax.rem(row, pos_mod).astype(F32) * inv_ref[...]
        cos_row_ref[...] = jnp.cos(ang)
        sin_row_ref[...] = jnp.sin(ang)

    @pl.when(j == 0)
    def _():
        h_ref[...] = _rmsnorm_rows(x_ref[...], nw_ref[...]).astype(BF16)
        ang = (pos_base + i * tile_stride).astype(F32) * inv_ref[...]
        ca, sa = jnp.cos(ang), jnp.sin(ang)
        cb, sb = cos_row_ref[...], sin_row_ref[...]
        cos_ref[...] = ca * cb - sa * sb
        sin_ref[...] = sa * cb + ca * sb

    section = j // n_split
    sb_heads = H_SB // n_split
    ret_heads = H_RET // n_split

    def proj():
        return _dot(h_ref[...], w_ref[...])

    def head(p, hd, width):
        return p[:, hd * width:(hd + 1) * width]

    @pl.when(section == 0)
    def _():
        p = proj()
        w = qn_ref[...] * (D_SB ** -0.5 * LOG2E)
        for hd in range(sb_heads):
            sq_ref[hd] = _rmsnorm_rows(head(p, hd, D_SB), w).astype(BF16)

    @pl.when(section == 1)
    def _():
        p = proj()
        for hd in range(sb_heads):
            k = _rmsnorm_rows(head(p, hd, D_SB), kn_ref[...])
            sk_ref[hd] = k
            sk16_ref[hd] = k.astype(BF16)

    @pl.when(section == 2)
    def _():
        p = proj()
        for hd in range(sb_heads):
            v = head(p, hd, D_SB)
            sv_ref[hd] = v
            sv16_ref[hd] = v.astype(BF16)

    def rotary(ph):
        half = DK_RET // 2
        x1, x2 = ph[:, :half], ph[:, half:]
        c, s = cos_ref[...], sin_ref[...]
        return jnp.concatenate([x1 * c - x2 * s, x1 * s + x2 * c], axis=-1)

    @pl.when(section == 3)
    def _():
        p = proj()
        for hd in range(ret_heads):
            rq_ref[hd] = rotary(head(p, hd, DK_RET)).astype(BF16)

    @pl.when(section == 4)
    def _():
        p = proj()
        for hd in range(ret_heads):
            rk_ref[hd] = rotary(head(p, hd, DK_RET)) * (DK_RET ** -0.5)

    @pl.when(section == 5)
    def _():
        p = proj()
        for hd in range(ret_heads):
            rv_ref[hd] = head(p, hd, DV_RET).astype(BF16)

    @pl.when(section == 6)
    def _():
        g_ref[...] = proj()


def _inproj(x, norm_w, w_in, qn, kn, inv_freq, *, tm, n_split, pos_base, pos_mod):
    m = x.shape[0]
    emit_bf16 = w_in.dtype == F32
    assert not emit_bf16 or m == tm
    tn = SECTION_W // n_split

    def part(section):
        return lambda j: jnp.clip(j - section * n_split, 0, n_split - 1)

    def head_spec(heads, width, section):
        pt = part(section)
        return pl.BlockSpec((heads // n_split, tm, width), lambda i, j: (pt(j), i, 0))

    gate_part = part(6)
    sb16 = jax.ShapeDtypeStruct((H_SB, m, D_SB), BF16)
    sb32 = jax.ShapeDtypeStruct((H_SB, m, D_SB), F32)
    ret16 = jax.ShapeDtypeStruct((H_RET, m, DK_RET), BF16)
    ret32 = jax.ShapeDtypeStruct((H_RET, m, DK_RET), F32)
    assert pos_mod >= m or tm % pos_mod == 0
    tile_stride = tm if pos_mod >= m else 0
    half = DK_RET // 2
    w_spec = pl.BlockSpec((D_MODEL, tn), lambda i, j: (0, j))
    out_shape = [sb16, sb32, sb16, sb32, sb16, ret16, ret32, ret16,
                 jax.ShapeDtypeStruct((m, SECTION_W), F32)]
    out_specs = [head_spec(H_SB, D_SB, 0), head_spec(H_SB, D_SB, 1), head_spec(H_SB, D_SB, 1),
                 head_spec(H_SB, D_SB, 2), head_spec(H_SB, D_SB, 2),
                 head_spec(H_RET, DK_RET, 3), head_spec(H_RET, DK_RET, 4),
                 head_spec(H_RET, DK_RET, 5),
                 pl.BlockSpec((tm, tn), lambda i, j: (i, gate_part(j)))]
    if emit_bf16:
        out_shape.append(jax.ShapeDtypeStruct(w_in.shape, BF16))
        out_specs.append(w_spec)
    out = pl.pallas_call(
        functools.partial(_inproj_kernel, tm=tm, n_split=n_split, tile_stride=tile_stride,
                          pos_base=pos_base, pos_mod=pos_mod, emit_bf16=emit_bf16),
        out_shape=out_shape,
        grid=(m // tm, N_SECTIONS * n_split),
        in_specs=[
            pl.BlockSpec((tm, D_MODEL), lambda i, j: (i, 0)),
            pl.BlockSpec((1, D_MODEL), lambda i, j: (0, 0)),
            w_spec,
            pl.BlockSpec((1, D_SB), lambda i, j: (0, 0)),
            pl.BlockSpec((1, D_SB), lambda i, j: (0, 0)),
            pl.BlockSpec((1, half), lambda i, j: (0, 0)),
        ],
        out_specs=out_specs,
        scratch_shapes=[pltpu.VMEM((tm, D_MODEL), BF16)] + [pltpu.VMEM((tm, half), F32)] * 4,
        compiler_params=_cparams(("arbitrary", "arbitrary")),
        name="inproj",
    )(x, norm_w, w_in, qn, kn, inv_freq)
    return (tuple(out[:9]), out[9]) if emit_bf16 else tuple(out)


def _softplus2(z):
    return jnp.maximum(z, 0.0) + jnp.log2(1.0 + jnp.exp2(-jnp.abs(z)))


def _suffix_matrix(n):
    r = lax.broadcasted_iota(jnp.int32, (2 * n, n), 0)
    c = lax.broadcasted_iota(jnp.int32, (2 * n, n), 1)
    return jnp.where(jnp.where(r >= n, r - n, r) >= c, 1.0, 0.0).astype(BF16)


def _sb_logits(q, k, mask):
    z = _dot_nt(q, k)
    sp = _softplus2(z)
    if mask is not None:
        sp = jnp.where(mask, sp, 0.0)
    hi = sp.astype(BF16)
    lo = (sp - hi.astype(F32)).astype(BF16)
    return z, jnp.concatenate([hi, lo], axis=1)


def _sb_suffix_sums(splits, tri):
    q = splits[0].shape[0]
    sums = _dot(jnp.concatenate(splits, axis=0), tri)
    return [sums[n * q:(n + 1) * q] for n in range(len(splits))]


def _sb_weights(z, incl, v, mask):
    a = jnp.exp2(z - incl)
    if mask is not None:
        a = jnp.where(mask, a, 0.0)
    return _dot(a.astype(BF16), v)


SB_DEAD_CARRY = 150.0


SB_HEADS_PER_STEP = 4


def _sb_prompt_kernel(q_ref, k_ref, v_ref, *rest):
    n_side = len(rest) // 2
    o_ref = rest[n_side]
    for src, dst in zip(rest[:n_side], rest[n_side + 1:]):
        dst[...] = src[...].astype(BF16)
    qi = pl.program_id(1)
    heads, _, d = q_ref.shape
    blk = SB_BLOCK
    tri = _suffix_matrix(blk)
    r = lax.broadcasted_iota(jnp.int32, (blk, blk), 0)
    c = lax.broadcasted_iota(jnp.int32, (blk, blk), 1)
    strict = c < r
    has_prev = qi > 0

    def kv(g, block):
        s0 = pl.multiple_of(block * blk, blk)
        return k_ref[g, pl.ds(s0, blk), :], v_ref[g, pl.ds(s0, blk), :]

    qs = [q_ref[g] for g in range(heads)]
    diag = [kv(g, qi) for g in range(heads)]
    left = [kv(g, jnp.maximum(qi - 1, 0)) for g in range(heads)]
    z_d, hl_d = zip(*[_sb_logits(qs[g], diag[g][0], strict) for g in range(heads)])
    z_l, hl_l = zip(*[_sb_logits(qs[g], left[g][0], None) for g in range(heads)])
    sums = _sb_suffix_sums(list(hl_d) + list(hl_l), tri)
    accs, carries = [], []
    for g in range(heads):
        incl_d = sums[g]
        incl_l = sums[heads + g] + incl_d[:, 0:1]
        acc = _sb_weights(z_d[g], incl_d, diag[g][1], strict)
        d_l = _sb_weights(z_l[g], incl_l, left[g][1], None)
        accs.append(acc + jnp.where(has_prev, d_l, 0.0))
        carries.append(jnp.where(has_prev, incl_l[:, 0:1], incl_d[:, 0:1]))
    accs, carries = tuple(accs), tuple(carries)

    def alive(carries):
        low = functools.reduce(jnp.minimum, carries)
        return (jnp.min(low) < SB_DEAD_CARRY).astype(jnp.int32)

    def cond(state):
        t, live, _, _ = state
        return jnp.logical_and(t >= 0, live > 0)

    def body(state):
        t, _, accs, carries = state
        blocks = [kv(g, t) for g in range(heads)]
        zs, hls = zip(*[_sb_logits(qs[g], blocks[g][0], None) for g in range(heads)])
        sums = _sb_suffix_sums(list(hls), tri)
        new_accs, new_carries = [], []
        for g in range(heads):
            incl = sums[g] + carries[g]
            new_accs.append(accs[g] + _sb_weights(zs[g], incl, blocks[g][1], None))
            new_carries.append(incl[:, 0:1])
        return t - 1, alive(new_carries), tuple(new_accs), tuple(new_carries)

    _, _, accs, _ = lax.while_loop(cond, body, (qi - 2, alive(carries), accs, carries))
    for g in range(heads):
        o_ref[:, g * d:(g + 1) * d] = accs[g].astype(o_ref.dtype)


def _sb_prompt(q16, k16, v16, side=()):
    h, s, d = q16.shape
    tq = SB_BLOCK
    g = SB_HEADS_PER_STEP
    n_groups, n_blocks = h // g, s // tq

    def side_spec(w):
        rows, cols = w.shape
        assert rows % (16 * n_blocks) == 0 and cols % (128 * n_groups) == 0
        return pl.BlockSpec((rows // n_blocks, cols // n_groups), lambda hg, qi: (qi, hg))

    side_specs = [side_spec(w) for w in side]
    out = pl.pallas_call(
        _sb_prompt_kernel,
        out_shape=[jax.ShapeDtypeStruct((s, h * d), BF16)]
        + [jax.ShapeDtypeStruct(w.shape, BF16) for w in side],
        grid=(n_groups, n_blocks),
        in_specs=[
            pl.BlockSpec((g, tq, d), lambda hg, qi: (hg, qi, 0)),
            pl.BlockSpec((g, s, d), lambda hg, qi: (hg, 0, 0)),
            pl.BlockSpec((g, s, d), lambda hg, qi: (hg, 0, 0)),
        ] + side_specs,
        out_specs=[pl.BlockSpec((tq, g * d), lambda hg, qi: (qi, hg))] + side_specs,
        compiler_params=_cparams(("arbitrary", "arbitrary")),
        name="sb_prompt",
    )(q16, k16, v16, *side)
    return out[0], tuple(out[1:])


def _sb_sample_kernel(q_ref, kn_ref, vn_ref, kc_hbm, vc_hbm, o_ref, kbuf, vbuf, sem, *, past):
    nh, t, d = q_ref.shape
    ht = nh * t
    blk = SB_BLOCK
    n_blocks = past // blk
    b = pl.program_id(0)
    slot = lax.rem(b, 2)

    def cache_copies(batch, block, to_slot):
        rows = pl.ds(pl.multiple_of(block * blk, blk), blk)
        return (pltpu.make_async_copy(kc_hbm.at[batch, :, rows, :], kbuf.at[to_slot], sem.at[0, to_slot]),
                pltpu.make_async_copy(vc_hbm.at[batch, :, rows, :], vbuf.at[to_slot], sem.at[1, to_slot]))

    @pl.when(b == 0)
    def _():
        for cp in cache_copies(0, n_blocks - 1, 0):
            cp.start()

    @pl.when(b + 1 < pl.num_programs(0))
    def _():
        for cp in cache_copies(b + 1, n_blocks - 1, 1 - slot):
            cp.start()

    q_all = q_ref[...].reshape(ht, d).astype(F32)
    q_head = lax.broadcasted_iota(jnp.int32, (ht, d), 0) // t
    q_masked = [jnp.where(q_head == h, q_all, 0.0).astype(BF16) for h in range(nh)]

    def logits(keys_of_head):
        z = _dot_nt(keys_of_head(0), q_masked[0])
        for h in range(1, nh):
            z = z + _dot_nt(keys_of_head(h), q_masked[h])
        return z

    def suffix_matrix(n):
        r = lax.broadcasted_iota(jnp.int32, (n, 2 * n), 0)
        c = lax.broadcasted_iota(jnp.int32, (n, 2 * n), 1)
        return jnp.where(jnp.where(c >= n, c - n, c) >= r, 1.0, 0.0).astype(BF16)

    def suffix_sum(sp, lmat):
        hi = sp.astype(BF16)
        lo = (sp - hi.astype(F32)).astype(BF16)
        return _dot(lmat, jnp.concatenate([hi, lo], axis=0))

    def emit(a_t, values_of_head, acc):
        a = a_t.T.astype(BF16)
        return [acc[h] + _dot(a[h * t:(h + 1) * t, :], values_of_head(h)) for h in range(nh)]

    z = logits(lambda h: kn_ref[h])
    s_idx = lax.broadcasted_iota(jnp.int32, (t, ht), 0)
    t_idx = lax.broadcasted_iota(jnp.int32, (t, ht), 1) % t
    strict = s_idx < t_idx
    sp = jnp.where(strict, _softplus2(z), 0.0)
    incl = suffix_sum(sp, suffix_matrix(t))
    a_t = jnp.where(strict, jnp.exp2(z - incl), 0.0)
    acc = emit(a_t, lambda h: vn_ref[h], [jnp.zeros((t, d), F32) for _ in range(nh)])
    carry = incl[0:1, :]

    lmat = suffix_matrix(blk)

    def cache_block(from_slot, acc, carry):
        z = logits(lambda h: kbuf[from_slot, h].astype(BF16))
        incl = suffix_sum(_softplus2(z), lmat) + carry
        acc = emit(jnp.exp2(z - incl), lambda h: vbuf[from_slot, h].astype(BF16), acc)
        return acc, incl[0:1, :]

    def alive(carry):
        return (jnp.min(carry) < SB_DEAD_CARRY).astype(jnp.int32)

    for cp in cache_copies(b, n_blocks - 1, slot):
        cp.wait()
    acc, carry = cache_block(slot, acc, carry)

    def cond(state):
        block, live, _, _ = state
        return jnp.logical_and(block >= 0, live > 0)

    def body(state):
        block, _, acc, carry = state
        copies = cache_copies(b, block, 2)
        for cp in copies:
            cp.start()
        for cp in copies:
            cp.wait()
        acc, carry = cache_block(2, list(acc), carry)
        return block - 1, alive(carry), tuple(acc), carry

    _, _, acc, _ = lax.while_loop(cond, body, (n_blocks - 2, alive(carry), tuple(acc), carry))
    for h in range(nh):
        o_ref[:, h * d:(h + 1) * d] = acc[h].astype(o_ref.dtype)


def _sb_sample(q16, k16, v16, cache_k, cache_v, *, t):
    h, m, d = q16.shape
    nb = m // t
    past = cache_k.shape[2]
    assert past % SB_BLOCK == 0
    new_spec = pl.BlockSpec((h, t, d), lambda b: (0, b, 0))
    cache_spec = pl.BlockSpec(memory_space=pl.ANY)
    return pl.pallas_call(
        functools.partial(_sb_sample_kernel, past=past),
        out_shape=jax.ShapeDtypeStruct((m, h * d), BF16),
        grid=(nb,),
        in_specs=[new_spec, new_spec, new_spec, cache_spec, cache_spec],
        out_specs=pl.BlockSpec((t, h * d), lambda b: (b, 0)),
        scratch_shapes=[pltpu.VMEM((3, h, SB_BLOCK, d), F32),
                        pltpu.VMEM((3, h, SB_BLOCK, d), F32),
                        pltpu.SemaphoreType.DMA((2, 3))],
        compiler_params=_cparams(("arbitrary",)),
        name="sb_sample",
    )(q16, k16, v16, cache_k, cache_v)


def _ret_log_decay(hd):
    return math.log(1.0 - 2.0 ** (-5.0 - hd))


def _ret_kernel(q_ref, k_ref, v_ref, g_ref, nw_ref, s0_ref, r_ref, st_ref,
                intra_ref, qdec_ref, kdec_ref, *, chunk):
    c = pl.program_id(1)

    @pl.when(jnp.logical_and(pl.program_id(0) == 0, c == 0))
    def _():
        row = lax.broadcasted_iota(jnp.int32, (chunk, chunk), 0)
        col = lax.broadcasted_iota(jnp.int32, (chunk, chunk), 1)
        diff = (row - col).astype(F32)
        pos = lax.broadcasted_iota(jnp.int32, (chunk, DK_RET), 0).astype(F32)
        for hd in range(H_RET):
            lg = _ret_log_decay(hd)
            intra_ref[hd] = jnp.where(row >= col, jnp.exp(lg * jnp.maximum(diff, 0.0)), 0.0)
            qdec_ref[hd] = jnp.exp(lg * (pos + 1.0))
            kdec_ref[hd] = jnp.exp(lg * (chunk - 1.0 - pos))

    @pl.when(c == 0)
    def _():
        st_ref[...] = s0_ref[...]

    for hd in range(H_RET):
        intra, q_dec, k_dec = intra_ref[hd], qdec_ref[hd], kdec_ref[hd]
        c_dec = math.exp(_ret_log_decay(hd) * chunk)
        q = q_ref[hd]
        k = k_ref[hd]
        v = v_ref[hd]
        state = st_ref[hd]
        scores = (_dot_nt(q, k.astype(BF16)) * intra).astype(BF16)
        o = _dot(scores, v) + _dot(q, state.astype(BF16)) * q_dec
        st_ref[hd] = c_dec * state + _dot_tn((k * k_dec).astype(BF16), v)
        o = o * lax.rsqrt(jnp.mean(o * o, axis=-1, keepdims=True) + EPS)
        sl = slice(hd * DV_RET, (hd + 1) * DV_RET)
        gate = g_ref[:, sl]
        r_ref[:, sl] = (o * nw_ref[:, sl] * (gate * jax.nn.sigmoid(gate))).astype(r_ref.dtype)


def _retention(rq, rk, rv, gate, norm_w, state0, *, chunk):
    h, m, _ = rq.shape
    nb = state0.shape[0]
    nc = m // (nb * chunk)
    qkv_spec = pl.BlockSpec((h, chunk, DK_RET), lambda b, c: (0, b * nc + c, 0))
    st_spec = pl.BlockSpec((None, h, DK_RET, DV_RET), lambda b, c: (b, 0, 0, 0))
    return pl.pallas_call(
        functools.partial(_ret_kernel, chunk=chunk),
        out_shape=(jax.ShapeDtypeStruct((m, h * DV_RET), BF16),
                   jax.ShapeDtypeStruct(state0.shape, F32)),
        grid=(nb, nc),
        in_specs=[qkv_spec, qkv_spec, qkv_spec,
                  pl.BlockSpec((chunk, h * DV_RET), lambda b, c: (b * nc + c, 0)),
                  pl.BlockSpec((1, h * DV_RET), lambda b, c: (0, 0)),
                  st_spec],
        out_specs=(pl.BlockSpec((chunk, h * DV_RET), lambda b, c: (b * nc + c, 0)), st_spec),
        scratch_shapes=[pltpu.VMEM((h, chunk, chunk), F32),
                        pltpu.VMEM((h, chunk, DK_RET), F32),
                        pltpu.VMEM((h, chunk, DK_RET), F32)],
        compiler_params=_cparams(("arbitrary", "arbitrary")),
        name="retention",
    )(rq, rk, rv, gate, norm_w, state0)


def _outproj_kernel(x_ref, a_ref, b_ref, wa_ref, wb_ref, o_ref):
    o_ref[...] = x_ref[...] + _dot(a_ref[...], wa_ref[...]) + _dot(b_ref[...], wb_ref[...])


def _outproj(x, a_sb, a_ret, w_out, *, tm):
    m = x.shape[0]
    row = lambda i: (i, 0)
    return pl.pallas_call(
        _outproj_kernel,
        out_shape=jax.ShapeDtypeStruct((m, D_MODEL), F32),
        grid=(m // tm,),
        in_specs=[
            pl.BlockSpec((tm, D_MODEL), row),
            pl.BlockSpec((tm, SB_W), row),
            pl.BlockSpec((tm, RET_W), row),
            pl.BlockSpec((SB_W, D_MODEL), lambda i: (0, 0)),
            pl.BlockSpec((RET_W, D_MODEL), lambda i: (1, 0)),
        ],
        out_specs=pl.BlockSpec((tm, D_MODEL), row),
        compiler_params=_cparams(("arbitrary",)),
        name="outproj",
    )(x, a_sb, a_ret, w_out, w_out)


OUTPROJ_TK = 512


def _outproj_f32w_kernel(x_ref, a_ref, w_ref, o_ref, w16_ref):
    @pl.when(pl.program_id(0) == 0)
    def _():
        o_ref[...] = x_ref[...]

    w16_ref[...] = w_ref[...].astype(BF16)
    o_ref[...] += _dot(a_ref[...], w16_ref[...])


def _outproj_f32w(x, a, w_out):
    m = x.shape[0]
    d_mix = w_out.shape[0]
    return pl.pallas_call(
        _outproj_f32w_kernel,
        out_shape=(jax.ShapeDtypeStruct((m, D_MODEL), F32),
                   jax.ShapeDtypeStruct(w_out.shape, BF16)),
        grid=(d_mix // OUTPROJ_TK,),
        in_specs=[
            pl.BlockSpec((m, D_MODEL), lambda k: (0, 0)),
            pl.BlockSpec((m, OUTPROJ_TK), lambda k: (0, k)),
            pl.BlockSpec((OUTPROJ_TK, D_MODEL), lambda k: (k, 0)),
        ],
        out_specs=(pl.BlockSpec((m, D_MODEL), lambda k: (0, 0)),
                   pl.BlockSpec((OUTPROJ_TK, D_MODEL), lambda k: (k, 0))),
        compiler_params=_cparams(("arbitrary",)),
        name="outproj_f32w",
    )(x, a, w_out)


def _mixers(x1, weights, sb_fn, state0, *, tm, proj_tm, proj_split, pos_base, pos_mod, chunk):
    (nmix, w_in, qn, kn, ron, w_out, inv_freq) = weights
    f32w = w_in.dtype == F32
    proj = _inproj(x1, nmix, w_in, qn, kn, inv_freq, tm=proj_tm, n_split=proj_split,
                   pos_base=pos_base, pos_mod=pos_mod)
    if f32w:
        proj, w_in = proj
    sq, sk, sk16, sv, sv16, rq, rk, rv, gate = proj
    a_sb, sb_side = sb_fn(sq, sk16, sv16)
    a_ret, state = _retention(rq, rk, rv, gate, ron, state0, chunk=chunk)
    if f32w:
        x2, w_out = _outproj_f32w(x1, jnp.concatenate([a_sb, a_ret], axis=1), w_out)
    else:
        x2 = _outproj(x1, a_sb, a_ret, w_out, tm=tm)
    return x2, sk, sv, state, (w_in, w_out), sb_side


def kernel(x_prompt, x_sample, cache_sb_k, cache_sb_v, state_ret, ffn1_norm, ffn1_w_gate, ffn1_w_up, ffn1_w_down, mix_norm, w_in, sb_q_norm, sb_k_norm, ret_out_norm, w_out, ffn2_norm, ffn2_w_gate, ffn2_w_up, ffn2_w_down, final_norm):
    depth = ffn1_norm.shape[0]
    assert depth == 1
    nb_p, seq, _ = x_prompt.shape
    nb_s, dec_seq, _ = x_sample.shape
    past = cache_sb_k.shape[3]
    assert nb_p == 1

    half = DK_RET // 2
    inv_freq = (ROPE_BASE ** (-jnp.arange(half, dtype=F32) / half)).reshape(1, half)
    l = 0
    n1, n2, nf = ffn1_norm[l][None], ffn2_norm[l][None], final_norm[l][None]

    def mixer_weights(w_in_, w_out_):
        return (mix_norm[l][None], w_in_, sb_q_norm[l][None], sb_k_norm[l][None],
                ret_out_norm[l][None], w_out_, inv_freq)

    ms = nb_s * dec_seq
    xs = x_sample.reshape(ms, D_MODEL)
    xp = x_prompt.reshape(seq, D_MODEL)
    x1s, ffn1_16 = _ffn(xs, n1, ffn1_w_gate[l], ffn1_w_up[l], ffn1_w_down[l], tm=ms)
    x1p = _ffn(xp, n1, *ffn1_16, tm=1024)

    def sb_s(sq, sk16, sv16):
        return _sb_sample(sq, sk16, sv16, cache_sb_k[l], cache_sb_v[l], t=dec_seq), ()

    x2s, sks, svs, sts, (w_in16, w_out16), _ = _mixers(
        x1s, mixer_weights(w_in[l], w_out[l]), sb_s, state_ret[l],
        tm=ms, proj_tm=ms, proj_split=1, pos_base=past, pos_mod=dec_seq, chunk=dec_seq)

    sb_p = functools.partial(_sb_prompt, side=(ffn2_w_gate[l], ffn2_w_up[l], ffn2_w_down[l]))
    zero_state = jnp.zeros((1, H_RET, DK_RET, DV_RET), F32)
    x2p, skp, svp, stp, _, ffn2_16 = _mixers(
        x1p, mixer_weights(w_in16, w_out16), sb_p, zero_state,
        tm=512, proj_tm=1024, proj_split=2, pos_base=0, pos_mod=seq, chunk=RET_CHUNK)

    ys = _ffn(x2s, n2, *ffn2_16, nf, tm=ms)
    yp = _ffn(x2p, n2, *ffn2_16, nf, tm=1024)

    def cache_layout(t):
        return t.reshape(H_SB, nb_s, dec_seq, D_SB).transpose(1, 0, 2, 3)[None]

    return (yp.reshape(1, seq, D_MODEL), ys.reshape(nb_s, dec_seq, D_MODEL),
            skp[None, None], svp[None, None], stp[None],
            cache_layout(sks), cache_layout(svs), sts[None])
```

```python
import functools
import math

import jax
import jax.numpy as jnp
from jax import lax
from jax.experimental import pallas as pl
from jax.experimental.pallas import tpu as pltpu

F32 = jnp.float32
BF16 = jnp.bfloat16

D_MODEL = 2048
D_FF = 5632
H_SB = 8
D_SB = 128
H_RET = 4
DK_RET = 256
DV_RET = 256
SB_W = H_SB * D_SB
RET_W = H_RET * DK_RET
N_SECTIONS = 7
SECTION_W = 1024
ROPE_BASE = 10000.0
EPS = 1e-6

V7X_VMEM_LIMIT_BYTES = 56 * 1024 * 1024
SB_BLOCK = 256
RET_CHUNK = 256
FFN_TF = 512

LOG2E = 1.4426950408889634
LN2 = 0.6931471805599453


def _cparams(sem):
    return pltpu.CompilerParams(dimension_semantics=sem,
                                vmem_limit_bytes=V7X_VMEM_LIMIT_BYTES)


def _rmsnorm_rows(x, w):
    ms = jnp.mean(x * x, axis=-1, keepdims=True)
    return x * lax.rsqrt(ms + EPS) * w


def _dot(a, b):
    return jnp.dot(a, b, preferred_element_type=F32)


def _dot_nt(a, b):
    return lax.dot_general(a, b, (((1,), (1,)), ((), ())), preferred_element_type=F32)


def _dot_tn(a, b):
    return lax.dot_general(a, b, (((0,), (0,)), ((), ())), preferred_element_type=F32)


def _ffn_kernel(*refs, n_steps, final_norm, emit_bf16):
    x_ref, nw_ref, wg_ref, wu_ref, wd_ref = refs[:5]
    refs = refs[5:]
    fw_ref = None
    if final_norm:
        fw_ref, refs = refs[0], refs[1:]
    o_ref, refs = refs[0], refs[1:]
    if emit_bf16:
        wg16_ref, wu16_ref, wd16_ref, h_ref = refs
        wg16_ref[...] = wg_ref[...].astype(BF16)
        wu16_ref[...] = wu_ref[...].astype(BF16)
        wd16_ref[...] = wd_ref[...].astype(BF16)
        wg_ref, wu_ref, wd_ref = wg16_ref, wu16_ref, wd16_ref
    else:
        (h_ref,) = refs
    j = pl.program_id(1)

    @pl.when(j == 0)
    def _():
        x = x_ref[...]
        h_ref[...] = _rmsnorm_rows(x, nw_ref[...]).astype(BF16)
        o_ref[...] = x

    h = h_ref[...]
    g = _dot(h, wg_ref[...])
    u = _dot(h, wu_ref[...])
    a = (g * jax.nn.sigmoid(g) * (0.5 * u)).astype(BF16)
    o_ref[...] += _dot(a, wd_ref[...])

    if final_norm:
        @pl.when(j == n_steps - 1)
        def _():
            o_ref[...] = _rmsnorm_rows(o_ref[...], fw_ref[...])


def _ffn(x, norm_w, wg, wu, wd, final_w=None, *, tm):
    m = x.shape[0]
    n_steps = D_FF // FFN_TF
    emit_bf16 = wg.dtype == F32
    assert not emit_bf16 or m == tm
    w_specs = [
        pl.BlockSpec((D_MODEL, FFN_TF), lambda i, j: (0, j)),
        pl.BlockSpec((D_MODEL, FFN_TF), lambda i, j: (0, j)),
        pl.BlockSpec((FFN_TF, D_MODEL), lambda i, j: (j, 0)),
    ]
    in_specs = [
        pl.BlockSpec((tm, D_MODEL), lambda i, j: (i, 0)),
        pl.BlockSpec((1, D_MODEL), lambda i, j: (0, 0)),
    ] + w_specs
    args = [x, norm_w, wg, wu, wd]
    if final_w is not None:
        in_specs.append(pl.BlockSpec((1, D_MODEL), lambda i, j: (0, 0)))
        args.append(final_w)
    out_shape = [jax.ShapeDtypeStruct((m, D_MODEL), F32)]
    out_specs = [pl.BlockSpec((tm, D_MODEL), lambda i, j: (i, 0))]
    if emit_bf16:
        out_shape += [jax.ShapeDtypeStruct(w.shape, BF16) for w in (wg, wu, wd)]
        out_specs += w_specs
    out = pl.pallas_call(
        functools.partial(_ffn_kernel, n_steps=n_steps, final_norm=final_w is not None,
                          emit_bf16=emit_bf16),
        out_shape=out_shape,
        grid=(m // tm, n_steps),
        in_specs=in_specs,
        out_specs=out_specs,
        scratch_shapes=[pltpu.VMEM((tm, D_MODEL), BF16)],
        compiler_params=_cparams(("arbitrary", "arbitrary")),
        name="ffn_final" if final_w is not None else "ffn",
    )(*args)
    return (out[0], tuple(out[1:])) if emit_bf16 else out[0]


def _inproj_kernel(x_ref, nw_ref, w_ref, qn_ref, kn_ref, inv_ref,
                   sq_ref, sk_ref, sk16_ref, sv_ref, sv16_ref,
                   rq_ref, rk_ref, rv_ref, g_ref, *rest,
                   tm, n_split, tile_stride, pos_base, pos_mod, emit_bf16):
    if emit_bf16:
        w16_ref, h_ref, cos_ref, sin_ref, cos_row_ref, sin_row_ref = rest
        w16_ref[...] = w_ref[...].astype(BF16)
        w_ref = w16_ref
    else:
        h_ref, cos_ref, sin_ref, cos_row_ref, sin_row_ref = rest
    i = pl.program_id(0)
    j = pl.program_id(1)

    @pl.when(jnp.logical_and(i == 0, j == 0))
    def _():
        row = lax.broadcasted_iota(jnp.int32, (tm, DK_RET // 2), 0)
        ang = lax.rem(row, pos_mod).astype(F32) * inv_ref[...]
        cos_row_ref[...] = jnp.cos(ang)
        sin_row_ref[...] = jnp.sin(ang)

    @pl.when(j == 0)
    def _():
        h_ref[...] = _rmsnorm_rows(x_ref[...], nw_ref[...]).astype(BF16)
        ang = (pos_base + i * tile_stride).astype(F32) * inv_ref[...]
        ca, sa = jnp.cos(ang), jnp.sin(ang)
        cb, sb = cos_row_ref[...], sin_row_ref[...]
        cos_ref[...] = ca * cb - sa * sb
        sin_ref[...] = sa * cb + ca * sb

    section = j // n_split
    sb_heads = H_SB // n_split
    ret_heads = H_RET // n_split

    def proj():
        return _dot(h_ref[...], w_ref[...])

    def head(p, hd, width):
        return p[:, hd * width:(hd + 1) * width]

    @pl.when(section == 0)
    def _():
        p = proj()
        w = qn_ref[...] * (D_SB ** -0.5 * LOG2E)
        for hd in range(sb_heads):
            sq_ref[hd] = _rmsnorm_rows(head(p, hd, D_SB), w).astype(BF16)

    @pl.when(section == 1)
    def _():
        p = proj()
        for hd in range(sb_heads):
            k = _rmsnorm_rows(head(p, hd, D_SB), kn_ref[...])
            sk_ref[hd] = k
            sk16_ref[hd] = k.astype(BF16)

    @pl.when(section == 2)
    def _():
        p = proj()
        for hd in range(sb_heads):
            v = head(p, hd, D_SB)
            sv_ref[hd] = v
            sv16_ref[hd] = v.astype(BF16)

    def rotary(ph):
        half = DK_RET // 2
        x1, x2 = ph[:, :half], ph[:, half:]
        c, s = cos_ref[...], sin_ref[...]
        return jnp.concatenate([x1 * c - x2 * s, x1 * s + x2 * c], axis=-1)

    @pl.when(section == 3)
    def _():
        p = proj()
        for hd in range(ret_heads):
            rq_ref[hd] = rotary(head(p, hd, DK_RET)).astype(BF16)

    @pl.when(section == 4)
    def _():
        p = proj()
        for hd in range(ret_heads):
            rk_ref[hd] = rotary(head(p, hd, DK_RET)) * (DK_RET ** -0.5)

    @pl.when(section == 5)
    def _():
        p = proj()
        for hd in range(ret_heads):
            rv_ref[hd] = head(p, hd, DV_RET).astype(BF16)

    @pl.when(section == 6)
    def _():
        g_ref[...] = proj()


def _inproj(x, norm_w, w_in, qn, kn, inv_freq, *, tm, n_split, pos_base, pos_mod):
    m = x.shape[0]
    emit_bf16 = w_in.dtype == F32
    assert not emit_bf16 or m == tm
    tn = SECTION_W // n_split

    def part(section):
        return lambda j: jnp.clip(j - section * n_split, 0, n_split - 1)

    def head_spec(heads, width, section):
        pt = part(section)
        return pl.BlockSpec((heads // n_split, tm, width), lambda i, j: (pt(j), i, 0))

    gate_part = part(6)
    sb16 = jax.ShapeDtypeStruct((H_SB, m, D_SB), BF16)
    sb32 = jax.ShapeDtypeStruct((H_SB, m, D_SB), F32)
    ret16 = jax.ShapeDtypeStruct((H_RET, m, DK_RET), BF16)
    ret32 = jax.ShapeDtypeStruct((H_RET, m, DK_RET), F32)
    assert pos_mod >= m or tm % pos_mod == 0
    tile_stride = tm if pos_mod >= m else 0
    half = DK_RET // 2
    w_spec = pl.BlockSpec((D_MODEL, tn), lambda i, j: (0, j))
    out_shape = [sb16, sb32, sb16, sb32, sb16, ret16, ret32, ret16,
                 jax.ShapeDtypeStruct((m, SECTION_W), F32)]
    out_specs = [head_spec(H_SB, D_SB, 0), head_spec(H_SB, D_SB, 1), head_spec(H_SB, D_SB, 1),
                 head_spec(H_SB, D_SB, 2), head_spec(H_SB, D_SB, 2),
                 head_spec(H_RET, DK_RET, 3), head_spec(H_RET, DK_RET, 4),
                 head_spec(H_RET, DK_RET, 5),
                 pl.BlockSpec((tm, tn), lambda i, j: (i, gate_part(j)))]
    if emit_bf16:
        out_shape.append(jax.ShapeDtypeStruct(w_in.shape, BF16))
        out_specs.append(w_spec)
    out = pl.pallas_call(
        functools.partial(_inproj_kernel, tm=tm, n_split=n_split, tile_stride=tile_stride,
                          pos_base=pos_base, pos_mod=pos_mod, emit_bf16=emit_bf16),
        out_shape=out_shape,
        grid=(m // tm, N_SECTIONS * n_split),
        in_specs=[
            pl.BlockSpec((tm, D_MODEL), lambda i, j: (i, 0)),
            pl.BlockSpec((1, D_MODEL), lambda i, j: (0, 0)),
            w_spec,
            pl.BlockSpec((1, D_SB), lambda i, j: (0, 0)),
            pl.BlockSpec((1, D_SB), lambda i, j: (0, 0)),
            pl.BlockSpec((1, half), lambda i, j: (0, 0)),
        ],
        out_specs=out_specs,
        scratch_shapes=[pltpu.VMEM((tm, D_MODEL), BF16)] + [pltpu.VMEM((tm, half), F32)] * 4,
        compiler_params=_cparams(("arbitrary", "arbitrary")),
        name="inproj",
    )(x, norm_w, w_in, qn, kn, inv_freq)
    return (tuple(out[:9]), out[9]) if emit_bf16 else tuple(out)


def _softplus2(z):
    return jnp.maximum(z, 0.0) + jnp.log2(1.0 + jnp.exp2(-jnp.abs(z)))


def _suffix_matrix(n):
    r = lax.broadcasted_iota(jnp.int32, (2 * n, n), 0)
    c = lax.broadcasted_iota(jnp.int32, (2 * n, n), 1)
    return jnp.where(jnp.where(r >= n, r - n, r) >= c, 1.0, 0.0).astype(BF16)


def _sb_logits(q, k, mask):
    z = _dot_nt(q, k)
    sp = _softplus2(z)
    if mask is not None:
        sp = jnp.where(mask, sp, 0.0)
    hi = sp.astype(BF16)
    lo = (sp - hi.astype(F32)).astype(BF16)
    return z, jnp.concatenate([hi, lo], axis=1)


def _sb_suffix_sums(splits, tri):
    q = splits[0].shape[0]
    sums = _dot(jnp.concatenate(splits, axis=0), tri)
    return [sums[n * q:(n + 1) * q] for n in range(len(splits))]


def _sb_weights(z, incl, v, mask):
    a = jnp.exp2(z - incl)
    if mask is not None:
        a = jnp.where(mask, a, 0.0)
    return _dot(a.astype(BF16), v)


SB_DEAD_CARRY = 150.0


SB_HEADS_PER_STEP = 4


def _sb_prompt_kernel(q_ref, k_ref, v_ref, *rest):
    n_side = len(rest) // 2
    o_ref = rest[n_side]
    for src, dst in zip(rest[:n_side], rest[n_side + 1:]):
        dst[...] = src[...].astype(BF16)
    qi = pl.program_id(1)
    heads, _, d = q_ref.shape
    blk = SB_BLOCK
    tri = _suffix_matrix(blk)
    r = lax.broadcasted_iota(jnp.int32, (blk, blk), 0)
    c = lax.broadcasted_iota(jnp.int32, (blk, blk), 1)
    strict = c < r
    has_prev = qi > 0

    def kv(g, block):
        s0 = pl.multiple_of(block * blk, blk)
        return k_ref[g, pl.ds(s0, blk), :], v_ref[g, pl.ds(s0, blk), :]

    qs = [q_ref[g] for g in range(heads)]
    diag = [kv(g, qi) for g in range(heads)]
    left = [kv(g, jnp.maximum(qi - 1, 0)) for g in range(heads)]
    z_d, hl_d = zip(*[_sb_logits(qs[g], diag[g][0], strict) for g in range(heads)])
    z_l, hl_l = zip(*[_sb_logits(qs[g], left[g][0], None) for g in range(heads)])
    sums = _sb_suffix_sums(list(hl_d) + list(hl_l), tri)
    accs, carries = [], []
    for g in range(heads):
        incl_d = sums[g]
        incl_l = sums[heads + g] + incl_d[:, 0:1]
        acc = _sb_weights(z_d[g], incl_d, diag[g][1], strict)
        d_l = _sb_weights(z_l[g], incl_l, left[g][1], None)
        accs.append(acc + jnp.where(has_prev, d_l, 0.0))
        carries.append(jnp.where(has_prev, incl_l[:, 0:1], incl_d[:, 0:1]))
    accs, carries = tuple(accs), tuple(carries)

    def alive(carries):
        low = functools.reduce(jnp.minimum, carries)
        return (jnp.min(low) < SB_DEAD_CARRY).astype(jnp.int32)

    def cond(state):
        t, live, _, _ = state
        return jnp.logical_and(t >= 0, live > 0)

    def body(state):
        t, _, accs, carries = state
        blocks = [kv(g, t) for g in range(heads)]
        zs, hls = zip(*[_sb_logits(qs[g], blocks[g][0], None) for g in range(heads)])
        sums = _sb_suffix_sums(list(hls), tri)
        new_accs, new_carries = [], []
        for g in range(heads):
            incl = sums[g] + carries[g]
            new_accs.append(accs[g] + _sb_weights(zs[g], incl, blocks[g][1], None))
            new_carries.append(incl[:, 0:1])
        return t - 1, alive(new_carries), tuple(new_accs), tuple(new_carries)

    _, _, accs, _ = lax.while_loop(cond, body, (qi - 2, alive(carries), accs, carries))
    for g in range(heads):
        o_ref[:, g * d:(g + 1) * d] = accs[g].astype(o_ref.dtype)


def _sb_prompt(q16, k16, v16, side=()):
    h, s, d = q16.shape
    tq = SB_BLOCK
    g = SB_HEADS_PER_STEP
    n_groups, n_blocks = h // g, s // tq

    def side_spec(w):
        rows, cols = w.shape
        assert rows % (16 * n_blocks) == 0 and cols % (128 * n_groups) == 0
        return pl.BlockSpec((rows // n_blocks, cols // n_groups), lambda hg, qi: (qi, hg))

    side_specs = [side_spec(w) for w in side]
    out = pl.pallas_call(
        _sb_prompt_kernel,
        out_shape=[jax.ShapeDtypeStruct((s, h * d), BF16)]
        + [jax.ShapeDtypeStruct(w.shape, BF16) for w in side],
        grid=(n_groups, n_blocks),
        in_specs=[
            pl.BlockSpec((g, tq, d), lambda hg, qi: (hg, qi, 0)),
            pl.BlockSpec((g, s, d), lambda hg, qi: (hg, 0, 0)),
            pl.BlockSpec((g, s, d), lambda hg, qi: (hg, 0, 0)),
        ] + side_specs,
        out_specs=[pl.BlockSpec((tq, g * d), lambda hg, qi: (qi, hg))] + side_specs,
        compiler_params=_cparams(("arbitrary", "arbitrary")),
        name="sb_prompt",
    )(q16, k16, v16, *side)
    return out[0], tuple(out[1:])


def _sb_sample_kernel(q_ref, kn_ref, vn_ref, kc_hbm, vc_hbm, o_ref, kbuf, vbuf, sem, *, past):
    nh, t, d = q_ref.shape
    ht = nh * t
    blk = SB_BLOCK
    n_blocks = past // blk
    b = pl.program_id(0)
    slot = lax.rem(b, 2)

    def cache_copies(batch, block, to_slot):
        rows = pl.ds(pl.multiple_of(block * blk, blk), blk)
        return (pltpu.make_async_copy(kc_hbm.at[batch, :, rows, :], kbuf.at[to_slot], sem.at[0, to_slot]),
                pltpu.make_async_copy(vc_hbm.at[batch, :, rows, :], vbuf.at[to_slot], sem.at[1, to_slot]))

    @pl.when(b == 0)
    def _():
        for cp in cache_copies(0, n_blocks - 1, 0):
            cp.start()

    @pl.when(b + 1 < pl.num_programs(0))
    def _():
        for cp in cache_copies(b + 1, n_blocks - 1, 1 - slot):
            cp.start()

    q_all = q_ref[...].reshape(ht, d).astype(F32)
    q_head = lax.broadcasted_iota(jnp.int32, (ht, d), 0) // t
    q_masked = [jnp.where(q_head == h, q_all, 0.0).astype(BF16) for h in range(nh)]

    def logits(keys_of_head):
        z = _dot_nt(keys_of_head(0), q_masked[0])
        for h in range(1, nh):
            z = z + _dot_nt(keys_of_head(h), q_masked[h])
        return z

    def suffix_matrix(n):
        r = lax.broadcasted_iota(jnp.int32, (n, 2 * n), 0)
        c = lax.broadcasted_iota(jnp.int32, (n, 2 * n), 1)
        return jnp.where(jnp.where(c >= n, c - n, c) >= r, 1.0, 0.0).astype(BF16)

    def suffix_sum(sp, lmat):
        hi = sp.astype(BF16)
        lo = (sp - hi.astype(F32)).astype(BF16)
        return _dot(lmat, jnp.concatenate([hi, lo], axis=0))

    def emit(a_t, values_of_head, acc):
        a = a_t.T.astype(BF16)
        return [acc[h] + _dot(a[h * t:(h + 1) * t, :], values_of_head(h)) for h in range(nh)]

    z = logits(lambda h: kn_ref[h])
    s_idx = lax.broadcasted_iota(jnp.int32, (t, ht), 0)
    t_idx = lax.broadcasted_iota(jnp.int32, (t, ht), 1) % t
    strict = s_idx < t_idx
    sp = jnp.where(strict, _softplus2(z), 0.0)
    incl = suffix_sum(sp, suffix_matrix(t))
    a_t = jnp.where(strict, jnp.exp2(z - incl), 0.0)
    acc = emit(a_t, lambda h: vn_ref[h], [jnp.zeros((t, d), F32) for _ in range(nh)])
    carry = incl[0:1, :]

    lmat = suffix_matrix(blk)

    def cache_block(from_slot, acc, carry):
        z = logits(lambda h: kbuf[from_slot, h].astype(BF16))
        incl = suffix_sum(_softplus2(z), lmat) + carry
        acc = emit(jnp.exp2(z - incl), lambda h: vbuf[from_slot, h].astype(BF16), acc)
        return acc, incl[0:1, :]

    def alive(carry):
        return (jnp.min(carry) < SB_DEAD_CARRY).astype(jnp.int32)

    for cp in cache_copies(b, n_blocks - 1, slot):
        cp.wait()
    acc, carry = cache_block(slot, acc, carry)

    def cond(state):
        block, live, _, _ = state
        return jnp.logical_and(block >= 0, live > 0)

    def body(state):
        block, _, acc, carry = state
        copies = cache_copies(b, block, 2)
        for cp in copies:
            cp.start()
        for cp in copies:
            cp.wait()
        acc, carry = cache_block(2, list(acc), carry)
        return block - 1, alive(carry), tuple(acc), carry

    _, _, acc, _ = lax.while_loop(cond, body, (n_blocks - 2, alive(carry), tuple(acc), carry))
    for h in range(nh):
        o_ref[:, h * d:(h + 1) * d] = acc[h].astype(o_ref.dtype)


def _sb_sample(q16, k16, v16, cache_k, cache_v, *, t):
    h, m, d = q16.shape
    nb = m // t
    past = cache_k.shape[2]
    assert past % SB_BLOCK == 0
    new_spec = pl.BlockSpec((h, t, d), lambda b: (0, b, 0))
    cache_spec = pl.BlockSpec(memory_space=pl.ANY)
    return pl.pallas_call(
        functools.partial(_sb_sample_kernel, past=past),
        out_shape=jax.ShapeDtypeStruct((m, h * d), BF16),
        grid=(nb,),
        in_specs=[new_spec, new_spec, new_spec, cache_spec, cache_spec],
        out_specs=pl.BlockSpec((t, h * d), lambda b: (b, 0)),
        scratch_shapes=[pltpu.VMEM((3, h, SB_BLOCK, d), F32),
                        pltpu.VMEM((3, h, SB_BLOCK, d), F32),
                        pltpu.SemaphoreType.DMA((2, 3))],
        compiler_params=_cparams(("arbitrary",)),
        name="sb_sample",
    )(q16, k16, v16, cache_k, cache_v)


def _ret_log_decay(hd):
    return math.log(1.0 - 2.0 ** (-5.0 - hd))


def _ret_kernel(q_ref, k_ref, v_ref, g_ref, nw_ref, s0_ref, r_ref, st_ref,
                intra_ref, qdec_ref, kdec_ref, *, chunk):
    c = pl.program_id(1)

    @pl.when(jnp.logical_and(pl.program_id(0) == 0, c == 0))
    def _():
        row = lax.broadcasted_iota(jnp.int32, (chunk, chunk), 0)
        col = lax.broadcasted_iota(jnp.int32, (chunk, chunk), 1)
        diff = (row - col).astype(F32)
        pos = lax.broadcasted_iota(jnp.int32, (chunk, DK_RET), 0).astype(F32)
        for hd in range(H_RET):
            lg = _ret_log_decay(hd)
            intra_ref[hd] = jnp.where(row >= col, jnp.exp(lg * jnp.maximum(diff, 0.0)), 0.0)
            qdec_ref[hd] = jnp.exp(lg * (pos + 1.0))
            kdec_ref[hd] = jnp.exp(lg * (chunk - 1.0 - pos))

    @pl.when(c == 0)
    def _():
        st_ref[...] = s0_ref[...]

    for sub in range(q_ref.shape[1] // chunk):
        rows = slice(sub * chunk, (sub + 1) * chunk)
        for hd in range(H_RET):
            intra, q_dec, k_dec = intra_ref[hd], qdec_ref[hd], kdec_ref[hd]
            c_dec = math.exp(_ret_log_decay(hd) * chunk)
            q = q_ref[hd, rows, :]
            k = k_ref[hd, rows, :]
            v = v_ref[hd, rows, :]
            state = st_ref[hd]
            scores = (_dot_nt(q, k.astype(BF16)) * intra).astype(BF16)
            o = _dot(scores, v) + _dot(q, state.astype(BF16)) * q_dec
            st_ref[hd] = c_dec * state + _dot_tn((k * k_dec).astype(BF16), v)
            o = o * lax.rsqrt(jnp.mean(o * o, axis=-1, keepdims=True) + EPS)
            sl = slice(hd * DV_RET, (hd + 1) * DV_RET)
            gate = g_ref[rows, sl]
            r_ref[rows, sl] = (o * nw_ref[:, sl] * (gate * jax.nn.sigmoid(gate))).astype(r_ref.dtype)


def _retention(rq, rk, rv, gate, norm_w, state0, *, chunk, per_step):
    h, m, _ = rq.shape
    nb = state0.shape[0]
    rows = per_step * chunk
    nc = m // (nb * rows)
    qkv_spec = pl.BlockSpec((h, rows, DK_RET), lambda b, c: (0, b * nc + c, 0))
    st_spec = pl.BlockSpec((None, h, DK_RET, DV_RET), lambda b, c: (b, 0, 0, 0))
    return pl.pallas_call(
        functools.partial(_ret_kernel, chunk=chunk),
        out_shape=(jax.ShapeDtypeStruct((m, h * DV_RET), BF16),
                   jax.ShapeDtypeStruct(state0.shape, F32)),
        grid=(nb, nc),
        in_specs=[qkv_spec, qkv_spec, qkv_spec,
                  pl.BlockSpec((rows, h * DV_RET), lambda b, c: (b * nc + c, 0)),
                  pl.BlockSpec((1, h * DV_RET), lambda b, c: (0, 0)),
                  st_spec],
        out_specs=(pl.BlockSpec((rows, h * DV_RET), lambda b, c: (b * nc + c, 0)), st_spec),
        scratch_shapes=[pltpu.VMEM((h, chunk, chunk), F32),
                        pltpu.VMEM((h, chunk, DK_RET), F32),
                        pltpu.VMEM((h, chunk, DK_RET), F32)],
        compiler_params=_cparams(("arbitrary", "arbitrary")),
        name="retention",
    )(rq, rk, rv, gate, norm_w, state0)


def _outproj_kernel(x_ref, a_ref, b_ref, wa_ref, wb_ref, o_ref):
    o_ref[...] = x_ref[...] + _dot(a_ref[...], wa_ref[...])
    o_ref[...] += _dot(b_ref[...], wb_ref[...])


def _outproj(x, a_sb, a_ret, w_out, *, tm):
    m = x.shape[0]
    row = lambda i: (i, 0)
    return pl.pallas_call(
        _outproj_kernel,
        out_shape=jax.ShapeDtypeStruct((m, D_MODEL), F32),
        grid=(m // tm,),
        in_specs=[
            pl.BlockSpec((tm, D_MODEL), row),
            pl.BlockSpec((tm, SB_W), row),
            pl.BlockSpec((tm, RET_W), row),
            pl.BlockSpec((SB_W, D_MODEL), lambda i: (0, 0), pipeline_mode=pl.Buffered(1)),
            pl.BlockSpec((RET_W, D_MODEL), lambda i: (1, 0), pipeline_mode=pl.Buffered(1)),
        ],
        out_specs=pl.BlockSpec((tm, D_MODEL), row),
        compiler_params=_cparams(("arbitrary",)),
        name="outproj",
    )(x, a_sb, a_ret, w_out, w_out)


OUTPROJ_TK = 512


def _outproj_f32w_kernel(x_ref, a_ref, w_ref, o_ref, w16_ref):
    @pl.when(pl.program_id(0) == 0)
    def _():
        o_ref[...] = x_ref[...]

    w16_ref[...] = w_ref[...].astype(BF16)
    o_ref[...] += _dot(a_ref[...], w16_ref[...])


def _outproj_f32w(x, a, w_out):
    m = x.shape[0]
    d_mix = w_out.shape[0]
    return pl.pallas_call(
        _outproj_f32w_kernel,
        out_shape=(jax.ShapeDtypeStruct((m, D_MODEL), F32),
                   jax.ShapeDtypeStruct(w_out.shape, BF16)),
        grid=(d_mix // OUTPROJ_TK,),
        in_specs=[
            pl.BlockSpec((m, D_MODEL), lambda k: (0, 0)),
            pl.BlockSpec((m, OUTPROJ_TK), lambda k: (0, k)),
            pl.BlockSpec((OUTPROJ_TK, D_MODEL), lambda k: (k, 0)),
        ],
        out_specs=(pl.BlockSpec((m, D_MODEL), lambda k: (0, 0)),
                   pl.BlockSpec((OUTPROJ_TK, D_MODEL), lambda k: (k, 0))),
        compiler_params=_cparams(("arbitrary",)),
        name="outproj_f32w",
    )(x, a, w_out)


def _mixers(x1, weights, sb_fn, state0, *, tm, proj_tm, proj_split, pos_base, pos_mod, chunk,
            chunks_per_step):
    (nmix, w_in, qn, kn, ron, w_out, inv_freq) = weights
    f32w = w_in.dtype == F32
    proj = _inproj(x1, nmix, w_in, qn, kn, inv_freq, tm=proj_tm, n_split=proj_split,
                   pos_base=pos_base, pos_mod=pos_mod)
    if f32w:
        proj, w_in = proj
    sq, sk, sk16, sv, sv16, rq, rk, rv, gate = proj
    a_sb, sb_side = sb_fn(sq, sk16, sv16)
    a_ret, state = _retention(rq, rk, rv, gate, ron, state0, chunk=chunk, per_step=chunks_per_step)
    if f32w:
        x2, w_out = _outproj_f32w(x1, jnp.concatenate([a_sb, a_ret], axis=1), w_out)
    else:
        x2 = _outproj(x1, a_sb, a_ret, w_out, tm=tm)
    return x2, sk, sv, state, (w_in, w_out), sb_side


def kernel(x_prompt, x_sample, cache_sb_k, cache_sb_v, state_ret, ffn1_norm, ffn1_w_gate, ffn1_w_up, ffn1_w_down, mix_norm, w_in, sb_q_norm, sb_k_norm, ret_out_norm, w_out, ffn2_norm, ffn2_w_gate, ffn2_w_up, ffn2_w_down, final_norm):
    depth = ffn1_norm.shape[0]
    assert depth == 1
    nb_p, seq, _ = x_prompt.shape
    nb_s, dec_seq, _ = x_sample.shape
    past = cache_sb_k.shape[3]
    assert nb_p == 1

    half = DK_RET // 2
    inv_freq = (ROPE_BASE ** (-jnp.arange(half, dtype=F32) / half)).reshape(1, half)
    l = 0
    n1, n2, nf = ffn1_norm[l][None], ffn2_norm[l][None], final_norm[l][None]

    def mixer_weights(w_in_, w_out_):
        return (mix_norm[l][None], w_in_, sb_q_norm[l][None], sb_k_norm[l][None],
                ret_out_norm[l][None], w_out_, inv_freq)

    ms = nb_s * dec_seq
    xs = x_sample.reshape(ms, D_MODEL)
    xp = x_prompt.reshape(seq, D_MODEL)
    x1s, ffn1_16 = _ffn(xs, n1, ffn1_w_gate[l], ffn1_w_up[l], ffn1_w_down[l], tm=ms)
    x1p = _ffn(xp, n1, *ffn1_16, tm=1024)

    def sb_s(sq, sk16, sv16):
        return _sb_sample(sq, sk16, sv16, cache_sb_k[l], cache_sb_v[l], t=dec_seq), ()

    x2s, sks, svs, sts, (w_in16, w_out16), _ = _mixers(
        x1s, mixer_weights(w_in[l], w_out[l]), sb_s, state_ret[l],
        tm=ms, proj_tm=ms, proj_split=1, pos_base=past, pos_mod=dec_seq, chunk=dec_seq,
        chunks_per_step=1)

    sb_p = functools.partial(_sb_prompt, side=(ffn2_w_gate[l], ffn2_w_up[l], ffn2_w_down[l]))
    zero_state = jnp.zeros((1, H_RET, DK_RET, DV_RET), F32)
    x2p, skp, svp, stp, _, ffn2_16 = _mixers(
        x1p, mixer_weights(w_in16, w_out16), sb_p, zero_state,
        tm=1024, proj_tm=1024, proj_split=2, pos_base=0, pos_mod=seq, chunk=RET_CHUNK,
        chunks_per_step=2)

    ys = _ffn(x2s, n2, *ffn2_16, nf, tm=ms)
    yp = _ffn(x2p, n2, *ffn2_16, nf, tm=1024)

    def cache_layout(t):
        return t.reshape(H_SB, nb_s, dec_seq, D_SB).transpose(1, 0, 2, 3)[None]

    return (yp.reshape(1, seq, D_MODEL), ys.reshape(nb_s, dec_seq, D_MODEL),
            skp[None, None], svp[None, None], stp[None],
            cache_layout(sks), cache_layout(svs), sts[None])
```

```python
import functools
import math

import jax
import jax.numpy as jnp
from jax import lax
from jax.experimental import pallas as pl
from jax.experimental.pallas import tpu as pltpu

F32 = jnp.float32
BF16 = jnp.bfloat16

D_MODEL = 2048
D_FF = 5632
H_SB = 8
D_SB = 128
H_RET = 4
DK_RET = 256
DV_RET = 256
SB_W = H_SB * D_SB
RET_W = H_RET * DK_RET
N_SECTIONS = 7
SECTION_W = 1024
ROPE_BASE = 10000.0
EPS = 1e-6

V7X_VMEM_LIMIT_BYTES = 56 * 1024 * 1024
SB_BLOCK = 256
RET_CHUNK = 256
FFN_TF = 512

LOG2E = 1.4426950408889634


def _cparams(sem):
    return pltpu.CompilerParams(dimension_semantics=sem,
                                vmem_limit_bytes=V7X_VMEM_LIMIT_BYTES)


def _rmsnorm_rows(x, w):
    ms = jnp.mean(x * x, axis=-1, keepdims=True)
    return x * lax.rsqrt(ms + EPS) * w


def _dot(a, b):
    return jnp.dot(a, b, preferred_element_type=F32)


def _dot_nt(a, b):
    return lax.dot_general(a, b, (((1,), (1,)), ((), ())), preferred_element_type=F32)


def _dot_tn(a, b):
    return lax.dot_general(a, b, (((0,), (0,)), ((), ())), preferred_element_type=F32)


def _ffn_kernel(*refs, n_steps, final_norm, emit_bf16):
    x_ref, nw_ref, wg_ref, wu_ref, wd_ref = refs[:5]
    refs = refs[5:]
    fw_ref = None
    if final_norm:
        fw_ref, refs = refs[0], refs[1:]
    o_ref, refs = refs[0], refs[1:]
    if emit_bf16:
        wg16_ref, wu16_ref, wd16_ref, h_ref = refs
        wg16_ref[...] = wg_ref[...].astype(BF16)
        wu16_ref[...] = wu_ref[...].astype(BF16)
        wd16_ref[...] = wd_ref[...].astype(BF16)
        wg_ref, wu_ref, wd_ref = wg16_ref, wu16_ref, wd16_ref
    else:
        (h_ref,) = refs
    j = pl.program_id(1)

    @pl.when(j == 0)
    def _():
        x = x_ref[...]
        h_ref[...] = _rmsnorm_rows(x, nw_ref[...]).astype(BF16)
        o_ref[...] = x

    h = h_ref[...]
    g = _dot(h, wg_ref[...])
    u = _dot(h, wu_ref[...])
    a = (g * jax.nn.sigmoid(g) * (0.5 * u)).astype(BF16)
    o_ref[...] += _dot(a, wd_ref[...])

    if final_norm:
        @pl.when(j == n_steps - 1)
        def _():
            o_ref[...] = _rmsnorm_rows(o_ref[...], fw_ref[...])


def _ffn(x, norm_w, wg, wu, wd, final_w=None, *, tm):
    m = x.shape[0]
    n_steps = D_FF // FFN_TF
    emit_bf16 = wg.dtype == F32
    assert not emit_bf16 or m == tm
    w_specs = [
        pl.BlockSpec((D_MODEL, FFN_TF), lambda i, j: (0, j)),
        pl.BlockSpec((D_MODEL, FFN_TF), lambda i, j: (0, j)),
        pl.BlockSpec((FFN_TF, D_MODEL), lambda i, j: (j, 0)),
    ]
    in_specs = [
        pl.BlockSpec((tm, D_MODEL), lambda i, j: (i, 0)),
        pl.BlockSpec((1, D_MODEL), lambda i, j: (0, 0)),
    ] + w_specs
    args = [x, norm_w, wg, wu, wd]
    if final_w is not None:
        in_specs.append(pl.BlockSpec((1, D_MODEL), lambda i, j: (0, 0)))
        args.append(final_w)
    out_shape = [jax.ShapeDtypeStruct((m, D_MODEL), F32)]
    out_specs = [pl.BlockSpec((tm, D_MODEL), lambda i, j: (i, 0))]
    if emit_bf16:
        out_shape += [jax.ShapeDtypeStruct(w.shape, BF16) for w in (wg, wu, wd)]
        out_specs += w_specs
    out = pl.pallas_call(
        functools.partial(_ffn_kernel, n_steps=n_steps, final_norm=final_w is not None,
                          emit_bf16=emit_bf16),
        out_shape=out_shape,
        grid=(m // tm, n_steps),
        in_specs=in_specs,
        out_specs=out_specs,
        scratch_shapes=[pltpu.VMEM((tm, D_MODEL), BF16)],
        compiler_params=_cparams(("arbitrary", "arbitrary")),
        name="ffn_final" if final_w is not None else "ffn",
    )(*args)
    return (out[0], tuple(out[1:])) if emit_bf16 else out[0]


def _inproj_kernel(x_ref, nw_ref, w_ref, qn_ref, kn_ref, inv_ref,
                   sq_ref, sk_ref, sk16_ref, sv_ref, sv16_ref,
                   rq_ref, rk_ref, rv_ref, g_ref, *rest,
                   tm, n_split, tile_stride, pos_base, pos_mod, emit_bf16):
    if emit_bf16:
        w16_ref, h_ref, cos_ref, sin_ref, cos_row_ref, sin_row_ref = rest
        w16_ref[...] = w_ref[...].astype(BF16)
        w_ref = w16_ref
    else:
        h_ref, cos_ref, sin_ref, cos_row_ref, sin_row_ref = rest
    i = pl.program_id(0)
    j = pl.program_id(1)

    @pl.when(jnp.logical_and(i == 0, j == 0))
    def _():
        row = lax.broadcasted_iota(jnp.int32, (tm, DK_RET // 2), 0)
        ang = lax.rem(row, pos_mod).astype(F32) * inv_ref[...]
        cos_row_ref[...] = jnp.cos(ang)
        sin_row_ref[...] = jnp.sin(ang)

    @pl.when(j == 0)
    def _():
        h_ref[...] = _rmsnorm_rows(x_ref[...], nw_ref[...]).astype(BF16)
        ang = (pos_base + i * tile_stride).astype(F32) * inv_ref[...]
        ca, sa = jnp.cos(ang), jnp.sin(ang)
        cb, sb = cos_row_ref[...], sin_row_ref[...]
        cos_ref[...] = ca * cb - sa * sb
        sin_ref[...] = sa * cb + ca * sb

    section = j // n_split
    sb_heads = H_SB // n_split
    ret_heads = H_RET // n_split

    def proj():
        return _dot(h_ref[...], w_ref[...])

    def head(p, hd, width):
        return p[:, hd * width:(hd + 1) * width]

    @pl.when(section == 0)
    def _():
        p = proj()
        w = qn_ref[...] * (D_SB ** -0.5 * LOG2E)
        for hd in range(sb_heads):
            sq_ref[hd] = _rmsnorm_rows(head(p, hd, D_SB), w).astype(BF16)

    @pl.when(section == 1)
    def _():
        p = proj()
        for hd in range(sb_heads):
            k = _rmsnorm_rows(head(p, hd, D_SB), kn_ref[...])
            sk_ref[hd] = k
            sk16_ref[hd] = k.astype(BF16)

    @pl.when(section == 2)
    def _():
        p = proj()
        for hd in range(sb_heads):
            v = head(p, hd, D_SB)
            sv_ref[hd] = v
            sv16_ref[hd] = v.astype(BF16)

    def rotary(ph):
        half = DK_RET // 2
        x1, x2 = ph[:, :half], ph[:, half:]
        c, s = cos_ref[...], sin_ref[...]
        return jnp.concatenate([x1 * c - x2 * s, x1 * s + x2 * c], axis=-1)

    @pl.when(section == 3)
    def _():
        p = proj()
        for hd in range(ret_heads):
            rq_ref[hd] = rotary(head(p, hd, DK_RET)).astype(BF16)

    @pl.when(section == 4)
    def _():
        p = proj()
        for hd in range(ret_heads):
            rk_ref[hd] = rotary(head(p, hd, DK_RET)) * (DK_RET ** -0.5)

    @pl.when(section == 5)
    def _():
        p = proj()
        for hd in range(ret_heads):
            rv_ref[hd] = head(p, hd, DV_RET).astype(BF16)

    @pl.when(section == 6)
    def _():
        g_ref[...] = proj()


def _inproj(x, norm_w, w_in, qn, kn, inv_freq, *, tm, n_split, pos_base, pos_mod):
    m = x.shape[0]
    emit_bf16 = w_in.dtype == F32
    assert not emit_bf16 or m == tm
    tn = SECTION_W // n_split

    def part(section):
        return lambda j: jnp.clip(j - section * n_split, 0, n_split - 1)

    def head_spec(heads, width, section):
        pt = part(section)
        return pl.BlockSpec((heads // n_split, tm, width), lambda i, j: (pt(j), i, 0))

    gate_part = part(6)
    sb16 = jax.ShapeDtypeStruct((H_SB, m, D_SB), BF16)
    sb32 = jax.ShapeDtypeStruct((H_SB, m, D_SB), F32)
    ret16 = jax.ShapeDtypeStruct((H_RET, m, DK_RET), BF16)
    ret32 = jax.ShapeDtypeStruct((H_RET, m, DK_RET), F32)
    assert pos_mod >= m or tm % pos_mod == 0
    tile_stride = tm if pos_mod >= m else 0
    half = DK_RET // 2
    w_spec = pl.BlockSpec((D_MODEL, tn), lambda i, j: (0, j))
    out_shape = [sb16, sb32, sb16, sb32, sb16, ret16, ret32, ret16,
                 jax.ShapeDtypeStruct((m, SECTION_W), F32)]
    out_specs = [head_spec(H_SB, D_SB, 0), head_spec(H_SB, D_SB, 1), head_spec(H_SB, D_SB, 1),
                 head_spec(H_SB, D_SB, 2), head_spec(H_SB, D_SB, 2),
                 head_spec(H_RET, DK_RET, 3), head_spec(H_RET, DK_RET, 4),
                 head_spec(H_RET, DK_RET, 5),
                 pl.BlockSpec((tm, tn), lambda i, j: (i, gate_part(j)))]
    if emit_bf16:
        out_shape.append(jax.ShapeDtypeStruct(w_in.shape, BF16))
        out_specs.append(w_spec)
    out = pl.pallas_call(
        functools.partial(_inproj_kernel, tm=tm, n_split=n_split, tile_stride=tile_stride,
                          pos_base=pos_base, pos_mod=pos_mod, emit_bf16=emit_bf16),
        out_shape=out_shape,
        grid=(m // tm, N_SECTIONS * n_split),
        in_specs=[
            pl.BlockSpec((tm, D_MODEL), lambda i, j: (i, 0)),
            pl.BlockSpec((1, D_MODEL), lambda i, j: (0, 0)),
            w_spec,
            pl.BlockSpec((1, D_SB), lambda i, j: (0, 0)),
            pl.BlockSpec((1, D_SB), lambda i, j: (0, 0)),
            pl.BlockSpec((1, half), lambda i, j: (0, 0)),
        ],
        out_specs=out_specs,
        scratch_shapes=[pltpu.VMEM((tm, D_MODEL), BF16)] + [pltpu.VMEM((tm, half), F32)] * 4,
        compiler_params=_cparams(("arbitrary", "arbitrary")),
        name="inproj",
    )(x, norm_w, w_in, qn, kn, inv_freq)
    return (tuple(out[:9]), out[9]) if emit_bf16 else tuple(out)


def _softplus2(z):
    return jnp.maximum(z, 0.0) + jnp.log2(1.0 + jnp.exp2(-jnp.abs(z)))


def _suffix_matrix(n):
    r = lax.broadcasted_iota(jnp.int32, (2 * n, n), 0)
    c = lax.broadcasted_iota(jnp.int32, (2 * n, n), 1)
    return jnp.where(jnp.where(r >= n, r - n, r) >= c, 1.0, 0.0).astype(BF16)


def _sb_logits(q, k, mask):
    z = _dot_nt(q, k)
    sp = _softplus2(z)
    if mask is not None:
        sp = jnp.where(mask, sp, 0.0)
    hi = sp.astype(BF16)
    lo = (sp - hi.astype(F32)).astype(BF16)
    return z, jnp.concatenate([hi, lo], axis=1)


def _sb_suffix_sums(splits, tri):
    q = splits[0].shape[0]
    sums = _dot(jnp.concatenate(splits, axis=0), tri)
    return [sums[n * q:(n + 1) * q] for n in range(len(splits))]


def _sb_weights(z, incl, v, mask):
    a = jnp.exp2(z - incl)
    if mask is not None:
        a = jnp.where(mask, a, 0.0)
    return _dot(a.astype(BF16), v)


SB_DEAD_CARRY = 150.0


SB_HEADS_PER_STEP = 4


def _sb_prompt_kernel(q_ref, k_ref, v_ref, *rest):
    n_side = len(rest) // 2
    o_ref = rest[n_side]
    for src, dst in zip(rest[:n_side], rest[n_side + 1:]):
        dst[...] = src[...].astype(BF16)
    qi = pl.program_id(1)
    heads, _, d = q_ref.shape
    blk = SB_BLOCK
    tri = _suffix_matrix(blk)
    r = lax.broadcasted_iota(jnp.int32, (blk, blk), 0)
    c = lax.broadcasted_iota(jnp.int32, (blk, blk), 1)
    strict = c < r
    has_prev = qi > 0

    def kv(g, block):
        s0 = pl.multiple_of(block * blk, blk)
        return k_ref[g, pl.ds(s0, blk), :], v_ref[g, pl.ds(s0, blk), :]

    qs = [q_ref[g] for g in range(heads)]
    diag = [kv(g, qi) for g in range(heads)]
    left = [kv(g, jnp.maximum(qi - 1, 0)) for g in range(heads)]
    z_d, hl_d = zip(*[_sb_logits(qs[g], diag[g][0], strict) for g in range(heads)])
    z_l, hl_l = zip(*[_sb_logits(qs[g], left[g][0], None) for g in range(heads)])
    sums = _sb_suffix_sums(list(hl_d) + list(hl_l), tri)
    accs, carries = [], []
    for g in range(heads):
        incl_d = sums[g]
        incl_l = sums[heads + g] + incl_d[:, 0:1]
        acc = _sb_weights(z_d[g], incl_d, diag[g][1], strict)
        d_l = _sb_weights(z_l[g], incl_l, left[g][1], None)
        accs.append(acc + jnp.where(has_prev, d_l, 0.0))
        carries.append(jnp.where(has_prev, incl_l[:, 0:1], incl_d[:, 0:1]))
    accs, carries = tuple(accs), tuple(carries)

    def alive(carries):
        low = functools.reduce(jnp.minimum, carries)
        return (jnp.min(low) < SB_DEAD_CARRY).astype(jnp.int32)

    def cond(state):
        t, live, _, _ = state
        return jnp.logical_and(t >= 0, live > 0)

    def body(state):
        t, _, accs, carries = state
        blocks = [kv(g, t) for g in range(heads)]
        zs, hls = zip(*[_sb_logits(qs[g], blocks[g][0], None) for g in range(heads)])
        sums = _sb_suffix_sums(list(hls), tri)
        new_accs, new_carries = [], []
        for g in range(heads):
            incl = sums[g] + carries[g]
            new_accs.append(accs[g] + _sb_weights(zs[g], incl, blocks[g][1], None))
            new_carries.append(incl[:, 0:1])
        return t - 1, alive(new_carries), tuple(new_accs), tuple(new_carries)

    _, _, accs, _ = lax.while_loop(cond, body, (qi - 2, alive(carries), accs, carries))
    for g in range(heads):
        o_ref[:, g * d:(g + 1) * d] = accs[g].astype(o_ref.dtype)


def _sb_prompt(q16, k16, v16, side=()):
    h, s, d = q16.shape
    tq = SB_BLOCK
    g = SB_HEADS_PER_STEP
    n_groups, n_blocks = h // g, s // tq

    def side_spec(w):
        rows, cols = w.shape
        assert rows % (16 * n_blocks) == 0 and cols % (128 * n_groups) == 0
        return pl.BlockSpec((rows // n_blocks, cols // n_groups), lambda hg, qi: (qi, hg))

    side_specs = [side_spec(w) for w in side]
    out = pl.pallas_call(
        _sb_prompt_kernel,
        out_shape=[jax.ShapeDtypeStruct((s, h * d), BF16)]
        + [jax.ShapeDtypeStruct(w.shape, BF16) for w in side],
        grid=(n_groups, n_blocks),
        in_specs=[
            pl.BlockSpec((g, tq, d), lambda hg, qi: (hg, qi, 0)),
            pl.BlockSpec((g, s, d), lambda hg, qi: (hg, 0, 0)),
            pl.BlockSpec((g, s, d), lambda hg, qi: (hg, 0, 0)),
        ] + side_specs,
        out_specs=[pl.BlockSpec((tq, g * d), lambda hg, qi: (qi, hg))] + side_specs,
        compiler_params=_cparams(("arbitrary", "arbitrary")),
        name="sb_prompt",
    )(q16, k16, v16, *side)
    return out[0], tuple(out[1:])


def _sb_sample_kernel(q_ref, kn_ref, vn_ref, kc_hbm, vc_hbm, o_ref, kbuf, vbuf, sem, *, past):
    nh, t, d = q_ref.shape
    ht = nh * t
    blk = SB_BLOCK
    n_blocks = past // blk
    b = pl.program_id(0)
    slot = lax.rem(b, 2)

    def cache_copies(batch, block, to_slot):
        rows = pl.ds(pl.multiple_of(block * blk, blk), blk)
        return (pltpu.make_async_copy(kc_hbm.at[batch, :, rows, :], kbuf.at[to_slot], sem.at[0, to_slot]),
                pltpu.make_async_copy(vc_hbm.at[batch, :, rows, :], vbuf.at[to_slot], sem.at[1, to_slot]))

    @pl.when(b == 0)
    def _():
        for cp in cache_copies(0, n_blocks - 1, 0):
            cp.start()

    @pl.when(b + 1 < pl.num_programs(0))
    def _():
        for cp in cache_copies(b + 1, n_blocks - 1, 1 - slot):
            cp.start()

    q_all = q_ref[...].reshape(ht, d).astype(F32)
    q_head = lax.broadcasted_iota(jnp.int32, (ht, d), 0) // t
    q_masked = [jnp.where(q_head == h, q_all, 0.0).astype(BF16) for h in range(nh)]

    def logits(keys_of_head):
        z = _dot_nt(keys_of_head(0), q_masked[0])
        for h in range(1, nh):
            z = z + _dot_nt(keys_of_head(h), q_masked[h])
        return z

    def suffix_matrix(n):
        r = lax.broadcasted_iota(jnp.int32, (n, 2 * n), 0)
        c = lax.broadcasted_iota(jnp.int32, (n, 2 * n), 1)
        return jnp.where(jnp.where(c >= n, c - n, c) >= r, 1.0, 0.0).astype(BF16)

    def suffix_sum(sp, lmat):
        hi = sp.astype(BF16)
        lo = (sp - hi.astype(F32)).astype(BF16)
        return _dot(lmat, jnp.concatenate([hi, lo], axis=0))

    def emit(a_t, values_of_head, acc):
        a = a_t.T.astype(BF16)
        return [acc[h] + _dot(a[h * t:(h + 1) * t, :], values_of_head(h)) for h in range(nh)]

    z = logits(lambda h: kn_ref[h])
    s_idx = lax.broadcasted_iota(jnp.int32, (t, ht), 0)
    t_idx = lax.broadcasted_iota(jnp.int32, (t, ht), 1) % t
    strict = s_idx < t_idx
    sp = jnp.where(strict, _softplus2(z), 0.0)
    incl = suffix_sum(sp, suffix_matrix(t))
    a_t = jnp.where(strict, jnp.exp2(z - incl), 0.0)
    acc = emit(a_t, lambda h: vn_ref[h], [jnp.zeros((t, d), F32) for _ in range(nh)])
    carry = incl[0:1, :]

    lmat = suffix_matrix(blk)

    def cache_block(from_slot, acc, carry):
        z = logits(lambda h: kbuf[from_slot, h].astype(BF16))
        incl = suffix_sum(_softplus2(z), lmat) + carry
        acc = emit(jnp.exp2(z - incl), lambda h: vbuf[from_slot, h].astype(BF16), acc)
        return acc, incl[0:1, :]

    def alive(carry):
        return (jnp.min(carry) < SB_DEAD_CARRY).astype(jnp.int32)

    for cp in cache_copies(b, n_blocks - 1, slot):
        cp.wait()
    acc, carry = cache_block(slot, acc, carry)

    def cond(state):
        block, live, _, _ = state
        return jnp.logical_and(block >= 0, live > 0)

    def body(state):
        block, _, acc, carry = state
        copies = cache_copies(b, block, 2)
        for cp in copies:
            cp.start()
        for cp in copies:
            cp.wait()
        acc, carry = cache_block(2, list(acc), carry)
        return block - 1, alive(carry), tuple(acc), carry

    _, _, acc, _ = lax.while_loop(cond, body, (n_blocks - 2, alive(carry), tuple(acc), carry))
    for h in range(nh):
        o_ref[:, h * d:(h + 1) * d] = acc[h].astype(o_ref.dtype)


def _sb_sample(q16, k16, v16, cache_k, cache_v, *, t):
    h, m, d = q16.shape
    nb = m // t
    past = cache_k.shape[2]
    assert past % SB_BLOCK == 0
    new_spec = pl.BlockSpec((h, t, d), lambda b: (0, b, 0))
    cache_spec = pl.BlockSpec(memory_space=pl.ANY)
    return pl.pallas_call(
        functools.partial(_sb_sample_kernel, past=past),
        out_shape=jax.ShapeDtypeStruct((m, h * d), BF16),
        grid=(nb,),
        in_specs=[new_spec, new_spec, new_spec, cache_spec, cache_spec],
        out_specs=pl.BlockSpec((t, h * d), lambda b: (b, 0)),
        scratch_shapes=[pltpu.VMEM((3, h, SB_BLOCK, d), F32),
                        pltpu.VMEM((3, h, SB_BLOCK, d), F32),
                        pltpu.SemaphoreType.DMA((2, 3))],
        compiler_params=_cparams(("arbitrary",)),
        name="sb_sample",
    )(q16, k16, v16, cache_k, cache_v)


def _ret_log_decay(hd):
    return math.log(1.0 - 2.0 ** (-5.0 - hd))


def _ret_kernel(q_ref, k_ref, v_ref, g_ref, nw_ref, s0_ref, r_ref, st_ref,
                intra_ref, qdec_ref, kdec_ref, *, chunk):
    c = pl.program_id(1)

    @pl.when(jnp.logical_and(pl.program_id(0) == 0, c == 0))
    def _():
        row = lax.broadcasted_iota(jnp.int32, (chunk, chunk), 0)
        col = lax.broadcasted_iota(jnp.int32, (chunk, chunk), 1)
        diff = (row - col).astype(F32)
        pos = lax.broadcasted_iota(jnp.int32, (chunk, DK_RET), 0).astype(F32)
        for hd in range(H_RET):
            lg = _ret_log_decay(hd)
            intra_ref[hd] = jnp.where(row >= col, jnp.exp(lg * jnp.maximum(diff, 0.0)), 0.0)
            qdec_ref[hd] = jnp.exp(lg * (pos + 1.0))
            kdec_ref[hd] = jnp.exp(lg * (chunk - 1.0 - pos))

    @pl.when(c == 0)
    def _():
        st_ref[...] = s0_ref[...]

    for sub in range(q_ref.shape[1] // chunk):
        rows = slice(sub * chunk, (sub + 1) * chunk)
        for hd in range(H_RET):
            intra, q_dec, k_dec = intra_ref[hd], qdec_ref[hd], kdec_ref[hd]
            c_dec = math.exp(_ret_log_decay(hd) * chunk)
            q = q_ref[hd, rows, :]
            k = k_ref[hd, rows, :]
            v = v_ref[hd, rows, :]
            state = st_ref[hd]
            scores = (_dot_nt(q, k.astype(BF16)) * intra).astype(BF16)
            o = _dot(scores, v) + _dot(q, state.astype(BF16)) * q_dec
            st_ref[hd] = c_dec * state + _dot_tn((k * k_dec).astype(BF16), v)
            o = o * lax.rsqrt(jnp.mean(o * o, axis=-1, keepdims=True) + EPS)
            sl = slice(hd * DV_RET, (hd + 1) * DV_RET)
            gate = g_ref[rows, sl]
            r_ref[rows, sl] = (o * nw_ref[:, sl] * (gate * jax.nn.sigmoid(gate))).astype(r_ref.dtype)


def _retention(rq, rk, rv, gate, norm_w, state0, *, chunk, per_step):
    h, m, _ = rq.shape
    nb = state0.shape[0]
    rows = per_step * chunk
    nc = m // (nb * rows)
    qkv_spec = pl.BlockSpec((h, rows, DK_RET), lambda b, c: (0, b * nc + c, 0))
    st_spec = pl.BlockSpec((None, h, DK_RET, DV_RET), lambda b, c: (b, 0, 0, 0))
    return pl.pallas_call(
        functools.partial(_ret_kernel, chunk=chunk),
        out_shape=(jax.ShapeDtypeStruct((m, h * DV_RET), BF16),
                   jax.ShapeDtypeStruct(state0.shape, F32)),
        grid=(nb, nc),
        in_specs=[qkv_spec, qkv_spec, qkv_spec,
                  pl.BlockSpec((rows, h * DV_RET), lambda b, c: (b * nc + c, 0)),
                  pl.BlockSpec((1, h * DV_RET), lambda b, c: (0, 0)),
                  st_spec],
        out_specs=(pl.BlockSpec((rows, h * DV_RET), lambda b, c: (b * nc + c, 0)), st_spec),
        scratch_shapes=[pltpu.VMEM((h, chunk, chunk), F32),
                        pltpu.VMEM((h, chunk, DK_RET), F32),
                        pltpu.VMEM((h, chunk, DK_RET), F32)],
        compiler_params=_cparams(("arbitrary", "arbitrary")),
        name="retention",
    )(rq, rk, rv, gate, norm_w, state0)


def _outproj_kernel(x_ref, a_ref, b_ref, wa_ref, wb_ref, o_ref):
    o_ref[...] = x_ref[...] + _dot(a_ref[...], wa_ref[...]) + _dot(b_ref[...], wb_ref[...])


def _outproj(x, a_sb, a_ret, w_out, *, tm):
    m = x.shape[0]
    row = lambda i: (i, 0)
    return pl.pallas_call(
        _outproj_kernel,
        out_shape=jax.ShapeDtypeStruct((m, D_MODEL), F32),
        grid=(m // tm,),
        in_specs=[
            pl.BlockSpec((tm, D_MODEL), row),
            pl.BlockSpec((tm, SB_W), row),
            pl.BlockSpec((tm, RET_W), row),
            pl.BlockSpec((SB_W, D_MODEL), lambda i: (0, 0)),
            pl.BlockSpec((RET_W, D_MODEL), lambda i: (1, 0)),
        ],
        out_specs=pl.BlockSpec((tm, D_MODEL), row),
        compiler_params=_cparams(("arbitrary",)),
        name="outproj",
    )(x, a_sb, a_ret, w_out, w_out)


OUTPROJ_TK = 512


def _outproj_f32w_kernel(x_ref, a_ref, w_ref, o_ref, w16_ref):
    @pl.when(pl.program_id(0) == 0)
    def _():
        o_ref[...] = x_ref[...]

    w16_ref[...] = w_ref[...].astype(BF16)
    o_ref[...] += _dot(a_ref[...], w16_ref[...])


def _outproj_f32w(x, a, w_out):
    m = x.shape[0]
    d_mix = w_out.shape[0]
    return pl.pallas_call(
        _outproj_f32w_kernel,
        out_shape=(jax.ShapeDtypeStruct((m, D_MODEL), F32),
                   jax.ShapeDtypeStruct(w_out.shape, BF16)),
        grid=(d_mix // OUTPROJ_TK,),
        in_specs=[
            pl.BlockSpec((m, D_MODEL), lambda k: (0, 0)),
            pl.BlockSpec((m, OUTPROJ_TK), lambda k: (0, k)),
            pl.BlockSpec((OUTPROJ_TK, D_MODEL), lambda k: (k, 0)),
        ],
        out_specs=(pl.BlockSpec((m, D_MODEL), lambda k: (0, 0)),
                   pl.BlockSpec((OUTPROJ_TK, D_MODEL), lambda k: (k, 0))),
        compiler_params=_cparams(("arbitrary",)),
        name="outproj_f32w",
    )(x, a, w_out)


def _mixers(x1, weights, sb_fn, state0, *, tm, proj_tm, proj_split, pos_base, pos_mod, chunk,
            chunks_per_step):
    (nmix, w_in, qn, kn, ron, w_out, inv_freq) = weights
    f32w = w_in.dtype == F32
    proj = _inproj(x1, nmix, w_in, qn, kn, inv_freq, tm=proj_tm, n_split=proj_split,
                   pos_base=pos_base, pos_mod=pos_mod)
    if f32w:
        proj, w_in = proj
    sq, sk, sk16, sv, sv16, rq, rk, rv, gate = proj
    a_sb, sb_side = sb_fn(sq, sk16, sv16)
    a_ret, state = _retention(rq, rk, rv, gate, ron, state0, chunk=chunk, per_step=chunks_per_step)
    if f32w:
        x2, w_out = _outproj_f32w(x1, jnp.concatenate([a_sb, a_ret], axis=1), w_out)
    else:
        x2 = _outproj(x1, a_sb, a_ret, w_out, tm=tm)
    return x2, sk, sv, state, (w_in, w_out), sb_side


def kernel(x_prompt, x_sample, cache_sb_k, cache_sb_v, state_ret, ffn1_norm, ffn1_w_gate, ffn1_w_up, ffn1_w_down, mix_norm, w_in, sb_q_norm, sb_k_norm, ret_out_norm, w_out, ffn2_norm, ffn2_w_gate, ffn2_w_up, ffn2_w_down, final_norm):
    depth = ffn1_norm.shape[0]
    assert depth == 1
    nb_p, seq, _ = x_prompt.shape
    nb_s, dec_seq, _ = x_sample.shape
    past = cache_sb_k.shape[3]
    assert nb_p == 1

    half = DK_RET // 2
    inv_freq = (ROPE_BASE ** (-jnp.arange(half, dtype=F32) / half)).reshape(1, half)
    l = 0
    n1, n2, nf = ffn1_norm[l][None], ffn2_norm[l][None], final_norm[l][None]

    def mixer_weights(w_in_, w_out_):
        return (mix_norm[l][None], w_in_, sb_q_norm[l][None], sb_k_norm[l][None],
                ret_out_norm[l][None], w_out_, inv_freq)

    ms = nb_s * dec_seq
    xs = x_sample.reshape(ms, D_MODEL)
    xp = x_prompt.reshape(seq, D_MODEL)
    x1s, ffn1_16 = _ffn(xs, n1, ffn1_w_gate[l], ffn1_w_up[l], ffn1_w_down[l], tm=ms)
    x1p = _ffn(xp, n1, *ffn1_16, tm=1024)

    def sb_s(sq, sk16, sv16):
        return _sb_sample(sq, sk16, sv16, cache_sb_k[l], cache_sb_v[l], t=dec_seq), ()

    x2s, sks, svs, sts, (w_in16, w_out16), _ = _mixers(
        x1s, mixer_weights(w_in[l], w_out[l]), sb_s, state_ret[l],
        tm=ms, proj_tm=ms, proj_split=4, pos_base=past, pos_mod=dec_seq, chunk=dec_seq,
        chunks_per_step=1)

    sb_p = functools.partial(_sb_prompt, side=(ffn2_w_gate[l], ffn2_w_up[l], ffn2_w_down[l]))
    zero_state = jnp.zeros((1, H_RET, DK_RET, DV_RET), F32)
    x2p, skp, svp, stp, _, ffn2_16 = _mixers(
        x1p, mixer_weights(w_in16, w_out16), sb_p, zero_state,
        tm=512, proj_tm=1024, proj_split=2, pos_base=0, pos_mod=seq, chunk=RET_CHUNK,
        chunks_per_step=2)

    ys = _ffn(x2s, n2, *ffn2_16, nf, tm=ms)
    yp = _ffn(x2p, n2, *ffn2_16, nf, tm=1024)

    def cache_layout(t):
        return t.reshape(H_SB, nb_s, dec_seq, D_SB).transpose(1, 0, 2, 3)[None]

    return (yp.reshape(1, seq, D_MODEL), ys.reshape(nb_s, dec_seq, D_MODEL),
            skp[None, None], svp[None, None], stp[None],
            cache_layout(sks), cache_layout(svs), sts[None])
```

```python
import functools
import math

import jax
import jax.numpy as jnp
from jax import lax
from jax.experimental import pallas as pl
from jax.experimental.pallas import tpu as pltpu

F32 = jnp.float32
BF16 = jnp.bfloat16

D_MODEL = 2048
D_FF = 5632
H_SB = 8
D_SB = 128
H_RET = 4
DK_RET = 256
DV_RET = 256
SB_W = H_SB * D_SB
RET_W = H_RET * DK_RET
N_SECTIONS = 7
SECTION_W = 1024
ROPE_BASE = 10000.0
EPS = 1e-6

V7X_VMEM_LIMIT_BYTES = 56 * 1024 * 1024
SB_BLOCK = 256
RET_CHUNK = 256
FFN_TF = 512

LOG2E = 1.4426950408889634


def _cparams(sem):
    return pltpu.CompilerParams(dimension_semantics=sem,
                                vmem_limit_bytes=V7X_VMEM_LIMIT_BYTES)


def _rmsnorm_rows(x, w):
    ms = jnp.mean(x * x, axis=-1, keepdims=True)
    return x * lax.rsqrt(ms + EPS) * w


def _dot(a, b):
    return jnp.dot(a, b, preferred_element_type=F32)


def _dot_nt(a, b):
    return lax.dot_general(a, b, (((1,), (1,)), ((), ())), preferred_element_type=F32)


def _dot_tn(a, b):
    return lax.dot_general(a, b, (((0,), (0,)), ((), ())), preferred_element_type=F32)


def _ffn_kernel(*refs, n_steps, final_norm, emit_bf16):
    x_ref, nw_ref, wg_ref, wu_ref, wd_ref = refs[:5]
    refs = refs[5:]
    fw_ref = None
    if final_norm:
        fw_ref, refs = refs[0], refs[1:]
    o_ref, refs = refs[0], refs[1:]
    if emit_bf16:
        wg16_ref, wu16_ref, wd16_ref, h_ref = refs
        wg16_ref[...] = wg_ref[...].astype(BF16)
        wu16_ref[...] = wu_ref[...].astype(BF16)
        wd16_ref[...] = wd_ref[...].astype(BF16)
        wg_ref, wu_ref, wd_ref = wg16_ref, wu16_ref, wd16_ref
    else:
        (h_ref,) = refs
    j = pl.program_id(1)

    @pl.when(j == 0)
    def _():
        x = x_ref[...]
        h_ref[...] = _rmsnorm_rows(x, nw_ref[...]).astype(BF16)
        o_ref[...] = x

    h = h_ref[...]
    g = _dot(h, wg_ref[...])
    u = _dot(h, wu_ref[...])
    a = (g * jax.nn.sigmoid(g) * (0.5 * u)).astype(BF16)
    o_ref[...] += _dot(a, wd_ref[...])

    if final_norm:
        @pl.when(j == n_steps - 1)
        def _():
            o_ref[...] = _rmsnorm_rows(o_ref[...], fw_ref[...])


def _ffn(x, norm_w, wg, wu, wd, final_w=None, *, tm):
    m = x.shape[0]
    n_steps = D_FF // FFN_TF
    emit_bf16 = wg.dtype == F32
    assert not emit_bf16 or m == tm
    w_specs = [
        pl.BlockSpec((D_MODEL, FFN_TF), lambda i, j: (0, j)),
        pl.BlockSpec((D_MODEL, FFN_TF), lambda i, j: (0, j)),
        pl.BlockSpec((FFN_TF, D_MODEL), lambda i, j: (j, 0)),
    ]
    in_specs = [
        pl.BlockSpec((tm, D_MODEL), lambda i, j: (i, 0)),
        pl.BlockSpec((1, D_MODEL), lambda i, j: (0, 0)),
    ] + w_specs
    args = [x, norm_w, wg, wu, wd]
    if final_w is not None:
        in_specs.append(pl.BlockSpec((1, D_MODEL), lambda i, j: (0, 0)))
        args.append(final_w)
    out_shape = [jax.ShapeDtypeStruct((m, D_MODEL), F32)]
    out_specs = [pl.BlockSpec((tm, D_MODEL), lambda i, j: (i, 0))]
    if emit_bf16:
        out_shape += [jax.ShapeDtypeStruct(w.shape, BF16) for w in (wg, wu, wd)]
        out_specs += w_specs
    out = pl.pallas_call(
        functools.partial(_ffn_kernel, n_steps=n_steps, final_norm=final_w is not None,
                          emit_bf16=emit_bf16),
        out_shape=out_shape,
        grid=(m // tm, n_steps),
        in_specs=in_specs,
        out_specs=out_specs,
        scratch_shapes=[pltpu.VMEM((tm, D_MODEL), BF16)],
        compiler_params=_cparams(("arbitrary", "arbitrary")),
        name="ffn_final" if final_w is not None else "ffn",
    )(*args)
    return (out[0], tuple(out[1:])) if emit_bf16 else out[0]


def _inproj_kernel(x_ref, nw_ref, w_ref, qn_ref, kn_ref, inv_ref,
                   sq_ref, sk_ref, sk16_ref, sv_ref, sv16_ref,
                   rq_ref, rk_ref, rv_ref, g_ref, *rest,
                   tm, n_split, tile_stride, pos_base, pos_mod, emit_bf16):
    if emit_bf16:
        w16_ref, h_ref, cos_ref, sin_ref, cos_row_ref, sin_row_ref = rest
        w16_ref[...] = w_ref[...].astype(BF16)
        w_ref = w16_ref
    else:
        h_ref, cos_ref, sin_ref, cos_row_ref, sin_row_ref = rest
    i = pl.program_id(0)
    j = pl.program_id(1)

    @pl.when(jnp.logical_and(i == 0, j == 0))
    def _():
        row = lax.broadcasted_iota(jnp.int32, (tm, DK_RET // 2), 0)
        ang = lax.rem(row, pos_mod).astype(F32) * inv_ref[...]
        cos_row_ref[...] = jnp.cos(ang)
        sin_row_ref[...] = jnp.sin(ang)

    @pl.when(j == 0)
    def _():
        h_ref[...] = _rmsnorm_rows(x_ref[...], nw_ref[...]).astype(BF16)
        ang = (pos_base + i * tile_stride).astype(F32) * inv_ref[...]
        ca, sa = jnp.cos(ang), jnp.sin(ang)
        cb, sb = cos_row_ref[...], sin_row_ref[...]
        cos_ref[...] = ca * cb - sa * sb
        sin_ref[...] = sa * cb + ca * sb

    section = j // n_split
    sb_heads = H_SB // n_split
    ret_heads = H_RET // n_split

    def proj():
        return _dot(h_ref[...], w_ref[...])

    def head(p, hd, width):
        return p[:, hd * width:(hd + 1) * width]

    @pl.when(section == 0)
    def _():
        p = proj()
        w = qn_ref[...] * (D_SB ** -0.5 * LOG2E)
        for hd in range(sb_heads):
            sq_ref[hd] = _rmsnorm_rows(head(p, hd, D_SB), w).astype(BF16)

    @pl.when(section == 1)
    def _():
        p = proj()
        for hd in range(sb_heads):
            k = _rmsnorm_rows(head(p, hd, D_SB), kn_ref[...])
            sk_ref[hd] = k
            sk16_ref[hd] = k.astype(BF16)

    @pl.when(section == 2)
    def _():
        p = proj()
        for hd in range(sb_heads):
            v = head(p, hd, D_SB)
            sv_ref[hd] = v
            sv16_ref[hd] = v.astype(BF16)

    def rotary(ph):
        half = DK_RET // 2
        x1, x2 = ph[:, :half], ph[:, half:]
        c, s = cos_ref[...], sin_ref[...]
        return jnp.concatenate([x1 * c - x2 * s, x1 * s + x2 * c], axis=-1)

    @pl.when(section == 3)
    def _():
        p = proj()
        for hd in range(ret_heads):
            rq_ref[hd] = rotary(head(p, hd, DK_RET)).astype(BF16)

    @pl.when(section == 4)
    def _():
        p = proj()
        for hd in range(ret_heads):
            rk_ref[hd] = rotary(head(p, hd, DK_RET)) * (DK_RET ** -0.5)

    @pl.when(section == 5)
    def _():
        p = proj()
        for hd in range(ret_heads):
            rv_ref[hd] = head(p, hd, DV_RET).astype(BF16)

    @pl.when(section == 6)
    def _():
        g_ref[...] = proj()


def _inproj(x, norm_w, w_in, qn, kn, inv_freq, *, tm, n_split, pos_base, pos_mod):
    m = x.shape[0]
    emit_bf16 = w_in.dtype == F32
    assert not emit_bf16 or m == tm
    tn = SECTION_W // n_split

    def part(section):
        return lambda j: jnp.clip(j - section * n_split, 0, n_split - 1)

    def head_spec(heads, width, section):
        pt = part(section)
        return pl.BlockSpec((heads // n_split, tm, width), lambda i, j: (pt(j), i, 0))

    gate_part = part(6)
    sb16 = jax.ShapeDtypeStruct((H_SB, m, D_SB), BF16)
    sb32 = jax.ShapeDtypeStruct((H_SB, m, D_SB), F32)
    ret16 = jax.ShapeDtypeStruct((H_RET, m, DK_RET), BF16)
    ret32 = jax.ShapeDtypeStruct((H_RET, m, DK_RET), F32)
    assert pos_mod >= m or tm % pos_mod == 0
    tile_stride = tm if pos_mod >= m else 0
    half = DK_RET // 2
    w_spec = pl.BlockSpec((D_MODEL, tn), lambda i, j: (0, j))
    out_shape = [sb16, sb32, sb16, sb32, sb16, ret16, ret32, ret16,
                 jax.ShapeDtypeStruct((m, SECTION_W), F32)]
    out_specs = [head_spec(H_SB, D_SB, 0), head_spec(H_SB, D_SB, 1), head_spec(H_SB, D_SB, 1),
                 head_spec(H_SB, D_SB, 2), head_spec(H_SB, D_SB, 2),
                 head_spec(H_RET, DK_RET, 3), head_spec(H_RET, DK_RET, 4),
                 head_spec(H_RET, DK_RET, 5),
                 pl.BlockSpec((tm, tn), lambda i, j: (i, gate_part(j)))]
    if emit_bf16:
        out_shape.append(jax.ShapeDtypeStruct(w_in.shape, BF16))
        out_specs.append(w_spec)
    out = pl.pallas_call(
        functools.partial(_inproj_kernel, tm=tm, n_split=n_split, tile_stride=tile_stride,
                          pos_base=pos_base, pos_mod=pos_mod, emit_bf16=emit_bf16),
        out_shape=out_shape,
        grid=(m // tm, N_SECTIONS * n_split),
        in_specs=[
            pl.BlockSpec((tm, D_MODEL), lambda i, j: (i, 0)),
            pl.BlockSpec((1, D_MODEL), lambda i, j: (0, 0)),
            w_spec,
            pl.BlockSpec((1, D_SB), lambda i, j: (0, 0)),
            pl.BlockSpec((1, D_SB), lambda i, j: (0, 0)),
            pl.BlockSpec((1, half), lambda i, j: (0, 0)),
        ],
        out_specs=out_specs,
        scratch_shapes=[pltpu.VMEM((tm, D_MODEL), BF16)] + [pltpu.VMEM((tm, half), F32)] * 4,
        compiler_params=_cparams(("arbitrary", "arbitrary")),
        name="inproj",
    )(x, norm_w, w_in, qn, kn, inv_freq)
    return (tuple(out[:9]), out[9]) if emit_bf16 else tuple(out)


def _softplus2(z):
    return jnp.maximum(z, 0.0) + jnp.log2(1.0 + jnp.exp2(-jnp.abs(z)))


def _suffix_matrix(n):
    r = lax.broadcasted_iota(jnp.int32, (2 * n, n), 0)
    c = lax.broadcasted_iota(jnp.int32, (2 * n, n), 1)
    return jnp.where(jnp.where(r >= n, r - n, r) >= c, 1.0, 0.0).astype(BF16)


def _sb_logits(q, k, mask):
    z = _dot_nt(q, k)
    sp = _softplus2(z)
    if mask is not None:
        sp = jnp.where(mask, sp, 0.0)
    hi = sp.astype(BF16)
    lo = (sp - hi.astype(F32)).astype(BF16)
    return z, jnp.concatenate([hi, lo], axis=1)


def _sb_suffix_sums(splits, tri):
    q = splits[0].shape[0]
    sums = _dot(jnp.concatenate(splits, axis=0), tri)
    return [sums[n * q:(n + 1) * q] for n in range(len(splits))]


def _sb_weights(z, incl, v, mask):
    a = jnp.exp2(z - incl)
    if mask is not None:
        a = jnp.where(mask, a, 0.0)
    return _dot(a.astype(BF16), v)


SB_DEAD_CARRY = 150.0


SB_HEADS_PER_STEP = 4


def _sb_prompt_kernel(q_ref, k_ref, v_ref, *rest):
    n_side = len(rest) // 2
    o_ref = rest[n_side]
    for src, dst in zip(rest[:n_side], rest[n_side + 1:]):
        dst[...] = src[...].astype(BF16)
    qi = pl.program_id(1)
    heads, _, d = q_ref.shape
    blk = SB_BLOCK
    tri = _suffix_matrix(blk)
    r = lax.broadcasted_iota(jnp.int32, (blk, blk), 0)
    c = lax.broadcasted_iota(jnp.int32, (blk, blk), 1)
    strict = c < r
    has_prev = qi > 0

    def kv(g, block):
        s0 = pl.multiple_of(block * blk, blk)
        return k_ref[g, pl.ds(s0, blk), :], v_ref[g, pl.ds(s0, blk), :]

    qs = [q_ref[g] for g in range(heads)]
    diag = [kv(g, qi) for g in range(heads)]
    left = [kv(g, jnp.maximum(qi - 1, 0)) for g in range(heads)]
    z_d, hl_d = zip(*[_sb_logits(qs[g], diag[g][0], strict) for g in range(heads)])
    z_l, hl_l = zip(*[_sb_logits(qs[g], left[g][0], None) for g in range(heads)])
    sums = _sb_suffix_sums(list(hl_d) + list(hl_l), tri)
    accs, carries = [], []
    for g in range(heads):
        incl_d = sums[g]
        incl_l = sums[heads + g] + incl_d[:, 0:1]
        acc = _sb_weights(z_d[g], incl_d, diag[g][1], strict)
        d_l = _sb_weights(z_l[g], incl_l, left[g][1], None)
        accs.append(acc + jnp.where(has_prev, d_l, 0.0))
        carries.append(jnp.where(has_prev, incl_l[:, 0:1], incl_d[:, 0:1]))
    accs, carries = tuple(accs), tuple(carries)

    def alive(carries):
        low = functools.reduce(jnp.minimum, carries)
        return (jnp.min(low) < SB_DEAD_CARRY).astype(jnp.int32)

    def cond(state):
        t, live, _, _ = state
        return jnp.logical_and(t >= 0, live > 0)

    def body(state):
        t, _, accs, carries = state
        blocks = [kv(g, t) for g in range(heads)]
        zs, hls = zip(*[_sb_logits(qs[g], blocks[g][0], None) for g in range(heads)])
        sums = _sb_suffix_sums(list(hls), tri)
        new_accs, new_carries = [], []
        for g in range(heads):
            incl = sums[g] + carries[g]
            new_accs.append(accs[g] + _sb_weights(zs[g], incl, blocks[g][1], None))
            new_carries.append(incl[:, 0:1])
        return t - 1, alive(new_carries), tuple(new_accs), tuple(new_carries)

    _, _, accs, _ = lax.while_loop(cond, body, (qi - 2, alive(carries), accs, carries))
    for g in range(heads):
        o_ref[:, g * d:(g + 1) * d] = accs[g].astype(o_ref.dtype)


def _sb_prompt(q16, k16, v16, side=()):
    h, s, d = q16.shape
    tq = SB_BLOCK
    g = SB_HEADS_PER_STEP
    n_groups, n_blocks = h // g, s // tq

    def side_spec(w):
        rows, cols = w.shape
        assert rows % (16 * n_blocks) == 0 and cols % (128 * n_groups) == 0
        return pl.BlockSpec((rows // n_blocks, cols // n_groups), lambda hg, qi: (qi, hg))

    side_specs = [side_spec(w) for w in side]
    out = pl.pallas_call(
        _sb_prompt_kernel,
        out_shape=[jax.ShapeDtypeStruct((s, h * d), BF16)]
        + [jax.ShapeDtypeStruct(w.shape, BF16) for w in side],
        grid=(n_groups, n_blocks),
        in_specs=[
            pl.BlockSpec((g, tq, d), lambda hg, qi: (hg, qi, 0)),
            pl.BlockSpec((g, s, d), lambda hg, qi: (hg, 0, 0)),
            pl.BlockSpec((g, s, d), lambda hg, qi: (hg, 0, 0)),
        ] + side_specs,
        out_specs=[pl.BlockSpec((tq, g * d), lambda hg, qi: (qi, hg))] + side_specs,
        compiler_params=_cparams(("arbitrary", "arbitrary")),
        name="sb_prompt",
    )(q16, k16, v16, *side)
    return out[0], tuple(out[1:])


def _sb_sample_kernel(q_ref, kn_ref, vn_ref, kc_hbm, vc_hbm, o_ref, kbuf, vbuf, sem, *, past):
    nh, t, d = q_ref.shape
    ht = nh * t
    blk = SB_BLOCK
    n_blocks = past // blk
    b = pl.program_id(0)
    slot = lax.rem(b, 2)

    def cache_copies(batch, block, to_slot):
        rows = pl.ds(pl.multiple_of(block * blk, blk), blk)
        return (pltpu.make_async_copy(kc_hbm.at[batch, :, rows, :], kbuf.at[to_slot], sem.at[0, to_slot]),
                pltpu.make_async_copy(vc_hbm.at[batch, :, rows, :], vbuf.at[to_slot], sem.at[1, to_slot]))

    @pl.when(b == 0)
    def _():
        for cp in cache_copies(0, n_blocks - 1, 0):
            cp.start()

    @pl.when(b + 1 < pl.num_programs(0))
    def _():
        for cp in cache_copies(b + 1, n_blocks - 1, 1 - slot):
            cp.start()

    q_all = q_ref[...].reshape(ht, d).astype(F32)
    q_head = lax.broadcasted_iota(jnp.int32, (ht, d), 0) // t
    q_masked = [jnp.where(q_head == h, q_all, 0.0).astype(BF16) for h in range(nh)]

    def logits(keys_of_head):
        z = _dot_nt(keys_of_head(0), q_masked[0])
        for h in range(1, nh):
            z = z + _dot_nt(keys_of_head(h), q_masked[h])
        return z

    def suffix_matrix(n):
        r = lax.broadcasted_iota(jnp.int32, (n, 2 * n), 0)
        c = lax.broadcasted_iota(jnp.int32, (n, 2 * n), 1)
        return jnp.where(jnp.where(c >= n, c - n, c) >= r, 1.0, 0.0).astype(BF16)

    def suffix_sum(sp, lmat):
        hi = sp.astype(BF16)
        lo = (sp - hi.astype(F32)).astype(BF16)
        return _dot(lmat, jnp.concatenate([hi, lo], axis=0))

    def emit(a_t, values_of_head, acc):
        a = a_t.T.astype(BF16)
        return [acc[h] + _dot(a[h * t:(h + 1) * t, :], values_of_head(h)) for h in range(nh)]

    z = logits(lambda h: kn_ref[h])
    s_idx = lax.broadcasted_iota(jnp.int32, (t, ht), 0)
    t_idx = lax.broadcasted_iota(jnp.int32, (t, ht), 1) % t
    strict = s_idx < t_idx
    sp = jnp.where(strict, _softplus2(z), 0.0)
    incl = suffix_sum(sp, suffix_matrix(t))
    a_t = jnp.where(strict, jnp.exp2(z - incl), 0.0)
    acc = emit(a_t, lambda h: vn_ref[h], [jnp.zeros((t, d), F32) for _ in range(nh)])
    carry = incl[0:1, :]

    lmat = suffix_matrix(blk)

    def cache_block(from_slot, acc, carry):
        z = logits(lambda h: kbuf[from_slot, h].astype(BF16))
        incl = suffix_sum(_softplus2(z), lmat) + carry
        acc = emit(jnp.exp2(z - incl), lambda h: vbuf[from_slot, h].astype(BF16), acc)
        return acc, incl[0:1, :]

    def alive(carry):
        return (jnp.min(carry) < SB_DEAD_CARRY).astype(jnp.int32)

    for cp in cache_copies(b, n_blocks - 1, slot):
        cp.wait()
    acc, carry = cache_block(slot, acc, carry)

    def cond(state):
        block, live, _, _ = state
        return jnp.logical_and(block >= 0, live > 0)

    def body(state):
        block, _, acc, carry = state
        copies = cache_copies(b, block, 2)
        for cp in copies:
            cp.start()
        for cp in copies:
            cp.wait()
        acc, carry = cache_block(2, list(acc), carry)
        return block - 1, alive(carry), tuple(acc), carry

    _, _, acc, _ = lax.while_loop(cond, body, (n_blocks - 2, alive(carry), tuple(acc), carry))
    for h in range(nh):
        o_ref[:, h * d:(h + 1) * d] = acc[h].astype(o_ref.dtype)


def _sb_sample(q16, k16, v16, cache_k, cache_v, *, t):
    h, m, d = q16.shape
    nb = m // t
    past = cache_k.shape[2]
    assert past % SB_BLOCK == 0
    new_spec = pl.BlockSpec((h, t, d), lambda b: (0, b, 0))
    cache_spec = pl.BlockSpec(memory_space=pl.ANY)
    return pl.pallas_call(
        functools.partial(_sb_sample_kernel, past=past),
        out_shape=jax.ShapeDtypeStruct((m, h * d), BF16),
        grid=(nb,),
        in_specs=[new_spec, new_spec, new_spec, cache_spec, cache_spec],
        out_specs=pl.BlockSpec((t, h * d), lambda b: (b, 0)),
        scratch_shapes=[pltpu.VMEM((3, h, SB_BLOCK, d), F32),
                        pltpu.VMEM((3, h, SB_BLOCK, d), F32),
                        pltpu.SemaphoreType.DMA((2, 3))],
        compiler_params=_cparams(("arbitrary",)),
        name="sb_sample",
    )(q16, k16, v16, cache_k, cache_v)


def _ret_log_decay(hd):
    return math.log(1.0 - 2.0 ** (-5.0 - hd))


def _ret_kernel(q_ref, k_ref, v_ref, g_ref, nw_ref, s0_ref, r_ref, st_ref,
                intra_ref, qdec_ref, kdec_ref, *, chunk):
    c = pl.program_id(1)

    @pl.when(jnp.logical_and(pl.program_id(0) == 0, c == 0))
    def _():
        row = lax.broadcasted_iota(jnp.int32, (chunk, chunk), 0)
        col = lax.broadcasted_iota(jnp.int32, (chunk, chunk), 1)
        diff = (row - col).astype(F32)
        pos = lax.broadcasted_iota(jnp.int32, (chunk, DK_RET), 0).astype(F32)
        for hd in range(H_RET):
            lg = _ret_log_decay(hd)
            intra_ref[hd] = jnp.where(row >= col, jnp.exp(lg * jnp.maximum(diff, 0.0)), 0.0)
            qdec_ref[hd] = jnp.exp(lg * (pos + 1.0))
            kdec_ref[hd] = jnp.exp(lg * (chunk - 1.0 - pos))

    @pl.when(c == 0)
    def _():
        st_ref[...] = s0_ref[...]

    for sub in range(q_ref.shape[1] // chunk):
        rows = slice(sub * chunk, (sub + 1) * chunk)
        for hd in range(H_RET):
            intra, q_dec, k_dec = intra_ref[hd], qdec_ref[hd], kdec_ref[hd]
            c_dec = math.exp(_ret_log_decay(hd) * chunk)
            q = q_ref[hd, rows, :]
            k = k_ref[hd, rows, :]
            v = v_ref[hd, rows, :]
            state = st_ref[hd]
            scores = (_dot_nt(q, k.astype(BF16)) * intra).astype(BF16)
            o = _dot(scores, v) + _dot(q, state.astype(BF16)) * q_dec
            st_ref[hd] = c_dec * state + _dot_tn((k * k_dec).astype(BF16), v)
            o = o * lax.rsqrt(jnp.mean(o * o, axis=-1, keepdims=True) + EPS)
            sl = slice(hd * DV_RET, (hd + 1) * DV_RET)
            gate = g_ref[rows, sl]
            r_ref[rows, sl] = (o * nw_ref[:, sl] * (gate * jax.nn.sigmoid(gate))).astype(r_ref.dtype)


def _retention(rq, rk, rv, gate, norm_w, state0, *, chunk, per_step):
    h, m, _ = rq.shape
    nb = state0.shape[0]
    rows = per_step * chunk
    nc = m // (nb * rows)
    qkv_spec = pl.BlockSpec((h, rows, DK_RET), lambda b, c: (0, b * nc + c, 0))
    st_spec = pl.BlockSpec((None, h, DK_RET, DV_RET), lambda b, c: (b, 0, 0, 0))
    return pl.pallas_call(
        functools.partial(_ret_kernel, chunk=chunk),
        out_shape=(jax.ShapeDtypeStruct((m, h * DV_RET), BF16),
                   jax.ShapeDtypeStruct(state0.shape, F32)),
        grid=(nb, nc),
        in_specs=[qkv_spec, qkv_spec, qkv_spec,
                  pl.BlockSpec((rows, h * DV_RET), lambda b, c: (b * nc + c, 0)),
                  pl.BlockSpec((1, h * DV_RET), lambda b, c: (0, 0)),
                  st_spec],
        out_specs=(pl.BlockSpec((rows, h * DV_RET), lambda b, c: (b * nc + c, 0)), st_spec),
        scratch_shapes=[pltpu.VMEM((h, chunk, chunk), F32),
                        pltpu.VMEM((h, chunk, DK_RET), F32),
                        pltpu.VMEM((h, chunk, DK_RET), F32)],
        compiler_params=_cparams(("arbitrary", "arbitrary")),
        name="retention",
    )(rq, rk, rv, gate, norm_w, state0)


def _outproj_kernel(x_ref, a_ref, b_ref, wa_ref, wb_ref, o_ref):
    o_ref[...] = x_ref[...] + _dot(a_ref[...], wa_ref[...]) + _dot(b_ref[...], wb_ref[...])


def _outproj(x, a_sb, a_ret, w_out, *, tm):
    m = x.shape[0]
    row = lambda i: (i, 0)
    return pl.pallas_call(
        _outproj_kernel,
        out_shape=jax.ShapeDtypeStruct((m, D_MODEL), F32),
        grid=(m // tm,),
        in_specs=[
            pl.BlockSpec((tm, D_MODEL), row),
            pl.BlockSpec((tm, SB_W), row),
            pl.BlockSpec((tm, RET_W), row),
            pl.BlockSpec((SB_W, D_MODEL), lambda i: (0, 0)),
            pl.BlockSpec((RET_W, D_MODEL), lambda i: (1, 0)),
        ],
        out_specs=pl.BlockSpec((tm, D_MODEL), row),
        compiler_params=_cparams(("arbitrary",)),
        name="outproj",
    )(x, a_sb, a_ret, w_out, w_out)


OUTPROJ_TK = 512


def _outproj_f32w_kernel(x_ref, a_ref, w_ref, o_ref, w16_ref):
    @pl.when(pl.program_id(0) == 0)
    def _():
        o_ref[...] = x_ref[...]

    w16_ref[...] = w_ref[...].astype(BF16)
    o_ref[...] += _dot(a_ref[...], w16_ref[...])


def _outproj_f32w(x, a, w_out):
    m = x.shape[0]
    d_mix = w_out.shape[0]
    return pl.pallas_call(
        _outproj_f32w_kernel,
        out_shape=(jax.ShapeDtypeStruct((m, D_MODEL), F32),
                   jax.ShapeDtypeStruct(w_out.shape, BF16)),
        grid=(d_mix // OUTPROJ_TK,),
        in_specs=[
            pl.BlockSpec((m, D_MODEL), lambda k: (0, 0)),
            pl.BlockSpec((m, OUTPROJ_TK), lambda k: (0, k)),
            pl.BlockSpec((OUTPROJ_TK, D_MODEL), lambda k: (k, 0)),
        ],
        out_specs=(pl.BlockSpec((m, D_MODEL), lambda k: (0, 0)),
                   pl.BlockSpec((OUTPROJ_TK, D_MODEL), lambda k: (k, 0))),
        compiler_params=_cparams(("arbitrary",)),
        name="outproj_f32w",
    )(x, a, w_out)


def _mixers(x1, weights, sb_fn, state0, *, tm, proj_tm, proj_split, pos_base, pos_mod, chunk,
            chunks_per_step):
    (nmix, w_in, qn, kn, ron, w_out, inv_freq) = weights
    f32w = w_in.dtype == F32
    proj = _inproj(x1, nmix, w_in, qn, kn, inv_freq, tm=proj_tm, n_split=proj_split,
                   pos_base=pos_base, pos_mod=pos_mod)
    if f32w:
        proj, w_in = proj
    sq, sk, sk16, sv, sv16, rq, rk, rv, gate = proj
    a_sb, sb_side = sb_fn(sq, sk16, sv16)
    a_ret, state = _retention(rq, rk, rv, gate, ron, state0, chunk=chunk, per_step=chunks_per_step)
    if f32w:
        x2, w_out = _outproj_f32w(x1, jnp.concatenate([a_sb, a_ret], axis=1), w_out)
    else:
        x2 = _outproj(x1, a_sb, a_ret, w_out, tm=tm)
    return x2, sk, sv, state, (w_in, w_out), sb_side


def kernel(x_prompt, x_sample, cache_sb_k, cache_sb_v, state_ret, ffn1_norm, ffn1_w_gate, ffn1_w_up, ffn1_w_down, mix_norm, w_in, sb_q_norm, sb_k_norm, ret_out_norm, w_out, ffn2_norm, ffn2_w_gate, ffn2_w_up, ffn2_w_down, final_norm):
    depth = ffn1_norm.shape[0]
    assert depth == 1
    nb_p, seq, _ = x_prompt.shape
    nb_s, dec_seq, _ = x_sample.shape
    past = cache_sb_k.shape[3]
    assert nb_p == 1

    half = DK_RET // 2
    inv_freq = (ROPE_BASE ** (-jnp.arange(half, dtype=F32) / half)).reshape(1, half)
    l = 0
    n1, n2, nf = ffn1_norm[l][None], ffn2_norm[l][None], final_norm[l][None]

    def mixer_weights(w_in_, w_out_):
        return (mix_norm[l][None], w_in_, sb_q_norm[l][None], sb_k_norm[l][None],
                ret_out_norm[l][None], w_out_, inv_freq)

    ms = nb_s * dec_seq
    xs = x_sample.reshape(ms, D_MODEL)
    xp = x_prompt.reshape(seq, D_MODEL)
    x1s, ffn1_16 = _ffn(xs, n1, ffn1_w_gate[l], ffn1_w_up[l], ffn1_w_down[l], tm=ms)
    x1p = _ffn(xp, n1, *ffn1_16, tm=1024)

    def sb_s(sq, sk16, sv16):
        return _sb_sample(sq, sk16, sv16, cache_sb_k[l], cache_sb_v[l], t=dec_seq), ()

    x2s, sks, svs, sts, (w_in16, w_out16), _ = _mixers(
        x1s, mixer_weights(w_in[l], w_out[l]), sb_s, state_ret[l],
        tm=ms, proj_tm=ms, proj_split=1, pos_base=past, pos_mod=dec_seq, chunk=dec_seq,
        chunks_per_step=1)

    sb_p = functools.partial(_sb_prompt, side=(ffn2_w_gate[l], ffn2_w_up[l], ffn2_w_down[l]))
    zero_state = jnp.zeros((1, H_RET, DK_RET, DV_RET), F32)
    x2p, skp, svp, stp, _, ffn2_16 = _mixers(
        x1p, mixer_weights(w_in16, w_out16), sb_p, zero_state,
        tm=512, proj_tm=1024, proj_split=2, pos_base=0, pos_mod=seq, chunk=RET_CHUNK,
        chunks_per_step=4)

    ys = _ffn(x2s, n2, *ffn2_16, nf, tm=ms)
    yp = _ffn(x2p, n2, *ffn2_16, nf, tm=1024)

    def cache_layout(t):
        return t.reshape(H_SB, nb_s, dec_seq, D_SB).transpose(1, 0, 2, 3)[None]

    return (yp.reshape(1, seq, D_MODEL), ys.reshape(nb_s, dec_seq, D_MODEL),
            skp[None, None], svp[None, None], stp[None],
            cache_layout(sks), cache_layout(svs), sts[None])
```

```python
import functools
import math

import jax
import jax.numpy as jnp
from jax import lax
from jax.experimental import pallas as pl
from jax.experimental.pallas import tpu as pltpu

F32 = jnp.float32
BF16 = jnp.bfloat16

D_MODEL = 2048
D_FF = 5632
H_SB = 8
D_SB = 128
H_RET = 4
DK_RET = 256
DV_RET = 256
SB_W = H_SB * D_SB
RET_W = H_RET * DK_RET
N_SECTIONS = 7
SECTION_W = 1024
ROPE_BASE = 10000.0
EPS = 1e-6

V7X_VMEM_LIMIT_BYTES = 56 * 1024 * 1024
SB_BLOCK = 256
RET_CHUNK = 256
FFN_TF = 512

LOG2E = 1.4426950408889634


def _cparams(sem):
    return pltpu.CompilerParams(dimension_semantics=sem,
                                vmem_limit_bytes=V7X_VMEM_LIMIT_BYTES)


def _rmsnorm_rows(x, w):
    ms = jnp.mean(x * x, axis=-1, keepdims=True)
    return x * lax.rsqrt(ms + EPS) * w


def _dot(a, b):
    return jnp.dot(a, b, preferred_element_type=F32)


def _dot_nt(a, b):
    return lax.dot_general(a, b, (((1,), (1,)), ((), ())), preferred_element_type=F32)


def _dot_tn(a, b):
    return lax.dot_general(a, b, (((0,), (0,)), ((), ())), preferred_element_type=F32)


def _ffn_kernel(*refs, n_steps, final_norm, emit_bf16):
    x_ref, nw_ref, wg_ref, wu_ref, wd_ref = refs[:5]
    refs = refs[5:]
    fw_ref = None
    if final_norm:
        fw_ref, refs = refs[0], refs[1:]
    o_ref, refs = refs[0], refs[1:]
    if emit_bf16:
        wg16_ref, wu16_ref, wd16_ref, h_ref = refs
        wg16_ref[...] = wg_ref[...].astype(BF16)
        wu16_ref[...] = wu_ref[...].astype(BF16)
        wd16_ref[...] = wd_ref[...].astype(BF16)
        wg_ref, wu_ref, wd_ref = wg16_ref, wu16_ref, wd16_ref
    else:
        (h_ref,) = refs
    j = pl.program_id(1)

    @pl.when(j == 0)
    def _():
        x = x_ref[...]
        h_ref[...] = _rmsnorm_rows(x, nw_ref[...]).astype(BF16)
        o_ref[...] = x

    h = h_ref[...]
    g = _dot(h, wg_ref[...])
    u = _dot(h, wu_ref[...])
    a = (g * jax.nn.sigmoid(g) * (0.5 * u)).astype(BF16)
    o_ref[...] += _dot(a, wd_ref[...])

    if final_norm:
        @pl.when(j == n_steps - 1)
        def _():
            o_ref[...] = _rmsnorm_rows(o_ref[...], fw_ref[...])


def _ffn(x, norm_w, wg, wu, wd, final_w=None, *, tm):
    m = x.shape[0]
    n_steps = D_FF // FFN_TF
    emit_bf16 = wg.dtype == F32
    assert not emit_bf16 or m == tm
    w_specs = [
        pl.BlockSpec((D_MODEL, FFN_TF), lambda i, j: (0, j)),
        pl.BlockSpec((D_MODEL, FFN_TF), lambda i, j: (0, j)),
        pl.BlockSpec((FFN_TF, D_MODEL), lambda i, j: (j, 0)),
    ]
    in_specs = [
        pl.BlockSpec((tm, D_MODEL), lambda i, j: (i, 0)),
        pl.BlockSpec((1, D_MODEL), lambda i, j: (0, 0)),
    ] + w_specs
    args = [x, norm_w, wg, wu, wd]
    if final_w is not None:
        in_specs.append(pl.BlockSpec((1, D_MODEL), lambda i, j: (0, 0)))
        args.append(final_w)
    out_shape = [jax.ShapeDtypeStruct((m, D_MODEL), F32)]
    out_specs = [pl.BlockSpec((tm, D_MODEL), lambda i, j: (i, 0))]
    if emit_bf16:
        out_shape += [jax.ShapeDtypeStruct(w.shape, BF16) for w in (wg, wu, wd)]
        out_specs += w_specs
    out = pl.pallas_call(
        functools.partial(_ffn_kernel, n_steps=n_steps, final_norm=final_w is not None,
                          emit_bf16=emit_bf16),
        out_shape=out_shape,
        grid=(m // tm, n_steps),
        in_specs=in_specs,
        out_specs=out_specs,
        scratch_shapes=[pltpu.VMEM((tm, D_MODEL), BF16)],
        compiler_params=_cparams(("arbitrary", "arbitrary")),
        name="ffn_final" if final_w is not None else "ffn",
    )(*args)
    return (out[0], tuple(out[1:])) if emit_bf16 else out[0]


def _inproj_kernel(x_ref, nw_ref, w_ref, qn_ref, kn_ref, inv_ref,
                   sq_ref, sk_ref, sk16_ref, sv_ref, sv16_ref,
                   rq_ref, rk_ref, rv_ref, g_ref, *rest,
                   tm, n_split, tile_stride, pos_base, pos_mod, emit_bf16):
    if emit_bf16:
        w16_ref, h_ref, cos_ref, sin_ref, cos_row_ref, sin_row_ref = rest
        w16_ref[...] = w_ref[...].astype(BF16)
        w_ref = w16_ref
    else:
        h_ref, cos_ref, sin_ref, cos_row_ref, sin_row_ref = rest
    i = pl.program_id(0)
    j = pl.program_id(1)

    @pl.when(jnp.logical_and(i == 0, j == 0))
    def _():
        row = lax.broadcasted_iota(jnp.int32, (tm, DK_RET // 2), 0)
        ang = lax.rem(row, pos_mod).astype(F32) * inv_ref[...]
        cos_row_ref[...] = jnp.cos(ang)
        sin_row_ref[...] = jnp.sin(ang)

    @pl.when(j == 0)
    def _():
        h_ref[...] = _rmsnorm_rows(x_ref[...], nw_ref[...]).astype(BF16)
        ang = (pos_base + i * tile_stride).astype(F32) * inv_ref[...]
        ca, sa = jnp.cos(ang), jnp.sin(ang)
        cb, sb = cos_row_ref[...], sin_row_ref[...]
        cos_ref[...] = ca * cb - sa * sb
        sin_ref[...] = sa * cb + ca * sb

    section = j // n_split
    sb_heads = H_SB // n_split
    ret_heads = H_RET // n_split

    def proj():
        return _dot(h_ref[...], w_ref[...])

    def head(p, hd, width):
        return p[:, hd * width:(hd + 1) * width]

    @pl.when(section == 0)
    def _():
        p = proj()
        w = qn_ref[...] * (D_SB ** -0.5 * LOG2E)
        for hd in range(sb_heads):
            sq_ref[hd] = _rmsnorm_rows(head(p, hd, D_SB), w).astype(BF16)

    @pl.when(section == 1)
    def _():
        p = proj()
        for hd in range(sb_heads):
            k = _rmsnorm_rows(head(p, hd, D_SB), kn_ref[...])
            sk_ref[hd] = k
            sk16_ref[hd] = k.astype(BF16)

    @pl.when(section == 2)
    def _():
        p = proj()
        for hd in range(sb_heads):
            v = head(p, hd, D_SB)
            sv_ref[hd] = v
            sv16_ref[hd] = v.astype(BF16)

    def rotary(ph):
        half = DK_RET // 2
        x1, x2 = ph[:, :half], ph[:, half:]
        c, s = cos_ref[...], sin_ref[...]
        return jnp.concatenate([x1 * c - x2 * s, x1 * s + x2 * c], axis=-1)

    @pl.when(section == 3)
    def _():
        p = proj()
        for hd in range(ret_heads):
            rq_ref[hd] = rotary(head(p, hd, DK_RET)).astype(BF16)

    @pl.when(section == 4)
    def _():
        p = proj()
        for hd in range(ret_heads):
            rk_ref[hd] = rotary(head(p, hd, DK_RET)) * (DK_RET ** -0.5)

    @pl.when(section == 5)
    def _():
        p = proj()
        for hd in range(ret_heads):
            rv_ref[hd] = head(p, hd, DV_RET).astype(BF16)

    @pl.when(section == 6)
    def _():
        g_ref[...] = proj()


def _inproj(x, norm_w, w_in, qn, kn, inv_freq, *, tm, n_split, pos_base, pos_mod):
    m = x.shape[0]
    emit_bf16 = w_in.dtype == F32
    assert not emit_bf16 or m == tm
    tn = SECTION_W // n_split

    def part(section):
        return lambda j: jnp.clip(j - section * n_split, 0, n_split - 1)

    def head_spec(heads, width, section):
        pt = part(section)
        return pl.BlockSpec((heads // n_split, tm, width), lambda i, j: (pt(j), i, 0))

    gate_part = part(6)
    sb16 = jax.ShapeDtypeStruct((H_SB, m, D_SB), BF16)
    sb32 = jax.ShapeDtypeStruct((H_SB, m, D_SB), F32)
    ret16 = jax.ShapeDtypeStruct((H_RET, m, DK_RET), BF16)
    ret32 = jax.ShapeDtypeStruct((H_RET, m, DK_RET), F32)
    assert pos_mod >= m or tm % pos_mod == 0
    tile_stride = tm if pos_mod >= m else 0
    half = DK_RET // 2
    w_spec = pl.BlockSpec((D_MODEL, tn), lambda i, j: (0, j))
    out_shape = [sb16, sb32, sb16, sb32, sb16, ret16, ret32, ret16,
                 jax.ShapeDtypeStruct((m, SECTION_W), F32)]
    out_specs = [head_spec(H_SB, D_SB, 0), head_spec(H_SB, D_SB, 1), head_spec(H_SB, D_SB, 1),
                 head_spec(H_SB, D_SB, 2), head_spec(H_SB, D_SB, 2),
                 head_spec(H_RET, DK_RET, 3), head_spec(H_RET, DK_RET, 4),
                 head_spec(H_RET, DK_RET, 5),
                 pl.BlockSpec((tm, tn), lambda i, j: (i, gate_part(j)))]
    if emit_bf16:
        out_shape.append(jax.ShapeDtypeStruct(w_in.shape, BF16))
        out_specs.append(w_spec)
    out = pl.pallas_call(
        functools.partial(_inproj_kernel, tm=tm, n_split=n_split, tile_stride=tile_stride,
                          pos_base=pos_base, pos_mod=pos_mod, emit_bf16=emit_bf16),
        out_shape=out_shape,
        grid=(m // tm, N_SECTIONS * n_split),
        in_specs=[
            pl.BlockSpec((tm, D_MODEL), lambda i, j: (i, 0)),
            pl.BlockSpec((1, D_MODEL), lambda i, j: (0, 0)),
            w_spec,
            pl.BlockSpec((1, D_SB), lambda i, j: (0, 0)),
            pl.BlockSpec((1, D_SB), lambda i, j: (0, 0)),
            pl.BlockSpec((1, half), lambda i, j: (0, 0)),
        ],
        out_specs=out_specs,
        scratch_shapes=[pltpu.VMEM((tm, D_MODEL), BF16)] + [pltpu.VMEM((tm, half), F32)] * 4,
        compiler_params=_cparams(("arbitrary", "arbitrary")),
        name="inproj",
    )(x, norm_w, w_in, qn, kn, inv_freq)
    return (tuple(out[:9]), out[9]) if emit_bf16 else tuple(out)


def _softplus2(z):
    return jnp.maximum(z, 0.0) + jnp.log2(1.0 + jnp.exp2(-jnp.abs(z)))


def _suffix_matrix(n):
    r = lax.broadcasted_iota(jnp.int32, (2 * n, n), 0)
    c = lax.broadcasted_iota(jnp.int32, (2 * n, n), 1)
    return jnp.where(jnp.where(r >= n, r - n, r) >= c, 1.0, 0.0).astype(BF16)


def _sb_logits(q, k, mask):
    z = _dot_nt(q, k)
    sp = _softplus2(z)
    if mask is not None:
        sp = jnp.where(mask, sp, 0.0)
    hi = sp.astype(BF16)
    lo = (sp - hi.astype(F32)).astype(BF16)
    return z, jnp.concatenate([hi, lo], axis=1)


def _sb_suffix_sums(splits, tri):
    q = splits[0].shape[0]
    sums = _dot(jnp.concatenate(splits, axis=0), tri)
    return [sums[n * q:(n + 1) * q] for n in range(len(splits))]


def _sb_weights(z, incl, v, mask):
    a = jnp.exp2(z - incl)
    if mask is not None:
        a = jnp.where(mask, a, 0.0)
    return _dot(a.astype(BF16), v)


SB_DEAD_CARRY = 150.0


SB_HEADS_PER_STEP = 4


def _sb_prompt_kernel(q_ref, k_ref, v_ref, *rest):
    n_side = len(rest) // 2
    o_ref = rest[n_side]
    for src, dst in zip(rest[:n_side], rest[n_side + 1:]):
        dst[...] = src[...].astype(BF16)
    qi = pl.program_id(1)
    heads, _, d = q_ref.shape
    blk = SB_BLOCK
    tri = _suffix_matrix(blk)
    r = lax.broadcasted_iota(jnp.int32, (blk, blk), 0)
    c = lax.broadcasted_iota(jnp.int32, (blk, blk), 1)
    strict = c < r
    has_prev = qi > 0

    def kv(g, block):
        s0 = pl.multiple_of(block * blk, blk)
        return k_ref[g, pl.ds(s0, blk), :], v_ref[g, pl.ds(s0, blk), :]

    qs = [q_ref[g] for g in range(heads)]
    diag = [kv(g, qi) for g in range(heads)]
    left = [kv(g, jnp.maximum(qi - 1, 0)) for g in range(heads)]
    z_d, hl_d = zip(*[_sb_logits(qs[g], diag[g][0], strict) for g in range(heads)])
    sums_d = _sb_suffix_sums(list(hl_d), tri)
    z_l, hl_l = zip(*[_sb_logits(qs[g], left[g][0], None) for g in range(heads)])
    sums_l = _sb_suffix_sums(list(hl_l), tri)
    accs, carries = [], []
    for g in range(heads):
        incl_d = sums_d[g]
        incl_l = sums_l[g] + incl_d[:, 0:1]
        acc = _sb_weights(z_d[g], incl_d, diag[g][1], strict)
        d_l = _sb_weights(z_l[g], incl_l, left[g][1], None)
        accs.append(acc + jnp.where(has_prev, d_l, 0.0))
        carries.append(jnp.where(has_prev, incl_l[:, 0:1], incl_d[:, 0:1]))
    accs, carries = tuple(accs), tuple(carries)

    def alive(carries):
        low = functools.reduce(jnp.minimum, carries)
        return (jnp.min(low) < SB_DEAD_CARRY).astype(jnp.int32)

    def cond(state):
        t, live, _, _ = state
        return jnp.logical_and(t >= 0, live > 0)

    def body(state):
        t, _, accs, carries = state
        blocks = [kv(g, t) for g in range(heads)]
        zs, hls = zip(*[_sb_logits(qs[g], blocks[g][0], None) for g in range(heads)])
        sums = _sb_suffix_sums(list(hls), tri)
        new_accs, new_carries = [], []
        for g in range(heads):
            incl = sums[g] + carries[g]
            new_accs.append(accs[g] + _sb_weights(zs[g], incl, blocks[g][1], None))
            new_carries.append(incl[:, 0:1])
        return t - 1, alive(new_carries), tuple(new_accs), tuple(new_carries)

    _, _, accs, _ = lax.while_loop(cond, body, (qi - 2, alive(carries), accs, carries))
    for g in range(heads):
        o_ref[:, g * d:(g + 1) * d] = accs[g].astype(o_ref.dtype)


def _sb_prompt(q16, k16, v16, side=()):
    h, s, d = q16.shape
    tq = SB_BLOCK
    g = SB_HEADS_PER_STEP
    n_groups, n_blocks = h // g, s // tq

    def side_spec(w):
        rows, cols = w.shape
        assert rows % (16 * n_blocks) == 0 and cols % (128 * n_groups) == 0
        return pl.BlockSpec((rows // n_blocks, cols // n_groups), lambda hg, qi: (qi, hg))

    side_specs = [side_spec(w) for w in side]
    out = pl.pallas_call(
        _sb_prompt_kernel,
        out_shape=[jax.ShapeDtypeStruct((s, h * d), BF16)]
        + [jax.ShapeDtypeStruct(w.shape, BF16) for w in side],
        grid=(n_groups, n_blocks),
        in_specs=[
            pl.BlockSpec((g, tq, d), lambda hg, qi: (hg, qi, 0)),
            pl.BlockSpec((g, s, d), lambda hg, qi: (hg, 0, 0)),
            pl.BlockSpec((g, s, d), lambda hg, qi: (hg, 0, 0)),
        ] + side_specs,
        out_specs=[pl.BlockSpec((tq, g * d), lambda hg, qi: (qi, hg))] + side_specs,
        compiler_params=_cparams(("arbitrary", "arbitrary")),
        name="sb_prompt",
    )(q16, k16, v16, *side)
    return out[0], tuple(out[1:])


def _sb_sample_kernel(q_ref, kn_ref, vn_ref, kc_hbm, vc_hbm, o_ref, kbuf, vbuf, sem, *, past):
    nh, t, d = q_ref.shape
    ht = nh * t
    blk = SB_BLOCK
    n_blocks = past // blk
    b = pl.program_id(0)
    slot = lax.rem(b, 2)

    def cache_copies(batch, block, to_slot):
        rows = pl.ds(pl.multiple_of(block * blk, blk), blk)
        return (pltpu.make_async_copy(kc_hbm.at[batch, :, rows, :], kbuf.at[to_slot], sem.at[0, to_slot]),
                pltpu.make_async_copy(vc_hbm.at[batch, :, rows, :], vbuf.at[to_slot], sem.at[1, to_slot]))

    @pl.when(b == 0)
    def _():
        for cp in cache_copies(0, n_blocks - 1, 0):
            cp.start()

    @pl.when(b + 1 < pl.num_programs(0))
    def _():
        for cp in cache_copies(b + 1, n_blocks - 1, 1 - slot):
            cp.start()

    q_all = q_ref[...].reshape(ht, d).astype(F32)
    q_head = lax.broadcasted_iota(jnp.int32, (ht, d), 0) // t
    q_masked = [jnp.where(q_head == h, q_all, 0.0).astype(BF16) for h in range(nh)]

    def logits(keys_of_head):
        z = _dot_nt(keys_of_head(0), q_masked[0])
        for h in range(1, nh):
            z = z + _dot_nt(keys_of_head(h), q_masked[h])
        return z

    def suffix_matrix(n):
        r = lax.broadcasted_iota(jnp.int32, (n, 2 * n), 0)
        c = lax.broadcasted_iota(jnp.int32, (n, 2 * n), 1)
        return jnp.where(jnp.where(c >= n, c - n, c) >= r, 1.0, 0.0).astype(BF16)

    def suffix_sum(sp, lmat):
        hi = sp.astype(BF16)
        lo = (sp - hi.astype(F32)).astype(BF16)
        return _dot(lmat, jnp.concatenate([hi, lo], axis=0))

    def emit(a_t, values_of_head, acc):
        a = a_t.T.astype(BF16)
        return [acc[h] + _dot(a[h * t:(h + 1) * t, :], values_of_head(h)) for h in range(nh)]

    z = logits(lambda h: kn_ref[h])
    s_idx = lax.broadcasted_iota(jnp.int32, (t, ht), 0)
    t_idx = lax.broadcasted_iota(jnp.int32, (t, ht), 1) % t
    strict = s_idx < t_idx
    sp = jnp.where(strict, _softplus2(z), 0.0)
    incl = suffix_sum(sp, suffix_matrix(t))
    a_t = jnp.where(strict, jnp.exp2(z - incl), 0.0)
    acc = emit(a_t, lambda h: vn_ref[h], [jnp.zeros((t, d), F32) for _ in range(nh)])
    carry = incl[0:1, :]

    lmat = suffix_matrix(blk)

    def cache_block(from_slot, acc, carry):
        z = logits(lambda h: kbuf[from_slot, h].astype(BF16))
        incl = suffix_sum(_softplus2(z), lmat) + carry
        acc = emit(jnp.exp2(z - incl), lambda h: vbuf[from_slot, h].astype(BF16), acc)
        return acc, incl[0:1, :]

    def alive(carry):
        return (jnp.min(carry) < SB_DEAD_CARRY).astype(jnp.int32)

    for cp in cache_copies(b, n_blocks - 1, slot):
        cp.wait()
    acc, carry = cache_block(slot, acc, carry)

    def cond(state):
        block, live, _, _ = state
        return jnp.logical_and(block >= 0, live > 0)

    def body(state):
        block, _, acc, carry = state
        copies = cache_copies(b, block, 2)
        for cp in copies:
            cp.start()
        for cp in copies:
            cp.wait()
        acc, carry = cache_block(2, list(acc), carry)
        return block - 1, alive(carry), tuple(acc), carry

    _, _, acc, _ = lax.while_loop(cond, body, (n_blocks - 2, alive(carry), tuple(acc), carry))
    for h in range(nh):
        o_ref[:, h * d:(h + 1) * d] = acc[h].astype(o_ref.dtype)


def _sb_sample(q16, k16, v16, cache_k, cache_v, *, t):
    h, m, d = q16.shape
    nb = m // t
    past = cache_k.shape[2]
    assert past % SB_BLOCK == 0
    new_spec = pl.BlockSpec((h, t, d), lambda b: (0, b, 0))
    cache_spec = pl.BlockSpec(memory_space=pl.ANY)
    return pl.pallas_call(
        functools.partial(_sb_sample_kernel, past=past),
        out_shape=jax.ShapeDtypeStruct((m, h * d), BF16),
        grid=(nb,),
        in_specs=[new_spec, new_spec, new_spec, cache_spec, cache_spec],
        out_specs=pl.BlockSpec((t, h * d), lambda b: (b, 0)),
        scratch_shapes=[pltpu.VMEM((3, h, SB_BLOCK, d), F32),
                        pltpu.VMEM((3, h, SB_BLOCK, d), F32),
                        pltpu.SemaphoreType.DMA((2, 3))],
        compiler_params=_cparams(("arbitrary",)),
        name="sb_sample",
    )(q16, k16, v16, cache_k, cache_v)


def _ret_log_decay(hd):
    return math.log(1.0 - 2.0 ** (-5.0 - hd))


def _ret_kernel(q_ref, k_ref, v_ref, g_ref, nw_ref, s0_ref, r_ref, st_ref,
                intra_ref, qdec_ref, kdec_ref, *, chunk):
    c = pl.program_id(1)

    @pl.when(jnp.logical_and(pl.program_id(0) == 0, c == 0))
    def _():
        row = lax.broadcasted_iota(jnp.int32, (chunk, chunk), 0)
        col = lax.broadcasted_iota(jnp.int32, (chunk, chunk), 1)
        diff = (row - col).astype(F32)
        pos = lax.broadcasted_iota(jnp.int32, (chunk, DK_RET), 0).astype(F32)
        for hd in range(H_RET):
            lg = _ret_log_decay(hd)
            intra_ref[hd] = jnp.where(row >= col, jnp.exp(lg * jnp.maximum(diff, 0.0)), 0.0)
            qdec_ref[hd] = jnp.exp(lg * (pos + 1.0))
            kdec_ref[hd] = jnp.exp(lg * (chunk - 1.0 - pos))

    @pl.when(c == 0)
    def _():
        st_ref[...] = s0_ref[...]

    for sub in range(q_ref.shape[1] // chunk):
        rows = slice(sub * chunk, (sub + 1) * chunk)
        for hd in range(H_RET):
            intra, q_dec, k_dec = intra_ref[hd], qdec_ref[hd], kdec_ref[hd]
            c_dec = math.exp(_ret_log_decay(hd) * chunk)
            q = q_ref[hd, rows, :]
            k = k_ref[hd, rows, :]
            v = v_ref[hd, rows, :]
            state = st_ref[hd]
            scores = (_dot_nt(q, k.astype(BF16)) * intra).astype(BF16)
            o = _dot(scores, v) + _dot(q, state.astype(BF16)) * q_dec
            st_ref[hd] = c_dec * state + _dot_tn((k * k_dec).astype(BF16), v)
            o = o * lax.rsqrt(jnp.mean(o * o, axis=-1, keepdims=True) + EPS)
            sl = slice(hd * DV_RET, (hd + 1) * DV_RET)
            gate = g_ref[rows, sl]
            r_ref[rows, sl] = (o * nw_ref[:, sl] * (gate * jax.nn.sigmoid(gate))).astype(r_ref.dtype)


def _retention(rq, rk, rv, gate, norm_w, state0, *, chunk, per_step):
    h, m, _ = rq.shape
    nb = state0.shape[0]
    rows = per_step * chunk
    nc = m // (nb * rows)
    qkv_spec = pl.BlockSpec((h, rows, DK_RET), lambda b, c: (0, b * nc + c, 0))
    st_spec = pl.BlockSpec((None, h, DK_RET, DV_RET), lambda b, c: (b, 0, 0, 0))
    return pl.pallas_call(
        functools.partial(_ret_kernel, chunk=chunk),
        out_shape=(jax.ShapeDtypeStruct((m, h * DV_RET), BF16),
                   jax.ShapeDtypeStruct(state0.shape, F32)),
        grid=(nb, nc),
        in_specs=[qkv_spec, qkv_spec, qkv_spec,
                  pl.BlockSpec((rows, h * DV_RET), lambda b, c: (b * nc + c, 0)),
                  pl.BlockSpec((1, h * DV_RET), lambda b, c: (0, 0)),
                  st_spec],
        out_specs=(pl.BlockSpec((rows, h * DV_RET), lambda b, c: (b * nc + c, 0)), st_spec),
        scratch_shapes=[pltpu.VMEM((h, chunk, chunk), F32),
                        pltpu.VMEM((h, chunk, DK_RET), F32),
                        pltpu.VMEM((h, chunk, DK_RET), F32)],
        compiler_params=_cparams(("arbitrary", "arbitrary")),
        name="retention",
    )(rq, rk, rv, gate, norm_w, state0)


def _outproj_kernel(x_ref, a_ref, b_ref, wa_ref, wb_ref, o_ref):
    o_ref[...] = x_ref[...] + _dot(a_ref[...], wa_ref[...]) + _dot(b_ref[...], wb_ref[...])


def _outproj(x, a_sb, a_ret, w_out, *, tm):
    m = x.shape[0]
    row = lambda i: (i, 0)
    return pl.pallas_call(
        _outproj_kernel,
        out_shape=jax.ShapeDtypeStruct((m, D_MODEL), F32),
        grid=(m // tm,),
        in_specs=[
            pl.BlockSpec((tm, D_MODEL), row),
            pl.BlockSpec((tm, SB_W), row),
            pl.BlockSpec((tm, RET_W), row),
            pl.BlockSpec((SB_W, D_MODEL), lambda i: (0, 0)),
            pl.BlockSpec((RET_W, D_MODEL), lambda i: (1, 0)),
        ],
        out_specs=pl.BlockSpec((tm, D_MODEL), row),
        compiler_params=_cparams(("arbitrary",)),
        name="outproj",
    )(x, a_sb, a_ret, w_out, w_out)


OUTPROJ_TK = 512


def _outproj_f32w_kernel(x_ref, a_ref, w_ref, o_ref, w16_ref):
    @pl.when(pl.program_id(0) == 0)
    def _():
        o_ref[...] = x_ref[...]

    w16_ref[...] = w_ref[...].astype(BF16)
    o_ref[...] += _dot(a_ref[...], w16_ref[...])


def _outproj_f32w(x, a, w_out):
    m = x.shape[0]
    d_mix = w_out.shape[0]
    return pl.pallas_call(
        _outproj_f32w_kernel,
        out_shape=(jax.ShapeDtypeStruct((m, D_MODEL), F32),
                   jax.ShapeDtypeStruct(w_out.shape, BF16)),
        grid=(d_mix // OUTPROJ_TK,),
        in_specs=[
            pl.BlockSpec((m, D_MODEL), lambda k: (0, 0)),
            pl.BlockSpec((m, OUTPROJ_TK), lambda k: (0, k)),
            pl.BlockSpec((OUTPROJ_TK, D_MODEL), lambda k: (k, 0)),
        ],
        out_specs=(pl.BlockSpec((m, D_MODEL), lambda k: (0, 0)),
                   pl.BlockSpec((OUTPROJ_TK, D_MODEL), lambda k: (k, 0))),
        compiler_params=_cparams(("arbitrary",)),
        name="outproj_f32w",
    )(x, a, w_out)


def _mixers(x1, weights, sb_fn, state0, *, tm, proj_tm, proj_split, pos_base, pos_mod, chunk,
            chunks_per_step):
    (nmix, w_in, qn, kn, ron, w_out, inv_freq) = weights
    f32w = w_in.dtype == F32
    proj = _inproj(x1, nmix, w_in, qn, kn, inv_freq, tm=proj_tm, n_split=proj_split,
                   pos_base=pos_base, pos_mod=pos_mod)
    if f32w:
        proj, w_in = proj
    sq, sk, sk16, sv, sv16, rq, rk, rv, gate = proj
    a_sb, sb_side = sb_fn(sq, sk16, sv16)
    a_ret, state = _retention(rq, rk, rv, gate, ron, state0, chunk=chunk, per_step=chunks_per_step)
    if f32w:
        x2, w_out = _outproj_f32w(x1, jnp.concatenate([a_sb, a_ret], axis=1), w_out)
    else:
        x2 = _outproj(x1, a_sb, a_ret, w_out, tm=tm)
    return x2, sk, sv, state, (w_in, w_out), sb_side


def kernel(x_prompt, x_sample, cache_sb_k, cache_sb_v, state_ret, ffn1_norm, ffn1_w_gate, ffn1_w_up, ffn1_w_down, mix_norm, w_in, sb_q_norm, sb_k_norm, ret_out_norm, w_out, ffn2_norm, ffn2_w_gate, ffn2_w_up, ffn2_w_down, final_norm):
    depth = ffn1_norm.shape[0]
    assert depth == 1
    nb_p, seq, _ = x_prompt.shape
    nb_s, dec_seq, _ = x_sample.shape
    past = cache_sb_k.shape[3]
    assert nb_p == 1

    half = DK_RET // 2
    inv_freq = (ROPE_BASE ** (-jnp.arange(half, dtype=F32) / half)).reshape(1, half)
    l = 0
    n1, n2, nf = ffn1_norm[l][None], ffn2_norm[l][None], final_norm[l][None]

    def mixer_weights(w_in_, w_out_):
        return (mix_norm[l][None], w_in_, sb_q_norm[l][None], sb_k_norm[l][None],
                ret_out_norm[l][None], w_out_, inv_freq)

    ms = nb_s * dec_seq
    xs = x_sample.reshape(ms, D_MODEL)
    xp = x_prompt.reshape(seq, D_MODEL)
    x1s, ffn1_16 = _ffn(xs, n1, ffn1_w_gate[l], ffn1_w_up[l], ffn1_w_down[l], tm=ms)
    x1p = _ffn(xp, n1, *ffn1_16, tm=1024)

    def sb_s(sq, sk16, sv16):
        return _sb_sample(sq, sk16, sv16, cache_sb_k[l], cache_sb_v[l], t=dec_seq), ()

    x2s, sks, svs, sts, (w_in16, w_out16), _ = _mixers(
        x1s, mixer_weights(w_in[l], w_out[l]), sb_s, state_ret[l],
        tm=ms, proj_tm=ms, proj_split=1, pos_base=past, pos_mod=dec_seq, chunk=dec_seq,
        chunks_per_step=1)

    sb_p = functools.partial(_sb_prompt, side=(ffn2_w_gate[l], ffn2_w_up[l], ffn2_w_down[l]))
    zero_state = jnp.zeros((1, H_RET, DK_RET, DV_RET), F32)
    x2p, skp, svp, stp, _, ffn2_16 = _mixers(
        x1p, mixer_weights(w_in16, w_out16), sb_p, zero_state,
        tm=512, proj_tm=1024, proj_split=2, pos_base=0, pos_mod=seq, chunk=RET_CHUNK,
        chunks_per_step=4)

    ys = _ffn(x2s, n2, *ffn2_16, nf, tm=ms)
    yp = _ffn(x2p, n2, *ffn2_16, nf, tm=1024)

    def cache_layout(t):
        return t.reshape(H_SB, nb_s, dec_seq, D_SB).transpose(1, 0, 2, 3)[None]

    return (yp.reshape(1, seq, D_MODEL), ys.reshape(nb_s, dec_seq, D_MODEL),
            skp[None, None], svp[None, None], stp[None],
            cache_layout(sks), cache_layout(svs), sts[None])
```

```python
import functools
import math

import jax
import jax.numpy as jnp
from jax import lax
from jax.experimental import pallas as pl
from jax.experimental.pallas import tpu as pltpu

F32 = jnp.float32
BF16 = jnp.bfloat16

D_MODEL = 2048
D_FF = 5632
H_SB = 8
D_SB = 128
H_RET = 4
DK_RET = 256
DV_RET = 256
SB_W = H_SB * D_SB
RET_W = H_RET * DK_RET
N_SECTIONS = 7
SECTION_W = 1024
ROPE_BASE = 10000.0
EPS = 1e-6

V7X_VMEM_LIMIT_BYTES = 56 * 1024 * 1024
SB_BLOCK = 256
RET_CHUNK = 256
FFN_TF = 512

LOG2E = 1.4426950408889634


def _cparams(sem):
    return pltpu.CompilerParams(dimension_semantics=sem,
                                vmem_limit_bytes=V7X_VMEM_LIMIT_BYTES)


def _rmsnorm_rows(x, w):
    ms = jnp.mean(x * x, axis=-1, keepdims=True)
    return x * lax.rsqrt(ms + EPS) * w


def _dot(a, b):
    return jnp.dot(a, b, preferred_element_type=F32)


def _dot_nt(a, b):
    return lax.dot_general(a, b, (((1,), (1,)), ((), ())), preferred_element_type=F32)


def _dot_tn(a, b):
    return lax.dot_general(a, b, (((0,), (0,)), ((), ())), preferred_element_type=F32)


def _ffn_kernel(*refs, n_steps, final_norm, emit_bf16):
    x_ref, nw_ref, wg_ref, wu_ref, wd_ref = refs[:5]
    refs = refs[5:]
    fw_ref = None
    if final_norm:
        fw_ref, refs = refs[0], refs[1:]
    o_ref, refs = refs[0], refs[1:]
    if emit_bf16:
        wg16_ref, wu16_ref, wd16_ref, h_ref = refs
        wg16_ref[...] = wg_ref[...].astype(BF16)
        wu16_ref[...] = wu_ref[...].astype(BF16)
        wd16_ref[...] = wd_ref[...].astype(BF16)
        wg_ref, wu_ref, wd_ref = wg16_ref, wu16_ref, wd16_ref
    else:
        (h_ref,) = refs
    j = pl.program_id(1)

    @pl.when(j == 0)
    def _():
        x = x_ref[...]
        h_ref[...] = _rmsnorm_rows(x, nw_ref[...]).astype(BF16)
        o_ref[...] = x

    h = h_ref[...]
    g = _dot(h, wg_ref[...])
    u = _dot(h, wu_ref[...])
    a = (g * jax.nn.sigmoid(g) * (0.5 * u)).astype(BF16)
    o_ref[...] += _dot(a, wd_ref[...])

    if final_norm:
        @pl.when(j == n_steps - 1)
        def _():
            o_ref[...] = _rmsnorm_rows(o_ref[...], fw_ref[...])


def _ffn(x, norm_w, wg, wu, wd, final_w=None, *, tm):
    m = x.shape[0]
    n_steps = D_FF // FFN_TF
    emit_bf16 = wg.dtype == F32
    assert not emit_bf16 or m == tm
    w_specs = [
        pl.BlockSpec((D_MODEL, FFN_TF), lambda i, j: (0, j)),
        pl.BlockSpec((D_MODEL, FFN_TF), lambda i, j: (0, j)),
        pl.BlockSpec((FFN_TF, D_MODEL), lambda i, j: (j, 0)),
    ]
    in_specs = [
        pl.BlockSpec((tm, D_MODEL), lambda i, j: (i, 0)),
        pl.BlockSpec((1, D_MODEL), lambda i, j: (0, 0)),
    ] + w_specs
    args = [x, norm_w, wg, wu, wd]
    if final_w is not None:
        in_specs.append(pl.BlockSpec((1, D_MODEL), lambda i, j: (0, 0)))
        args.append(final_w)
    out_shape = [jax.ShapeDtypeStruct((m, D_MODEL), F32)]
    out_specs = [pl.BlockSpec((tm, D_MODEL), lambda i, j: (i, 0))]
    if emit_bf16:
        out_shape += [jax.ShapeDtypeStruct(w.shape, BF16) for w in (wg, wu, wd)]
        out_specs += w_specs
    out = pl.pallas_call(
        functools.partial(_ffn_kernel, n_steps=n_steps, final_norm=final_w is not None,
                          emit_bf16=emit_bf16),
        out_shape=out_shape,
        grid=(m // tm, n_steps),
        in_specs=in_specs,
        out_specs=out_specs,
        scratch_shapes=[pltpu.VMEM((tm, D_MODEL), BF16)],
        compiler_params=_cparams(("arbitrary", "arbitrary")),
        name="ffn_final" if final_w is not None else "ffn",
    )(*args)
    return (out[0], tuple(out[1:])) if emit_bf16 else out[0]


def _inproj_kernel(x_ref, nw_ref, w_ref, qn_ref, kn_ref, inv_ref,
                   sq_ref, sk_ref, sk16_ref, sv_ref, sv16_ref,
                   rq_ref, rk_ref, rv_ref, g_ref, *rest,
                   tm, n_split, tile_stride, pos_base, pos_mod, emit_bf16):
    if emit_bf16:
        w16_ref, h_ref, cos_ref, sin_ref, cos_row_ref, sin_row_ref = rest
        w16_ref[...] = w_ref[...].astype(BF16)
        w_ref = w16_ref
    else:
        h_ref, cos_ref, sin_ref, cos_row_ref, sin_row_ref = rest
    i = pl.program_id(0)
    j = pl.program_id(1)

    @pl.when(jnp.logical_and(i == 0, j == 0))
    def _():
        row = lax.broadcasted_iota(jnp.int32, (tm, DK_RET // 2), 0)
        ang = lax.rem(row, pos_mod).astype(F32) * inv_ref[...]
        cos_row_ref[...] = jnp.cos(ang)
        sin_row_ref[...] = jnp.sin(ang)

    @pl.when(j == 0)
    def _():
        h_ref[...] = _rmsnorm_rows(x_ref[...], nw_ref[...]).astype(BF16)
        ang = (pos_base + i * tile_stride).astype(F32) * inv_ref[...]
        ca, sa = jnp.cos(ang), jnp.sin(ang)
        cb, sb = cos_row_ref[...], sin_row_ref[...]
        cos_ref[...] = ca * cb - sa * sb
        sin_ref[...] = sa * cb + ca * sb

    section = j // n_split
    sb_heads = H_SB // n_split
    ret_heads = H_RET // n_split

    def proj():
        return _dot(h_ref[...], w_ref[...])

    def head(p, hd, width):
        return p[:, hd * width:(hd + 1) * width]

    @pl.when(section == 0)
    def _():
        p = proj()
        w = qn_ref[...] * (D_SB ** -0.5 * LOG2E)
        for hd in range(sb_heads):
            sq_ref[hd] = _rmsnorm_rows(head(p, hd, D_SB), w).astype(BF16)

    @pl.when(section == 1)
    def _():
        p = proj()
        for hd in range(sb_heads):
            k = _rmsnorm_rows(head(p, hd, D_SB), kn_ref[...])
            sk_ref[hd] = k
            sk16_ref[hd] = k.astype(BF16)

    @pl.when(section == 2)
    def _():
        p = proj()
        for hd in range(sb_heads):
            v = head(p, hd, D_SB)
            sv_ref[hd] = v
            sv16_ref[hd] = v.astype(BF16)

    def rotary(ph):
        half = DK_RET // 2
        x1, x2 = ph[:, :half], ph[:, half:]
        c, s = cos_ref[...], sin_ref[...]
        return jnp.concatenate([x1 * c - x2 * s, x1 * s + x2 * c], axis=-1)

    @pl.when(section == 3)
    def _():
        p = proj()
        for hd in range(ret_heads):
            rq_ref[hd] = rotary(head(p, hd, DK_RET)).astype(BF16)

    @pl.when(section == 4)
    def _():
        p = proj()
        for hd in range(ret_heads):
            rk_ref[hd] = rotary(head(p, hd, DK_RET)) * (DK_RET ** -0.5)

    @pl.when(section == 5)
    def _():
        p = proj()
        for hd in range(ret_heads):
            rv_ref[hd] = head(p, hd, DV_RET).astype(BF16)

    @pl.when(section == 6)
    def _():
        g_ref[...] = proj()


def _inproj(x, norm_w, w_in, qn, kn, inv_freq, *, tm, n_split, pos_base, pos_mod):
    m = x.shape[0]
    emit_bf16 = w_in.dtype == F32
    assert not emit_bf16 or m == tm
    tn = SECTION_W // n_split

    def part(section):
        return lambda j: jnp.clip(j - section * n_split, 0, n_split - 1)

    def head_spec(heads, width, section):
        pt = part(section)
        return pl.BlockSpec((heads // n_split, tm, width), lambda i, j: (pt(j), i, 0))

    gate_part = part(6)
    sb16 = jax.ShapeDtypeStruct((H_SB, m, D_SB), BF16)
    sb32 = jax.ShapeDtypeStruct((H_SB, m, D_SB), F32)
    ret16 = jax.ShapeDtypeStruct((H_RET, m, DK_RET), BF16)
    ret32 = jax.ShapeDtypeStruct((H_RET, m, DK_RET), F32)
    assert pos_mod >= m or tm % pos_mod == 0
    tile_stride = tm if pos_mod >= m else 0
    half = DK_RET // 2
    w_spec = pl.BlockSpec((D_MODEL, tn), lambda i, j: (0, j))
    out_shape = [sb16, sb32, sb16, sb32, sb16, ret16, ret32, ret16,
                 jax.ShapeDtypeStruct((m, SECTION_W), F32)]
    out_specs = [head_spec(H_SB, D_SB, 0), head_spec(H_SB, D_SB, 1), head_spec(H_SB, D_SB, 1),
                 head_spec(H_SB, D_SB, 2), head_spec(H_SB, D_SB, 2),
                 head_spec(H_RET, DK_RET, 3), head_spec(H_RET, DK_RET, 4),
                 head_spec(H_RET, DK_RET, 5),
                 pl.BlockSpec((tm, tn), lambda i, j: (i, gate_part(j)))]
    if emit_bf16:
        out_shape.append(jax.ShapeDtypeStruct(w_in.shape, BF16))
        out_specs.append(w_spec)
    out = pl.pallas_call(
        functools.partial(_inproj_kernel, tm=tm, n_split=n_split, tile_stride=tile_stride,
                          pos_base=pos_base, pos_mod=pos_mod, emit_bf16=emit_bf16),
        out_shape=out_shape,
        grid=(m // tm, N_SECTIONS * n_split),
        in_specs=[
            pl.BlockSpec((tm, D_MODEL), lambda i, j: (i, 0)),
            pl.BlockSpec((1, D_MODEL), lambda i, j: (0, 0)),
            w_spec,
            pl.BlockSpec((1, D_SB), lambda i, j: (0, 0)),
            pl.BlockSpec((1, D_SB), lambda i, j: (0, 0)),
            pl.BlockSpec((1, half), lambda i, j: (0, 0)),
        ],
        out_specs=out_specs,
        scratch_shapes=[pltpu.VMEM((tm, D_MODEL), BF16)] + [pltpu.VMEM((tm, half), F32)] * 4,
        compiler_params=_cparams(("arbitrary", "arbitrary")),
        name="inproj",
    )(x, norm_w, w_in, qn, kn, inv_freq)
    return (tuple(out[:9]), out[9]) if emit_bf16 else tuple(out)


def _softplus2(z):
    return jnp.maximum(z, 0.0) + jnp.log2(1.0 + jnp.exp2(-jnp.abs(z)))


def _suffix_matrix(n):
    r = lax.broadcasted_iota(jnp.int32, (2 * n, n), 0)
    c = lax.broadcasted_iota(jnp.int32, (2 * n, n), 1)
    return jnp.where(jnp.where(r >= n, r - n, r) >= c, 1.0, 0.0).astype(BF16)


def _sb_logits(q, k, mask):
    z = _dot_nt(q, k)
    sp = _softplus2(z)
    if mask is not None:
        sp = jnp.where(mask, sp, 0.0)
    hi = sp.astype(BF16)
    lo = (sp - hi.astype(F32)).astype(BF16)
    return z, jnp.concatenate([hi, lo], axis=1)


def _sb_suffix_sums(splits, tri):
    q = splits[0].shape[0]
    sums = _dot(jnp.concatenate(splits, axis=0), tri)
    return [sums[n * q:(n + 1) * q] for n in range(len(splits))]


def _sb_weights(z, incl, v, mask):
    a = jnp.exp2(z - incl)
    if mask is not None:
        a = jnp.where(mask, a, 0.0)
    return _dot(a.astype(BF16), v)


SB_DEAD_CARRY = 150.0


SB_HEADS_PER_STEP = 4
SB_LEFT_SUFFIX_GROUP = 2


def _sb_prompt_kernel(q_ref, k_ref, v_ref, *rest):
    n_side = len(rest) // 2
    o_ref = rest[n_side]
    for src, dst in zip(rest[:n_side], rest[n_side + 1:]):
        dst[...] = src[...].astype(BF16)
    qi = pl.program_id(1)
    heads, _, d = q_ref.shape
    blk = SB_BLOCK
    tri = _suffix_matrix(blk)
    r = lax.broadcasted_iota(jnp.int32, (blk, blk), 0)
    c = lax.broadcasted_iota(jnp.int32, (blk, blk), 1)
    strict = c < r
    has_prev = qi > 0

    def kv(g, block):
        s0 = pl.multiple_of(block * blk, blk)
        return k_ref[g, pl.ds(s0, blk), :], v_ref[g, pl.ds(s0, blk), :]

    qs = [q_ref[g] for g in range(heads)]
    diag = [kv(g, qi) for g in range(heads)]
    left = [kv(g, jnp.maximum(qi - 1, 0)) for g in range(heads)]

    def grouped_sums(splits, group):
        sums = []
        for n in range(0, len(splits), group):
            sums += _sb_suffix_sums(list(splits[n:n + group]), tri)
        return sums

    z_d, hl_d = zip(*[_sb_logits(qs[g], diag[g][0], strict) for g in range(heads)])
    sums_d = grouped_sums(hl_d, heads)
    z_l, hl_l = zip(*[_sb_logits(qs[g], left[g][0], None) for g in range(heads)])
    sums_l = grouped_sums(hl_l, SB_LEFT_SUFFIX_GROUP)
    acc_d = [_sb_weights(z_d[g], sums_d[g], diag[g][1], strict) for g in range(heads)]
    accs, carries = [], []
    for g in range(heads):
        carry_d = sums_d[g][:, 0:1]
        incl_l = sums_l[g] + carry_d
        d_l = _sb_weights(z_l[g], incl_l, left[g][1], None)
        accs.append(acc_d[g] + jnp.where(has_prev, d_l, 0.0))
        carries.append(jnp.where(has_prev, incl_l[:, 0:1], carry_d))
    accs, carries = tuple(accs), tuple(carries)

    def alive(carries):
        low = functools.reduce(jnp.minimum, carries)
        return (jnp.min(low) < SB_DEAD_CARRY).astype(jnp.int32)

    def cond(state):
        t, live, _, _ = state
        return jnp.logical_and(t >= 0, live > 0)

    def body(state):
        t, _, accs, carries = state
        blocks = [kv(g, t) for g in range(heads)]
        zs, hls = zip(*[_sb_logits(qs[g], blocks[g][0], None) for g in range(heads)])
        sums = _sb_suffix_sums(list(hls), tri)
        new_accs, new_carries = [], []
        for g in range(heads):
            incl = sums[g] + carries[g]
            new_accs.append(accs[g] + _sb_weights(zs[g], incl, blocks[g][1], None))
            new_carries.append(incl[:, 0:1])
        return t - 1, alive(new_carries), tuple(new_accs), tuple(new_carries)

    _, _, accs, _ = lax.while_loop(cond, body, (qi - 2, alive(carries), accs, carries))
    for g in range(heads):
        o_ref[:, g * d:(g + 1) * d] = accs[g].astype(o_ref.dtype)


def _sb_prompt(q16, k16, v16, side=()):
    h, s, d = q16.shape
    tq = SB_BLOCK
    g = SB_HEADS_PER_STEP
    n_groups, n_blocks = h // g, s // tq

    def side_spec(w):
        rows, cols = w.shape
        assert rows % (16 * n_blocks) == 0 and cols % (128 * n_groups) == 0
        return pl.BlockSpec((rows // n_blocks, cols // n_groups), lambda hg, qi: (qi, hg))

    side_specs = [side_spec(w) for w in side]
    out = pl.pallas_call(
        _sb_prompt_kernel,
        out_shape=[jax.ShapeDtypeStruct((s, h * d), BF16)]
        + [jax.ShapeDtypeStruct(w.shape, BF16) for w in side],
        grid=(n_groups, n_blocks),
        in_specs=[
            pl.BlockSpec((g, tq, d), lambda hg, qi: (hg, qi, 0)),
            pl.BlockSpec((g, s, d), lambda hg, qi: (hg, 0, 0)),
            pl.BlockSpec((g, s, d), lambda hg, qi: (hg, 0, 0)),
        ] + side_specs,
        out_specs=[pl.BlockSpec((tq, g * d), lambda hg, qi: (qi, hg))] + side_specs,
        compiler_params=_cparams(("arbitrary", "arbitrary")),
        name="sb_prompt",
    )(q16, k16, v16, *side)
    return out[0], tuple(out[1:])


def _sb_sample_kernel(q_ref, kn_ref, vn_ref, kc_hbm, vc_hbm, o_ref, kbuf, vbuf, sem, *, past):
    nh, t, d = q_ref.shape
    ht = nh * t
    blk = SB_BLOCK
    n_blocks = past // blk
    b = pl.program_id(0)
    slot = lax.rem(b, 2)

    def cache_copies(batch, block, to_slot):
        rows = pl.ds(pl.multiple_of(block * blk, blk), blk)
        return (pltpu.make_async_copy(kc_hbm.at[batch, :, rows, :], kbuf.at[to_slot], sem.at[0, to_slot]),
                pltpu.make_async_copy(vc_hbm.at[batch, :, rows, :], vbuf.at[to_slot], sem.at[1, to_slot]))

    @pl.when(b == 0)
    def _():
        for cp in cache_copies(0, n_blocks - 1, 0):
            cp.start()

    @pl.when(b + 1 < pl.num_programs(0))
    def _():
        for cp in cache_copies(b + 1, n_blocks - 1, 1 - slot):
            cp.start()

    q_all = q_ref[...].reshape(ht, d).astype(F32)
    q_head = lax.broadcasted_iota(jnp.int32, (ht, d), 0) // t
    q_masked = [jnp.where(q_head == h, q_all, 0.0).astype(BF16) for h in range(nh)]

    def logits(keys_of_head):
        z = _dot_nt(keys_of_head(0), q_masked[0])
        for h in range(1, nh):
            z = z + _dot_nt(keys_of_head(h), q_masked[h])
        return z

    def suffix_matrix(n):
        r = lax.broadcasted_iota(jnp.int32, (n, 2 * n), 0)
        c = lax.broadcasted_iota(jnp.int32, (n, 2 * n), 1)
        return jnp.where(jnp.where(c >= n, c - n, c) >= r, 1.0, 0.0).astype(BF16)

    def suffix_sum(sp, lmat):
        hi = sp.astype(BF16)
        lo = (sp - hi.astype(F32)).astype(BF16)
        return _dot(lmat, jnp.concatenate([hi, lo], axis=0))

    def emit(a_t, values_of_head, acc):
        a = a_t.T.astype(BF16)
        return [acc[h] + _dot(a[h * t:(h + 1) * t, :], values_of_head(h)) for h in range(nh)]

    z = logits(lambda h: kn_ref[h])
    s_idx = lax.broadcasted_iota(jnp.int32, (t, ht), 0)
    t_idx = lax.broadcasted_iota(jnp.int32, (t, ht), 1) % t
    strict = s_idx < t_idx
    sp = jnp.where(strict, _softplus2(z), 0.0)
    incl = suffix_sum(sp, suffix_matrix(t))
    a_t = jnp.where(strict, jnp.exp2(z - incl), 0.0)
    acc = emit(a_t, lambda h: vn_ref[h], [jnp.zeros((t, d), F32) for _ in range(nh)])
    carry = incl[0:1, :]

    lmat = suffix_matrix(blk)

    def cache_block(from_slot, acc, carry):
        z = logits(lambda h: kbuf[from_slot, h].astype(BF16))
        incl = suffix_sum(_softplus2(z), lmat) + carry
        acc = emit(jnp.exp2(z - incl), lambda h: vbuf[from_slot, h].astype(BF16), acc)
        return acc, incl[0:1, :]

    def alive(carry):
        return (jnp.min(carry) < SB_DEAD_CARRY).astype(jnp.int32)

    for cp in cache_copies(b, n_blocks - 1, slot):
        cp.wait()
    acc, carry = cache_block(slot, acc, carry)

    def cond(state):
        block, live, _, _ = state
        return jnp.logical_and(block >= 0, live > 0)

    def body(state):
        block, _, acc, carry = state
        copies = cache_copies(b, block, 2)
        for cp in copies:
            cp.start()
        for cp in copies:
            cp.wait()
        acc, carry = cache_block(2, list(acc), carry)
        return block - 1, alive(carry), tuple(acc), carry

    _, _, acc, _ = lax.while_loop(cond, body, (n_blocks - 2, alive(carry), tuple(acc), carry))
    for h in range(nh):
        o_ref[:, h * d:(h + 1) * d] = acc[h].astype(o_ref.dtype)


def _sb_sample(q16, k16, v16, cache_k, cache_v, *, t):
    h, m, d = q16.shape
    nb = m // t
    past = cache_k.shape[2]
    assert past % SB_BLOCK == 0
    new_spec = pl.BlockSpec((h, t, d), lambda b: (0, b, 0))
    cache_spec = pl.BlockSpec(memory_space=pl.ANY)
    return pl.pallas_call(
        functools.partial(_sb_sample_kernel, past=past),
        out_shape=jax.ShapeDtypeStruct((m, h * d), BF16),
        grid=(nb,),
        in_specs=[new_spec, new_spec, new_spec, cache_spec, cache_spec],
        out_specs=pl.BlockSpec((t, h * d), lambda b: (b, 0)),
        scratch_shapes=[pltpu.VMEM((3, h, SB_BLOCK, d), F32),
                        pltpu.VMEM((3, h, SB_BLOCK, d), F32),
                        pltpu.SemaphoreType.DMA((2, 3))],
        compiler_params=_cparams(("arbitrary",)),
        name="sb_sample",
    )(q16, k16, v16, cache_k, cache_v)


def _ret_log_decay(hd):
    return math.log(1.0 - 2.0 ** (-5.0 - hd))


def _ret_kernel(q_ref, k_ref, v_ref, g_ref, nw_ref, s0_ref, r_ref, st_ref,
                intra_ref, qdec_ref, kdec_ref, *, chunk):
    c = pl.program_id(1)

    @pl.when(jnp.logical_and(pl.program_id(0) == 0, c == 0))
    def _():
        row = lax.broadcasted_iota(jnp.int32, (chunk, chunk), 0)
        col = lax.broadcasted_iota(jnp.int32, (chunk, chunk), 1)
        diff = (row - col).astype(F32)
        pos = lax.broadcasted_iota(jnp.int32, (chunk, DK_RET), 0).astype(F32)
        for hd in range(H_RET):
            lg = _ret_log_decay(hd)
            intra_ref[hd] = jnp.where(row >= col, jnp.exp(lg * jnp.maximum(diff, 0.0)), 0.0)
            qdec_ref[hd] = jnp.exp(lg * (pos + 1.0))
            kdec_ref[hd] = jnp.exp(lg * (chunk - 1.0 - pos))

    @pl.when(c == 0)
    def _():
        st_ref[...] = s0_ref[...]

    for sub in range(q_ref.shape[1] // chunk):
        rows = slice(sub * chunk, (sub + 1) * chunk)
        for hd in range(H_RET):
            intra, q_dec, k_dec = intra_ref[hd], qdec_ref[hd], kdec_ref[hd]
            c_dec = math.exp(_ret_log_decay(hd) * chunk)
            q = q_ref[hd, rows, :]
            k = k_ref[hd, rows, :]
            v = v_ref[hd, rows, :]
            state = st_ref[hd]
            scores = (_dot_nt(q, k.astype(BF16)) * intra).astype(BF16)
            o = _dot(scores, v) + _dot(q, state.astype(BF16)) * q_dec
            st_ref[hd] = c_dec * state + _dot_tn((k * k_dec).astype(BF16), v)
            o = o * lax.rsqrt(jnp.mean(o * o, axis=-1, keepdims=True) + EPS)
            sl = slice(hd * DV_RET, (hd + 1) * DV_RET)
            gate = g_ref[rows, sl]
            r_ref[rows, sl] = (o * nw_ref[:, sl] * (gate * jax.nn.sigmoid(gate))).astype(r_ref.dtype)


def _retention(rq, rk, rv, gate, norm_w, state0, *, chunk, per_step):
    h, m, _ = rq.shape
    nb = state0.shape[0]
    rows = per_step * chunk
    nc = m // (nb * rows)
    qkv_spec = pl.BlockSpec((h, rows, DK_RET), lambda b, c: (0, b * nc + c, 0))
    st_spec = pl.BlockSpec((None, h, DK_RET, DV_RET), lambda b, c: (b, 0, 0, 0))
    return pl.pallas_call(
        functools.partial(_ret_kernel, chunk=chunk),
        out_shape=(jax.ShapeDtypeStruct((m, h * DV_RET), BF16),
                   jax.ShapeDtypeStruct(state0.shape, F32)),
        grid=(nb, nc),
        in_specs=[qkv_spec, qkv_spec, qkv_spec,
                  pl.BlockSpec((rows, h * DV_RET), lambda b, c: (b * nc + c, 0)),
                  pl.BlockSpec((1, h * DV_RET), lambda b, c: (0, 0)),
                  st_spec],
        out_specs=(pl.BlockSpec((rows, h * DV_RET), lambda b, c: (b * nc + c, 0)), st_spec),
        scratch_shapes=[pltpu.VMEM((h, chunk, chunk), F32),
                        pltpu.VMEM((h, chunk, DK_RET), F32),
                        pltpu.VMEM((h, chunk, DK_RET), F32)],
        compiler_params=_cparams(("arbitrary", "arbitrary")),
        name="retention",
    )(rq, rk, rv, gate, norm_w, state0)


def _outproj_kernel(x_ref, a_ref, b_ref, wa_ref, wb_ref, o_ref):
    o_ref[...] = x_ref[...] + _dot(a_ref[...], wa_ref[...]) + _dot(b_ref[...], wb_ref[...])


def _outproj(x, a_sb, a_ret, w_out, *, tm):
    m = x.shape[0]
    row = lambda i: (i, 0)
    return pl.pallas_call(
        _outproj_kernel,
        out_shape=jax.ShapeDtypeStruct((m, D_MODEL), F32),
        grid=(m // tm,),
        in_specs=[
            pl.BlockSpec((tm, D_MODEL), row),
            pl.BlockSpec((tm, SB_W), row),
            pl.BlockSpec((tm, RET_W), row),
            pl.BlockSpec((SB_W, D_MODEL), lambda i: (0, 0)),
            pl.BlockSpec((RET_W, D_MODEL), lambda i: (1, 0)),
        ],
        out_specs=pl.BlockSpec((tm, D_MODEL), row),
        compiler_params=_cparams(("arbitrary",)),
        name="outproj",
    )(x, a_sb, a_ret, w_out, w_out)


OUTPROJ_TK = 512


def _outproj_f32w_kernel(x_ref, a_ref, w_ref, o_ref, w16_ref):
    @pl.when(pl.program_id(0) == 0)
    def _():
        o_ref[...] = x_ref[...]

    w16_ref[...] = w_ref[...].astype(BF16)
    o_ref[...] += _dot(a_ref[...], w16_ref[...])


def _outproj_f32w(x, a, w_out):
    m = x.shape[0]
    d_mix = w_out.shape[0]
    return pl.pallas_call(
        _outproj_f32w_kernel,
        out_shape=(jax.ShapeDtypeStruct((m, D_MODEL), F32),
                   jax.ShapeDtypeStruct(w_out.shape, BF16)),
        grid=(d_mix // OUTPROJ_TK,),
        in_specs=[
            pl.BlockSpec((m, D_MODEL), lambda k: (0, 0)),
            pl.BlockSpec((m, OUTPROJ_TK), lambda k: (0, k)),
            pl.BlockSpec((OUTPROJ_TK, D_MODEL), lambda k: (k, 0)),
        ],
        out_specs=(pl.BlockSpec((m, D_MODEL), lambda k: (0, 0)),
                   pl.BlockSpec((OUTPROJ_TK, D_MODEL), lambda k: (k, 0))),
        compiler_params=_cparams(("arbitrary",)),
        name="outproj_f32w",
    )(x, a, w_out)


def _mixers(x1, weights, sb_fn, state0, *, tm, proj_tm, proj_split, pos_base, pos_mod, chunk,
            chunks_per_step):
    (nmix, w_in, qn, kn, ron, w_out, inv_freq) = weights
    f32w = w_in.dtype == F32
    proj = _inproj(x1, nmix, w_in, qn, kn, inv_freq, tm=proj_tm, n_split=proj_split,
                   pos_base=pos_base, pos_mod=pos_mod)
    if f32w:
        proj, w_in = proj
    sq, sk, sk16, sv, sv16, rq, rk, rv, gate = proj
    a_sb, sb_side = sb_fn(sq, sk16, sv16)
    a_ret, state = _retention(rq, rk, rv, gate, ron, state0, chunk=chunk, per_step=chunks_per_step)
    if f32w:
        x2, w_out = _outproj_f32w(x1, jnp.concatenate([a_sb, a_ret], axis=1), w_out)
    else:
        x2 = _outproj(x1, a_sb, a_ret, w_out, tm=tm)
    return x2, sk, sv, state, (w_in, w_out), sb_side


def kernel(x_prompt, x_sample, cache_sb_k, cache_sb_v, state_ret, ffn1_norm, ffn1_w_gate, ffn1_w_up, ffn1_w_down, mix_norm, w_in, sb_q_norm, sb_k_norm, ret_out_norm, w_out, ffn2_norm, ffn2_w_gate, ffn2_w_up, ffn2_w_down, final_norm):
    depth = ffn1_norm.shape[0]
    assert depth == 1
    nb_p, seq, _ = x_prompt.shape
    nb_s, dec_seq, _ = x_sample.shape
    past = cache_sb_k.shape[3]
    assert nb_p == 1

    half = DK_RET // 2
    inv_freq = (ROPE_BASE ** (-jnp.arange(half, dtype=F32) / half)).reshape(1, half)
    l = 0
    n1, n2, nf = ffn1_norm[l][None], ffn2_norm[l][None], final_norm[l][None]

    def mixer_weights(w_in_, w_out_):
        return (mix_norm[l][None], w_in_, sb_q_norm[l][None], sb_k_norm[l][None],
                ret_out_norm[l][None], w_out_, inv_freq)

    ms = nb_s * dec_seq
    xs = x_sample.reshape(ms, D_MODEL)
    xp = x_prompt.reshape(seq, D_MODEL)
    x1s, ffn1_16 = _ffn(xs, n1, ffn1_w_gate[l], ffn1_w_up[l], ffn1_w_down[l], tm=ms)
    x1p = _ffn(xp, n1, *ffn1_16, tm=1024)

    def sb_s(sq, sk16, sv16):
        return _sb_sample(sq, sk16, sv16, cache_sb_k[l], cache_sb_v[l], t=dec_seq), ()

    x2s, sks, svs, sts, (w_in16, w_out16), _ = _mixers(
        x1s, mixer_weights(w_in[l], w_out[l]), sb_s, state_ret[l],
        tm=ms, proj_tm=ms, proj_split=1, pos_base=past, pos_mod=dec_seq, chunk=dec_seq,
        chunks_per_step=1)

    sb_p = functools.partial(_sb_prompt, side=(ffn2_w_gate[l], ffn2_w_up[l], ffn2_w_down[l]))
    zero_state = jnp.zeros((1, H_RET, DK_RET, DV_RET), F32)
    x2p, skp, svp, stp, _, ffn2_16 = _mixers(
        x1p, mixer_weights(w_in16, w_out16), sb_p, zero_state,
        tm=512, proj_tm=1024, proj_split=2, pos_base=0, pos_mod=seq, chunk=RET_CHUNK,
        chunks_per_step=4)

    ys = _ffn(x2s, n2, *ffn2_16, nf, tm=ms)
    yp = _ffn(x2p, n2, *ffn2_16, nf, tm=1024)

    def cache_layout(t):
        return t.reshape(H_SB, nb_s, dec_seq, D_SB).transpose(1, 0, 2, 3)[None]

    return (yp.reshape(1, seq, D_MODEL), ys.reshape(nb_s, dec_seq, D_MODEL),
            skp[None, None], svp[None, None], stp[None],
            cache_layout(sks), cache_layout(svs), sts[None])
```

```python
import functools
import math

import jax
import jax.numpy as jnp
from jax import lax
from jax.experimental import pallas as pl
from jax.experimental.pallas import tpu as pltpu

F32 = jnp.float32
BF16 = jnp.bfloat16

D_MODEL = 2048
D_FF = 5632
H_SB = 8
D_SB = 128
H_RET = 4
DK_RET = 256
DV_RET = 256
SB_W = H_SB * D_SB
RET_W = H_RET * DK_RET
N_SECTIONS = 7
SECTION_W = 1024
ROPE_BASE = 10000.0
EPS = 1e-6

V7X_VMEM_LIMIT_BYTES = 56 * 1024 * 1024
SB_BLOCK = 256
RET_CHUNK = 256
FFN_TF = 512

LOG2E = 1.4426950408889634


def _cparams(sem):
    return pltpu.CompilerParams(dimension_semantics=sem,
                                vmem_limit_bytes=V7X_VMEM_LIMIT_BYTES)


def _rmsnorm_rows(x, w):
    ms = jnp.mean(x * x, axis=-1, keepdims=True)
    return x * lax.rsqrt(ms + EPS) * w


def _dot(a, b):
    return jnp.dot(a, b, preferred_element_type=F32)


def _dot_nt(a, b):
    return lax.dot_general(a, b, (((1,), (1,)), ((), ())), preferred_element_type=F32)


def _dot_tn(a, b):
    return lax.dot_general(a, b, (((0,), (0,)), ((), ())), preferred_element_type=F32)


def _ffn_kernel(*refs, n_steps, final_norm, emit_bf16):
    x_ref, nw_ref, wg_ref, wu_ref, wd_ref = refs[:5]
    refs = refs[5:]
    fw_ref = None
    if final_norm:
        fw_ref, refs = refs[0], refs[1:]
    o_ref, refs = refs[0], refs[1:]
    if emit_bf16:
        wg16_ref, wu16_ref, wd16_ref, h_ref = refs
        wg16_ref[...] = wg_ref[...].astype(BF16)
        wu16_ref[...] = wu_ref[...].astype(BF16)
        wd16_ref[...] = wd_ref[...].astype(BF16)
        wg_ref, wu_ref, wd_ref = wg16_ref, wu16_ref, wd16_ref
    else:
        (h_ref,) = refs
    j = pl.program_id(1)

    @pl.when(j == 0)
    def _():
        x = x_ref[...]
        h_ref[...] = _rmsnorm_rows(x, nw_ref[...]).astype(BF16)
        o_ref[...] = x

    h = h_ref[...]
    g = _dot(h, wg_ref[...])
    u = _dot(h, wu_ref[...])
    a = (g * jax.nn.sigmoid(g) * (0.5 * u)).astype(BF16)
    o_ref[...] += _dot(a, wd_ref[...])

    if final_norm:
        @pl.when(j == n_steps - 1)
        def _():
            o_ref[...] = _rmsnorm_rows(o_ref[...], fw_ref[...])


def _ffn(x, norm_w, wg, wu, wd, final_w=None, *, tm):
    m = x.shape[0]
    n_steps = D_FF // FFN_TF
    emit_bf16 = wg.dtype == F32
    assert not emit_bf16 or m == tm
    w_specs = [
        pl.BlockSpec((D_MODEL, FFN_TF), lambda i, j: (0, j)),
        pl.BlockSpec((D_MODEL, FFN_TF), lambda i, j: (0, j)),
        pl.BlockSpec((FFN_TF, D_MODEL), lambda i, j: (j, 0)),
    ]
    in_specs = [
        pl.BlockSpec((tm, D_MODEL), lambda i, j: (i, 0)),
        pl.BlockSpec((1, D_MODEL), lambda i, j: (0, 0)),
    ] + w_specs
    args = [x, norm_w, wg, wu, wd]
    if final_w is not None:
        in_specs.append(pl.BlockSpec((1, D_MODEL), lambda i, j: (0, 0)))
        args.append(final_w)
    out_shape = [jax.ShapeDtypeStruct((m, D_MODEL), F32)]
    out_specs = [pl.BlockSpec((tm, D_MODEL), lambda i, j: (i, 0))]
    if emit_bf16:
        out_shape += [jax.ShapeDtypeStruct(w.shape, BF16) for w in (wg, wu, wd)]
        out_specs += w_specs
    out = pl.pallas_call(
        functools.partial(_ffn_kernel, n_steps=n_steps, final_norm=final_w is not None,
                          emit_bf16=emit_bf16),
        out_shape=out_shape,
        grid=(m // tm, n_steps),
        in_specs=in_specs,
        out_specs=out_specs,
        scratch_shapes=[pltpu.VMEM((tm, D_MODEL), BF16)],
        compiler_params=_cparams(("arbitrary", "arbitrary")),
        name="ffn_final" if final_w is not None else "ffn",
    )(*args)
    return (out[0], tuple(out[1:])) if emit_bf16 else out[0]


INPROJ_SECTION_OUTPUTS = ((None, "sq"), ("sk", "sk16"), ("sv", "sv16"), (None, "rq"),
                          ("rk", None), (None, "rv"), ("gate", None))
INPROJ_OUTPUT_ORDER = ("sq", "sk", "sk16", "sv", "sv16", "rq", "rk", "rv", "gate")
N_SLICES = SECTION_W // D_SB


def _inproj_kernel(x_ref, nw_ref, w_ref, qn_ref, kn_ref, inv_ref, *rest,
                   tm, tile_stride, pos_base, pos_mod, emit_bf16):
    out = dict(zip(INPROJ_OUTPUT_ORDER, rest[:9]))
    rest = rest[9:]
    if emit_bf16:
        w16_ref, rest = rest[0], rest[1:]
        w16_ref[...] = w_ref[...].astype(BF16)
        w_ref = w16_ref
    h_ref, cos_ref, sin_ref, cos_row_ref, sin_row_ref, st32, st16, sem = rest
    stage = (st32, st16)
    i = pl.program_id(0)
    j = pl.program_id(1)
    step = i * N_SECTIONS + j
    slot = lax.rem(step, 2)

    def copies(section, tile, from_slot):
        rows = pl.ds(pl.multiple_of(tile * tm, tm), tm)
        return [pltpu.make_async_copy(stage[kind].at[from_slot], out[name].at[:, rows, :],
                                      sem.at[kind, from_slot])
                for kind, name in enumerate(INPROJ_SECTION_OUTPUTS[section]) if name is not None]

    @pl.when(jnp.logical_and(i == 0, j == 0))
    def _():
        row = lax.broadcasted_iota(jnp.int32, (tm, DK_RET // 2), 0)
        ang = lax.rem(row, pos_mod).astype(F32) * inv_ref[...]
        cos_row_ref[...] = jnp.cos(ang)
        sin_row_ref[...] = jnp.sin(ang)

    @pl.when(j == 0)
    def _():
        h_ref[...] = _rmsnorm_rows(x_ref[...], nw_ref[...]).astype(BF16)
        ang = (pos_base + i * tile_stride).astype(F32) * inv_ref[...]
        ca, sa = jnp.cos(ang), jnp.sin(ang)
        cb, sb = cos_row_ref[...], sin_row_ref[...]
        cos_ref[...] = ca * cb - sa * sb
        sin_ref[...] = sa * cb + ca * sb

    def epilogue(section, p, put32, put16):
        def piece(n):
            return p[:, n * D_SB:(n + 1) * D_SB]

        if section == 0:
            w = qn_ref[...] * (D_SB ** -0.5 * LOG2E)
            for n in range(N_SLICES):
                put16(n, _rmsnorm_rows(piece(n), w).astype(BF16))
        elif section == 1:
            for n in range(N_SLICES):
                k = _rmsnorm_rows(piece(n), kn_ref[...])
                put32(n, k)
                put16(n, k.astype(BF16))
        elif section == 2:
            for n in range(N_SLICES):
                put32(n, piece(n))
                put16(n, piece(n).astype(BF16))
        elif section in (3, 4):
            c, s = cos_ref[...], sin_ref[...]
            for hd in range(H_RET):
                x1, x2 = piece(2 * hd), piece(2 * hd + 1)
                r1, r2 = x1 * c - x2 * s, x1 * s + x2 * c
                if section == 3:
                    put16(2 * hd, r1.astype(BF16))
                    put16(2 * hd + 1, r2.astype(BF16))
                else:
                    put32(2 * hd, r1 * (DK_RET ** -0.5))
                    put32(2 * hd + 1, r2 * (DK_RET ** -0.5))
        elif section == 5:
            for n in range(N_SLICES):
                put16(n, piece(n).astype(BF16))
        else:
            for n in range(N_SLICES):
                put32(n, piece(n))

    last_step = pl.num_programs(0) * N_SECTIONS - 1
    for section in range(N_SECTIONS):
        @pl.when(j == section)
        def _(section=section):
            @pl.when(step >= 2)
            def _():
                for cp in copies((section - 2) % N_SECTIONS, i if section >= 2 else i - 1, slot):
                    cp.wait()

            def put32(n, value):
                st32[slot, n] = value

            def put16(n, value):
                st16[slot, n] = value

            epilogue(section, _dot(h_ref[...], w_ref[...]), put32, put16)
            for cp in copies(section, i, slot):
                cp.start()

            if section == N_SECTIONS - 1:
                @pl.when(step == last_step)
                def _():
                    for cp in copies(section - 1, i, 1 - slot) + copies(section, i, slot):
                        cp.wait()


def _inproj(x, norm_w, w_in, qn, kn, inv_freq, *, tm, pos_base, pos_mod):
    m = x.shape[0]
    emit_bf16 = w_in.dtype == F32
    assert not emit_bf16 or m == tm
    assert pos_mod >= m or tm % pos_mod == 0
    tile_stride = tm if pos_mod >= m else 0
    half = DK_RET // 2
    f32_outputs = {name for name, _ in INPROJ_SECTION_OUTPUTS if name is not None}
    out_shape = [jax.ShapeDtypeStruct((N_SLICES, m, D_SB), F32 if name in f32_outputs else BF16)
                 for name in INPROJ_OUTPUT_ORDER]
    out_specs = [pl.BlockSpec(memory_space=pl.ANY)] * len(out_shape)
    w_spec = pl.BlockSpec((D_MODEL, SECTION_W), lambda i, j: (0, j))
    if emit_bf16:
        out_shape.append(jax.ShapeDtypeStruct(w_in.shape, BF16))
        out_specs.append(w_spec)
    out = pl.pallas_call(
        functools.partial(_inproj_kernel, tm=tm, tile_stride=tile_stride,
                          pos_base=pos_base, pos_mod=pos_mod, emit_bf16=emit_bf16),
        out_shape=out_shape,
        grid=(m // tm, N_SECTIONS),
        in_specs=[
            pl.BlockSpec((tm, D_MODEL), lambda i, j: (i, 0)),
            pl.BlockSpec((1, D_MODEL), lambda i, j: (0, 0)),
            w_spec,
            pl.BlockSpec((1, D_SB), lambda i, j: (0, 0)),
            pl.BlockSpec((1, D_SB), lambda i, j: (0, 0)),
            pl.BlockSpec((1, half), lambda i, j: (0, 0)),
        ],
        out_specs=out_specs,
        scratch_shapes=[pltpu.VMEM((tm, D_MODEL), BF16)] + [pltpu.VMEM((tm, half), F32)] * 4
        + [pltpu.VMEM((2, N_SLICES, tm, D_SB), F32), pltpu.VMEM((2, N_SLICES, tm, D_SB), BF16),
           pltpu.SemaphoreType.DMA((2, 2))],
        compiler_params=_cparams(("arbitrary", "arbitrary")),
        name="inproj",
    )(x, norm_w, w_in, qn, kn, inv_freq)
    return (tuple(out[:9]), out[9]) if emit_bf16 else tuple(out)


def _softplus2(z):
    return jnp.maximum(z, 0.0) + jnp.log2(1.0 + jnp.exp2(-jnp.abs(z)))


def _suffix_matrix(n):
    r = lax.broadcasted_iota(jnp.int32, (2 * n, n), 0)
    c = lax.broadcasted_iota(jnp.int32, (2 * n, n), 1)
    return jnp.where(jnp.where(r >= n, r - n, r) >= c, 1.0, 0.0).astype(BF16)


def _sb_logits(q, k, mask):
    z = _dot_nt(q, k)
    sp = _softplus2(z)
    if mask is not None:
        sp = jnp.where(mask, sp, 0.0)
    hi = sp.astype(BF16)
    lo = (sp - hi.astype(F32)).astype(BF16)
    return z, jnp.concatenate([hi, lo], axis=1)


def _sb_suffix_sums(splits, tri):
    q = splits[0].shape[0]
    sums = _dot(jnp.concatenate(splits, axis=0), tri)
    return [sums[n * q:(n + 1) * q] for n in range(len(splits))]


def _sb_weights(z, incl, v, mask):
    a = jnp.exp2(z - incl)
    if mask is not None:
        a = jnp.where(mask, a, 0.0)
    return _dot(a.astype(BF16), v)


SB_DEAD_CARRY = 150.0


SB_HEADS_PER_STEP = 4
SB_LEFT_SUFFIX_GROUP = 2


def _sb_prompt_kernel(q_ref, k_ref, v_ref, *rest):
    n_side = len(rest) // 2
    o_ref = rest[n_side]
    for src, dst in zip(rest[:n_side], rest[n_side + 1:]):
        dst[...] = src[...].astype(BF16)
    qi = pl.program_id(1)
    heads, _, d = q_ref.shape
    blk = SB_BLOCK
    tri = _suffix_matrix(blk)
    r = lax.broadcasted_iota(jnp.int32, (blk, blk), 0)
    c = lax.broadcasted_iota(jnp.int32, (blk, blk), 1)
    strict = c < r
    has_prev = qi > 0

    def kv(g, block):
        s0 = pl.multiple_of(block * blk, blk)
        return k_ref[g, pl.ds(s0, blk), :], v_ref[g, pl.ds(s0, blk), :]

    qs = [q_ref[g] for g in range(heads)]
    diag = [kv(g, qi) for g in range(heads)]
    left = [kv(g, jnp.maximum(qi - 1, 0)) for g in range(heads)]

    def grouped_sums(splits, group):
        sums = []
        for n in range(0, len(splits), group):
            sums += _sb_suffix_sums(list(splits[n:n + group]), tri)
        return sums

    z_d, hl_d = zip(*[_sb_logits(qs[g], diag[g][0], strict) for g in range(heads)])
    sums_d = grouped_sums(hl_d, heads)
    z_l, hl_l = zip(*[_sb_logits(qs[g], left[g][0], None) for g in range(heads)])
    sums_l = grouped_sums(hl_l, SB_LEFT_SUFFIX_GROUP)
    acc_d = [_sb_weights(z_d[g], sums_d[g], diag[g][1], strict) for g in range(heads)]
    accs, carries = [], []
    for g in range(heads):
        carry_d = sums_d[g][:, 0:1]
        incl_l = sums_l[g] + carry_d
        d_l = _sb_weights(z_l[g], incl_l, left[g][1], None)
        accs.append(acc_d[g] + jnp.where(has_prev, d_l, 0.0))
        carries.append(jnp.where(has_prev, incl_l[:, 0:1], carry_d))
    accs, carries = tuple(accs), tuple(carries)

    def alive(carries):
        low = functools.reduce(jnp.minimum, carries)
        return (jnp.min(low) < SB_DEAD_CARRY).astype(jnp.int32)

    def cond(state):
        t, live, _, _ = state
        return jnp.logical_and(t >= 0, live > 0)

    def body(state):
        t, _, accs, carries = state
        blocks = [kv(g, t) for g in range(heads)]
        zs, hls = zip(*[_sb_logits(qs[g], blocks[g][0], None) for g in range(heads)])
        sums = _sb_suffix_sums(list(hls), tri)
        new_accs, new_carries = [], []
        for g in range(heads):
            incl = sums[g] + carries[g]
            new_accs.append(accs[g] + _sb_weights(zs[g], incl, blocks[g][1], None))
            new_carries.append(incl[:, 0:1])
        return t - 1, alive(new_carries), tuple(new_accs), tuple(new_carries)

    _, _, accs, _ = lax.while_loop(cond, body, (qi - 2, alive(carries), accs, carries))
    for g in range(heads):
        o_ref[:, g * d:(g + 1) * d] = accs[g].astype(o_ref.dtype)


def _sb_prompt(q16, k16, v16, side=()):
    h, s, d = q16.shape
    tq = SB_BLOCK
    g = SB_HEADS_PER_STEP
    n_groups, n_blocks = h // g, s // tq

    def side_spec(w):
        rows, cols = w.shape
        assert rows % (16 * n_blocks) == 0 and cols % (128 * n_groups) == 0
        return pl.BlockSpec((rows // n_blocks, cols // n_groups), lambda hg, qi: (qi, hg))

    side_specs = [side_spec(w) for w in side]
    out = pl.pallas_call(
        _sb_prompt_kernel,
        out_shape=[jax.ShapeDtypeStruct((s, h * d), BF16)]
        + [jax.ShapeDtypeStruct(w.shape, BF16) for w in side],
        grid=(n_groups, n_blocks),
        in_specs=[
            pl.BlockSpec((g, tq, d), lambda hg, qi: (hg, qi, 0)),
            pl.BlockSpec((g, s, d), lambda hg, qi: (hg, 0, 0)),
            pl.BlockSpec((g, s, d), lambda hg, qi: (hg, 0, 0)),
        ] + side_specs,
        out_specs=[pl.BlockSpec((tq, g * d), lambda hg, qi: (qi, hg))] + side_specs,
        compiler_params=_cparams(("arbitrary", "arbitrary")),
        name="sb_prompt",
    )(q16, k16, v16, *side)
    return out[0], tuple(out[1:])


def _sb_sample_kernel(q_ref, kn_ref, vn_ref, kc_hbm, vc_hbm, o_ref, kbuf, vbuf, sem, *, past):
    nh, t, d = q_ref.shape
    ht = nh * t
    blk = SB_BLOCK
    n_blocks = past // blk
    b = pl.program_id(0)
    slot = lax.rem(b, 2)

    def cache_copies(batch, block, to_slot):
        rows = pl.ds(pl.multiple_of(block * blk, blk), blk)
        return (pltpu.make_async_copy(kc_hbm.at[batch, :, rows, :], kbuf.at[to_slot], sem.at[0, to_slot]),
                pltpu.make_async_copy(vc_hbm.at[batch, :, rows, :], vbuf.at[to_slot], sem.at[1, to_slot]))

    @pl.when(b == 0)
    def _():
        for cp in cache_copies(0, n_blocks - 1, 0):
            cp.start()

    @pl.when(b + 1 < pl.num_programs(0))
    def _():
        for cp in cache_copies(b + 1, n_blocks - 1, 1 - slot):
            cp.start()

    q_all = q_ref[...].reshape(ht, d).astype(F32)
    q_head = lax.broadcasted_iota(jnp.int32, (ht, d), 0) // t
    q_masked = [jnp.where(q_head == h, q_all, 0.0).astype(BF16) for h in range(nh)]

    def logits(keys_of_head):
        z = _dot_nt(keys_of_head(0), q_masked[0])
        for h in range(1, nh):
            z = z + _dot_nt(keys_of_head(h), q_masked[h])
        return z

    def suffix_matrix(n):
        r = lax.broadcasted_iota(jnp.int32, (n, 2 * n), 0)
        c = lax.broadcasted_iota(jnp.int32, (n, 2 * n), 1)
        return jnp.where(jnp.where(c >= n, c - n, c) >= r, 1.0, 0.0).astype(BF16)

    def suffix_sum(sp, lmat):
        hi = sp.astype(BF16)
        lo = (sp - hi.astype(F32)).astype(BF16)
        return _dot(lmat, jnp.concatenate([hi, lo], axis=0))

    def emit(a_t, values_of_head, acc):
        a = a_t.T.astype(BF16)
        return [acc[h] + _dot(a[h * t:(h + 1) * t, :], values_of_head(h)) for h in range(nh)]

    z = logits(lambda h: kn_ref[h])
    s_idx = lax.broadcasted_iota(jnp.int32, (t, ht), 0)
    t_idx = lax.broadcasted_iota(jnp.int32, (t, ht), 1) % t
    strict = s_idx < t_idx
    sp = jnp.where(strict, _softplus2(z), 0.0)
    incl = suffix_sum(sp, suffix_matrix(t))
    a_t = jnp.where(strict, jnp.exp2(z - incl), 0.0)
    acc = emit(a_t, lambda h: vn_ref[h], [jnp.zeros((t, d), F32) for _ in range(nh)])
    carry = incl[0:1, :]

    lmat = suffix_matrix(blk)

    def cache_block(from_slot, acc, carry):
        z = logits(lambda h: kbuf[from_slot, h].astype(BF16))
        incl = suffix_sum(_softplus2(z), lmat) + carry
        acc = emit(jnp.exp2(z - incl), lambda h: vbuf[from_slot, h].astype(BF16), acc)
        return acc, incl[0:1, :]

    def alive(carry):
        return (jnp.min(carry) < SB_DEAD_CARRY).astype(jnp.int32)

    for cp in cache_copies(b, n_blocks - 1, slot):
        cp.wait()
    acc, carry = cache_block(slot, acc, carry)

    def cond(state):
        block, live, _, _ = state
        return jnp.logical_and(block >= 0, live > 0)

    def body(state):
        block, _, acc, carry = state
        copies = cache_copies(b, block, 2)
        for cp in copies:
            cp.start()
        for cp in copies:
            cp.wait()
        acc, carry = cache_block(2, list(acc), carry)
        return block - 1, alive(carry), tuple(acc), carry

    _, _, acc, _ = lax.while_loop(cond, body, (n_blocks - 2, alive(carry), tuple(acc), carry))
    for h in range(nh):
        o_ref[:, h * d:(h + 1) * d] = acc[h].astype(o_ref.dtype)


def _sb_sample(q16, k16, v16, cache_k, cache_v, *, t):
    h, m, d = q16.shape
    nb = m // t
    past = cache_k.shape[2]
    assert past % SB_BLOCK == 0
    new_spec = pl.BlockSpec((h, t, d), lambda b: (0, b, 0))
    cache_spec = pl.BlockSpec(memory_space=pl.ANY)
    return pl.pallas_call(
        functools.partial(_sb_sample_kernel, past=past),
        out_shape=jax.ShapeDtypeStruct((m, h * d), BF16),
        grid=(nb,),
        in_specs=[new_spec, new_spec, new_spec, cache_spec, cache_spec],
        out_specs=pl.BlockSpec((t, h * d), lambda b: (b, 0)),
        scratch_shapes=[pltpu.VMEM((3, h, SB_BLOCK, d), F32),
                        pltpu.VMEM((3, h, SB_BLOCK, d), F32),
                        pltpu.SemaphoreType.DMA((2, 3))],
        compiler_params=_cparams(("arbitrary",)),
        name="sb_sample",
    )(q16, k16, v16, cache_k, cache_v)


def _ret_log_decay(hd):
    return math.log(1.0 - 2.0 ** (-5.0 - hd))


def _ret_kernel(q_ref, k_ref, v_ref, g_ref, nw_ref, s0_ref, r_ref, st_ref,
                intra_ref, qdec_ref, kdec_ref, *, chunk):
    c = pl.program_id(1)

    @pl.when(jnp.logical_and(pl.program_id(0) == 0, c == 0))
    def _():
        row = lax.broadcasted_iota(jnp.int32, (chunk, chunk), 0)
        col = lax.broadcasted_iota(jnp.int32, (chunk, chunk), 1)
        diff = (row - col).astype(F32)
        pos = lax.broadcasted_iota(jnp.int32, (chunk, DK_RET), 0).astype(F32)
        for hd in range(H_RET):
            lg = _ret_log_decay(hd)
            intra_ref[hd] = jnp.where(row >= col, jnp.exp(lg * jnp.maximum(diff, 0.0)), 0.0)
            qdec_ref[hd] = jnp.exp(lg * (pos + 1.0))
            kdec_ref[hd] = jnp.exp(lg * (chunk - 1.0 - pos))

    @pl.when(c == 0)
    def _():
        st_ref[...] = s0_ref[...]

    for sub in range(q_ref.shape[1] // chunk):
        rows = slice(sub * chunk, (sub + 1) * chunk)
        for hd in range(H_RET):
            intra, q_dec, k_dec = intra_ref[hd], qdec_ref[hd], kdec_ref[hd]
            c_dec = math.exp(_ret_log_decay(hd) * chunk)
            def head(ref):
                return jnp.concatenate([ref[2 * hd, rows, :], ref[2 * hd + 1, rows, :]], axis=-1)

            q, k, v = head(q_ref), head(k_ref), head(v_ref)
            state = st_ref[hd]
            scores = (_dot_nt(q, k.astype(BF16)) * intra).astype(BF16)
            o = _dot(scores, v) + _dot(q, state.astype(BF16)) * q_dec
            st_ref[hd] = c_dec * state + _dot_tn((k * k_dec).astype(BF16), v)
            o = o * lax.rsqrt(jnp.mean(o * o, axis=-1, keepdims=True) + EPS)
            sl = slice(hd * DV_RET, (hd + 1) * DV_RET)
            gate = head(g_ref)
            r_ref[rows, sl] = (o * nw_ref[:, sl] * (gate * jax.nn.sigmoid(gate))).astype(r_ref.dtype)


def _retention(rq, rk, rv, gate, norm_w, state0, *, chunk, per_step):
    n_slices, m, width = rq.shape
    h = n_slices * width // DK_RET
    nb = state0.shape[0]
    rows = per_step * chunk
    nc = m // (nb * rows)
    qkv_spec = pl.BlockSpec((n_slices, rows, width), lambda b, c: (0, b * nc + c, 0))
    st_spec = pl.BlockSpec((None, h, DK_RET, DV_RET), lambda b, c: (b, 0, 0, 0))
    return pl.pallas_call(
        functools.partial(_ret_kernel, chunk=chunk),
        out_shape=(jax.ShapeDtypeStruct((m, h * DV_RET), BF16),
                   jax.ShapeDtypeStruct(state0.shape, F32)),
        grid=(nb, nc),
        in_specs=[qkv_spec, qkv_spec, qkv_spec, qkv_spec,
                  pl.BlockSpec((1, h * DV_RET), lambda b, c: (0, 0)),
                  st_spec],
        out_specs=(pl.BlockSpec((rows, h * DV_RET), lambda b, c: (b * nc + c, 0)), st_spec),
        scratch_shapes=[pltpu.VMEM((h, chunk, chunk), F32),
                        pltpu.VMEM((h, chunk, DK_RET), F32),
                        pltpu.VMEM((h, chunk, DK_RET), F32)],
        compiler_params=_cparams(("arbitrary", "arbitrary")),
        name="retention",
    )(rq, rk, rv, gate, norm_w, state0)


def _outproj_kernel(x_ref, a_ref, b_ref, wa_ref, wb_ref, o_ref):
    o_ref[...] = x_ref[...] + _dot(a_ref[...], wa_ref[...]) + _dot(b_ref[...], wb_ref[...])


def _outproj(x, a_sb, a_ret, w_out, *, tm):
    m = x.shape[0]
    row = lambda i: (i, 0)
    return pl.pallas_call(
        _outproj_kernel,
        out_shape=jax.ShapeDtypeStruct((m, D_MODEL), F32),
        grid=(m // tm,),
        in_specs=[
            pl.BlockSpec((tm, D_MODEL), row),
            pl.BlockSpec((tm, SB_W), row),
            pl.BlockSpec((tm, RET_W), row),
            pl.BlockSpec((SB_W, D_MODEL), lambda i: (0, 0)),
            pl.BlockSpec((RET_W, D_MODEL), lambda i: (1, 0)),
        ],
        out_specs=pl.BlockSpec((tm, D_MODEL), row),
        compiler_params=_cparams(("arbitrary",)),
        name="outproj",
    )(x, a_sb, a_ret, w_out, w_out)


OUTPROJ_TK = 512


def _outproj_f32w_kernel(x_ref, a_ref, w_ref, o_ref, w16_ref):
    @pl.when(pl.program_id(0) == 0)
    def _():
        o_ref[...] = x_ref[...]

    w16_ref[...] = w_ref[...].astype(BF16)
    o_ref[...] += _dot(a_ref[...], w16_ref[...])


def _outproj_f32w(x, a, w_out):
    m = x.shape[0]
    d_mix = w_out.shape[0]
    return pl.pallas_call(
        _outproj_f32w_kernel,
        out_shape=(jax.ShapeDtypeStruct((m, D_MODEL), F32),
                   jax.ShapeDtypeStruct(w_out.shape, BF16)),
        grid=(d_mix // OUTPROJ_TK,),
        in_specs=[
            pl.BlockSpec((m, D_MODEL), lambda k: (0, 0)),
            pl.BlockSpec((m, OUTPROJ_TK), lambda k: (0, k)),
            pl.BlockSpec((OUTPROJ_TK, D_MODEL), lambda k: (k, 0)),
        ],
        out_specs=(pl.BlockSpec((m, D_MODEL), lambda k: (0, 0)),
                   pl.BlockSpec((OUTPROJ_TK, D_MODEL), lambda k: (k, 0))),
        compiler_params=_cparams(("arbitrary",)),
        name="outproj_f32w",
    )(x, a, w_out)


def _mixers(x1, weights, sb_fn, state0, *, tm, proj_tm, pos_base, pos_mod, chunk,
            chunks_per_step):
    (nmix, w_in, qn, kn, ron, w_out, inv_freq) = weights
    f32w = w_in.dtype == F32
    proj = _inproj(x1, nmix, w_in, qn, kn, inv_freq, tm=proj_tm,
                   pos_base=pos_base, pos_mod=pos_mod)
    if f32w:
        proj, w_in = proj
    sq, sk, sk16, sv, sv16, rq, rk, rv, gate = proj
    a_sb, sb_side = sb_fn(sq, sk16, sv16)
    a_ret, state = _retention(rq, rk, rv, gate, ron, state0, chunk=chunk, per_step=chunks_per_step)
    if f32w:
        x2, w_out = _outproj_f32w(x1, jnp.concatenate([a_sb, a_ret], axis=1), w_out)
    else:
        x2 = _outproj(x1, a_sb, a_ret, w_out, tm=tm)
    return x2, sk, sv, state, (w_in, w_out), sb_side


def kernel(x_prompt, x_sample, cache_sb_k, cache_sb_v, state_ret, ffn1_norm, ffn1_w_gate, ffn1_w_up, ffn1_w_down, mix_norm, w_in, sb_q_norm, sb_k_norm, ret_out_norm, w_out, ffn2_norm, ffn2_w_gate, ffn2_w_up, ffn2_w_down, final_norm):
    depth = ffn1_norm.shape[0]
    assert depth == 1
    nb_p, seq, _ = x_prompt.shape
    nb_s, dec_seq, _ = x_sample.shape
    past = cache_sb_k.shape[3]
    assert nb_p == 1

    half = DK_RET // 2
    inv_freq = (ROPE_BASE ** (-jnp.arange(half, dtype=F32) / half)).reshape(1, half)
    l = 0
    n1, n2, nf = ffn1_norm[l][None], ffn2_norm[l][None], final_norm[l][None]

    def mixer_weights(w_in_, w_out_):
        return (mix_norm[l][None], w_in_, sb_q_norm[l][None], sb_k_norm[l][None],
                ret_out_norm[l][None], w_out_, inv_freq)

    ms = nb_s * dec_seq
    xs = x_sample.reshape(ms, D_MODEL)
    xp = x_prompt.reshape(seq, D_MODEL)
    x1s, ffn1_16 = _ffn(xs, n1, ffn1_w_gate[l], ffn1_w_up[l], ffn1_w_down[l], tm=ms)
    x1p = _ffn(xp, n1, *ffn1_16, tm=1024)

    def sb_s(sq, sk16, sv16):
        return _sb_sample(sq, sk16, sv16, cache_sb_k[l], cache_sb_v[l], t=dec_seq), ()

    x2s, sks, svs, sts, (w_in16, w_out16), _ = _mixers(
        x1s, mixer_weights(w_in[l], w_out[l]), sb_s, state_ret[l],
        tm=ms, proj_tm=ms, pos_base=past, pos_mod=dec_seq, chunk=dec_seq,
        chunks_per_step=1)

    sb_p = functools.partial(_sb_prompt, side=(ffn2_w_gate[l], ffn2_w_up[l], ffn2_w_down[l]))
    zero_state = jnp.zeros((1, H_RET, DK_RET, DV_RET), F32)
    x2p, skp, svp, stp, _, ffn2_16 = _mixers(
        x1p, mixer_weights(w_in16, w_out16), sb_p, zero_state,
        tm=512, proj_tm=1024, pos_base=0, pos_mod=seq, chunk=RET_CHUNK,
        chunks_per_step=4)

    ys = _ffn(x2s, n2, *ffn2_16, nf, tm=ms)
    yp = _ffn(x2p, n2, *ffn2_16, nf, tm=1024)

    def cache_layout(t):
        return t.reshape(H_SB, nb_s, dec_seq, D_SB).transpose(1, 0, 2, 3)[None]

    return (yp.reshape(1, seq, D_MODEL), ys.reshape(nb_s, dec_seq, D_MODEL),
            skp[None, None], svp[None, None], stp[None],
            cache_layout(sks), cache_layout(svs), sts[None])
```

```python
import functools
import math

import jax
import jax.numpy as jnp
from jax import lax
from jax.experimental import pallas as pl
from jax.experimental.pallas import tpu as pltpu

F32 = jnp.float32
BF16 = jnp.bfloat16

D_MODEL = 2048
D_FF = 5632
H_SB = 8
D_SB = 128
H_RET = 4
DK_RET = 256
DV_RET = 256
SB_W = H_SB * D_SB
RET_W = H_RET * DK_RET
N_SECTIONS = 7
SECTION_W = 1024
ROPE_BASE = 10000.0
EPS = 1e-6

V7X_VMEM_LIMIT_BYTES = 56 * 1024 * 1024
SB_BLOCK = 256
RET_CHUNK = 256
FFN_TF = 512
FFN_TF_SINGLE_TILE = 1408

LOG2E = 1.4426950408889634


def _cparams(sem):
    return pltpu.CompilerParams(dimension_semantics=sem,
                                vmem_limit_bytes=V7X_VMEM_LIMIT_BYTES)


def _rmsnorm_rows(x, w):
    ms = jnp.mean(x * x, axis=-1, keepdims=True)
    return x * lax.rsqrt(ms + EPS) * w


def _dot(a, b):
    return jnp.dot(a, b, preferred_element_type=F32)


def _dot_nt(a, b):
    return lax.dot_general(a, b, (((1,), (1,)), ((), ())), preferred_element_type=F32)


def _dot_tn(a, b):
    return lax.dot_general(a, b, (((0,), (0,)), ((), ())), preferred_element_type=F32)


def _ffn_kernel(*refs, n_steps, final_norm, emit_bf16):
    x_ref, nw_ref, wg_ref, wu_ref, wd_ref = refs[:5]
    refs = refs[5:]
    fw_ref = None
    if final_norm:
        fw_ref, refs = refs[0], refs[1:]
    o_ref, refs = refs[0], refs[1:]
    if emit_bf16:
        wg16_ref, wu16_ref, wd16_ref, h_ref = refs
        wg16_ref[...] = wg_ref[...].astype(BF16)
        wu16_ref[...] = wu_ref[...].astype(BF16)
        wd16_ref[...] = wd_ref[...].astype(BF16)
        wg_ref, wu_ref, wd_ref = wg16_ref, wu16_ref, wd16_ref
    else:
        (h_ref,) = refs
    j = pl.program_id(1)

    @pl.when(j == 0)
    def _():
        x = x_ref[...]
        h_ref[...] = _rmsnorm_rows(x, nw_ref[...]).astype(BF16)
        o_ref[...] = x

    h = h_ref[...]
    g = _dot(h, wg_ref[...])
    u = _dot(h, wu_ref[...])
    a = (g * jax.nn.sigmoid(g) * (0.5 * u)).astype(BF16)
    o_ref[...] += _dot(a, wd_ref[...])

    if final_norm:
        @pl.when(j == n_steps - 1)
        def _():
            o_ref[...] = _rmsnorm_rows(o_ref[...], fw_ref[...])


def _ffn(x, norm_w, wg, wu, wd, final_w=None, *, tm, tf=FFN_TF):
    m = x.shape[0]
    assert D_FF % tf == 0
    n_steps = D_FF // tf
    emit_bf16 = wg.dtype == F32
    assert not emit_bf16 or m == tm
    w_specs = [
        pl.BlockSpec((D_MODEL, tf), lambda i, j: (0, j)),
        pl.BlockSpec((D_MODEL, tf), lambda i, j: (0, j)),
        pl.BlockSpec((tf, D_MODEL), lambda i, j: (j, 0)),
    ]
    in_specs = [
        pl.BlockSpec((tm, D_MODEL), lambda i, j: (i, 0)),
        pl.BlockSpec((1, D_MODEL), lambda i, j: (0, 0)),
    ] + w_specs
    args = [x, norm_w, wg, wu, wd]
    if final_w is not None:
        in_specs.append(pl.BlockSpec((1, D_MODEL), lambda i, j: (0, 0)))
        args.append(final_w)
    out_shape = [jax.ShapeDtypeStruct((m, D_MODEL), F32)]
    out_specs = [pl.BlockSpec((tm, D_MODEL), lambda i, j: (i, 0))]
    if emit_bf16:
        out_shape += [jax.ShapeDtypeStruct(w.shape, BF16) for w in (wg, wu, wd)]
        out_specs += w_specs
    out = pl.pallas_call(
        functools.partial(_ffn_kernel, n_steps=n_steps, final_norm=final_w is not None,
                          emit_bf16=emit_bf16),
        out_shape=out_shape,
        grid=(m // tm, n_steps),
        in_specs=in_specs,
        out_specs=out_specs,
        scratch_shapes=[pltpu.VMEM((tm, D_MODEL), BF16)],
        compiler_params=_cparams(("arbitrary", "arbitrary")),
        name="ffn_final" if final_w is not None else "ffn",
    )(*args)
    return (out[0], tuple(out[1:])) if emit_bf16 else out[0]


INPROJ_SECTION_OUTPUTS = ((None, "sq"), ("sk", "sk16"), ("sv", "sv16"), (None, "rq"),
                          ("rk", None), (None, "rv"), ("gate", None))
INPROJ_OUTPUT_ORDER = ("sq", "sk", "sk16", "sv", "sv16", "rq", "rk", "rv", "gate")
N_SLICES = SECTION_W // D_SB


def _inproj_kernel(x_ref, nw_ref, w_ref, qn_ref, kn_ref, inv_ref, *rest,
                   tm, tile_stride, pos_base, pos_mod, emit_bf16):
    out = dict(zip(INPROJ_OUTPUT_ORDER, rest[:9]))
    rest = rest[9:]
    if emit_bf16:
        w16_ref, rest = rest[0], rest[1:]
        w16_ref[...] = w_ref[...].astype(BF16)
        w_ref = w16_ref
    h_ref, cos_ref, sin_ref, cos_row_ref, sin_row_ref, st32, st16, sem = rest
    stage = (st32, st16)
    i = pl.program_id(0)
    j = pl.program_id(1)
    step = i * N_SECTIONS + j
    slot = lax.rem(step, 2)

    def copies(section, tile, from_slot):
        rows = pl.ds(pl.multiple_of(tile * tm, tm), tm)
        return [pltpu.make_async_copy(stage[kind].at[from_slot], out[name].at[:, rows, :],
                                      sem.at[kind, from_slot])
                for kind, name in enumerate(INPROJ_SECTION_OUTPUTS[section]) if name is not None]

    @pl.when(jnp.logical_and(i == 0, j == 0))
    def _():
        row = lax.broadcasted_iota(jnp.int32, (tm, DK_RET // 2), 0)
        ang = lax.rem(row, pos_mod).astype(F32) * inv_ref[...]
        cos_row_ref[...] = jnp.cos(ang)
        sin_row_ref[...] = jnp.sin(ang)

    @pl.when(j == 0)
    def _():
        h_ref[...] = _rmsnorm_rows(x_ref[...], nw_ref[...]).astype(BF16)
        ang = (pos_base + i * tile_stride).astype(F32) * inv_ref[...]
        ca, sa = jnp.cos(ang), jnp.sin(ang)
        cb, sb = cos_row_ref[...], sin_row_ref[...]
        cos_ref[...] = ca * cb - sa * sb
        sin_ref[...] = sa * cb + ca * sb

    def epilogue(section, p, put32, put16):
        def piece(n):
            return p[:, n * D_SB:(n + 1) * D_SB]

        if section == 0:
            w = qn_ref[...] * (D_SB ** -0.5 * LOG2E)
            for n in range(N_SLICES):
                put16(n, _rmsnorm_rows(piece(n), w).astype(BF16))
        elif section == 1:
            for n in range(N_SLICES):
                k = _rmsnorm_rows(piece(n), kn_ref[...])
                put32(n, k)
                put16(n, k.astype(BF16))
        elif section == 2:
            for n in range(N_SLICES):
                put32(n, piece(n))
                put16(n, piece(n).astype(BF16))
        elif section in (3, 4):
            c, s = cos_ref[...], sin_ref[...]
            for hd in range(H_RET):
                x1, x2 = piece(2 * hd), piece(2 * hd + 1)
                r1, r2 = x1 * c - x2 * s, x1 * s + x2 * c
                if section == 3:
                    put16(2 * hd, r1.astype(BF16))
                    put16(2 * hd + 1, r2.astype(BF16))
                else:
                    put32(2 * hd, r1 * (DK_RET ** -0.5))
                    put32(2 * hd + 1, r2 * (DK_RET ** -0.5))
        elif section == 5:
            for n in range(N_SLICES):
                put16(n, piece(n).astype(BF16))
        else:
            for n in range(N_SLICES):
                put32(n, piece(n))

    last_step = pl.num_programs(0) * N_SECTIONS - 1
    for section in range(N_SECTIONS):
        @pl.when(j == section)
        def _(section=section):
            @pl.when(step >= 2)
            def _():
                for cp in copies((section - 2) % N_SECTIONS, i if section >= 2 else i - 1, slot):
                    cp.wait()

            def put32(n, value):
                st32[slot, n] = value

            def put16(n, value):
                st16[slot, n] = value

            epilogue(section, _dot(h_ref[...], w_ref[...]), put32, put16)
            for cp in copies(section, i, slot):
                cp.start()

            if section == N_SECTIONS - 1:
                @pl.when(step == last_step)
                def _():
                    for cp in copies(section - 1, i, 1 - slot) + copies(section, i, slot):
                        cp.wait()


def _inproj(x, norm_w, w_in, qn, kn, inv_freq, *, tm, pos_base, pos_mod):
    m = x.shape[0]
    emit_bf16 = w_in.dtype == F32
    assert not emit_bf16 or m == tm
    assert pos_mod >= m or tm % pos_mod == 0
    tile_stride = tm if pos_mod >= m else 0
    half = DK_RET // 2
    f32_outputs = {name for name, _ in INPROJ_SECTION_OUTPUTS if name is not None}
    out_shape = [jax.ShapeDtypeStruct((N_SLICES, m, D_SB), F32 if name in f32_outputs else BF16)
                 for name in INPROJ_OUTPUT_ORDER]
    out_specs = [pl.BlockSpec(memory_space=pl.ANY)] * len(out_shape)
    w_spec = pl.BlockSpec((D_MODEL, SECTION_W), lambda i, j: (0, j))
    if emit_bf16:
        out_shape.append(jax.ShapeDtypeStruct(w_in.shape, BF16))
        out_specs.append(w_spec)
    out = pl.pallas_call(
        functools.partial(_inproj_kernel, tm=tm, tile_stride=tile_stride,
                          pos_base=pos_base, pos_mod=pos_mod, emit_bf16=emit_bf16),
        out_shape=out_shape,
        grid=(m // tm, N_SECTIONS),
        in_specs=[
            pl.BlockSpec((tm, D_MODEL), lambda i, j: (i, 0)),
            pl.BlockSpec((1, D_MODEL), lambda i, j: (0, 0)),
            w_spec,
            pl.BlockSpec((1, D_SB), lambda i, j: (0, 0)),
            pl.BlockSpec((1, D_SB), lambda i, j: (0, 0)),
            pl.BlockSpec((1, half), lambda i, j: (0, 0)),
        ],
        out_specs=out_specs,
        scratch_shapes=[pltpu.VMEM((tm, D_MODEL), BF16)] + [pltpu.VMEM((tm, half), F32)] * 4
        + [pltpu.VMEM((2, N_SLICES, tm, D_SB), F32), pltpu.VMEM((2, N_SLICES, tm, D_SB), BF16),
           pltpu.SemaphoreType.DMA((2, 2))],
        compiler_params=_cparams(("arbitrary", "arbitrary")),
        name="inproj",
    )(x, norm_w, w_in, qn, kn, inv_freq)
    return (tuple(out[:9]), out[9]) if emit_bf16 else tuple(out)


def _softplus2(z):
    return jnp.maximum(z, 0.0) + jnp.log2(1.0 + jnp.exp2(-jnp.abs(z)))


def _suffix_matrix(n):
    r = lax.broadcasted_iota(jnp.int32, (2 * n, n), 0)
    c = lax.broadcasted_iota(jnp.int32, (2 * n, n), 1)
    return jnp.where(jnp.where(r >= n, r - n, r) >= c, 1.0, 0.0).astype(BF16)


def _sb_logits(q, k, mask):
    z = _dot_nt(q, k)
    sp = _softplus2(z)
    if mask is not None:
        sp = jnp.where(mask, sp, 0.0)
    hi = sp.astype(BF16)
    lo = (sp - hi.astype(F32)).astype(BF16)
    return z, jnp.concatenate([hi, lo], axis=1)


def _sb_suffix_sums(splits, tri):
    q = splits[0].shape[0]
    sums = _dot(jnp.concatenate(splits, axis=0), tri)
    return [sums[n * q:(n + 1) * q] for n in range(len(splits))]


def _sb_weights(z, incl, v, mask):
    a = jnp.exp2(z - incl)
    if mask is not None:
        a = jnp.where(mask, a, 0.0)
    return _dot(a.astype(BF16), v)


SB_DEAD_CARRY = 150.0


SB_HEADS_PER_STEP = 4
SB_LEFT_SUFFIX_GROUP = 2


def _sb_prompt_kernel(q_ref, k_ref, v_ref, *rest):
    n_side = len(rest) // 2
    o_ref = rest[n_side]
    for src, dst in zip(rest[:n_side], rest[n_side + 1:]):
        dst[...] = src[...].astype(BF16)
    qi = pl.program_id(1)
    heads, _, d = q_ref.shape
    blk = SB_BLOCK
    tri = _suffix_matrix(blk)
    r = lax.broadcasted_iota(jnp.int32, (blk, blk), 0)
    c = lax.broadcasted_iota(jnp.int32, (blk, blk), 1)
    strict = c < r
    has_prev = qi > 0

    def kv(g, block):
        s0 = pl.multiple_of(block * blk, blk)
        return k_ref[g, pl.ds(s0, blk), :], v_ref[g, pl.ds(s0, blk), :]

    qs = [q_ref[g] for g in range(heads)]
    diag = [kv(g, qi) for g in range(heads)]
    left = [kv(g, jnp.maximum(qi - 1, 0)) for g in range(heads)]

    def grouped_sums(splits, group):
        sums = []
        for n in range(0, len(splits), group):
            sums += _sb_suffix_sums(list(splits[n:n + group]), tri)
        return sums

    z_d, hl_d = zip(*[_sb_logits(qs[g], diag[g][0], strict) for g in range(heads)])
    sums_d = grouped_sums(hl_d, heads)
    z_l, hl_l = zip(*[_sb_logits(qs[g], left[g][0], None) for g in range(heads)])
    sums_l = grouped_sums(hl_l, SB_LEFT_SUFFIX_GROUP)
    acc_d = [_sb_weights(z_d[g], sums_d[g], diag[g][1], strict) for g in range(heads)]
    accs, carries = [], []
    for g in range(heads):
        carry_d = sums_d[g][:, 0:1]
        incl_l = sums_l[g] + carry_d
        d_l = _sb_weights(z_l[g], incl_l, left[g][1], None)
        accs.append(acc_d[g] + jnp.where(has_prev, d_l, 0.0))
        carries.append(jnp.where(has_prev, incl_l[:, 0:1], carry_d))
    accs, carries = tuple(accs), tuple(carries)

    def alive(carries):
        low = functools.reduce(jnp.minimum, carries)
        return (jnp.min(low) < SB_DEAD_CARRY).astype(jnp.int32)

    def cond(state):
        t, live, _, _ = state
        return jnp.logical_and(t >= 0, live > 0)

    def body(state):
        t, _, accs, carries = state
        blocks = [kv(g, t) for g in range(heads)]
        zs, hls = zip(*[_sb_logits(qs[g], blocks[g][0], None) for g in range(heads)])
        sums = _sb_suffix_sums(list(hls), tri)
        new_accs, new_carries = [], []
        for g in range(heads):
            incl = sums[g] + carries[g]
            new_accs.append(accs[g] + _sb_weights(zs[g], incl, blocks[g][1], None))
            new_carries.append(incl[:, 0:1])
        return t - 1, alive(new_carries), tuple(new_accs), tuple(new_carries)

    _, _, accs, _ = lax.while_loop(cond, body, (qi - 2, alive(carries), accs, carries))
    for g in range(heads):
        o_ref[:, g * d:(g + 1) * d] = accs[g].astype(o_ref.dtype)


def _sb_prompt(q16, k16, v16, side=()):
    h, s, d = q16.shape
    tq = SB_BLOCK
    g = SB_HEADS_PER_STEP
    n_groups, n_blocks = h // g, s // tq

    def side_spec(w):
        rows, cols = w.shape
        assert rows % (16 * n_blocks) == 0 and cols % (128 * n_groups) == 0
        return pl.BlockSpec((rows // n_blocks, cols // n_groups), lambda hg, qi: (qi, hg))

    side_specs = [side_spec(w) for w in side]
    out = pl.pallas_call(
        _sb_prompt_kernel,
        out_shape=[jax.ShapeDtypeStruct((s, h * d), BF16)]
        + [jax.ShapeDtypeStruct(w.shape, BF16) for w in side],
        grid=(n_groups, n_blocks),
        in_specs=[
            pl.BlockSpec((g, tq, d), lambda hg, qi: (hg, qi, 0)),
            pl.BlockSpec((g, s, d), lambda hg, qi: (hg, 0, 0)),
            pl.BlockSpec((g, s, d), lambda hg, qi: (hg, 0, 0)),
        ] + side_specs,
        out_specs=[pl.BlockSpec((tq, g * d), lambda hg, qi: (qi, hg))] + side_specs,
        compiler_params=_cparams(("arbitrary", "arbitrary")),
        name="sb_prompt",
    )(q16, k16, v16, *side)
    return out[0], tuple(out[1:])


def _sb_sample_kernel(q_ref, kn_ref, vn_ref, kc_hbm, vc_hbm, o_ref, kbuf, vbuf, sem, *, past):
    nh, t, d = q_ref.shape
    ht = nh * t
    blk = SB_BLOCK
    n_blocks = past // blk
    b = pl.program_id(0)
    slot = lax.rem(b, 2)

    def cache_copies(batch, block, to_slot):
        rows = pl.ds(pl.multiple_of(block * blk, blk), blk)
        return (pltpu.make_async_copy(kc_hbm.at[batch, :, rows, :], kbuf.at[to_slot], sem.at[0, to_slot]),
                pltpu.make_async_copy(vc_hbm.at[batch, :, rows, :], vbuf.at[to_slot], sem.at[1, to_slot]))

    @pl.when(b == 0)
    def _():
        for cp in cache_copies(0, n_blocks - 1, 0):
            cp.start()

    @pl.when(b + 1 < pl.num_programs(0))
    def _():
        for cp in cache_copies(b + 1, n_blocks - 1, 1 - slot):
            cp.start()

    q_all = q_ref[...].reshape(ht, d).astype(F32)
    q_head = lax.broadcasted_iota(jnp.int32, (ht, d), 0) // t
    q_masked = [jnp.where(q_head == h, q_all, 0.0).astype(BF16) for h in range(nh)]

    def logits(keys_of_head):
        z = _dot_nt(keys_of_head(0), q_masked[0])
        for h in range(1, nh):
            z = z + _dot_nt(keys_of_head(h), q_masked[h])
        return z

    def suffix_matrix(n):
        r = lax.broadcasted_iota(jnp.int32, (n, 2 * n), 0)
        c = lax.broadcasted_iota(jnp.int32, (n, 2 * n), 1)
        return jnp.where(jnp.where(c >= n, c - n, c) >= r, 1.0, 0.0).astype(BF16)

    def suffix_sum(sp, lmat):
        hi = sp.astype(BF16)
        lo = (sp - hi.astype(F32)).astype(BF16)
        return _dot(lmat, jnp.concatenate([hi, lo], axis=0))

    def emit(a_t, values_of_head, acc):
        a = a_t.T.astype(BF16)
        return [acc[h] + _dot(a[h * t:(h + 1) * t, :], values_of_head(h)) for h in range(nh)]

    z = logits(lambda h: kn_ref[h])
    s_idx = lax.broadcasted_iota(jnp.int32, (t, ht), 0)
    t_idx = lax.broadcasted_iota(jnp.int32, (t, ht), 1) % t
    strict = s_idx < t_idx
    sp = jnp.where(strict, _softplus2(z), 0.0)
    incl = suffix_sum(sp, suffix_matrix(t))
    a_t = jnp.where(strict, jnp.exp2(z - incl), 0.0)
    acc = emit(a_t, lambda h: vn_ref[h], [jnp.zeros((t, d), F32) for _ in range(nh)])
    carry = incl[0:1, :]

    lmat = suffix_matrix(blk)

    def cache_block(from_slot, acc, carry):
        z = logits(lambda h: kbuf[from_slot, h].astype(BF16))
        incl = suffix_sum(_softplus2(z), lmat) + carry
        acc = emit(jnp.exp2(z - incl), lambda h: vbuf[from_slot, h].astype(BF16), acc)
        return acc, incl[0:1, :]

    def alive(carry):
        return (jnp.min(carry) < SB_DEAD_CARRY).astype(jnp.int32)

    for cp in cache_copies(b, n_blocks - 1, slot):
        cp.wait()
    acc, carry = cache_block(slot, acc, carry)

    def cond(state):
        block, live, _, _ = state
        return jnp.logical_and(block >= 0, live > 0)

    def body(state):
        block, _, acc, carry = state
        copies = cache_copies(b, block, 2)
        for cp in copies:
            cp.start()
        for cp in copies:
            cp.wait()
        acc, carry = cache_block(2, list(acc), carry)
        return block - 1, alive(carry), tuple(acc), carry

    _, _, acc, _ = lax.while_loop(cond, body, (n_blocks - 2, alive(carry), tuple(acc), carry))
    for h in range(nh):
        o_ref[:, h * d:(h + 1) * d] = acc[h].astype(o_ref.dtype)


def _sb_sample(q16, k16, v16, cache_k, cache_v, *, t):
    h, m, d = q16.shape
    nb = m // t
    past = cache_k.shape[2]
    assert past % SB_BLOCK == 0
    new_spec = pl.BlockSpec((h, t, d), lambda b: (0, b, 0))
    cache_spec = pl.BlockSpec(memory_space=pl.ANY)
    return pl.pallas_call(
        functools.partial(_sb_sample_kernel, past=past),
        out_shape=jax.ShapeDtypeStruct((m, h * d), BF16),
        grid=(nb,),
        in_specs=[new_spec, new_spec, new_spec, cache_spec, cache_spec],
        out_specs=pl.BlockSpec((t, h * d), lambda b: (b, 0)),
        scratch_shapes=[pltpu.VMEM((3, h, SB_BLOCK, d), F32),
                        pltpu.VMEM((3, h, SB_BLOCK, d), F32),
                        pltpu.SemaphoreType.DMA((2, 3))],
        compiler_params=_cparams(("arbitrary",)),
        name="sb_sample",
    )(q16, k16, v16, cache_k, cache_v)


def _ret_log_decay(hd):
    return math.log(1.0 - 2.0 ** (-5.0 - hd))


def _ret_kernel(q_ref, k_ref, v_ref, g_ref, nw_ref, s0_ref, r_ref, st_ref,
                intra_ref, qdec_ref, kdec_ref, *, chunk):
    c = pl.program_id(1)

    @pl.when(jnp.logical_and(pl.program_id(0) == 0, c == 0))
    def _():
        row = lax.broadcasted_iota(jnp.int32, (chunk, chunk), 0)
        col = lax.broadcasted_iota(jnp.int32, (chunk, chunk), 1)
        diff = (row - col).astype(F32)
        pos = lax.broadcasted_iota(jnp.int32, (chunk, DK_RET), 0).astype(F32)
        for hd in range(H_RET):
            lg = _ret_log_decay(hd)
            intra_ref[hd] = jnp.where(row >= col, jnp.exp(lg * jnp.maximum(diff, 0.0)), 0.0)
            qdec_ref[hd] = jnp.exp(lg * (pos + 1.0))
            kdec_ref[hd] = jnp.exp(lg * (chunk - 1.0 - pos))

    @pl.when(c == 0)
    def _():
        st_ref[...] = s0_ref[...]

    for sub in range(q_ref.shape[1] // chunk):
        rows = slice(sub * chunk, (sub + 1) * chunk)
        for hd in range(H_RET):
            intra, q_dec, k_dec = intra_ref[hd], qdec_ref[hd], kdec_ref[hd]
            c_dec = math.exp(_ret_log_decay(hd) * chunk)
            def head(ref):
                return jnp.concatenate([ref[2 * hd, rows, :], ref[2 * hd + 1, rows, :]], axis=-1)

            q, k, v = head(q_ref), head(k_ref), head(v_ref)
            state = st_ref[hd]
            scores = (_dot_nt(q, k.astype(BF16)) * intra).astype(BF16)
            o = _dot(scores, v) + _dot(q, state.astype(BF16)) * q_dec
            st_ref[hd] = c_dec * state + _dot_tn((k * k_dec).astype(BF16), v)
            o = o * lax.rsqrt(jnp.mean(o * o, axis=-1, keepdims=True) + EPS)
            sl = slice(hd * DV_RET, (hd + 1) * DV_RET)
            gate = head(g_ref)
            r_ref[rows, sl] = (o * nw_ref[:, sl] * (gate * jax.nn.sigmoid(gate))).astype(r_ref.dtype)


def _retention(rq, rk, rv, gate, norm_w, state0, *, chunk, per_step):
    n_slices, m, width = rq.shape
    h = n_slices * width // DK_RET
    nb = state0.shape[0]
    rows = per_step * chunk
    nc = m // (nb * rows)
    qkv_spec = pl.BlockSpec((n_slices, rows, width), lambda b, c: (0, b * nc + c, 0))
    st_spec = pl.BlockSpec((None, h, DK_RET, DV_RET), lambda b, c: (b, 0, 0, 0))
    return pl.pallas_call(
        functools.partial(_ret_kernel, chunk=chunk),
        out_shape=(jax.ShapeDtypeStruct((m, h * DV_RET), BF16),
                   jax.ShapeDtypeStruct(state0.shape, F32)),
        grid=(nb, nc),
        in_specs=[qkv_spec, qkv_spec, qkv_spec, qkv_spec,
                  pl.BlockSpec((1, h * DV_RET), lambda b, c: (0, 0)),
                  st_spec],
        out_specs=(pl.BlockSpec((rows, h * DV_RET), lambda b, c: (b * nc + c, 0)), st_spec),
        scratch_shapes=[pltpu.VMEM((h, chunk, chunk), F32),
                        pltpu.VMEM((h, chunk, DK_RET), F32),
                        pltpu.VMEM((h, chunk, DK_RET), F32)],
        compiler_params=_cparams(("arbitrary", "arbitrary")),
        name="retention",
    )(rq, rk, rv, gate, norm_w, state0)


def _outproj_kernel(x_ref, a_ref, b_ref, wa_ref, wb_ref, o_ref):
    o_ref[...] = x_ref[...] + _dot(a_ref[...], wa_ref[...]) + _dot(b_ref[...], wb_ref[...])


def _outproj(x, a_sb, a_ret, w_out, *, tm):
    m = x.shape[0]
    row = lambda i: (i, 0)
    return pl.pallas_call(
        _outproj_kernel,
        out_shape=jax.ShapeDtypeStruct((m, D_MODEL), F32),
        grid=(m // tm,),
        in_specs=[
            pl.BlockSpec((tm, D_MODEL), row),
            pl.BlockSpec((tm, SB_W), row),
            pl.BlockSpec((tm, RET_W), row),
            pl.BlockSpec((SB_W, D_MODEL), lambda i: (0, 0)),
            pl.BlockSpec((RET_W, D_MODEL), lambda i: (1, 0)),
        ],
        out_specs=pl.BlockSpec((tm, D_MODEL), row),
        compiler_params=_cparams(("arbitrary",)),
        name="outproj",
    )(x, a_sb, a_ret, w_out, w_out)


OUTPROJ_TK = 512


def _outproj_f32w_kernel(x_ref, a_ref, w_ref, o_ref, w16_ref):
    @pl.when(pl.program_id(0) == 0)
    def _():
        o_ref[...] = x_ref[...]

    w16_ref[...] = w_ref[...].astype(BF16)
    o_ref[...] += _dot(a_ref[...], w16_ref[...])


def _outproj_f32w(x, a, w_out):
    m = x.shape[0]
    d_mix = w_out.shape[0]
    return pl.pallas_call(
        _outproj_f32w_kernel,
        out_shape=(jax.ShapeDtypeStruct((m, D_MODEL), F32),
                   jax.ShapeDtypeStruct(w_out.shape, BF16)),
        grid=(d_mix // OUTPROJ_TK,),
        in_specs=[
            pl.BlockSpec((m, D_MODEL), lambda k: (0, 0)),
            pl.BlockSpec((m, OUTPROJ_TK), lambda k: (0, k)),
            pl.BlockSpec((OUTPROJ_TK, D_MODEL), lambda k: (k, 0)),
        ],
        out_specs=(pl.BlockSpec((m, D_MODEL), lambda k: (0, 0)),
                   pl.BlockSpec((OUTPROJ_TK, D_MODEL), lambda k: (k, 0))),
        compiler_params=_cparams(("arbitrary",)),
        name="outproj_f32w",
    )(x, a, w_out)


def _mixers(x1, weights, sb_fn, state0, *, tm, proj_tm, pos_base, pos_mod, chunk,
            chunks_per_step):
    (nmix, w_in, qn, kn, ron, w_out, inv_freq) = weights
    f32w = w_in.dtype == F32
    proj = _inproj(x1, nmix, w_in, qn, kn, inv_freq, tm=proj_tm,
                   pos_base=pos_base, pos_mod=pos_mod)
    if f32w:
        proj, w_in = proj
    sq, sk, sk16, sv, sv16, rq, rk, rv, gate = proj
    a_sb, sb_side = sb_fn(sq, sk16, sv16)
    a_ret, state = _retention(rq, rk, rv, gate, ron, state0, chunk=chunk, per_step=chunks_per_step)
    if f32w:
        x2, w_out = _outproj_f32w(x1, jnp.concatenate([a_sb, a_ret], axis=1), w_out)
    else:
        x2 = _outproj(x1, a_sb, a_ret, w_out, tm=tm)
    return x2, sk, sv, state, (w_in, w_out), sb_side


def kernel(x_prompt, x_sample, cache_sb_k, cache_sb_v, state_ret, ffn1_norm, ffn1_w_gate, ffn1_w_up, ffn1_w_down, mix_norm, w_in, sb_q_norm, sb_k_norm, ret_out_norm, w_out, ffn2_norm, ffn2_w_gate, ffn2_w_up, ffn2_w_down, final_norm):
    depth = ffn1_norm.shape[0]
    assert depth == 1
    nb_p, seq, _ = x_prompt.shape
    nb_s, dec_seq, _ = x_sample.shape
    past = cache_sb_k.shape[3]
    assert nb_p == 1

    half = DK_RET // 2
    inv_freq = (ROPE_BASE ** (-jnp.arange(half, dtype=F32) / half)).reshape(1, half)
    l = 0
    n1, n2, nf = ffn1_norm[l][None], ffn2_norm[l][None], final_norm[l][None]

    def mixer_weights(w_in_, w_out_):
        return (mix_norm[l][None], w_in_, sb_q_norm[l][None], sb_k_norm[l][None],
                ret_out_norm[l][None], w_out_, inv_freq)

    ms = nb_s * dec_seq
    xs = x_sample.reshape(ms, D_MODEL)
    xp = x_prompt.reshape(seq, D_MODEL)
    x1s, ffn1_16 = _ffn(xs, n1, ffn1_w_gate[l], ffn1_w_up[l], ffn1_w_down[l], tm=ms)
    x1p = _ffn(xp, n1, *ffn1_16, tm=1024)

    def sb_s(sq, sk16, sv16):
        return _sb_sample(sq, sk16, sv16, cache_sb_k[l], cache_sb_v[l], t=dec_seq), ()

    x2s, sks, svs, sts, (w_in16, w_out16), _ = _mixers(
        x1s, mixer_weights(w_in[l], w_out[l]), sb_s, state_ret[l],
        tm=ms, proj_tm=ms, pos_base=past, pos_mod=dec_seq, chunk=dec_seq,
        chunks_per_step=1)

    sb_p = functools.partial(_sb_prompt, side=(ffn2_w_gate[l], ffn2_w_up[l], ffn2_w_down[l]))
    zero_state = jnp.zeros((1, H_RET, DK_RET, DV_RET), F32)
    x2p, skp, svp, stp, _, ffn2_16 = _mixers(
        x1p, mixer_weights(w_in16, w_out16), sb_p, zero_state,
        tm=512, proj_tm=1024, pos_base=0, pos_mod=seq, chunk=RET_CHUNK,
        chunks_per_step=4)

    ys = _ffn(x2s, n2, *ffn2_16, nf, tm=ms, tf=FFN_TF_SINGLE_TILE)
    yp = _ffn(x2p, n2, *ffn2_16, nf, tm=1024)

    def cache_layout(t):
        return t.reshape(H_SB, nb_s, dec_seq, D_SB).transpose(1, 0, 2, 3)[None]

    return (yp.reshape(1, seq, D_MODEL), ys.reshape(nb_s, dec_seq, D_MODEL),
            skp[None, None], svp[None, None], stp[None],
            cache_layout(sks), cache_layout(svs), sts[None])
```

```python
import functools
import math

import jax
import jax.numpy as jnp
from jax import lax
from jax.experimental import pallas as pl
from jax.experimental.pallas import tpu as pltpu

F32 = jnp.float32
BF16 = jnp.bfloat16

D_MODEL = 2048
D_FF = 5632
H_SB = 8
D_SB = 128
H_RET = 4
DK_RET = 256
DV_RET = 256
SB_W = H_SB * D_SB
RET_W = H_RET * DK_RET
N_SECTIONS = 7
SECTION_W = 1024
ROPE_BASE = 10000.0
EPS = 1e-6

V7X_VMEM_LIMIT_BYTES = 56 * 1024 * 1024
SB_BLOCK = 256
RET_CHUNK = 256
FFN_TF = 512

LOG2E = 1.4426950408889634


def _cparams(sem):
    return pltpu.CompilerParams(dimension_semantics=sem,
                                vmem_limit_bytes=V7X_VMEM_LIMIT_BYTES)


def _rmsnorm_rows(x, w):
    ms = jnp.mean(x * x, axis=-1, keepdims=True)
    return x * lax.rsqrt(ms + EPS) * w


def _dot(a, b):
    return jnp.dot(a, b, preferred_element_type=F32)


def _dot_nt(a, b):
    return lax.dot_general(a, b, (((1,), (1,)), ((), ())), preferred_element_type=F32)


def _dot_tn(a, b):
    return lax.dot_general(a, b, (((0,), (0,)), ((), ())), preferred_element_type=F32)


def _ffn_kernel(*refs, n_steps, final_norm, emit_bf16):
    x_ref, nw_ref, wg_ref, wu_ref, wd_ref = refs[:5]
    refs = refs[5:]
    fw_ref = None
    if final_norm:
        fw_ref, refs = refs[0], refs[1:]
    o_ref, refs = refs[0], refs[1:]
    if emit_bf16:
        wg16_ref, wu16_ref, wd16_ref, h_ref = refs
        wg16_ref[...] = wg_ref[...].astype(BF16)
        wu16_ref[...] = wu_ref[...].astype(BF16)
        wd16_ref[...] = wd_ref[...].astype(BF16)
        wg_ref, wu_ref, wd_ref = wg16_ref, wu16_ref, wd16_ref
    else:
        (h_ref,) = refs
    j = pl.program_id(1)

    @pl.when(j == 0)
    def _():
        x = x_ref[...]
        h_ref[...] = _rmsnorm_rows(x, nw_ref[...]).astype(BF16)
        o_ref[...] = x

    h = h_ref[...]
    g = _dot(h, wg_ref[...])
    u = _dot(h, wu_ref[...])
    a = (g * jax.nn.sigmoid(g) * (0.5 * u)).astype(BF16)
    o_ref[...] += _dot(a, wd_ref[...])

    if final_norm:
        @pl.when(j == n_steps - 1)
        def _():
            o_ref[...] = _rmsnorm_rows(o_ref[...], fw_ref[...])


def _ffn(x, norm_w, wg, wu, wd, final_w=None, *, tm):
    m = x.shape[0]
    n_steps = D_FF // FFN_TF
    emit_bf16 = wg.dtype == F32
    assert not emit_bf16 or m == tm
    w_specs = [
        pl.BlockSpec((D_MODEL, FFN_TF), lambda i, j: (0, j)),
        pl.BlockSpec((D_MODEL, FFN_TF), lambda i, j: (0, j)),
        pl.BlockSpec((FFN_TF, D_MODEL), lambda i, j: (j, 0)),
    ]
    in_specs = [
        pl.BlockSpec((tm, D_MODEL), lambda i, j: (i, 0)),
        pl.BlockSpec((1, D_MODEL), lambda i, j: (0, 0)),
    ] + w_specs
    args = [x, norm_w, wg, wu, wd]
    if final_w is not None:
        in_specs.append(pl.BlockSpec((1, D_MODEL), lambda i, j: (0, 0)))
        args.append(final_w)
    out_shape = [jax.ShapeDtypeStruct((m, D_MODEL), F32)]
    out_specs = [pl.BlockSpec((tm, D_MODEL), lambda i, j: (i, 0))]
    if emit_bf16:
        out_shape += [jax.ShapeDtypeStruct(w.shape, BF16) for w in (wg, wu, wd)]
        out_specs += w_specs
    out = pl.pallas_call(
        functools.partial(_ffn_kernel, n_steps=n_steps, final_norm=final_w is not None,
                          emit_bf16=emit_bf16),
        out_shape=out_shape,
        grid=(m // tm, n_steps),
        in_specs=in_specs,
        out_specs=out_specs,
        scratch_shapes=[pltpu.VMEM((tm, D_MODEL), BF16)],
        compiler_params=_cparams(("arbitrary", "arbitrary")),
        name="ffn_final" if final_w is not None else "ffn",
    )(*args)
    return (out[0], tuple(out[1:])) if emit_bf16 else out[0]


INPROJ_SECTION_OUTPUTS = ((None, "sq"), ("sk", "sk16"), ("sv", "sv16"), (None, "rq"),
                          ("rk", None), (None, "rv"), ("gate", None))
INPROJ_OUTPUT_ORDER = ("sq", "sk", "sk16", "sv", "sv16", "rq", "rk", "rv", "gate")
N_SLICES = SECTION_W // D_SB


def _inproj_kernel(x_ref, nw_ref, w_ref, qn_ref, kn_ref, inv_ref, *rest,
                   tm, tile_stride, pos_base, pos_mod, emit_bf16):
    out = dict(zip(INPROJ_OUTPUT_ORDER, rest[:9]))
    rest = rest[9:]
    if emit_bf16:
        w16_ref, rest = rest[0], rest[1:]
        w16_ref[...] = w_ref[...].astype(BF16)
        w_ref = w16_ref
    h_ref, cos_ref, sin_ref, cos_row_ref, sin_row_ref, st32, st16, sem = rest
    stage = (st32, st16)
    i = pl.program_id(0)
    j = pl.program_id(1)
    step = i * N_SECTIONS + j
    slot = lax.rem(step, 2)

    def copies(section, tile, from_slot):
        rows = pl.ds(pl.multiple_of(tile * tm, tm), tm)
        return [pltpu.make_async_copy(stage[kind].at[from_slot], out[name].at[:, rows, :],
                                      sem.at[kind, from_slot])
                for kind, name in enumerate(INPROJ_SECTION_OUTPUTS[section]) if name is not None]

    @pl.when(jnp.logical_and(i == 0, j == 0))
    def _():
        row = lax.broadcasted_iota(jnp.int32, (tm, DK_RET // 2), 0)
        ang = lax.rem(row, pos_mod).astype(F32) * inv_ref[...]
        cos_row_ref[...] = jnp.cos(ang)
        sin_row_ref[...] = jnp.sin(ang)

    @pl.when(j == 0)
    def _():
        h_ref[...] = _rmsnorm_rows(x_ref[...], nw_ref[...]).astype(BF16)
        ang = (pos_base + i * tile_stride).astype(F32) * inv_ref[...]
        ca, sa = jnp.cos(ang), jnp.sin(ang)
        cb, sb = cos_row_ref[...], sin_row_ref[...]
        cos_ref[...] = ca * cb - sa * sb
        sin_ref[...] = sa * cb + ca * sb

    def epilogue(section, p, put32, put16):
        def piece(n):
            return p[:, n * D_SB:(n + 1) * D_SB]

        if section == 0:
            w = qn_ref[...] * (D_SB ** -0.5 * LOG2E)
            for n in range(N_SLICES):
                put16(n, _rmsnorm_rows(piece(n), w).astype(BF16))
        elif section == 1:
            for n in range(N_SLICES):
                k = _rmsnorm_rows(piece(n), kn_ref[...])
                put32(n, k)
                put16(n, k.astype(BF16))
        elif section == 2:
            for n in range(N_SLICES):
                put32(n, piece(n))
                put16(n, piece(n).astype(BF16))
        elif section in (3, 4):
            c, s = cos_ref[...], sin_ref[...]
            for hd in range(H_RET):
                x1, x2 = piece(2 * hd), piece(2 * hd + 1)
                r1, r2 = x1 * c - x2 * s, x1 * s + x2 * c
                if section == 3:
                    put16(2 * hd, r1.astype(BF16))
                    put16(2 * hd + 1, r2.astype(BF16))
                else:
                    put32(2 * hd, r1 * (DK_RET ** -0.5))
                    put32(2 * hd + 1, r2 * (DK_RET ** -0.5))
        elif section == 5:
            for n in range(N_SLICES):
                put16(n, piece(n).astype(BF16))
        else:
            for n in range(N_SLICES):
                put32(n, piece(n))

    last_step = pl.num_programs(0) * N_SECTIONS - 1
    for section in range(N_SECTIONS):
        @pl.when(j == section)
        def _(section=section):
            @pl.when(step >= 2)
            def _():
                for cp in copies((section - 2) % N_SECTIONS, i if section >= 2 else i - 1, slot):
                    cp.wait()

            def put32(n, value):
                st32[slot, n] = value

            def put16(n, value):
                st16[slot, n] = value

            epilogue(section, _dot(h_ref[...], w_ref[...]), put32, put16)
            for cp in copies(section, i, slot):
                cp.start()

            if section == N_SECTIONS - 1:
                @pl.when(step == last_step)
                def _():
                    for cp in copies(section - 1, i, 1 - slot) + copies(section, i, slot):
                        cp.wait()


def _inproj(x, norm_w, w_in, qn, kn, inv_freq, *, tm, pos_base, pos_mod):
    m = x.shape[0]
    emit_bf16 = w_in.dtype == F32
    assert not emit_bf16 or m == tm
    assert pos_mod >= m or tm % pos_mod == 0
    tile_stride = tm if pos_mod >= m else 0
    half = DK_RET // 2
    f32_outputs = {name for name, _ in INPROJ_SECTION_OUTPUTS if name is not None}
    out_shape = [jax.ShapeDtypeStruct((N_SLICES, m, D_SB), F32 if name in f32_outputs else BF16)
                 for name in INPROJ_OUTPUT_ORDER]
    out_specs = [pl.BlockSpec(memory_space=pl.ANY)] * len(out_shape)
    w_spec = pl.BlockSpec((D_MODEL, SECTION_W), lambda i, j: (0, j))
    if emit_bf16:
        out_shape.append(jax.ShapeDtypeStruct(w_in.shape, BF16))
        out_specs.append(w_spec)
    out = pl.pallas_call(
        functools.partial(_inproj_kernel, tm=tm, tile_stride=tile_stride,
                          pos_base=pos_base, pos_mod=pos_mod, emit_bf16=emit_bf16),
        out_shape=out_shape,
        grid=(m // tm, N_SECTIONS),
        in_specs=[
            pl.BlockSpec((tm, D_MODEL), lambda i, j: (i, 0)),
            pl.BlockSpec((1, D_MODEL), lambda i, j: (0, 0)),
            w_spec,
            pl.BlockSpec((1, D_SB), lambda i, j: (0, 0)),
            pl.BlockSpec((1, D_SB), lambda i, j: (0, 0)),
            pl.BlockSpec((1, half), lambda i, j: (0, 0)),
        ],
        out_specs=out_specs,
        scratch_shapes=[pltpu.VMEM((tm, D_MODEL), BF16)] + [pltpu.VMEM((tm, half), F32)] * 4
        + [pltpu.VMEM((2, N_SLICES, tm, D_SB), F32), pltpu.VMEM((2, N_SLICES, tm, D_SB), BF16),
           pltpu.SemaphoreType.DMA((2, 2))],
        compiler_params=_cparams(("arbitrary", "arbitrary")),
        name="inproj",
    )(x, norm_w, w_in, qn, kn, inv_freq)
    return (tuple(out[:9]), out[9]) if emit_bf16 else tuple(out)


def _softplus2(z):
    return jnp.maximum(z, 0.0) + jnp.log2(1.0 + jnp.exp2(-jnp.abs(z)))


def _suffix_matrix(n):
    r = lax.broadcasted_iota(jnp.int32, (2 * n, n), 0)
    c = lax.broadcasted_iota(jnp.int32, (2 * n, n), 1)
    return jnp.where(jnp.where(r >= n, r - n, r) >= c, 1.0, 0.0).astype(BF16)


def _sb_logits(q, k, mask):
    z = _dot_nt(q, k)
    sp = _softplus2(z)
    if mask is not None:
        sp = jnp.where(mask, sp, 0.0)
    hi = sp.astype(BF16)
    lo = (sp - hi.astype(F32)).astype(BF16)
    return z, jnp.concatenate([hi, lo], axis=1)


def _sb_suffix_sums(splits, tri):
    q = splits[0].shape[0]
    sums = _dot(jnp.concatenate(splits, axis=0), tri)
    return [sums[n * q:(n + 1) * q] for n in range(len(splits))]


def _sb_weights(z, incl, v, mask):
    a = jnp.exp2(z - incl)
    if mask is not None:
        a = jnp.where(mask, a, 0.0)
    return _dot(a.astype(BF16), v)


SB_DEAD_CARRY = 150.0


SB_HEADS_PER_STEP = 4
SB_LEFT_SUFFIX_GROUP = 2


def _sb_prompt_kernel(q_ref, k_ref, v_ref, *rest):
    n_side = len(rest) // 2
    o_ref = rest[n_side]
    for src, dst in zip(rest[:n_side], rest[n_side + 1:]):
        dst[...] = src[...].astype(BF16)
    qi = pl.program_id(1)
    heads, _, d = q_ref.shape
    blk = SB_BLOCK
    tri = _suffix_matrix(blk)
    r = lax.broadcasted_iota(jnp.int32, (blk, blk), 0)
    c = lax.broadcasted_iota(jnp.int32, (blk, blk), 1)
    strict = c < r
    has_prev = qi > 0

    def kv(g, block):
        s0 = pl.multiple_of(block * blk, blk)
        return k_ref[g, pl.ds(s0, blk), :], v_ref[g, pl.ds(s0, blk), :]

    qs = [q_ref[g] for g in range(heads)]
    diag = [kv(g, qi) for g in range(heads)]
    left = [kv(g, jnp.maximum(qi - 1, 0)) for g in range(heads)]

    def grouped_sums(splits, group):
        sums = []
        for n in range(0, len(splits), group):
            sums += _sb_suffix_sums(list(splits[n:n + group]), tri)
        return sums

    z_d, hl_d = zip(*[_sb_logits(qs[g], diag[g][0], strict) for g in range(heads)])
    sums_d = grouped_sums(hl_d, heads)
    z_l, hl_l = zip(*[_sb_logits(qs[g], left[g][0], None) for g in range(heads)])
    sums_l = grouped_sums(hl_l, SB_LEFT_SUFFIX_GROUP)
    acc_d = [_sb_weights(z_d[g], sums_d[g], diag[g][1], strict) for g in range(heads)]
    accs, carries = [], []
    for g in range(heads):
        carry_d = sums_d[g][:, 0:1]
        incl_l = sums_l[g] + carry_d
        d_l = _sb_weights(z_l[g], incl_l, left[g][1], None)
        accs.append(acc_d[g] + jnp.where(has_prev, d_l, 0.0))
        carries.append(jnp.where(has_prev, incl_l[:, 0:1], carry_d))
    accs, carries = tuple(accs), tuple(carries)

    def alive(carries):
        low = functools.reduce(jnp.minimum, carries)
        return (jnp.min(low) < SB_DEAD_CARRY).astype(jnp.int32)

    def cond(state):
        t, live, _, _ = state
        return jnp.logical_and(t >= 0, live > 0)

    def body(state):
        t, _, accs, carries = state
        blocks = [kv(g, t) for g in range(heads)]
        zs, hls = zip(*[_sb_logits(qs[g], blocks[g][0], None) for g in range(heads)])
        sums = _sb_suffix_sums(list(hls), tri)
        new_accs, new_carries = [], []
        for g in range(heads):
            incl = sums[g] + carries[g]
            new_accs.append(accs[g] + _sb_weights(zs[g], incl, blocks[g][1], None))
            new_carries.append(incl[:, 0:1])
        return t - 1, alive(new_carries), tuple(new_accs), tuple(new_carries)

    _, _, accs, _ = lax.while_loop(cond, body, (qi - 2, alive(carries), accs, carries))
    for g in range(heads):
        o_ref[:, g * d:(g + 1) * d] = accs[g].astype(o_ref.dtype)


def _sb_prompt(q16, k16, v16, side=()):
    h, s, d = q16.shape
    tq = SB_BLOCK
    g = SB_HEADS_PER_STEP
    n_groups, n_blocks = h // g, s // tq

    def side_spec(w):
        rows, cols = w.shape
        assert rows % (16 * n_blocks) == 0 and cols % (128 * n_groups) == 0
        return pl.BlockSpec((rows // n_blocks, cols // n_groups), lambda hg, qi: (qi, hg))

    side_specs = [side_spec(w) for w in side]
    out = pl.pallas_call(
        _sb_prompt_kernel,
        out_shape=[jax.ShapeDtypeStruct((s, h * d), BF16)]
        + [jax.ShapeDtypeStruct(w.shape, BF16) for w in side],
        grid=(n_groups, n_blocks),
        in_specs=[
            pl.BlockSpec((g, tq, d), lambda hg, qi: (hg, qi, 0)),
            pl.BlockSpec((g, s, d), lambda hg, qi: (hg, 0, 0)),
            pl.BlockSpec((g, s, d), lambda hg, qi: (hg, 0, 0)),
        ] + side_specs,
        out_specs=[pl.BlockSpec((tq, g * d), lambda hg, qi: (qi, hg))] + side_specs,
        compiler_params=_cparams(("arbitrary", "arbitrary")),
        name="sb_prompt",
    )(q16, k16, v16, *side)
    return out[0], tuple(out[1:])


def _sb_sample_kernel(q_ref, kn_ref, vn_ref, kc_hbm, vc_hbm, o_ref, kbuf, vbuf, sem, *, past):
    nh, t, d = q_ref.shape
    ht = nh * t
    blk = SB_BLOCK
    n_blocks = past // blk
    b = pl.program_id(0)
    slot = lax.rem(b, 2)

    def cache_copies(batch, block, to_slot):
        rows = pl.ds(pl.multiple_of(block * blk, blk), blk)
        return (pltpu.make_async_copy(kc_hbm.at[batch, :, rows, :], kbuf.at[to_slot], sem.at[0, to_slot]),
                pltpu.make_async_copy(vc_hbm.at[batch, :, rows, :], vbuf.at[to_slot], sem.at[1, to_slot]))

    @pl.when(b == 0)
    def _():
        for cp in cache_copies(0, n_blocks - 1, 0):
            cp.start()

    @pl.when(b + 1 < pl.num_programs(0))
    def _():
        for cp in cache_copies(b + 1, n_blocks - 1, 1 - slot):
            cp.start()

    q_all = q_ref[...].reshape(ht, d).astype(F32)
    q_head = lax.broadcasted_iota(jnp.int32, (ht, d), 0) // t
    q_masked = [jnp.where(q_head == h, q_all, 0.0).astype(BF16) for h in range(nh)]

    def logits(keys_of_head):
        z = _dot_nt(keys_of_head(0), q_masked[0])
        for h in range(1, nh):
            z = z + _dot_nt(keys_of_head(h), q_masked[h])
        return z

    def suffix_matrix(n):
        r = lax.broadcasted_iota(jnp.int32, (n, 2 * n), 0)
        c = lax.broadcasted_iota(jnp.int32, (n, 2 * n), 1)
        return jnp.where(jnp.where(c >= n, c - n, c) >= r, 1.0, 0.0).astype(BF16)

    def suffix_sum(sp, lmat):
        hi = sp.astype(BF16)
        lo = (sp - hi.astype(F32)).astype(BF16)
        return _dot(lmat, jnp.concatenate([hi, lo], axis=0))

    def emit(a_t, values_of_head, acc):
        a = a_t.T.astype(BF16)
        return [acc[h] + _dot(a[h * t:(h + 1) * t, :], values_of_head(h)) for h in range(nh)]

    z = logits(lambda h: kn_ref[h])
    s_idx = lax.broadcasted_iota(jnp.int32, (t, ht), 0)
    t_idx = lax.broadcasted_iota(jnp.int32, (t, ht), 1) % t
    strict = s_idx < t_idx
    sp = jnp.where(strict, _softplus2(z), 0.0)
    incl = suffix_sum(sp, suffix_matrix(t))
    a_t = jnp.where(strict, jnp.exp2(z - incl), 0.0)
    acc = emit(a_t, lambda h: vn_ref[h], [jnp.zeros((t, d), F32) for _ in range(nh)])
    carry = incl[0:1, :]

    lmat = suffix_matrix(blk)

    def cache_block(from_slot, acc, carry):
        z = logits(lambda h: kbuf[from_slot, h].astype(BF16))
        incl = suffix_sum(_softplus2(z), lmat) + carry
        acc = emit(jnp.exp2(z - incl), lambda h: vbuf[from_slot, h].astype(BF16), acc)
        return acc, incl[0:1, :]

    def alive(carry):
        return (jnp.min(carry) < SB_DEAD_CARRY).astype(jnp.int32)

    for cp in cache_copies(b, n_blocks - 1, slot):
        cp.wait()
    acc, carry = cache_block(slot, acc, carry)

    def cond(state):
        block, live, _, _ = state
        return jnp.logical_and(block >= 0, live > 0)

    def body(state):
        block, _, acc, carry = state
        copies = cache_copies(b, block, 2)
        for cp in copies:
            cp.start()
        for cp in copies:
            cp.wait()
        acc, carry = cache_block(2, list(acc), carry)
        return block - 1, alive(carry), tuple(acc), carry

    _, _, acc, _ = lax.while_loop(cond, body, (n_blocks - 2, alive(carry), tuple(acc), carry))
    for h in range(nh):
        o_ref[:, h * d:(h + 1) * d] = acc[h].astype(o_ref.dtype)


def _sb_sample(q16, k16, v16, cache_k, cache_v, *, t):
    h, m, d = q16.shape
    nb = m // t
    past = cache_k.shape[2]
    assert past % SB_BLOCK == 0
    new_spec = pl.BlockSpec((h, t, d), lambda b: (0, b, 0))
    cache_spec = pl.BlockSpec(memory_space=pl.ANY)
    return pl.pallas_call(
        functools.partial(_sb_sample_kernel, past=past),
        out_shape=jax.ShapeDtypeStruct((m, h * d), BF16),
        grid=(nb,),
        in_specs=[new_spec, new_spec, new_spec, cache_spec, cache_spec],
        out_specs=pl.BlockSpec((t, h * d), lambda b: (b, 0)),
        scratch_shapes=[pltpu.VMEM((3, h, SB_BLOCK, d), F32),
                        pltpu.VMEM((3, h, SB_BLOCK, d), F32),
                        pltpu.SemaphoreType.DMA((2, 3))],
        compiler_params=_cparams(("arbitrary",)),
        name="sb_sample",
    )(q16, k16, v16, cache_k, cache_v)


def _ret_log_decay(hd):
    return math.log(1.0 - 2.0 ** (-5.0 - hd))


def _ret_kernel(q_ref, k_ref, v_ref, g_ref, nw_ref, s0_ref, r_ref, st_ref,
                intra_ref, qdec_ref, kdec_ref, *, chunk):
    c = pl.program_id(1)

    @pl.when(jnp.logical_and(pl.program_id(0) == 0, c == 0))
    def _():
        row = lax.broadcasted_iota(jnp.int32, (chunk, chunk), 0)
        col = lax.broadcasted_iota(jnp.int32, (chunk, chunk), 1)
        diff = (row - col).astype(F32)
        pos = lax.broadcasted_iota(jnp.int32, (chunk, DK_RET), 0).astype(F32)
        for hd in range(H_RET):
            lg = _ret_log_decay(hd)
            intra_ref[hd] = jnp.where(row >= col, jnp.exp(lg * jnp.maximum(diff, 0.0)), 0.0)
            qdec_ref[hd] = jnp.exp(lg * (pos + 1.0))
            kdec_ref[hd] = jnp.exp(lg * (chunk - 1.0 - pos))

    @pl.when(c == 0)
    def _():
        st_ref[...] = s0_ref[...]

    for sub in range(q_ref.shape[1] // chunk):
        rows = slice(sub * chunk, (sub + 1) * chunk)
        for hd in range(H_RET):
            intra, q_dec, k_dec = intra_ref[hd], qdec_ref[hd], kdec_ref[hd]
            c_dec = math.exp(_ret_log_decay(hd) * chunk)
            def head(ref):
                return jnp.concatenate([ref[2 * hd, rows, :], ref[2 * hd + 1, rows, :]], axis=-1)

            q, k, v = head(q_ref), head(k_ref), head(v_ref)
            state = st_ref[hd]
            scores = (_dot_nt(q, k.astype(BF16)) * intra).astype(BF16)
            o = _dot(scores, v) + _dot(q, state.astype(BF16)) * q_dec
            st_ref[hd] = c_dec * state + _dot_tn((k * k_dec).astype(BF16), v)
            o = o * lax.rsqrt(jnp.mean(o * o, axis=-1, keepdims=True) + EPS)
            sl = slice(hd * DV_RET, (hd + 1) * DV_RET)
            gate = head(g_ref)
            r_ref[rows, sl] = (o * nw_ref[:, sl] * (gate * jax.nn.sigmoid(gate))).astype(r_ref.dtype)


def _retention(rq, rk, rv, gate, norm_w, state0, *, chunk, per_step):
    n_slices, m, width = rq.shape
    h = n_slices * width // DK_RET
    nb = state0.shape[0]
    rows = per_step * chunk
    nc = m // (nb * rows)
    qkv_spec = pl.BlockSpec((n_slices, rows, width), lambda b, c: (0, b * nc + c, 0))
    st_spec = pl.BlockSpec((None, h, DK_RET, DV_RET), lambda b, c: (b, 0, 0, 0))
    return pl.pallas_call(
        functools.partial(_ret_kernel, chunk=chunk),
        out_shape=(jax.ShapeDtypeStruct((m, h * DV_RET), BF16),
                   jax.ShapeDtypeStruct(state0.shape, F32)),
        grid=(nb, nc),
        in_specs=[qkv_spec, qkv_spec, qkv_spec, qkv_spec,
                  pl.BlockSpec((1, h * DV_RET), lambda b, c: (0, 0)),
                  st_spec],
        out_specs=(pl.BlockSpec((rows, h * DV_RET), lambda b, c: (b * nc + c, 0)), st_spec),
        scratch_shapes=[pltpu.VMEM((h, chunk, chunk), F32),
                        pltpu.VMEM((h, chunk, DK_RET), F32),
                        pltpu.VMEM((h, chunk, DK_RET), F32)],
        compiler_params=_cparams(("arbitrary", "arbitrary")),
        name="retention",
    )(rq, rk, rv, gate, norm_w, state0)


def _outproj_kernel(x_ref, a_ref, b_ref, wa_ref, wb_ref, o_ref):
    o_ref[...] = x_ref[...] + _dot(a_ref[...], wa_ref[...]) + _dot(b_ref[...], wb_ref[...])


def _outproj(x, a_sb, a_ret, w_out, *, tm):
    m = x.shape[0]
    row = lambda i: (i, 0)
    return pl.pallas_call(
        _outproj_kernel,
        out_shape=jax.ShapeDtypeStruct((m, D_MODEL), F32),
        grid=(m // tm,),
        in_specs=[
            pl.BlockSpec((tm, D_MODEL), row),
            pl.BlockSpec((tm, SB_W), row),
            pl.BlockSpec((tm, RET_W), row),
            pl.BlockSpec((SB_W, D_MODEL), lambda i: (0, 0)),
            pl.BlockSpec((RET_W, D_MODEL), lambda i: (1, 0)),
        ],
        out_specs=pl.BlockSpec((tm, D_MODEL), row),
        compiler_params=_cparams(("arbitrary",)),
        name="outproj",
    )(x, a_sb, a_ret, w_out, w_out)


OUTPROJ_TK = 512


def _outproj_f32w_kernel(x_ref, a_ref, w_ref, o_ref, w16_ref):
    @pl.when(pl.program_id(0) == 0)
    def _():
        o_ref[...] = x_ref[...]

    w16_ref[...] = w_ref[...].astype(BF16)
    o_ref[...] += _dot(a_ref[...], w16_ref[...])


def _outproj_f32w(x, a, w_out):
    m = x.shape[0]
    d_mix = w_out.shape[0]
    return pl.pallas_call(
        _outproj_f32w_kernel,
        out_shape=(jax.ShapeDtypeStruct((m, D_MODEL), F32),
                   jax.ShapeDtypeStruct(w_out.shape, BF16)),
        grid=(d_mix // OUTPROJ_TK,),
        in_specs=[
            pl.BlockSpec((m, D_MODEL), lambda k: (0, 0)),
            pl.BlockSpec((m, OUTPROJ_TK), lambda k: (0, k)),
            pl.BlockSpec((OUTPROJ_TK, D_MODEL), lambda k: (k, 0)),
        ],
        out_specs=(pl.BlockSpec((m, D_MODEL), lambda k: (0, 0)),
                   pl.BlockSpec((OUTPROJ_TK, D_MODEL), lambda k: (k, 0))),
        compiler_params=_cparams(("arbitrary",)),
        name="outproj_f32w",
    )(x, a, w_out)


def _mixers(x1, weights, sb_fn, state0, *, tm, proj_tm, pos_base, pos_mod, chunk,
            chunks_per_step):
    (nmix, w_in, qn, kn, ron, w_out, inv_freq) = weights
    f32w = w_in.dtype == F32
    proj = _inproj(x1, nmix, w_in, qn, kn, inv_freq, tm=proj_tm,
                   pos_base=pos_base, pos_mod=pos_mod)
    if f32w:
        proj, w_in = proj
    sq, sk, sk16, sv, sv16, rq, rk, rv, gate = proj
    a_sb, sb_side = sb_fn(sq, sk16, sv16)
    a_ret, state = _retention(rq, rk, rv, gate, ron, state0, chunk=chunk, per_step=chunks_per_step)
    if f32w:
        x2, w_out = _outproj_f32w(x1, jnp.concatenate([a_sb, a_ret], axis=1), w_out)
    else:
        x2 = _outproj(x1, a_sb, a_ret, w_out, tm=tm)
    return x2, sk, sv, state, (w_in, w_out), sb_side


def kernel(x_prompt, x_sample, cache_sb_k, cache_sb_v, state_ret, ffn1_norm, ffn1_w_gate, ffn1_w_up, ffn1_w_down, mix_norm, w_in, sb_q_norm, sb_k_norm, ret_out_norm, w_out, ffn2_norm, ffn2_w_gate, ffn2_w_up, ffn2_w_down, final_norm):
    depth = ffn1_norm.shape[0]
    assert depth == 1
    nb_p, seq, _ = x_prompt.shape
    nb_s, dec_seq, _ = x_sample.shape
    past = cache_sb_k.shape[3]
    assert nb_p == 1

    half = DK_RET // 2
    inv_freq = (ROPE_BASE ** (-jnp.arange(half, dtype=F32) / half)).reshape(1, half)
    l = 0
    n1, n2, nf = ffn1_norm[l][None], ffn2_norm[l][None], final_norm[l][None]

    def mixer_weights(w_in_, w_out_):
        return (mix_norm[l][None], w_in_, sb_q_norm[l][None], sb_k_norm[l][None],
                ret_out_norm[l][None], w_out_, inv_freq)

    ms = nb_s * dec_seq
    xs = x_sample.reshape(ms, D_MODEL)
    xp = x_prompt.reshape(seq, D_MODEL)
    x1s, ffn1_16 = _ffn(xs, n1, ffn1_w_gate[l], ffn1_w_up[l], ffn1_w_down[l], tm=ms)
    x1p = _ffn(xp, n1, *ffn1_16, tm=1024)

    def sb_s(sq, sk16, sv16):
        return _sb_sample(sq, sk16, sv16, cache_sb_k[l], cache_sb_v[l], t=dec_seq), ()

    x2s, sks, svs, sts, (w_in16, w_out16), _ = _mixers(
        x1s, mixer_weights(w_in[l], w_out[l]), sb_s, state_ret[l],
        tm=ms, proj_tm=ms, pos_base=past, pos_mod=dec_seq, chunk=dec_seq,
        chunks_per_step=1)

    sb_p = functools.partial(_sb_prompt, side=(ffn2_w_gate[l], ffn2_w_up[l], ffn2_w_down[l]))
    zero_state = jnp.zeros((1, H_RET, DK_RET, DV_RET), F32)
    x2p, skp, svp, stp, _, ffn2_16 = _mixers(
        x1p, mixer_weights(w_in16, w_out16), sb_p, zero_state,
        tm=512, proj_tm=1024, pos_base=0, pos_mod=seq, chunk=RET_CHUNK,
        chunks_per_step=4)

    ys = _ffn(x2s, n2, *ffn2_16, nf, tm=ms)
    yp = _ffn(x2p, n2, *ffn2_16, nf, tm=1024)

    def cache_layout(t):
        return t.reshape(H_SB, nb_s, dec_seq, D_SB).transpose(1, 0, 2, 3)[None]

    return (yp.reshape(1, seq, D_MODEL), ys.reshape(nb_s, dec_seq, D_MODEL),
            skp[None, None], svp[None, None], stp[None],
            cache_layout(sks), cache_layout(svs), sts[None])
```

```python
import functools
import math

import jax
import jax.numpy as jnp
from jax import lax
from jax.experimental import pallas as pl
from jax.experimental.pallas import tpu as pltpu

F32 = jnp.float32
BF16 = jnp.bfloat16

D_MODEL = 2048
D_FF = 5632
H_SB = 8
D_SB = 128
H_RET = 4
DK_RET = 256
DV_RET = 256
SB_W = H_SB * D_SB
RET_W = H_RET * DK_RET
N_SECTIONS = 7
SECTION_W = 1024
ROPE_BASE = 10000.0
EPS = 1e-6

V7X_VMEM_LIMIT_BYTES = 56 * 1024 * 1024
SB_BLOCK = 256
RET_CHUNK = 256
FFN_TF = 512

LOG2E = 1.4426950408889634


def _cparams(sem):
    return pltpu.CompilerParams(dimension_semantics=sem,
                                vmem_limit_bytes=V7X_VMEM_LIMIT_BYTES)


def _rmsnorm_rows(x, w):
    ms = jnp.mean(x * x, axis=-1, keepdims=True)
    return x * lax.rsqrt(ms + EPS) * w


def _dot(a, b):
    return jnp.dot(a, b, preferred_element_type=F32)


def _dot_nt(a, b):
    return lax.dot_general(a, b, (((1,), (1,)), ((), ())), preferred_element_type=F32)


def _dot_tn(a, b):
    return lax.dot_general(a, b, (((0,), (0,)), ((), ())), preferred_element_type=F32)


def _ffn_kernel(*refs, n_steps, final_norm, emit_bf16):
    x_ref, nw_ref, wg_ref, wu_ref, wd_ref = refs[:5]
    refs = refs[5:]
    fw_ref = None
    if final_norm:
        fw_ref, refs = refs[0], refs[1:]
    o_ref, refs = refs[0], refs[1:]
    if emit_bf16:
        wg16_ref, wu16_ref, wd16_ref, h_ref = refs
        wg16_ref[...] = wg_ref[...].astype(BF16)
        wu16_ref[...] = wu_ref[...].astype(BF16)
        wd16_ref[...] = wd_ref[...].astype(BF16)
        wg_ref, wu_ref, wd_ref = wg16_ref, wu16_ref, wd16_ref
    else:
        (h_ref,) = refs
    j = pl.program_id(1)

    @pl.when(j == 0)
    def _():
        x = x_ref[...]
        h_ref[...] = _rmsnorm_rows(x, nw_ref[...]).astype(BF16)
        o_ref[...] = x

    h = h_ref[...]
    g = _dot(h, wg_ref[...])
    u = _dot(h, wu_ref[...])
    a = (g * jax.nn.sigmoid(g) * (0.5 * u)).astype(BF16)
    o_ref[...] += _dot(a, wd_ref[...])

    if final_norm:
        @pl.when(j == n_steps - 1)
        def _():
            o_ref[...] = _rmsnorm_rows(o_ref[...], fw_ref[...])


def _ffn(x, norm_w, wg, wu, wd, final_w=None, *, tm):
    m = x.shape[0]
    n_steps = D_FF // FFN_TF
    emit_bf16 = wg.dtype == F32
    assert not emit_bf16 or m == tm
    w_specs = [
        pl.BlockSpec((D_MODEL, FFN_TF), lambda i, j: (0, j)),
        pl.BlockSpec((D_MODEL, FFN_TF), lambda i, j: (0, j)),
        pl.BlockSpec((FFN_TF, D_MODEL), lambda i, j: (j, 0)),
    ]
    in_specs = [
        pl.BlockSpec((tm, D_MODEL), lambda i, j: (i, 0)),
        pl.BlockSpec((1, D_MODEL), lambda i, j: (0, 0)),
    ] + w_specs
    args = [x, norm_w, wg, wu, wd]
    if final_w is not None:
        in_specs.append(pl.BlockSpec((1, D_MODEL), lambda i, j: (0, 0)))
        args.append(final_w)
    out_shape = [jax.ShapeDtypeStruct((m, D_MODEL), F32)]
    out_specs = [pl.BlockSpec((tm, D_MODEL), lambda i, j: (i, 0))]
    if emit_bf16:
        out_shape += [jax.ShapeDtypeStruct(w.shape, BF16) for w in (wg, wu, wd)]
        out_specs += w_specs
    out = pl.pallas_call(
        functools.partial(_ffn_kernel, n_steps=n_steps, final_norm=final_w is not None,
                          emit_bf16=emit_bf16),
        out_shape=out_shape,
        grid=(m // tm, n_steps),
        in_specs=in_specs,
        out_specs=out_specs,
        scratch_shapes=[pltpu.VMEM((tm, D_MODEL), BF16)],
        compiler_params=_cparams(("arbitrary", "arbitrary")),
        name="ffn_final" if final_w is not None else "ffn",
    )(*args)
    return (out[0], tuple(out[1:])) if emit_bf16 else out[0]


INPROJ_SECTION_OUTPUTS = ((None, "sq"), ("sk", "sk16"), ("sv", "sv16"), (None, "rq"),
                          ("rk", None), (None, "rv"), ("gate", None))
INPROJ_OUTPUT_ORDER = ("sq", "sk", "sk16", "sv", "sv16", "rq", "rk", "rv", "gate")
N_SLICES = SECTION_W // D_SB


def _inproj_kernel(x_ref, nw_ref, w_ref, qn_ref, kn_ref, inv_ref, *rest,
                   tm, tile_stride, pos_base, pos_mod, emit_bf16):
    out = dict(zip(INPROJ_OUTPUT_ORDER, rest[:9]))
    rest = rest[9:]
    if emit_bf16:
        w16_ref, rest = rest[0], rest[1:]
        w16_ref[...] = w_ref[...].astype(BF16)
        w_ref = w16_ref
    h_ref, cos_ref, sin_ref, cos_row_ref, sin_row_ref, st32, st16, sem = rest
    stage = (st32, st16)
    i = pl.program_id(0)
    j = pl.program_id(1)
    step = i * N_SECTIONS + j
    slot = lax.rem(step, 2)

    def copies(section, tile, from_slot):
        rows = pl.ds(pl.multiple_of(tile * tm, tm), tm)
        return [pltpu.make_async_copy(stage[kind].at[from_slot], out[name].at[:, rows, :],
                                      sem.at[kind, from_slot])
                for kind, name in enumerate(INPROJ_SECTION_OUTPUTS[section]) if name is not None]

    @pl.when(jnp.logical_and(i == 0, j == 0))
    def _():
        row = lax.broadcasted_iota(jnp.int32, (tm, DK_RET // 2), 0)
        ang = lax.rem(row, pos_mod).astype(F32) * inv_ref[...]
        cos_row_ref[...] = jnp.cos(ang)
        sin_row_ref[...] = jnp.sin(ang)

    @pl.when(j == 0)
    def _():
        h_ref[...] = _rmsnorm_rows(x_ref[...], nw_ref[...]).astype(BF16)
        ang = (pos_base + i * tile_stride).astype(F32) * inv_ref[...]
        ca, sa = jnp.cos(ang), jnp.sin(ang)
        cb, sb = cos_row_ref[...], sin_row_ref[...]
        cos_ref[...] = ca * cb - sa * sb
        sin_ref[...] = sa * cb + ca * sb

    def epilogue(section, p, put32, put16):
        def piece(n):
            return p[:, n * D_SB:(n + 1) * D_SB]

        if section == 0:
            w = qn_ref[...] * (D_SB ** -0.5 * LOG2E)
            for n in range(N_SLICES):
                put16(n, _rmsnorm_rows(piece(n), w).astype(BF16))
        elif section == 1:
            for n in range(N_SLICES):
                k = _rmsnorm_rows(piece(n), kn_ref[...])
                put32(n, k)
                put16(n, k.astype(BF16))
        elif section == 2:
            for n in range(N_SLICES):
                put32(n, piece(n))
                put16(n, piece(n).astype(BF16))
        elif section in (3, 4):
            c, s = cos_ref[...], sin_ref[...]
            for hd in range(H_RET):
                x1, x2 = piece(2 * hd), piece(2 * hd + 1)
                r1, r2 = x1 * c - x2 * s, x1 * s + x2 * c
                if section == 3:
                    put16(2 * hd, r1.astype(BF16))
                    put16(2 * hd + 1, r2.astype(BF16))
                else:
                    put32(2 * hd, r1 * (DK_RET ** -0.5))
                    put32(2 * hd + 1, r2 * (DK_RET ** -0.5))
        elif section == 5:
            for n in range(N_SLICES):
                put16(n, piece(n).astype(BF16))
        else:
            for n in range(N_SLICES):
                put32(n, piece(n))

    last_step = pl.num_programs(0) * N_SECTIONS - 1
    for section in range(N_SECTIONS):
        @pl.when(j == section)
        def _(section=section):
            @pl.when(step >= 2)
            def _():
                for cp in copies((section - 2) % N_SECTIONS, i if section >= 2 else i - 1, slot):
                    cp.wait()

            def put32(n, value):
                st32[slot, n] = value

            def put16(n, value):
                st16[slot, n] = value

            epilogue(section, _dot(h_ref[...], w_ref[...]), put32, put16)
            kinds = [kind for kind, name in enumerate(INPROJ_SECTION_OUTPUTS[section])
                     if name is not None]
            for cp, kind in zip(copies(section, i, slot), kinds):
                cp.start(priority=kind)

            if section == N_SECTIONS - 1:
                @pl.when(step == last_step)
                def _():
                    for cp in copies(section - 1, i, 1 - slot) + copies(section, i, slot):
                        cp.wait()


def _inproj(x, norm_w, w_in, qn, kn, inv_freq, *, tm, pos_base, pos_mod):
    m = x.shape[0]
    emit_bf16 = w_in.dtype == F32
    assert not emit_bf16 or m == tm
    assert pos_mod >= m or tm % pos_mod == 0
    tile_stride = tm if pos_mod >= m else 0
    half = DK_RET // 2
    f32_outputs = {name for name, _ in INPROJ_SECTION_OUTPUTS if name is not None}
    out_shape = [jax.ShapeDtypeStruct((N_SLICES, m, D_SB), F32 if name in f32_outputs else BF16)
                 for name in INPROJ_OUTPUT_ORDER]
    out_specs = [pl.BlockSpec(memory_space=pl.ANY)] * len(out_shape)
    w_spec = pl.BlockSpec((D_MODEL, SECTION_W), lambda i, j: (0, j))
    if emit_bf16:
        out_shape.append(jax.ShapeDtypeStruct(w_in.shape, BF16))
        out_specs.append(w_spec)
    out = pl.pallas_call(
        functools.partial(_inproj_kernel, tm=tm, tile_stride=tile_stride,
                          pos_base=pos_base, pos_mod=pos_mod, emit_bf16=emit_bf16),
        out_shape=out_shape,
        grid=(m // tm, N_SECTIONS),
        in_specs=[
            pl.BlockSpec((tm, D_MODEL), lambda i, j: (i, 0)),
            pl.BlockSpec((1, D_MODEL), lambda i, j: (0, 0)),
            w_spec,
            pl.BlockSpec((1, D_SB), lambda i, j: (0, 0)),
            pl.BlockSpec((1, D_SB), lambda i, j: (0, 0)),
            pl.BlockSpec((1, half), lambda i, j: (0, 0)),
        ],
        out_specs=out_specs,
        scratch_shapes=[pltpu.VMEM((tm, D_MODEL), BF16)] + [pltpu.VMEM((tm, half), F32)] * 4
        + [pltpu.VMEM((2, N_SLICES, tm, D_SB), F32), pltpu.VMEM((2, N_SLICES, tm, D_SB), BF16),
           pltpu.SemaphoreType.DMA((2, 2))],
        compiler_params=_cparams(("arbitrary", "arbitrary")),
        name="inproj",
    )(x, norm_w, w_in, qn, kn, inv_freq)
    return (tuple(out[:9]), out[9]) if emit_bf16 else tuple(out)


def _softplus2(z):
    return jnp.maximum(z, 0.0) + jnp.log2(1.0 + jnp.exp2(-jnp.abs(z)))


def _suffix_matrix(n):
    r = lax.broadcasted_iota(jnp.int32, (2 * n, n), 0)
    c = lax.broadcasted_iota(jnp.int32, (2 * n, n), 1)
    return jnp.where(jnp.where(r >= n, r - n, r) >= c, 1.0, 0.0).astype(BF16)


def _sb_logits(q, k, mask):
    z = _dot_nt(q, k)
    sp = _softplus2(z)
    if mask is not None:
        sp = jnp.where(mask, sp, 0.0)
    hi = sp.astype(BF16)
    lo = (sp - hi.astype(F32)).astype(BF16)
    return z, jnp.concatenate([hi, lo], axis=1)


def _sb_suffix_sums(splits, tri):
    q = splits[0].shape[0]
    sums = _dot(jnp.concatenate(splits, axis=0), tri)
    return [sums[n * q:(n + 1) * q] for n in range(len(splits))]


def _sb_weights(z, incl, v, mask):
    a = jnp.exp2(z - incl)
    if mask is not None:
        a = jnp.where(mask, a, 0.0)
    return _dot(a.astype(BF16), v)


SB_DEAD_CARRY = 150.0


SB_HEADS_PER_STEP = 4
SB_LEFT_SUFFIX_GROUP = 2


def _sb_prompt_kernel(q_ref, k_ref, v_ref, *rest):
    n_side = len(rest) // 2
    o_ref = rest[n_side]
    for src, dst in zip(rest[:n_side], rest[n_side + 1:]):
        dst[...] = src[...].astype(BF16)
    qi = pl.program_id(1)
    heads, _, d = q_ref.shape
    blk = SB_BLOCK
    tri = _suffix_matrix(blk)
    r = lax.broadcasted_iota(jnp.int32, (blk, blk), 0)
    c = lax.broadcasted_iota(jnp.int32, (blk, blk), 1)
    strict = c < r
    has_prev = qi > 0

    def kv(g, block):
        s0 = pl.multiple_of(block * blk, blk)
        return k_ref[g, pl.ds(s0, blk), :], v_ref[g, pl.ds(s0, blk), :]

    qs = [q_ref[g] for g in range(heads)]
    prev = jnp.maximum(qi - 1, 0)

    def kblk(g, block):
        return k_ref[g, pl.ds(pl.multiple_of(block * blk, blk), blk), :]

    def vblk(g, block):
        return v_ref[g, pl.ds(pl.multiple_of(block * blk, blk), blk), :]

    def grouped_sums(splits, group):
        sums = []
        for n in range(0, len(splits), group):
            sums += _sb_suffix_sums(list(splits[n:n + group]), tri)
        return sums

    z_d, hl_d = zip(*[_sb_logits(qs[g], kblk(g, qi), strict) for g in range(heads)])
    sums_d = grouped_sums(hl_d, heads)
    z_l, hl_l = zip(*[_sb_logits(qs[g], kblk(g, prev), None) for g in range(heads)])
    sums_l = grouped_sums(hl_l, SB_LEFT_SUFFIX_GROUP)
    acc_d = [_sb_weights(z_d[g], sums_d[g], vblk(g, qi), strict) for g in range(heads)]
    accs, carries = [], []
    for g in range(heads):
        carry_d = sums_d[g][:, 0:1]
        incl_l = sums_l[g] + carry_d
        d_l = _sb_weights(z_l[g], incl_l, vblk(g, prev), None)
        accs.append(acc_d[g] + jnp.where(has_prev, d_l, 0.0))
        carries.append(jnp.where(has_prev, incl_l[:, 0:1], carry_d))
    accs, carries = tuple(accs), tuple(carries)

    def alive(carries):
        low = functools.reduce(jnp.minimum, carries)
        return (jnp.min(low) < SB_DEAD_CARRY).astype(jnp.int32)

    def cond(state):
        t, live, _, _ = state
        return jnp.logical_and(t >= 0, live > 0)

    def body(state):
        t, _, accs, carries = state
        blocks = [kv(g, t) for g in range(heads)]
        zs, hls = zip(*[_sb_logits(qs[g], blocks[g][0], None) for g in range(heads)])
        sums = _sb_suffix_sums(list(hls), tri)
        new_accs, new_carries = [], []
        for g in range(heads):
            incl = sums[g] + carries[g]
            new_accs.append(accs[g] + _sb_weights(zs[g], incl, blocks[g][1], None))
            new_carries.append(incl[:, 0:1])
        return t - 1, alive(new_carries), tuple(new_accs), tuple(new_carries)

    _, _, accs, _ = lax.while_loop(cond, body, (qi - 2, alive(carries), accs, carries))
    for g in range(heads):
        o_ref[:, g * d:(g + 1) * d] = accs[g].astype(o_ref.dtype)


def _sb_prompt(q16, k16, v16, side=()):
    h, s, d = q16.shape
    tq = SB_BLOCK
    g = SB_HEADS_PER_STEP
    n_groups, n_blocks = h // g, s // tq

    def side_spec(w):
        rows, cols = w.shape
        assert rows % (16 * n_blocks) == 0 and cols % (128 * n_groups) == 0
        return pl.BlockSpec((rows // n_blocks, cols // n_groups), lambda hg, qi: (qi, hg))

    side_specs = [side_spec(w) for w in side]
    out = pl.pallas_call(
        _sb_prompt_kernel,
        out_shape=[jax.ShapeDtypeStruct((s, h * d), BF16)]
        + [jax.ShapeDtypeStruct(w.shape, BF16) for w in side],
        grid=(n_groups, n_blocks),
        in_specs=[
            pl.BlockSpec((g, tq, d), lambda hg, qi: (hg, qi, 0)),
            pl.BlockSpec((g, s, d), lambda hg, qi: (hg, 0, 0)),
            pl.BlockSpec((g, s, d), lambda hg, qi: (hg, 0, 0)),
        ] + side_specs,
        out_specs=[pl.BlockSpec((tq, g * d), lambda hg, qi: (qi, hg))] + side_specs,
        compiler_params=_cparams(("arbitrary", "arbitrary")),
        name="sb_prompt",
    )(q16, k16, v16, *side)
    return out[0], tuple(out[1:])


def _sb_sample_kernel(q_ref, kn_ref, vn_ref, kc_hbm, vc_hbm, o_ref, kbuf, vbuf, sem, *, past):
    nh, t, d = q_ref.shape
    ht = nh * t
    blk = SB_BLOCK
    n_blocks = past // blk
    b = pl.program_id(0)
    slot = lax.rem(b, 2)

    def cache_copies(batch, block, to_slot):
        rows = pl.ds(pl.multiple_of(block * blk, blk), blk)
        return (pltpu.make_async_copy(kc_hbm.at[batch, :, rows, :], kbuf.at[to_slot], sem.at[0, to_slot]),
                pltpu.make_async_copy(vc_hbm.at[batch, :, rows, :], vbuf.at[to_slot], sem.at[1, to_slot]))

    @pl.when(b == 0)
    def _():
        for n, cp in enumerate(cache_copies(0, n_blocks - 1, 0)):
            cp.start(priority=n)

    @pl.when(b + 1 < pl.num_programs(0))
    def _():
        for n, cp in enumerate(cache_copies(b + 1, n_blocks - 1, 1 - slot)):
            cp.start(priority=n)

    q_all = q_ref[...].reshape(ht, d).astype(F32)
    q_head = lax.broadcasted_iota(jnp.int32, (ht, d), 0) // t
    q_masked = [jnp.where(q_head == h, q_all, 0.0).astype(BF16) for h in range(nh)]

    def logits(keys_of_head):
        z = _dot_nt(keys_of_head(0), q_masked[0])
        for h in range(1, nh):
            z = z + _dot_nt(keys_of_head(h), q_masked[h])
        return z

    def suffix_matrix(n):
        r = lax.broadcasted_iota(jnp.int32, (n, 2 * n), 0)
        c = lax.broadcasted_iota(jnp.int32, (n, 2 * n), 1)
        return jnp.where(jnp.where(c >= n, c - n, c) >= r, 1.0, 0.0).astype(BF16)

    def suffix_sum(sp, lmat):
        hi = sp.astype(BF16)
        lo = (sp - hi.astype(F32)).astype(BF16)
        return _dot(lmat, jnp.concatenate([hi, lo], axis=0))

    def emit(a_t, values_of_head, acc):
        a = a_t.T.astype(BF16)
        return [acc[h] + _dot(a[h * t:(h + 1) * t, :], values_of_head(h)) for h in range(nh)]

    z = logits(lambda h: kn_ref[h])
    s_idx = lax.broadcasted_iota(jnp.int32, (t, ht), 0)
    t_idx = lax.broadcasted_iota(jnp.int32, (t, ht), 1) % t
    strict = s_idx < t_idx
    sp = jnp.where(strict, _softplus2(z), 0.0)
    incl = suffix_sum(sp, suffix_matrix(t))
    a_t = jnp.where(strict, jnp.exp2(z - incl), 0.0)
    acc = emit(a_t, lambda h: vn_ref[h], [jnp.zeros((t, d), F32) for _ in range(nh)])
    carry = incl[0:1, :]

    lmat = suffix_matrix(blk)

    def cache_block(from_slot, acc, carry):
        z = logits(lambda h: kbuf[from_slot, h].astype(BF16))
        incl = suffix_sum(_softplus2(z), lmat) + carry
        acc = emit(jnp.exp2(z - incl), lambda h: vbuf[from_slot, h].astype(BF16), acc)
        return acc, incl[0:1, :]

    def alive(carry):
        return (jnp.min(carry) < SB_DEAD_CARRY).astype(jnp.int32)

    for cp in cache_copies(b, n_blocks - 1, slot):
        cp.wait()
    acc, carry = cache_block(slot, acc, carry)

    def cond(state):
        block, live, _, _ = state
        return jnp.logical_and(block >= 0, live > 0)

    def body(state):
        block, _, acc, carry = state
        copies = cache_copies(b, block, 2)
        for n, cp in enumerate(copies):
            cp.start(priority=n)
        for cp in copies:
            cp.wait()
        acc, carry = cache_block(2, list(acc), carry)
        return block - 1, alive(carry), tuple(acc), carry

    _, _, acc, _ = lax.while_loop(cond, body, (n_blocks - 2, alive(carry), tuple(acc), carry))
    for h in range(nh):
        o_ref[:, h * d:(h + 1) * d] = acc[h].astype(o_ref.dtype)


def _sb_sample(q16, k16, v16, cache_k, cache_v, *, t):
    h, m, d = q16.shape
    nb = m // t
    past = cache_k.shape[2]
    assert past % SB_BLOCK == 0
    new_spec = pl.BlockSpec((h, t, d), lambda b: (0, b, 0))
    cache_spec = pl.BlockSpec(memory_space=pl.ANY)
    return pl.pallas_call(
        functools.partial(_sb_sample_kernel, past=past),
        out_shape=jax.ShapeDtypeStruct((m, h * d), BF16),
        grid=(nb,),
        in_specs=[new_spec, new_spec, new_spec, cache_spec, cache_spec],
        out_specs=pl.BlockSpec((t, h * d), lambda b: (b, 0)),
        scratch_shapes=[pltpu.VMEM((3, h, SB_BLOCK, d), F32),
                        pltpu.VMEM((3, h, SB_BLOCK, d), F32),
                        pltpu.SemaphoreType.DMA((2, 3))],
        compiler_params=_cparams(("arbitrary",)),
        name="sb_sample",
    )(q16, k16, v16, cache_k, cache_v)


def _ret_log_decay(hd):
    return math.log(1.0 - 2.0 ** (-5.0 - hd))


def _ret_kernel(q_ref, k_ref, v_ref, g_ref, nw_ref, s0_ref, r_ref, st_ref,
                intra_ref, qdec_ref, kdec_ref, *, chunk):
    c = pl.program_id(1)

    @pl.when(jnp.logical_and(pl.program_id(0) == 0, c == 0))
    def _():
        row = lax.broadcasted_iota(jnp.int32, (chunk, chunk), 0)
        col = lax.broadcasted_iota(jnp.int32, (chunk, chunk), 1)
        diff = (row - col).astype(F32)
        pos = lax.broadcasted_iota(jnp.int32, (chunk, DK_RET), 0).astype(F32)
        for hd in range(H_RET):
            lg = _ret_log_decay(hd)
            intra_ref[hd] = jnp.where(row >= col, jnp.exp(lg * jnp.maximum(diff, 0.0)), 0.0)
            qdec_ref[hd] = jnp.exp(lg * (pos + 1.0))
            kdec_ref[hd] = jnp.exp(lg * (chunk - 1.0 - pos))

    @pl.when(c == 0)
    def _():
        st_ref[...] = s0_ref[...]

    for sub in range(q_ref.shape[1] // chunk):
        rows = slice(sub * chunk, (sub + 1) * chunk)
        for hd in range(H_RET):
            intra, q_dec, k_dec = intra_ref[hd], qdec_ref[hd], kdec_ref[hd]
            c_dec = math.exp(_ret_log_decay(hd) * chunk)
            def head(ref):
                return jnp.concatenate([ref[2 * hd, rows, :], ref[2 * hd + 1, rows, :]], axis=-1)

            q, k, v = head(q_ref), head(k_ref), head(v_ref)
            state = st_ref[hd]
            scores = (_dot_nt(q, k.astype(BF16)) * intra).astype(BF16)
            o = _dot(scores, v) + _dot(q, state.astype(BF16)) * q_dec
            st_ref[hd] = c_dec * state + _dot_tn((k * k_dec).astype(BF16), v)
            o = o * lax.rsqrt(jnp.mean(o * o, axis=-1, keepdims=True) + EPS)
            sl = slice(hd * DV_RET, (hd + 1) * DV_RET)
            gate = head(g_ref)
            r_ref[rows, sl] = (o * nw_ref[:, sl] * (gate * jax.nn.sigmoid(gate))).astype(r_ref.dtype)


def _retention(rq, rk, rv, gate, norm_w, state0, *, chunk, per_step):
    n_slices, m, width = rq.shape
    h = n_slices * width // DK_RET
    nb = state0.shape[0]
    rows = per_step * chunk
    nc = m // (nb * rows)
    qkv_spec = pl.BlockSpec((n_slices, rows, width), lambda b, c: (0, b * nc + c, 0))
    st_spec = pl.BlockSpec((None, h, DK_RET, DV_RET), lambda b, c: (b, 0, 0, 0))
    return pl.pallas_call(
        functools.partial(_ret_kernel, chunk=chunk),
        out_shape=(jax.ShapeDtypeStruct((m, h * DV_RET), BF16),
                   jax.ShapeDtypeStruct(state0.shape, F32)),
        grid=(nb, nc),
        in_specs=[qkv_spec, qkv_spec, qkv_spec, qkv_spec,
                  pl.BlockSpec((1, h * DV_RET), lambda b, c: (0, 0)),
                  st_spec],
        out_specs=(pl.BlockSpec((rows, h * DV_RET), lambda b, c: (b * nc + c, 0)), st_spec),
        scratch_shapes=[pltpu.VMEM((h, chunk, chunk), F32),
                        pltpu.VMEM((h, chunk, DK_RET), F32),
                        pltpu.VMEM((h, chunk, DK_RET), F32)],
        compiler_params=_cparams(("arbitrary", "arbitrary")),
        name="retention",
    )(rq, rk, rv, gate, norm_w, state0)


def _outproj_kernel(x_ref, a_ref, b_ref, wa_ref, wb_ref, o_ref):
    o_ref[...] = x_ref[...] + _dot(a_ref[...], wa_ref[...]) + _dot(b_ref[...], wb_ref[...])


def _outproj(x, a_sb, a_ret, w_out, *, tm):
    m = x.shape[0]
    row = lambda i: (i, 0)
    return pl.pallas_call(
        _outproj_kernel,
        out_shape=jax.ShapeDtypeStruct((m, D_MODEL), F32),
        grid=(m // tm,),
        in_specs=[
            pl.BlockSpec((tm, D_MODEL), row),
            pl.BlockSpec((tm, SB_W), row),
            pl.BlockSpec((tm, RET_W), row),
            pl.BlockSpec((SB_W, D_MODEL), lambda i: (0, 0)),
            pl.BlockSpec((RET_W, D_MODEL), lambda i: (1, 0)),
        ],
        out_specs=pl.BlockSpec((tm, D_MODEL), row),
        compiler_params=_cparams(("arbitrary",)),
        name="outproj",
    )(x, a_sb, a_ret, w_out, w_out)


OUTPROJ_TK = 512


def _outproj_f32w_kernel(x_ref, a_ref, w_ref, o_ref, w16_ref):
    @pl.when(pl.program_id(0) == 0)
    def _():
        o_ref[...] = x_ref[...]

    w16_ref[...] = w_ref[...].astype(BF16)
    o_ref[...] += _dot(a_ref[...], w16_ref[...])


def _outproj_f32w(x, a, w_out):
    m = x.shape[0]
    d_mix = w_out.shape[0]
    return pl.pallas_call(
        _outproj_f32w_kernel,
        out_shape=(jax.ShapeDtypeStruct((m, D_MODEL), F32),
                   jax.ShapeDtypeStruct(w_out.shape, BF16)),
        grid=(d_mix // OUTPROJ_TK,),
        in_specs=[
            pl.BlockSpec((m, D_MODEL), lambda k: (0, 0)),
            pl.BlockSpec((m, OUTPROJ_TK), lambda k: (0, k)),
            pl.BlockSpec((OUTPROJ_TK, D_MODEL), lambda k: (k, 0)),
        ],
        out_specs=(pl.BlockSpec((m, D_MODEL), lambda k: (0, 0)),
                   pl.BlockSpec((OUTPROJ_TK, D_MODEL), lambda k: (k, 0))),
        compiler_params=_cparams(("arbitrary",)),
        name="outproj_f32w",
    )(x, a, w_out)


def _mixers(x1, weights, sb_fn, state0, *, tm, proj_tm, pos_base, pos_mod, chunk,
            chunks_per_step):
    (nmix, w_in, qn, kn, ron, w_out, inv_freq) = weights
    f32w = w_in.dtype == F32
    proj = _inproj(x1, nmix, w_in, qn, kn, inv_freq, tm=proj_tm,
                   pos_base=pos_base, pos_mod=pos_mod)
    if f32w:
        proj, w_in = proj
    sq, sk, sk16, sv, sv16, rq, rk, rv, gate = proj
    a_sb, sb_side = sb_fn(sq, sk16, sv16)
    a_ret, state = _retention(rq, rk, rv, gate, ron, state0, chunk=chunk, per_step=chunks_per_step)
    if f32w:
        x2, w_out = _outproj_f32w(x1, jnp.concatenate([a_sb, a_ret], axis=1), w_out)
    else:
        x2 = _outproj(x1, a_sb, a_ret, w_out, tm=tm)
    return x2, sk, sv, state, (w_in, w_out), sb_side


def kernel(x_prompt, x_sample, cache_sb_k, cache_sb_v, state_ret, ffn1_norm, ffn1_w_gate, ffn1_w_up, ffn1_w_down, mix_norm, w_in, sb_q_norm, sb_k_norm, ret_out_norm, w_out, ffn2_norm, ffn2_w_gate, ffn2_w_up, ffn2_w_down, final_norm):
    depth = ffn1_norm.shape[0]
    assert depth == 1
    nb_p, seq, _ = x_prompt.shape
    nb_s, dec_seq, _ = x_sample.shape
    past = cache_sb_k.shape[3]
    assert nb_p == 1

    half = DK_RET // 2
    inv_freq = (ROPE_BASE ** (-jnp.arange(half, dtype=F32) / half)).reshape(1, half)
    l = 0
    n1, n2, nf = ffn1_norm[l][None], ffn2_norm[l][None], final_norm[l][None]

    def mixer_weights(w_in_, w_out_):
        return (mix_norm[l][None], w_in_, sb_q_norm[l][None], sb_k_norm[l][None],
                ret_out_norm[l][None], w_out_, inv_freq)

    ms = nb_s * dec_seq
    xs = x_sample.reshape(ms, D_MODEL)
    xp = x_prompt.reshape(seq, D_MODEL)
    x1s, ffn1_16 = _ffn(xs, n1, ffn1_w_gate[l], ffn1_w_up[l], ffn1_w_down[l], tm=ms)
    x1p = _ffn(xp, n1, *ffn1_16, tm=1024)

    def sb_s(sq, sk16, sv16):
        return _sb_sample(sq, sk16, sv16, cache_sb_k[l], cache_sb_v[l], t=dec_seq), ()

    x2s, sks, svs, sts, (w_in16, w_out16), _ = _mixers(
        x1s, mixer_weights(w_in[l], w_out[l]), sb_s, state_ret[l],
        tm=ms, proj_tm=ms, pos_base=past, pos_mod=dec_seq, chunk=dec_seq,
        chunks_per_step=1)

    sb_p = functools.partial(_sb_prompt, side=(ffn2_w_gate[l], ffn2_w_up[l], ffn2_w_down[l]))
    zero_state = jnp.zeros((1, H_RET, DK_RET, DV_RET), F32)
    x2p, skp, svp, stp, _, ffn2_16 = _mixers(
        x1p, mixer_weights(w_in16, w_out16), sb_p, zero_state,
        tm=512, proj_tm=1024, pos_base=0, pos_mod=seq, chunk=RET_CHUNK,
        chunks_per_step=4)

    ys = _ffn(x2s, n2, *ffn2_16, nf, tm=ms)
    yp = _ffn(x2p, n2, *ffn2_16, nf, tm=1024)

    def cache_layout(t):
        return t.reshape(H_SB, nb_s, dec_seq, D_SB).transpose(1, 0, 2, 3)[None]

    return (yp.reshape(1, seq, D_MODEL), ys.reshape(nb_s, dec_seq, D_MODEL),
            skp[None, None], svp[None, None], stp[None],
            cache_layout(sks), cache_layout(svs), sts[None])
```

```python
import functools
import math

import jax
import jax.numpy as jnp
from jax import lax
from jax.experimental import pallas as pl
from jax.experimental.pallas import tpu as pltpu

F32 = jnp.float32
BF16 = jnp.bfloat16

D_MODEL = 2048
D_FF = 5632
H_SB = 8
D_SB = 128
H_RET = 4
DK_RET = 256
DV_RET = 256
SB_W = H_SB * D_SB
RET_W = H_RET * DK_RET
N_SECTIONS = 7
SECTION_W = 1024
ROPE_BASE = 10000.0
EPS = 1e-6

V7X_VMEM_LIMIT_BYTES = 56 * 1024 * 1024
SB_BLOCK = 256
RET_CHUNK = 256
FFN_TF = 512

LOG2E = 1.4426950408889634


def _cparams(sem):
    return pltpu.CompilerParams(dimension_semantics=sem,
                                vmem_limit_bytes=V7X_VMEM_LIMIT_BYTES)


def _rmsnorm_rows(x, w):
    ms = jnp.mean(x * x, axis=-1, keepdims=True)
    return x * lax.rsqrt(ms + EPS) * w


def _dot(a, b):
    return jnp.dot(a, b, preferred_element_type=F32)


def _dot_nt(a, b):
    return lax.dot_general(a, b, (((1,), (1,)), ((), ())), preferred_element_type=F32)


def _dot_tn(a, b):
    return lax.dot_general(a, b, (((0,), (0,)), ((), ())), preferred_element_type=F32)


def _ffn_kernel(*refs, n_steps, final_norm, emit_bf16):
    x_ref, nw_ref, wg_ref, wu_ref, wd_ref = refs[:5]
    refs = refs[5:]
    fw_ref = None
    if final_norm:
        fw_ref, refs = refs[0], refs[1:]
    o_ref, refs = refs[0], refs[1:]
    if emit_bf16:
        wg16_ref, wu16_ref, wd16_ref, h_ref = refs
        wg16_ref[...] = wg_ref[...].astype(BF16)
        wu16_ref[...] = wu_ref[...].astype(BF16)
        wd16_ref[...] = wd_ref[...].astype(BF16)
        wg_ref, wu_ref, wd_ref = wg16_ref, wu16_ref, wd16_ref
    else:
        (h_ref,) = refs
    j = pl.program_id(1)

    @pl.when(j == 0)
    def _():
        x = x_ref[...]
        h_ref[...] = _rmsnorm_rows(x, nw_ref[...]).astype(BF16)
        o_ref[...] = x

    h = h_ref[...]
    g = _dot(h, wg_ref[...])
    u = _dot(h, wu_ref[...])
    a = (g * jax.nn.sigmoid(g) * (0.5 * u)).astype(BF16)
    o_ref[...] += _dot(a, wd_ref[...])

    if final_norm:
        @pl.when(j == n_steps - 1)
        def _():
            o_ref[...] = _rmsnorm_rows(o_ref[...], fw_ref[...])


def _ffn(x, norm_w, wg, wu, wd, final_w=None, *, tm):
    m = x.shape[0]
    n_steps = D_FF // FFN_TF
    emit_bf16 = wg.dtype == F32
    assert not emit_bf16 or m == tm
    w_specs = [
        pl.BlockSpec((D_MODEL, FFN_TF), lambda i, j: (0, j)),
        pl.BlockSpec((D_MODEL, FFN_TF), lambda i, j: (0, j)),
        pl.BlockSpec((FFN_TF, D_MODEL), lambda i, j: (j, 0)),
    ]
    in_specs = [
        pl.BlockSpec((tm, D_MODEL), lambda i, j: (i, 0)),
        pl.BlockSpec((1, D_MODEL), lambda i, j: (0, 0)),
    ] + w_specs
    args = [x, norm_w, wg, wu, wd]
    if final_w is not None:
        in_specs.append(pl.BlockSpec((1, D_MODEL), lambda i, j: (0, 0)))
        args.append(final_w)
    out_shape = [jax.ShapeDtypeStruct((m, D_MODEL), F32)]
    out_specs = [pl.BlockSpec((tm, D_MODEL), lambda i, j: (i, 0))]
    if emit_bf16:
        out_shape += [jax.ShapeDtypeStruct(w.shape, BF16) for w in (wg, wu, wd)]
        out_specs += w_specs
    out = pl.pallas_call(
        functools.partial(_ffn_kernel, n_steps=n_steps, final_norm=final_w is not None,
                          emit_bf16=emit_bf16),
        out_shape=out_shape,
        grid=(m // tm, n_steps),
        in_specs=in_specs,
        out_specs=out_specs,
        scratch_shapes=[pltpu.VMEM((tm, D_MODEL), BF16)],
        compiler_params=_cparams(("arbitrary", "arbitrary")),
        name="ffn_final" if final_w is not None else "ffn",
    )(*args)
    return (out[0], tuple(out[1:])) if emit_bf16 else out[0]


INPROJ_SECTION_OUTPUTS = ((None, "sq"), ("sk", "sk16"), ("sv", "sv16"), (None, "rq"),
                          ("rk", None), (None, "rv"), ("gate", None))
INPROJ_OUTPUT_ORDER = ("sq", "sk", "sk16", "sv", "sv16", "rq", "rk", "rv", "gate")
N_SLICES = SECTION_W // D_SB
INPROJ_STAGE_SLOTS = 3


def _inproj_kernel(x_ref, nw_ref, w_ref, qn_ref, kn_ref, inv_ref, *rest,
                   tm, tile_stride, pos_base, pos_mod, emit_bf16):
    out = dict(zip(INPROJ_OUTPUT_ORDER, rest[:9]))
    rest = rest[9:]
    if emit_bf16:
        w16_ref, rest = rest[0], rest[1:]
        w16_ref[...] = w_ref[...].astype(BF16)
        w_ref = w16_ref
    h_ref, cos_ref, sin_ref, cos_row_ref, sin_row_ref, st32, st16, sem = rest
    stage = (st32, st16)
    i = pl.program_id(0)
    j = pl.program_id(1)
    step = i * N_SECTIONS + j
    slot = lax.rem(step, INPROJ_STAGE_SLOTS)

    def copies(section, tile, from_slot):
        rows = pl.ds(pl.multiple_of(tile * tm, tm), tm)
        return [pltpu.make_async_copy(stage[kind].at[from_slot], out[name].at[:, rows, :],
                                      sem.at[kind, from_slot])
                for kind, name in enumerate(INPROJ_SECTION_OUTPUTS[section]) if name is not None]

    @pl.when(jnp.logical_and(i == 0, j == 0))
    def _():
        row = lax.broadcasted_iota(jnp.int32, (tm, DK_RET // 2), 0)
        ang = lax.rem(row, pos_mod).astype(F32) * inv_ref[...]
        cos_row_ref[...] = jnp.cos(ang)
        sin_row_ref[...] = jnp.sin(ang)

    @pl.when(j == 0)
    def _():
        h_ref[...] = _rmsnorm_rows(x_ref[...], nw_ref[...]).astype(BF16)
        ang = (pos_base + i * tile_stride).astype(F32) * inv_ref[...]
        ca, sa = jnp.cos(ang), jnp.sin(ang)
        cb, sb = cos_row_ref[...], sin_row_ref[...]
        cos_ref[...] = ca * cb - sa * sb
        sin_ref[...] = sa * cb + ca * sb

    def epilogue(section, p, put32, put16):
        def piece(n):
            return p[:, n * D_SB:(n + 1) * D_SB]

        if section == 0:
            w = qn_ref[...] * (D_SB ** -0.5 * LOG2E)
            for n in range(N_SLICES):
                put16(n, _rmsnorm_rows(piece(n), w).astype(BF16))
        elif section == 1:
            for n in range(N_SLICES):
                k = _rmsnorm_rows(piece(n), kn_ref[...])
                put32(n, k)
                put16(n, k.astype(BF16))
        elif section == 2:
            for n in range(N_SLICES):
                put32(n, piece(n))
                put16(n, piece(n).astype(BF16))
        elif section in (3, 4):
            c, s = cos_ref[...], sin_ref[...]
            for hd in range(H_RET):
                x1, x2 = piece(2 * hd), piece(2 * hd + 1)
                r1, r2 = x1 * c - x2 * s, x1 * s + x2 * c
                if section == 3:
                    put16(2 * hd, r1.astype(BF16))
                    put16(2 * hd + 1, r2.astype(BF16))
                else:
                    put32(2 * hd, r1 * (DK_RET ** -0.5))
                    put32(2 * hd + 1, r2 * (DK_RET ** -0.5))
        elif section == 5:
            for n in range(N_SLICES):
                put16(n, piece(n).astype(BF16))
        else:
            for n in range(N_SLICES):
                put32(n, piece(n))

    last_step = pl.num_programs(0) * N_SECTIONS - 1
    for section in range(N_SECTIONS):
        @pl.when(j == section)
        def _(section=section):
            back = INPROJ_STAGE_SLOTS

            @pl.when(step >= back)
            def _():
                for cp in copies((section - back) % N_SECTIONS,
                                 i if section >= back else i - 1, slot):
                    cp.wait()

            def put32(n, value):
                st32[slot, n] = value

            def put16(n, value):
                st16[slot, n] = value

            epilogue(section, _dot(h_ref[...], w_ref[...]), put32, put16)
            kinds = [kind for kind, name in enumerate(INPROJ_SECTION_OUTPUTS[section])
                     if name is not None]
            for cp, kind in zip(copies(section, i, slot), kinds):
                cp.start(priority=kind)

            if section == N_SECTIONS - 1:
                @pl.when(step == last_step)
                def _():
                    for age in range(INPROJ_STAGE_SLOTS - 1, -1, -1):
                        for cp in copies(section - age, i,
                                         lax.rem(step - age, INPROJ_STAGE_SLOTS)):
                            cp.wait()


def _inproj(x, norm_w, w_in, qn, kn, inv_freq, *, tm, pos_base, pos_mod):
    m = x.shape[0]
    emit_bf16 = w_in.dtype == F32
    assert not emit_bf16 or m == tm
    assert pos_mod >= m or tm % pos_mod == 0
    tile_stride = tm if pos_mod >= m else 0
    half = DK_RET // 2
    f32_outputs = {name for name, _ in INPROJ_SECTION_OUTPUTS if name is not None}
    out_shape = [jax.ShapeDtypeStruct((N_SLICES, m, D_SB), F32 if name in f32_outputs else BF16)
                 for name in INPROJ_OUTPUT_ORDER]
    out_specs = [pl.BlockSpec(memory_space=pl.ANY)] * len(out_shape)
    w_spec = pl.BlockSpec((D_MODEL, SECTION_W), lambda i, j: (0, j))
    if emit_bf16:
        out_shape.append(jax.ShapeDtypeStruct(w_in.shape, BF16))
        out_specs.append(w_spec)
    out = pl.pallas_call(
        functools.partial(_inproj_kernel, tm=tm, tile_stride=tile_stride,
                          pos_base=pos_base, pos_mod=pos_mod, emit_bf16=emit_bf16),
        out_shape=out_shape,
        grid=(m // tm, N_SECTIONS),
        in_specs=[
            pl.BlockSpec((tm, D_MODEL), lambda i, j: (i, 0)),
            pl.BlockSpec((1, D_MODEL), lambda i, j: (0, 0)),
            w_spec,
            pl.BlockSpec((1, D_SB), lambda i, j: (0, 0)),
            pl.BlockSpec((1, D_SB), lambda i, j: (0, 0)),
            pl.BlockSpec((1, half), lambda i, j: (0, 0)),
        ],
        out_specs=out_specs,
        scratch_shapes=[pltpu.VMEM((tm, D_MODEL), BF16)] + [pltpu.VMEM((tm, half), F32)] * 4
        + [pltpu.VMEM((INPROJ_STAGE_SLOTS, N_SLICES, tm, D_SB), F32),
           pltpu.VMEM((INPROJ_STAGE_SLOTS, N_SLICES, tm, D_SB), BF16),
           pltpu.SemaphoreType.DMA((2, INPROJ_STAGE_SLOTS))],
        compiler_params=_cparams(("arbitrary", "arbitrary")),
        name="inproj",
    )(x, norm_w, w_in, qn, kn, inv_freq)
    return (tuple(out[:9]), out[9]) if emit_bf16 else tuple(out)


def _softplus2(z):
    return jnp.maximum(z, 0.0) + jnp.log2(1.0 + jnp.exp2(-jnp.abs(z)))


def _suffix_matrix(n):
    r = lax.broadcasted_iota(jnp.int32, (2 * n, n), 0)
    c = lax.broadcasted_iota(jnp.int32, (2 * n, n), 1)
    return jnp.where(jnp.where(r >= n, r - n, r) >= c, 1.0, 0.0).astype(BF16)


def _sb_logits(q, k, mask):
    z = _dot_nt(q, k)
    sp = _softplus2(z)
    if mask is not None:
        sp = jnp.where(mask, sp, 0.0)
    hi = sp.astype(BF16)
    lo = (sp - hi.astype(F32)).astype(BF16)
    return z, jnp.concatenate([hi, lo], axis=1)


def _sb_suffix_sums(splits, tri):
    q = splits[0].shape[0]
    sums = _dot(jnp.concatenate(splits, axis=0), tri)
    return [sums[n * q:(n + 1) * q] for n in range(len(splits))]


def _sb_weights(z, incl, v, mask):
    a = jnp.exp2(z - incl)
    if mask is not None:
        a = jnp.where(mask, a, 0.0)
    return _dot(a.astype(BF16), v)


SB_DEAD_CARRY = 150.0


SB_HEADS_PER_STEP = 4
SB_LEFT_SUFFIX_GROUP = 2


def _sb_prompt_kernel(q_ref, k_ref, v_ref, *rest):
    n_side = len(rest) // 2
    o_ref = rest[n_side]
    for src, dst in zip(rest[:n_side], rest[n_side + 1:]):
        dst[...] = src[...].astype(BF16)
    qi = pl.program_id(1)
    heads, _, d = q_ref.shape
    blk = SB_BLOCK
    tri = _suffix_matrix(blk)
    r = lax.broadcasted_iota(jnp.int32, (blk, blk), 0)
    c = lax.broadcasted_iota(jnp.int32, (blk, blk), 1)
    strict = c < r
    has_prev = qi > 0

    def kv(g, block):
        s0 = pl.multiple_of(block * blk, blk)
        return k_ref[g, pl.ds(s0, blk), :], v_ref[g, pl.ds(s0, blk), :]

    qs = [q_ref[g] for g in range(heads)]
    prev = jnp.maximum(qi - 1, 0)

    def kblk(g, block):
        return k_ref[g, pl.ds(pl.multiple_of(block * blk, blk), blk), :]

    def vblk(g, block):
        return v_ref[g, pl.ds(pl.multiple_of(block * blk, blk), blk), :]

    def grouped_sums(splits, group):
        sums = []
        for n in range(0, len(splits), group):
            sums += _sb_suffix_sums(list(splits[n:n + group]), tri)
        return sums

    z_d, hl_d = zip(*[_sb_logits(qs[g], kblk(g, qi), strict) for g in range(heads)])
    sums_d = grouped_sums(hl_d, heads)
    z_l, hl_l = zip(*[_sb_logits(qs[g], kblk(g, prev), None) for g in range(heads)])
    sums_l = grouped_sums(hl_l, SB_LEFT_SUFFIX_GROUP)
    acc_d = [_sb_weights(z_d[g], sums_d[g], vblk(g, qi), strict) for g in range(heads)]
    accs, carries = [], []
    for g in range(heads):
        carry_d = sums_d[g][:, 0:1]
        incl_l = sums_l[g] + carry_d
        d_l = _sb_weights(z_l[g], incl_l, vblk(g, prev), None)
        accs.append(acc_d[g] + jnp.where(has_prev, d_l, 0.0))
        carries.append(jnp.where(has_prev, incl_l[:, 0:1], carry_d))
    accs, carries = tuple(accs), tuple(carries)

    def alive(carries):
        low = functools.reduce(jnp.minimum, carries)
        return (jnp.min(low) < SB_DEAD_CARRY).astype(jnp.int32)

    def cond(state):
        t, live, _, _ = state
        return jnp.logical_and(t >= 0, live > 0)

    def body(state):
        t, _, accs, carries = state
        blocks = [kv(g, t) for g in range(heads)]
        zs, hls = zip(*[_sb_logits(qs[g], blocks[g][0], None) for g in range(heads)])
        sums = _sb_suffix_sums(list(hls), tri)
        new_accs, new_carries = [], []
        for g in range(heads):
            incl = sums[g] + carries[g]
            new_accs.append(accs[g] + _sb_weights(zs[g], incl, blocks[g][1], None))
            new_carries.append(incl[:, 0:1])
        return t - 1, alive(new_carries), tuple(new_accs), tuple(new_carries)

    _, _, accs, _ = lax.while_loop(cond, body, (qi - 2, alive(carries), accs, carries))
    for g in range(heads):
        o_ref[:, g * d:(g + 1) * d] = accs[g].astype(o_ref.dtype)


def _sb_prompt(q16, k16, v16, side=()):
    h, s, d = q16.shape
    tq = SB_BLOCK
    g = SB_HEADS_PER_STEP
    n_groups, n_blocks = h // g, s // tq

    def side_spec(w):
        rows, cols = w.shape
        assert rows % (16 * n_blocks) == 0 and cols % (128 * n_groups) == 0
        return pl.BlockSpec((rows // n_blocks, cols // n_groups), lambda hg, qi: (qi, hg))

    side_specs = [side_spec(w) for w in side]
    out = pl.pallas_call(
        _sb_prompt_kernel,
        out_shape=[jax.ShapeDtypeStruct((s, h * d), BF16)]
        + [jax.ShapeDtypeStruct(w.shape, BF16) for w in side],
        grid=(n_groups, n_blocks),
        in_specs=[
            pl.BlockSpec((g, tq, d), lambda hg, qi: (hg, qi, 0)),
            pl.BlockSpec((g, s, d), lambda hg, qi: (hg, 0, 0)),
            pl.BlockSpec((g, s, d), lambda hg, qi: (hg, 0, 0)),
        ] + side_specs,
        out_specs=[pl.BlockSpec((tq, g * d), lambda hg, qi: (qi, hg))] + side_specs,
        compiler_params=_cparams(("arbitrary", "arbitrary")),
        name="sb_prompt",
    )(q16, k16, v16, *side)
    return out[0], tuple(out[1:])


def _sb_sample_kernel(q_ref, kn_ref, vn_ref, kc_hbm, vc_hbm, o_ref, kbuf, vbuf, sem, *, past):
    nh, t, d = q_ref.shape
    ht = nh * t
    blk = SB_BLOCK
    n_blocks = past // blk
    b = pl.program_id(0)
    slot = lax.rem(b, 2)

    def cache_copies(batch, block, to_slot):
        rows = pl.ds(pl.multiple_of(block * blk, blk), blk)
        return (pltpu.make_async_copy(kc_hbm.at[batch, :, rows, :], kbuf.at[to_slot], sem.at[0, to_slot]),
                pltpu.make_async_copy(vc_hbm.at[batch, :, rows, :], vbuf.at[to_slot], sem.at[1, to_slot]))

    @pl.when(b == 0)
    def _():
        for n, cp in enumerate(cache_copies(0, n_blocks - 1, 0)):
            cp.start(priority=n)

    @pl.when(b + 1 < pl.num_programs(0))
    def _():
        for n, cp in enumerate(cache_copies(b + 1, n_blocks - 1, 1 - slot)):
            cp.start(priority=n)

    q_all = q_ref[...].reshape(ht, d).astype(F32)
    q_head = lax.broadcasted_iota(jnp.int32, (ht, d), 0) // t
    q_masked = [jnp.where(q_head == h, q_all, 0.0).astype(BF16) for h in range(nh)]

    def logits(keys_of_head):
        z = _dot_nt(keys_of_head(0), q_masked[0])
        for h in range(1, nh):
            z = z + _dot_nt(keys_of_head(h), q_masked[h])
        return z

    def suffix_matrix(n):
        r = lax.broadcasted_iota(jnp.int32, (n, 2 * n), 0)
        c = lax.broadcasted_iota(jnp.int32, (n, 2 * n), 1)
        return jnp.where(jnp.where(c >= n, c - n, c) >= r, 1.0, 0.0).astype(BF16)

    def suffix_sum(sp, lmat):
        hi = sp.astype(BF16)
        lo = (sp - hi.astype(F32)).astype(BF16)
        return _dot(lmat, jnp.concatenate([hi, lo], axis=0))

    def emit(a_t, values_of_head, acc):
        a = a_t.T.astype(BF16)
        return [acc[h] + _dot(a[h * t:(h + 1) * t, :], values_of_head(h)) for h in range(nh)]

    z = logits(lambda h: kn_ref[h])
    s_idx = lax.broadcasted_iota(jnp.int32, (t, ht), 0)
    t_idx = lax.broadcasted_iota(jnp.int32, (t, ht), 1) % t
    strict = s_idx < t_idx
    sp = jnp.where(strict, _softplus2(z), 0.0)
    incl = suffix_sum(sp, suffix_matrix(t))
    a_t = jnp.where(strict, jnp.exp2(z - incl), 0.0)
    acc = emit(a_t, lambda h: vn_ref[h], [jnp.zeros((t, d), F32) for _ in range(nh)])
    carry = incl[0:1, :]

    lmat = suffix_matrix(blk)

    def cache_block(from_slot, acc, carry):
        z = logits(lambda h: kbuf[from_slot, h].astype(BF16))
        incl = suffix_sum(_softplus2(z), lmat) + carry
        acc = emit(jnp.exp2(z - incl), lambda h: vbuf[from_slot, h].astype(BF16), acc)
        return acc, incl[0:1, :]

    def alive(carry):
        return (jnp.min(carry) < SB_DEAD_CARRY).astype(jnp.int32)

    for cp in cache_copies(b, n_blocks - 1, slot):
        cp.wait()
    acc, carry = cache_block(slot, acc, carry)

    def cond(state):
        block, live, _, _ = state
        return jnp.logical_and(block >= 0, live > 0)

    def body(state):
        block, _, acc, carry = state
        copies = cache_copies(b, block, 2)
        for n, cp in enumerate(copies):
            cp.start(priority=n)
        for cp in copies:
            cp.wait()
        acc, carry = cache_block(2, list(acc), carry)
        return block - 1, alive(carry), tuple(acc), carry

    _, _, acc, _ = lax.while_loop(cond, body, (n_blocks - 2, alive(carry), tuple(acc), carry))
    for h in range(nh):
        o_ref[:, h * d:(h + 1) * d] = acc[h].astype(o_ref.dtype)


def _sb_sample(q16, k16, v16, cache_k, cache_v, *, t):
    h, m, d = q16.shape
    nb = m // t
    past = cache_k.shape[2]
    assert past % SB_BLOCK == 0
    new_spec = pl.BlockSpec((h, t, d), lambda b: (0, b, 0))
    cache_spec = pl.BlockSpec(memory_space=pl.ANY)
    return pl.pallas_call(
        functools.partial(_sb_sample_kernel, past=past),
        out_shape=jax.ShapeDtypeStruct((m, h * d), BF16),
        grid=(nb,),
        in_specs=[new_spec, new_spec, new_spec, cache_spec, cache_spec],
        out_specs=pl.BlockSpec((t, h * d), lambda b: (b, 0)),
        scratch_shapes=[pltpu.VMEM((3, h, SB_BLOCK, d), F32),
                        pltpu.VMEM((3, h, SB_BLOCK, d), F32),
                        pltpu.SemaphoreType.DMA((2, 3))],
        compiler_params=_cparams(("arbitrary",)),
        name="sb_sample",
    )(q16, k16, v16, cache_k, cache_v)


def _ret_log_decay(hd):
    return math.log(1.0 - 2.0 ** (-5.0 - hd))


def _ret_kernel(q_ref, k_ref, v_ref, g_ref, nw_ref, s0_ref, r_ref, st_ref,
                intra_ref, qdec_ref, kdec_ref, *, chunk):
    c = pl.program_id(1)

    @pl.when(jnp.logical_and(pl.program_id(0) == 0, c == 0))
    def _():
        row = lax.broadcasted_iota(jnp.int32, (chunk, chunk), 0)
        col = lax.broadcasted_iota(jnp.int32, (chunk, chunk), 1)
        diff = (row - col).astype(F32)
        pos = lax.broadcasted_iota(jnp.int32, (chunk, DK_RET), 0).astype(F32)
        for hd in range(H_RET):
            lg = _ret_log_decay(hd)
            intra_ref[hd] = jnp.where(row >= col, jnp.exp(lg * jnp.maximum(diff, 0.0)), 0.0)
            qdec_ref[hd] = jnp.exp(lg * (pos + 1.0))
            kdec_ref[hd] = jnp.exp(lg * (chunk - 1.0 - pos))

    @pl.when(c == 0)
    def _():
        st_ref[...] = s0_ref[...]

    for sub in range(q_ref.shape[1] // chunk):
        rows = slice(sub * chunk, (sub + 1) * chunk)
        for hd in range(H_RET):
            intra, q_dec, k_dec = intra_ref[hd], qdec_ref[hd], kdec_ref[hd]
            c_dec = math.exp(_ret_log_decay(hd) * chunk)
            def head(ref):
                return jnp.concatenate([ref[2 * hd, rows, :], ref[2 * hd + 1, rows, :]], axis=-1)

            q, k, v = head(q_ref), head(k_ref), head(v_ref)
            state = st_ref[hd]
            scores = (_dot_nt(q, k.astype(BF16)) * intra).astype(BF16)
            o = _dot(scores, v) + _dot(q, state.astype(BF16)) * q_dec
            st_ref[hd] = c_dec * state + _dot_tn((k * k_dec).astype(BF16), v)
            o = o * lax.rsqrt(jnp.mean(o * o, axis=-1, keepdims=True) + EPS)
            sl = slice(hd * DV_RET, (hd + 1) * DV_RET)
            gate = head(g_ref)
            r_ref[rows, sl] = (o * nw_ref[:, sl] * (gate * jax.nn.sigmoid(gate))).astype(r_ref.dtype)


def _retention(rq, rk, rv, gate, norm_w, state0, *, chunk, per_step):
    n_slices, m, width = rq.shape
    h = n_slices * width // DK_RET
    nb = state0.shape[0]
    rows = per_step * chunk
    nc = m // (nb * rows)
    qkv_spec = pl.BlockSpec((n_slices, rows, width), lambda b, c: (0, b * nc + c, 0))
    st_spec = pl.BlockSpec((None, h, DK_RET, DV_RET), lambda b, c: (b, 0, 0, 0))
    return pl.pallas_call(
        functools.partial(_ret_kernel, chunk=chunk),
        out_shape=(jax.ShapeDtypeStruct((m, h * DV_RET), BF16),
                   jax.ShapeDtypeStruct(state0.shape, F32)),
        grid=(nb, nc),
        in_specs=[qkv_spec, qkv_spec, qkv_spec, qkv_spec,
                  pl.BlockSpec((1, h * DV_RET), lambda b, c: (0, 0)),
                  st_spec],
        out_specs=(pl.BlockSpec((rows, h * DV_RET), lambda b, c: (b * nc + c, 0)), st_spec),
        scratch_shapes=[pltpu.VMEM((h, chunk, chunk), F32),
                        pltpu.VMEM((h, chunk, DK_RET), F32),
                        pltpu.VMEM((h, chunk, DK_RET), F32)],
        compiler_params=_cparams(("arbitrary", "arbitrary")),
        name="retention",
    )(rq, rk, rv, gate, norm_w, state0)


def _outproj_kernel(x_ref, a_ref, b_ref, wa_ref, wb_ref, o_ref):
    o_ref[...] = x_ref[...] + _dot(a_ref[...], wa_ref[...]) + _dot(b_ref[...], wb_ref[...])


def _outproj(x, a_sb, a_ret, w_out, *, tm):
    m = x.shape[0]
    row = lambda i: (i, 0)
    return pl.pallas_call(
        _outproj_kernel,
        out_shape=jax.ShapeDtypeStruct((m, D_MODEL), F32),
        grid=(m // tm,),
        in_specs=[
            pl.BlockSpec((tm, D_MODEL), row),
            pl.BlockSpec((tm, SB_W), row),
            pl.BlockSpec((tm, RET_W), row),
            pl.BlockSpec((SB_W, D_MODEL), lambda i: (0, 0)),
            pl.BlockSpec((RET_W, D_MODEL), lambda i: (1, 0)),
        ],
        out_specs=pl.BlockSpec((tm, D_MODEL), row),
        compiler_params=_cparams(("arbitrary",)),
        name="outproj",
    )(x, a_sb, a_ret, w_out, w_out)


OUTPROJ_TK = 512


def _outproj_f32w_kernel(x_ref, a_ref, w_ref, o_ref, w16_ref):
    @pl.when(pl.program_id(0) == 0)
    def _():
        o_ref[...] = x_ref[...]

    w16_ref[...] = w_ref[...].astype(BF16)
    o_ref[...] += _dot(a_ref[...], w16_ref[...])


def _outproj_f32w(x, a, w_out):
    m = x.shape[0]
    d_mix = w_out.shape[0]
    return pl.pallas_call(
        _outproj_f32w_kernel,
        out_shape=(jax.ShapeDtypeStruct((m, D_MODEL), F32),
                   jax.ShapeDtypeStruct(w_out.shape, BF16)),
        grid=(d_mix // OUTPROJ_TK,),
        in_specs=[
            pl.BlockSpec((m, D_MODEL), lambda k: (0, 0)),
            pl.BlockSpec((m, OUTPROJ_TK), lambda k: (0, k)),
            pl.BlockSpec((OUTPROJ_TK, D_MODEL), lambda k: (k, 0)),
        ],
        out_specs=(pl.BlockSpec((m, D_MODEL), lambda k: (0, 0)),
                   pl.BlockSpec((OUTPROJ_TK, D_MODEL), lambda k: (k, 0))),
        compiler_params=_cparams(("arbitrary",)),
        name="outproj_f32w",
    )(x, a, w_out)


def _mixers(x1, weights, sb_fn, state0, *, tm, proj_tm, pos_base, pos_mod, chunk,
            chunks_per_step):
    (nmix, w_in, qn, kn, ron, w_out, inv_freq) = weights
    f32w = w_in.dtype == F32
    proj = _inproj(x1, nmix, w_in, qn, kn, inv_freq, tm=proj_tm,
                   pos_base=pos_base, pos_mod=pos_mod)
    if f32w:
        proj, w_in = proj
    sq, sk, sk16, sv, sv16, rq, rk, rv, gate = proj
    a_sb, sb_side = sb_fn(sq, sk16, sv16)
    a_ret, state = _retention(rq, rk, rv, gate, ron, state0, chunk=chunk, per_step=chunks_per_step)
    if f32w:
        x2, w_out = _outproj_f32w(x1, jnp.concatenate([a_sb, a_ret], axis=1), w_out)
    else:
        x2 = _outproj(x1, a_sb, a_ret, w_out, tm=tm)
    return x2, sk, sv, state, (w_in, w_out), sb_side


def kernel(x_prompt, x_sample, cache_sb_k, cache_sb_v, state_ret, ffn1_norm, ffn1_w_gate, ffn1_w_up, ffn1_w_down, mix_norm, w_in, sb_q_norm, sb_k_norm, ret_out_norm, w_out, ffn2_norm, ffn2_w_gate, ffn2_w_up, ffn2_w_down, final_norm):
    depth = ffn1_norm.shape[0]
    assert depth == 1
    nb_p, seq, _ = x_prompt.shape
    nb_s, dec_seq, _ = x_sample.shape
    past = cache_sb_k.shape[3]
    assert nb_p == 1

    half = DK_RET // 2
    inv_freq = (ROPE_BASE ** (-jnp.arange(half, dtype=F32) / half)).reshape(1, half)
    l = 0
    n1, n2, nf = ffn1_norm[l][None], ffn2_norm[l][None], final_norm[l][None]

    def mixer_weights(w_in_, w_out_):
        return (mix_norm[l][None], w_in_, sb_q_norm[l][None], sb_k_norm[l][None],
                ret_out_norm[l][None], w_out_, inv_freq)

    ms = nb_s * dec_seq
    xs = x_sample.reshape(ms, D_MODEL)
    xp = x_prompt.reshape(seq, D_MODEL)
    x1s, ffn1_16 = _ffn(xs, n1, ffn1_w_gate[l], ffn1_w_up[l], ffn1_w_down[l], tm=ms)
    x1p = _ffn(xp, n1, *ffn1_16, tm=1024)

    def sb_s(sq, sk16, sv16):
        return _sb_sample(sq, sk16, sv16, cache_sb_k[l], cache_sb_v[l], t=dec_seq), ()

    x2s, sks, svs, sts, (w_in16, w_out16), _ = _mixers(
        x1s, mixer_weights(w_in[l], w_out[l]), sb_s, state_ret[l],
        tm=ms, proj_tm=ms, pos_base=past, pos_mod=dec_seq, chunk=dec_seq,
        chunks_per_step=1)

    sb_p = functools.partial(_sb_prompt, side=(ffn2_w_gate[l], ffn2_w_up[l], ffn2_w_down[l]))
    zero_state = jnp.zeros((1, H_RET, DK_RET, DV_RET), F32)
    x2p, skp, svp, stp, _, ffn2_16 = _mixers(
        x1p, mixer_weights(w_in16, w_out16), sb_p, zero_state,
        tm=512, proj_tm=1024, pos_base=0, pos_mod=seq, chunk=RET_CHUNK,
        chunks_per_step=4)

    ys = _ffn(x2s, n2, *ffn2_16, nf, tm=ms)
    yp = _ffn(x2p, n2, *ffn2_16, nf, tm=1024)

    def cache_layout(t):
        return t.reshape(H_SB, nb_s, dec_seq, D_SB).transpose(1, 0, 2, 3)[None]

    return (yp.reshape(1, seq, D_MODEL), ys.reshape(nb_s, dec_seq, D_MODEL),
            skp[None, None], svp[None, None], stp[None],
            cache_layout(sks), cache_layout(svs), sts[None])
```
